```python
import math
import jax
import jax.numpy as jnp
from jax import lax
import numpy as np

D_MODEL = 1024
BATCH = 8
SEQ = 16384
DEPTH = 4

CTX_LEN = 256
GRID_W = 64

RET_HEADS = 4
RET_DK = 128
RET_DV = 256
RET_CHUNK = 128
RET_QK = RET_HEADS * RET_DK
RET_V = RET_HEADS * RET_DV

DN_HEADS = 4
DN_DK = 128
DN_DV = 256
DN_CHUNK = 64
DN_CONV = 5
DN_QK = DN_HEADS * DN_DK
DN_V = DN_HEADS * DN_DV
DN_CONV_CH = 2 * DN_QK + DN_V

FFN_HIDDEN = 2816

ROPE_BASE = 10000.0
EPS = 1e-6
N_MOD = 9

IN_SIZES = (RET_QK, RET_QK, RET_V, RET_V, DN_QK, DN_QK, DN_V, DN_V, 4 * DN_HEADS, D_MODEL, D_MODEL)
IN_WIDTH = sum(IN_SIZES)

kernel_name = "hybrid_retention_deltanet_prefix_dit"


def _split_points(sizes):
    pts, acc = [], 0
    for s in sizes[:-1]:
        acc += s
        pts.append(acc)
    return pts


def rmsnorm(x, w):
    xf = x.astype(jnp.float32)
    y = xf * lax.rsqrt(jnp.mean(xf * xf, axis=-1, keepdims=True) + EPS)
    return (y * w.astype(jnp.float32)).astype(x.dtype)


def modulate(h, shift, scale):
    return h * (1.0 + scale) + shift


def swiglu(h, w_gu, w_down):
    gate, up = jnp.split(h @ w_gu, 2, axis=-1)
    return (jax.nn.silu(gate) * up) @ w_down


def _heads(t, n_heads):
    b, l, _ = t.shape
    return t.reshape(b, l, n_heads, -1).transpose(0, 2, 1, 3).astype(jnp.float32)


def _merge_heads(t):
    b, h, l, d = t.shape
    return t.transpose(0, 2, 1, 3).reshape(b, l, h * d)


def _flip(t):
    return jnp.flip(t, axis=2)


def l2norm(t):
    return t * lax.rsqrt(jnp.sum(t * t, axis=-1, keepdims=True) + EPS)


def axial_rope(rows):
    half = RET_DK // 2
    n_freq = half // 2
    inv = ROPE_BASE ** (-jnp.arange(n_freq, dtype=jnp.float32) / n_freq)
    ang_r = jnp.arange(rows, dtype=jnp.float32)[:, None] * inv
    ang_c = jnp.arange(GRID_W, dtype=jnp.float32)[:, None] * inv
    ang = jnp.concatenate([
        jnp.broadcast_to(ang_r[:, None, :], (rows, GRID_W, n_freq)),
        jnp.broadcast_to(ang_c[None, :, :], (rows, GRID_W, n_freq)),
    ], axis=-1).reshape(rows * GRID_W, half)
    return jnp.cos(ang), jnp.sin(ang)


def apply_rope(t, cos, sin):
    half = t.shape[-1] // 2
    t1, t2 = t[..., :half], t[..., half:]
    return jnp.concatenate([t1 * cos - t2 * sin, t2 * cos + t1 * sin], axis=-1)


def short_conv(u, w):
    pad = DN_CONV // 2
    y = lax.conv_general_dilated(
        u, w[:, None, :], window_strides=(1,), padding=[(pad, pad)],
        dimension_numbers=("NWC", "WIO", "NWC"), feature_group_count=u.shape[-1])
    return jax.nn.silu(y)


def retention_chunked(q, k, v, log_gamma, s0, include_diag):
    b, h, l, dk = q.shape
    dv = v.shape[-1]
    c = RET_CHUNK
    n = l // c
    qc = q.reshape(b, h, n, c, dk)
    kc = k.reshape(b, h, n, c, dk)
    vc = v.reshape(b, h, n, c, dv)
    idx = jnp.arange(c, dtype=jnp.float32)
    diff = idx[:, None] - idx[None, :]
    mask = diff >= 0 if include_diag else diff > 0
    decay = jnp.where(mask, jnp.exp(log_gamma[:, None, None] * jnp.where(mask, diff, 0.0)), 0.0)
    scores = jnp.einsum("bhncd,bhnmd->bhncm", qc, kc) * decay[None, :, None]
    o_intra = jnp.einsum("bhncm,bhnme->bhnce", scores, vc)
    k_dec = kc * jnp.exp(log_gamma[:, None] * (c - 1.0 - idx)[None, :])[None, :, None, :, None]
    chunk_kv = jnp.einsum("bhncd,bhnce->bhnde", k_dec, vc)
    gamma_c = jnp.exp(log_gamma * c)[None, :, None, None]

    def step(s, kv):
        return s * gamma_c + kv, s

    s_fin, s_prev = lax.scan(step, s0, jnp.moveaxis(chunk_kv, 2, 0))
    s_prev = jnp.moveaxis(s_prev, 0, 2)
    q_dec = qc * jnp.exp(log_gamma[:, None] * (idx + 1.0)[None, :])[None, :, None, :, None]
    o_inter = jnp.einsum("bhncd,bhnde->bhnce", q_dec, s_prev)
    return (o_intra + o_inter).reshape(b, h, l, dv), s_fin


def gated_delta_chunked(q, k, v, g, beta, s0):
    b, h, l, dk = q.shape
    dv = v.shape[-1]
    c = DN_CHUNK
    n = l // c
    q = q.reshape(b, h, n, c, dk)
    k = k.reshape(b, h, n, c, dk)
    v = v.reshape(b, h, n, c, dv)
    g = g.reshape(b, h, n, c)
    beta = beta.reshape(b, h, n, c)
    gc = jnp.cumsum(g, axis=-1)
    incl = jnp.tril(jnp.ones((c, c), dtype=bool))
    strict = jnp.tril(jnp.ones((c, c), dtype=bool), -1)
    gdiff = gc[..., :, None] - gc[..., None, :]
    decay = jnp.where(incl, jnp.exp(jnp.where(incl, gdiff, 0.0)), 0.0)
    kb = k * beta[..., None]
    a = jnp.where(strict, jnp.einsum("bhnid,bhnjd->bhnij", kb, k) * decay, 0.0)
    eye = jnp.eye(c, dtype=jnp.float32)
    t = lax.linalg.triangular_solve(eye + a, jnp.broadcast_to(eye, a.shape),
                                    left_side=True, lower=True, unit_diagonal=True)
    u = jnp.einsum("bhnij,bhnje->bhnie", t, v * beta[..., None])
    w = jnp.einsum("bhnij,bhnjd->bhnid", t, kb * jnp.exp(gc)[..., None])
    attn = jnp.where(incl, jnp.einsum("bhnid,bhnjd->bhnij", q, k) * decay, 0.0)
    qg = q * jnp.exp(gc)[..., None]
    glast = gc[..., -1]
    kd = k * jnp.exp(glast[..., None] - gc)[..., None]

    def step(s, xs):
        u_n, w_n, qg_n, attn_n, kd_n, gl_n = xs
        v_new = u_n - jnp.einsum("bhcd,bhde->bhce", w_n, s)
        o = jnp.einsum("bhcd,bhde->bhce", qg_n, s) + jnp.einsum("bhij,bhje->bhie", attn_n, v_new)
        s = s * jnp.exp(gl_n)[..., None, None] + jnp.einsum("bhcd,bhce->bhde", kd_n, v_new)
        return s, o

    xs = tuple(jnp.moveaxis(z, 2, 0) for z in (u, w, qg, attn, kd, glast))
    s_fin, o = lax.scan(step, s0, xs)
    return jnp.moveaxis(o, 0, 2).reshape(b, h, l, dv), s_fin


def mixer_features(p, cos_sin, conv_w, a_log, dt_bias):
    b, l, _ = p.shape
    (rq, rk, rv, rg, dn_q, dn_k, dn_v, dz, dba, gate_a, gate_b) = jnp.split(p, _split_points(IN_SIZES), axis=-1)
    rq = _heads(rq, RET_HEADS)
    rk = _heads(rk, RET_HEADS) * (RET_DK ** -0.5)
    if cos_sin is not None:
        cos, sin = cos_sin
        rq = apply_rope(rq, cos, sin)
        rk = apply_rope(rk, cos, sin)
    rv = _heads(rv, RET_HEADS)
    qkv = short_conv(jnp.concatenate([dn_q, dn_k, dn_v], axis=-1).astype(jnp.float32), conv_w.astype(jnp.float32))
    dn_q, dn_k, dn_v = jnp.split(qkv, [DN_QK, 2 * DN_QK], axis=-1)
    dn_q = l2norm(_heads(dn_q, DN_HEADS)) * (DN_DK ** -0.5)
    dn_k = l2norm(_heads(dn_k, DN_HEADS))
    dn_v = _heads(dn_v, DN_HEADS)
    ba = dba.astype(jnp.float32).reshape(b, l, 4, DN_HEADS).transpose(2, 0, 3, 1)
    beta = jax.nn.sigmoid(ba[:2])
    g = -jnp.exp(a_log.astype(jnp.float32))[:, None, :, None] * jax.nn.softplus(
        ba[2:] + dt_bias.astype(jnp.float32)[:, None, :, None])
    return (rq, rk, rv, dn_q, dn_k, dn_v, beta, g), (rg, dz, gate_a, gate_b)


def bidirectional_mix(fc, fx, log_gamma):
    crq, crk, crv, cdq, cdk, cdv, cbeta, cg = fc
    xrq, xrk, xrv, xdq, xdk, xdv, xbeta, xg = fx
    b = crq.shape[0]
    s_ret0 = jnp.zeros((b, RET_HEADS, RET_DK, RET_DV), jnp.float32)
    s_dn0 = jnp.zeros((b, DN_HEADS, DN_DK, DN_DV), jnp.float32)
    f = _flip
    rc_f, sr_f = retention_chunked(crq, crk, crv, log_gamma, s_ret0, True)
    rc_b, sr_b = retention_chunked(f(crq), f(crk), f(crv), log_gamma, s_ret0, False)
    rx_f, _ = retention_chunked(xrq, xrk, xrv, log_gamma, sr_f, True)
    rx_b, _ = retention_chunked(f(xrq), f(xrk), f(xrv), log_gamma, sr_b, False)
    dc_f, sd_f = gated_delta_chunked(cdq, cdk, cdv, cg[0], cbeta[0], s_dn0)
    dc_b, sd_b = gated_delta_chunked(f(cdq), f(cdk), f(cdv), f(cg[1]), f(cbeta[1]), s_dn0)
    dx_f, _ = gated_delta_chunked(xdq, xdk, xdv, xg[0], xbeta[0], sd_f)
    dx_b, _ = gated_delta_chunked(f(xdq), f(xdk), f(xdv), f(xg[1]), f(xbeta[1]), sd_b)
    ctx_out = (rc_f + f(rc_b), dc_f + f(dc_b))
    lat_out = (rx_f + f(rx_b), dx_f + f(dx_b))
    return ctx_out, lat_out


def mixer_output(ret_o, dn_o, gates, dn_norm_w, w_ret_out, w_dn_out, w_o, dtype):
    rg, dz, gate_a, gate_b = gates
    ret = ret_o * lax.rsqrt(jnp.mean(ret_o * ret_o, axis=-1, keepdims=True) + EPS)
    ret = _merge_heads(ret).astype(dtype) * jax.nn.silu(rg)
    dn = dn_o * lax.rsqrt(jnp.mean(dn_o * dn_o, axis=-1, keepdims=True) + EPS) * dn_norm_w.astype(jnp.float32)
    dn = _merge_heads(dn).astype(dtype) * jax.nn.silu(dz)
    y = jax.nn.sigmoid(gate_a) * (ret @ w_ret_out) + jax.nn.sigmoid(gate_b) * (dn @ w_dn_out)
    return y @ w_o


def _fwd_setup_inputs(seed: int = 0) -> dict:
    key = jax.random.key(seed)
    ks = jax.random.split(key, 24)
    f32 = jnp.float32

    def nrm(k, shape, fan_in, gain=1.0):
        return jax.random.normal(k, shape, f32) * (gain * fan_in ** -0.5)

    a_init = jax.random.uniform(ks[10], (DEPTH, 2, DN_HEADS), f32, 1.0, 16.0)
    dt = jnp.exp(jax.random.uniform(ks[11], (DEPTH, 2, DN_HEADS), f32, math.log(1e-3), math.log(1e-1)))
    dt_bias = dt + jnp.log(-jnp.expm1(-dt))
    return {
        "x": jax.random.normal(ks[0], (BATCH, SEQ, D_MODEL), f32),
        "c": jax.random.normal(ks[1], (BATCH, D_MODEL), f32),
        "ctx": jax.random.normal(ks[2], (BATCH, CTX_LEN, D_MODEL), f32),
        "c_ctx": jax.random.normal(ks[3], (D_MODEL,), f32),
        "ada_w": nrm(ks[4], (DEPTH, D_MODEL, N_MOD * D_MODEL), D_MODEL, 0.5),
        "ada_b": 0.02 * jax.random.normal(ks[5], (DEPTH, N_MOD * D_MODEL), f32),
        "norm_w": 1.0 + 0.05 * jax.random.normal(ks[6], (DEPTH, 3, D_MODEL), f32),
        "ffn1_wgu": nrm(ks[7], (DEPTH, D_MODEL, 2 * FFN_HIDDEN), D_MODEL),
        "ffn1_wd": nrm(ks[8], (DEPTH, FFN_HIDDEN, D_MODEL), FFN_HIDDEN),
        "w_in": nrm(ks[9], (DEPTH, D_MODEL, IN_WIDTH), D_MODEL),
        "dn_conv_w": nrm(ks[12], (DEPTH, DN_CONV, DN_CONV_CH), DN_CONV),
        "dn_a_log": jnp.log(a_init),
        "dn_dt_bias": dt_bias,
        "dn_norm_w": 1.0 + 0.05 * jax.random.normal(ks[13], (DEPTH, DN_DV), f32),
        "w_ret_out": nrm(ks[14], (DEPTH, RET_V, D_MODEL), RET_V),
        "w_dn_out": nrm(ks[15], (DEPTH, DN_V, D_MODEL), DN_V),
        "w_o": nrm(ks[16], (DEPTH, D_MODEL, D_MODEL), D_MODEL),
        "ffn2_wgu": nrm(ks[17], (DEPTH, D_MODEL, 2 * FFN_HIDDEN), D_MODEL),
        "ffn2_wd": nrm(ks[18], (DEPTH, FFN_HIDDEN, D_MODEL), FFN_HIDDEN),
        "final_norm_w": 1.0 + 0.05 * jax.random.normal(ks[19], (D_MODEL,), f32),
    }


def _fwd_reference(x, c, ctx, c_ctx, ada_w, ada_b, norm_w, ffn1_wgu, ffn1_wd, w_in, dn_conv_w,
              dn_a_log, dn_dt_bias, dn_norm_w, w_ret_out, w_dn_out, w_o, ffn2_wgu, ffn2_wd,
              final_norm_w):
    b, l, _ = x.shape
    rows = l // GRID_W
    rope = axial_rope(rows)
    log_gamma = jnp.log1p(-jnp.power(2.0, -5.0 - jnp.arange(RET_HEADS, dtype=jnp.float32)))
    silu_c = jax.nn.silu(c)
    silu_cc = jax.nn.silu(c_ctx)[None, :]
    cx = ctx
    for layer in range(DEPTH):
        last = layer == DEPTH - 1
        mx = jnp.split((silu_c @ ada_w[layer] + ada_b[layer])[:, None, :], N_MOD, axis=-1)
        mc = jnp.split((silu_cc @ ada_w[layer] + ada_b[layer])[:, None, :], N_MOD, axis=-1)

        x = x + 0.5 * mx[2] * swiglu(modulate(rmsnorm(x, norm_w[layer, 0]), mx[0], mx[1]),
                                     ffn1_wgu[layer], ffn1_wd[layer])
        cx = cx + 0.5 * mc[2] * swiglu(modulate(rmsnorm(cx, norm_w[layer, 0]), mc[0], mc[1]),
                                       ffn1_wgu[layer], ffn1_wd[layer])

        hx = modulate(rmsnorm(x, norm_w[layer, 1]), mx[3], mx[4])
        hc = modulate(rmsnorm(cx, norm_w[layer, 1]), mc[3], mc[4])
        fx, gx = mixer_features(hx @ w_in[layer], rope, dn_conv_w[layer], dn_a_log[layer], dn_dt_bias[layer])
        fc, gc = mixer_features(hc @ w_in[layer], None, dn_conv_w[layer], dn_a_log[layer], dn_dt_bias[layer])
        (ret_c, dn_c), (ret_x, dn_x) = bidirectional_mix(fc, fx, log_gamma)
        x = x + mx[5] * mixer_output(ret_x, dn_x, gx, dn_norm_w[layer], w_ret_out[layer],
                                     w_dn_out[layer], w_o[layer], x.dtype)

        x = x + 0.5 * mx[8] * swiglu(modulate(rmsnorm(x, norm_w[layer, 2]), mx[6], mx[7]),
                                     ffn2_wgu[layer], ffn2_wd[layer])
        if not last:
            cx = cx + mc[5] * mixer_output(ret_c, dn_c, gc, dn_norm_w[layer], w_ret_out[layer],
                                           w_dn_out[layer], w_o[layer], cx.dtype)
            cx = cx + 0.5 * mc[8] * swiglu(modulate(rmsnorm(cx, norm_w[layer, 2]), mc[6], mc[7]),
                                           ffn2_wgu[layer], ffn2_wd[layer])
    return rmsnorm(x, final_norm_w)


import jax as _jax
import jax.numpy as _jnp

TWIN_FORMAT = 'train_step'
FWD_PARAMS = ['x', 'c', 'ctx', 'c_ctx', 'ada_w', 'ada_b', 'norm_w', 'ffn1_wgu', 'ffn1_wd', 'w_in', 'dn_conv_w', 'dn_a_log', 'dn_dt_bias', 'dn_norm_w', 'w_ret_out', 'w_dn_out', 'w_o', 'ffn2_wgu', 'ffn2_wd', 'final_norm_w']
TWIN_WEIGHTS = ['c_ctx', 'ada_w', 'ada_b', 'norm_w', 'ffn1_wgu', 'ffn1_wd', 'w_in', 'dn_conv_w', 'dn_a_log', 'dn_dt_bias', 'dn_norm_w', 'w_ret_out', 'w_dn_out', 'w_o', 'ffn2_wgu', 'ffn2_wd', 'final_norm_w']
TWIN_DIFF_INPUT = 'x'
TWIN_INPUTS = ['x', 'c', 'ctx', 'c_ctx', 'ada_w', 'ada_b', 'norm_w', 'ffn1_wgu', 'ffn1_wd', 'w_in', 'dn_conv_w', 'dn_a_log', 'dn_dt_bias', 'dn_norm_w', 'w_ret_out', 'w_dn_out', 'w_o', 'ffn2_wgu', 'ffn2_wd', 'final_norm_w', 'loss_target', 'm_c_ctx', 'm_ada_w', 'm_ada_b', 'm_norm_w', 'm_ffn1_wgu', 'm_ffn1_wd', 'm_w_in', 'm_dn_conv_w', 'm_dn_a_log', 'm_dn_dt_bias', 'm_dn_norm_w', 'm_w_ret_out', 'm_w_dn_out', 'm_w_o', 'm_ffn2_wgu', 'm_ffn2_wd', 'm_final_norm_w', 'v_c_ctx', 'v_ada_w', 'v_ada_b', 'v_norm_w', 'v_ffn1_wgu', 'v_ffn1_wd', 'v_w_in', 'v_dn_conv_w', 'v_dn_a_log', 'v_dn_dt_bias', 'v_dn_norm_w', 'v_w_ret_out', 'v_w_dn_out', 'v_w_o', 'v_ffn2_wgu', 'v_ffn2_wd', 'v_final_norm_w']
TWIN_OUTPUTS = ['loss', 'grad_x', 'grad_c_ctx', 'grad_ada_w', 'grad_ada_b', 'grad_norm_w', 'grad_ffn1_wgu', 'grad_ffn1_wd', 'grad_w_in', 'grad_dn_conv_w', 'grad_dn_a_log', 'grad_dn_dt_bias', 'grad_dn_norm_w', 'grad_w_ret_out', 'grad_w_dn_out', 'grad_w_o', 'grad_ffn2_wgu', 'grad_ffn2_wd', 'grad_final_norm_w', 'delta_c_ctx', 'delta_ada_w', 'delta_ada_b', 'delta_norm_w', 'delta_ffn1_wgu', 'delta_ffn1_wd', 'delta_w_in', 'delta_dn_conv_w', 'delta_dn_a_log', 'delta_dn_dt_bias', 'delta_dn_norm_w', 'delta_w_ret_out', 'delta_w_dn_out', 'delta_w_o', 'delta_ffn2_wgu', 'delta_ffn2_wd', 'delta_final_norm_w', 'new_m_c_ctx', 'new_m_ada_w', 'new_m_ada_b', 'new_m_norm_w', 'new_m_ffn1_wgu', 'new_m_ffn1_wd', 'new_m_w_in', 'new_m_dn_conv_w', 'new_m_dn_a_log', 'new_m_dn_dt_bias', 'new_m_dn_norm_w', 'new_m_w_ret_out', 'new_m_w_dn_out', 'new_m_w_o', 'new_m_ffn2_wgu', 'new_m_ffn2_wd', 'new_m_final_norm_w', 'new_v_c_ctx', 'new_v_ada_w', 'new_v_ada_b', 'new_v_norm_w', 'new_v_ffn1_wgu', 'new_v_ffn1_wd', 'new_v_w_in', 'new_v_dn_conv_w', 'new_v_dn_a_log', 'new_v_dn_dt_bias', 'new_v_dn_norm_w', 'new_v_w_ret_out', 'new_v_w_dn_out', 'new_v_w_o', 'new_v_ffn2_wgu', 'new_v_ffn2_wd', 'new_v_final_norm_w']
TWIN_LEAF_KINDS = {'loss': 'loss', 'grad_x': 'grad_x', 'grad_c_ctx': 'grad_w', 'grad_ada_w': 'grad_w', 'grad_ada_b': 'grad_w', 'grad_norm_w': 'grad_w', 'grad_ffn1_wgu': 'grad_w', 'grad_ffn1_wd': 'grad_w', 'grad_w_in': 'grad_w', 'grad_dn_conv_w': 'grad_w', 'grad_dn_a_log': 'grad_w', 'grad_dn_dt_bias': 'grad_w', 'grad_dn_norm_w': 'grad_w', 'grad_w_ret_out': 'grad_w', 'grad_w_dn_out': 'grad_w', 'grad_w_o': 'grad_w', 'grad_ffn2_wgu': 'grad_w', 'grad_ffn2_wd': 'grad_w', 'grad_final_norm_w': 'grad_w', 'delta_c_ctx': 'delta_w', 'delta_ada_w': 'delta_w', 'delta_ada_b': 'delta_w', 'delta_norm_w': 'delta_w', 'delta_ffn1_wgu': 'delta_w', 'delta_ffn1_wd': 'delta_w', 'delta_w_in': 'delta_w', 'delta_dn_conv_w': 'delta_w', 'delta_dn_a_log': 'delta_w', 'delta_dn_dt_bias': 'delta_w', 'delta_dn_norm_w': 'delta_w', 'delta_w_ret_out': 'delta_w', 'delta_w_dn_out': 'delta_w', 'delta_w_o': 'delta_w', 'delta_ffn2_wgu': 'delta_w', 'delta_ffn2_wd': 'delta_w', 'delta_final_norm_w': 'delta_w', 'new_m_c_ctx': 'new_m', 'new_m_ada_w': 'new_m', 'new_m_ada_b': 'new_m', 'new_m_norm_w': 'new_m', 'new_m_ffn1_wgu': 'new_m', 'new_m_ffn1_wd': 'new_m', 'new_m_w_in': 'new_m', 'new_m_dn_conv_w': 'new_m', 'new_m_dn_a_log': 'new_m', 'new_m_dn_dt_bias': 'new_m', 'new_m_dn_norm_w': 'new_m', 'new_m_w_ret_out': 'new_m', 'new_m_w_dn_out': 'new_m', 'new_m_w_o': 'new_m', 'new_m_ffn2_wgu': 'new_m', 'new_m_ffn2_wd': 'new_m', 'new_m_final_norm_w': 'new_m', 'new_v_c_ctx': 'new_v', 'new_v_ada_w': 'new_v', 'new_v_ada_b': 'new_v', 'new_v_norm_w': 'new_v', 'new_v_ffn1_wgu': 'new_v', 'new_v_ffn1_wd': 'new_v', 'new_v_w_in': 'new_v', 'new_v_dn_conv_w': 'new_v', 'new_v_dn_a_log': 'new_v', 'new_v_dn_dt_bias': 'new_v', 'new_v_dn_norm_w': 'new_v', 'new_v_w_ret_out': 'new_v', 'new_v_w_dn_out': 'new_v', 'new_v_w_o': 'new_v', 'new_v_ffn2_wgu': 'new_v', 'new_v_ffn2_wd': 'new_v', 'new_v_final_norm_w': 'new_v'}


def _forward(args):
    return _fwd_reference(*[args[k] for k in FWD_PARAMS])


def _output_shape():
    def fwd():
        inp = _fwd_setup_inputs(0)
        return _fwd_reference(*[inp[k] for k in FWD_PARAMS])
    out = _jax.eval_shape(fwd)
    return out.shape, out.dtype

N_MICROBATCH = 1
ADAM_LR = 0.001
ADAM_B1 = 0.9
ADAM_B2 = 0.999
ADAM_EPS = 1e-08
ADAM_WD = 0.01
ADAM_STEP = 10
PER_EXAMPLE_BATCH_AXIS = {'x': 0, 'c': 0, 'ctx': 0, 'loss_target': 0}
SHARED_INPUTS = []
_WEIGHT_DTYPES = {'c_ctx': _jnp.float32, 'ada_w': _jnp.float32, 'ada_b': _jnp.float32, 'norm_w': _jnp.float32, 'ffn1_wgu': _jnp.float32, 'ffn1_wd': _jnp.float32, 'w_in': _jnp.float32, 'dn_conv_w': _jnp.float32, 'dn_a_log': _jnp.float32, 'dn_dt_bias': _jnp.float32, 'dn_norm_w': _jnp.float32, 'w_ret_out': _jnp.float32, 'w_dn_out': _jnp.float32, 'w_o': _jnp.float32, 'ffn2_wgu': _jnp.float32, 'ffn2_wd': _jnp.float32, 'final_norm_w': _jnp.float32}
MOMENT_SCALE = {'c_ctx': 4.361969e-02, 'ada_w': 7.646855e-02, 'ada_b': 1.364940e-01, 'norm_w': 7.228144e-02, 'ffn1_wgu': 2.459102e-02, 'ffn1_wd': 4.020862e-02, 'w_in': 3.851452e-02, 'dn_conv_w': 3.534066e-02, 'dn_a_log': 1.935003e-01, 'dn_dt_bias': 1.934535e-01, 'dn_norm_w': 8.136793e-02, 'w_ret_out': 3.900354e-02, 'w_dn_out': 3.972984e-02, 'w_o': 5.569865e-02, 'ffn2_wgu': 2.354085e-02, 'ffn2_wd': 3.850359e-02, 'final_norm_w': 1.278963e+02}


def _to_microbatches(a, axis):
    t = _jnp.moveaxis(a, axis, 0)
    t = t.reshape((N_MICROBATCH, t.shape[0] // N_MICROBATCH) + t.shape[1:])
    return _jnp.moveaxis(t, 1, axis + 1)


def setup_inputs(seed: int = 0) -> dict:
    inp = _fwd_setup_inputs(seed)
    key = _jax.random.fold_in(_jax.random.key(seed), 7919)
    shape, _ = _output_shape()
    out = dict(inp)
    out["loss_target"] = _jax.random.normal(_jax.random.fold_in(key, 0), shape, _jnp.float32)
    for i, name in enumerate(TWIN_WEIGHTS):
        w = inp[name].astype(_jnp.float32)
        if MOMENT_SCALE is None:
            s = _jnp.sqrt(_jnp.mean(_jnp.square(w)) + 1e-30)
        else:
            s = MOMENT_SCALE[name]
        km, kv = _jax.random.split(_jax.random.fold_in(key, i + 1))
        out[name] = w
        out["m_" + name] = s * _jax.random.normal(km, w.shape, _jnp.float32)
        out["v_" + name] = (s * s) * _jax.random.uniform(kv, w.shape, _jnp.float32, 0.5, 1.5)
    if N_MICROBATCH > 1:
        for name, axis in PER_EXAMPLE_BATCH_AXIS.items():
            out[name] = _to_microbatches(out[name], axis)
    return {'x': out['x'], 'c': out['c'], 'ctx': out['ctx'], 'c_ctx': out['c_ctx'], 'ada_w': out['ada_w'], 'ada_b': out['ada_b'], 'norm_w': out['norm_w'], 'ffn1_wgu': out['ffn1_wgu'], 'ffn1_wd': out['ffn1_wd'], 'w_in': out['w_in'], 'dn_conv_w': out['dn_conv_w'], 'dn_a_log': out['dn_a_log'], 'dn_dt_bias': out['dn_dt_bias'], 'dn_norm_w': out['dn_norm_w'], 'w_ret_out': out['w_ret_out'], 'w_dn_out': out['w_dn_out'], 'w_o': out['w_o'], 'ffn2_wgu': out['ffn2_wgu'], 'ffn2_wd': out['ffn2_wd'], 'final_norm_w': out['final_norm_w'], 'loss_target': out['loss_target'], 'm_c_ctx': out['m_c_ctx'], 'm_ada_w': out['m_ada_w'], 'm_ada_b': out['m_ada_b'], 'm_norm_w': out['m_norm_w'], 'm_ffn1_wgu': out['m_ffn1_wgu'], 'm_ffn1_wd': out['m_ffn1_wd'], 'm_w_in': out['m_w_in'], 'm_dn_conv_w': out['m_dn_conv_w'], 'm_dn_a_log': out['m_dn_a_log'], 'm_dn_dt_bias': out['m_dn_dt_bias'], 'm_dn_norm_w': out['m_dn_norm_w'], 'm_w_ret_out': out['m_w_ret_out'], 'm_w_dn_out': out['m_w_dn_out'], 'm_w_o': out['m_w_o'], 'm_ffn2_wgu': out['m_ffn2_wgu'], 'm_ffn2_wd': out['m_ffn2_wd'], 'm_final_norm_w': out['m_final_norm_w'], 'v_c_ctx': out['v_c_ctx'], 'v_ada_w': out['v_ada_w'], 'v_ada_b': out['v_ada_b'], 'v_norm_w': out['v_norm_w'], 'v_ffn1_wgu': out['v_ffn1_wgu'], 'v_ffn1_wd': out['v_ffn1_wd'], 'v_w_in': out['v_w_in'], 'v_dn_conv_w': out['v_dn_conv_w'], 'v_dn_a_log': out['v_dn_a_log'], 'v_dn_dt_bias': out['v_dn_dt_bias'], 'v_dn_norm_w': out['v_dn_norm_w'], 'v_w_ret_out': out['v_w_ret_out'], 'v_w_dn_out': out['v_w_dn_out'], 'v_w_o': out['v_w_o'], 'v_ffn2_wgu': out['v_ffn2_wgu'], 'v_ffn2_wd': out['v_ffn2_wd'], 'v_final_norm_w': out['v_final_norm_w']}


def _loss(weights, diff, rest, loss_target):
    with _jax.named_scope("forward"):
        args = {**rest, TWIN_DIFF_INPUT: diff, **{k: w.astype(_WEIGHT_DTYPES[k]) for k, w in weights.items()}}
        y = _forward(args)
    with _jax.named_scope("loss_head"):
        err = _jnp.square(y.astype(_jnp.float32) - loss_target)
        return 0.5 * _jnp.sum(_jnp.mean(err, axis=-1)) if err.ndim else 0.5 * err


def _adamw(w, g, m, v):
    m = ADAM_B1 * m + (1.0 - ADAM_B1) * g
    v = ADAM_B2 * v + (1.0 - ADAM_B2) * _jnp.square(g)
    m_hat = m / (1.0 - ADAM_B1 ** ADAM_STEP)
    v_hat = v / (1.0 - ADAM_B2 ** ADAM_STEP)
    delta = -ADAM_LR * (m_hat / (_jnp.sqrt(v_hat) + ADAM_EPS) + ADAM_WD * w)
    return delta, m, v


def reference(x, c, ctx, c_ctx, ada_w, ada_b, norm_w, ffn1_wgu, ffn1_wd, w_in, dn_conv_w, dn_a_log, dn_dt_bias, dn_norm_w, w_ret_out, w_dn_out, w_o, ffn2_wgu, ffn2_wd, final_norm_w, loss_target, m_c_ctx, m_ada_w, m_ada_b, m_norm_w, m_ffn1_wgu, m_ffn1_wd, m_w_in, m_dn_conv_w, m_dn_a_log, m_dn_dt_bias, m_dn_norm_w, m_w_ret_out, m_w_dn_out, m_w_o, m_ffn2_wgu, m_ffn2_wd, m_final_norm_w, v_c_ctx, v_ada_w, v_ada_b, v_norm_w, v_ffn1_wgu, v_ffn1_wd, v_w_in, v_dn_conv_w, v_dn_a_log, v_dn_dt_bias, v_dn_norm_w, v_w_ret_out, v_w_dn_out, v_w_o, v_ffn2_wgu, v_ffn2_wd, v_final_norm_w):
    given = dict(x=x, c=c, ctx=ctx, c_ctx=c_ctx, ada_w=ada_w, ada_b=ada_b, norm_w=norm_w, ffn1_wgu=ffn1_wgu, ffn1_wd=ffn1_wd, w_in=w_in, dn_conv_w=dn_conv_w, dn_a_log=dn_a_log, dn_dt_bias=dn_dt_bias, dn_norm_w=dn_norm_w, w_ret_out=w_ret_out, w_dn_out=w_dn_out, w_o=w_o, ffn2_wgu=ffn2_wgu, ffn2_wd=ffn2_wd, final_norm_w=final_norm_w, loss_target=loss_target, m_c_ctx=m_c_ctx, m_ada_w=m_ada_w, m_ada_b=m_ada_b, m_norm_w=m_norm_w, m_ffn1_wgu=m_ffn1_wgu, m_ffn1_wd=m_ffn1_wd, m_w_in=m_w_in, m_dn_conv_w=m_dn_conv_w, m_dn_a_log=m_dn_a_log, m_dn_dt_bias=m_dn_dt_bias, m_dn_norm_w=m_dn_norm_w, m_w_ret_out=m_w_ret_out, m_w_dn_out=m_w_dn_out, m_w_o=m_w_o, m_ffn2_wgu=m_ffn2_wgu, m_ffn2_wd=m_ffn2_wd, m_final_norm_w=m_final_norm_w, v_c_ctx=v_c_ctx, v_ada_w=v_ada_w, v_ada_b=v_ada_b, v_norm_w=v_norm_w, v_ffn1_wgu=v_ffn1_wgu, v_ffn1_wd=v_ffn1_wd, v_w_in=v_w_in, v_dn_conv_w=v_dn_conv_w, v_dn_a_log=v_dn_a_log, v_dn_dt_bias=v_dn_dt_bias, v_dn_norm_w=v_dn_norm_w, v_w_ret_out=v_w_ret_out, v_w_dn_out=v_w_dn_out, v_w_o=v_w_o, v_ffn2_wgu=v_ffn2_wgu, v_ffn2_wd=v_ffn2_wd, v_final_norm_w=v_final_norm_w)
    weights = {n: given[n] for n in TWIN_WEIGHTS}
    shared = {n: given[n] for n in SHARED_INPUTS}
    per_example = {n: given[n] for n in ['x', 'c', 'ctx']}
    grad_fn = _jax.value_and_grad(_loss, argnums=(0, 1))

    def one_microbatch(ex, loss_target):
        ex = dict(ex)
        diff = ex.pop(TWIN_DIFF_INPUT)
        return grad_fn(weights, diff, {**shared, **ex}, loss_target)

    if N_MICROBATCH == 1:
        loss, (grad_w, grad_x) = one_microbatch(per_example, given["loss_target"])
    else:
        def body(carry, xs):
            loss_sum, grad_sum = carry
            l_k, (gw_k, gx_k) = one_microbatch(xs[0], xs[1])
            with _jax.named_scope("update"):
                return (loss_sum + l_k, _jax.tree.map(_jnp.add, grad_sum, gw_k)), gx_k

        init = (_jnp.zeros((), _jnp.float32), _jax.tree.map(_jnp.zeros_like, weights))
        (loss, grad_w), grad_x = _jax.lax.scan(body, init, (per_example, given["loss_target"]))
    with _jax.named_scope("update"):
        delta_w, new_m, new_v = {}, {}, {}
        for n in TWIN_WEIGHTS:
            delta_w[n], new_m[n], new_v[n] = _adamw(weights[n], grad_w[n], given["m_" + n], given["v_" + n])
    return (loss, grad_x, *[grad_w[n] for n in TWIN_WEIGHTS], *[delta_w[n] for n in TWIN_WEIGHTS],
            *[new_m[n] for n in TWIN_WEIGHTS], *[new_v[n] for n in TWIN_WEIGHTS])
```

```python
import functools
import math

import jax
import jax.numpy as jnp
from jax import lax
from jax.experimental import pallas as pl
from jax.experimental.pallas import tpu as pltpu

F32 = jnp.float32
BF16 = jnp.bfloat16

D = 1024
NH = 4
DK = 128
DV = 256
RET_C = 128
DN_C = 64
FFN_H = 2816
FFN_HB = 1408
N_MOD = 9
DEPTH = 4
N_DEV = 8
EPS = 1e-6
CONV_K = 5
HALO = 8

PW = 8320
C_RQK, C_RV, C_RG, C_DQK, C_DVV, C_DZ, C_GA, C_GB = 0, 1, 2, 3, 4, 5, 6, 7
C_DBA = 64
PROJ_TN = 1664

LANE = 128
VMEM_LIMIT = 56 * 1024 * 1024

ADAM_LR, ADAM_B1, ADAM_B2, ADAM_EPS, ADAM_WD, ADAM_STEP = 0.001, 0.9, 0.999, 1e-08, 0.01, 10

NN = ((1,), (0,))
NT = ((1,), (1,))
TN = ((0,), (0,))
HI = lax.Precision.HIGHEST


def _params(n_grid):
    return pltpu.CompilerParams(dimension_semantics=("arbitrary",) * n_grid, vmem_limit_bytes=VMEM_LIMIT)


def _dot(a, b, dims):
    return lax.dot_general(a.astype(BF16), b.astype(BF16), (dims, ((), ())), preferred_element_type=F32)


def _dot_hi(a, b, dims=NN):
    return lax.dot_general(a, b, (dims, ((), ())), precision=HI, preferred_element_type=F32)


@jax.custom_vjp
def mm_nn(a, b):
    return _dot(a, b, NN)


mm_nn.defvjp(lambda a, b: (_dot(a, b, NN), (a, b)),
             lambda r, g: (_dot(g, r[1], NT), _dot(r[0], g, TN)))


@jax.custom_vjp
def mm_nt(a, b):
    return _dot(a, b, NT)


mm_nt.defvjp(lambda a, b: (_dot(a, b, NT), (a, b)),
             lambda r, g: (_dot(g, r[1], NN), _dot(g, r[0], TN)))


@jax.custom_vjp
def mm_tn(a, b):
    return _dot(a, b, TN)


mm_tn.defvjp(lambda a, b: (_dot(a, b, TN), (a, b)),
             lambda r, g: (_dot(r[1], g, NT), _dot(r[0], g, NN)))


def _normmod(x, nw, shift, scale):
    r = lax.rsqrt(jnp.mean(x * x, axis=-1, keepdims=True) + EPS)
    return (x * r * nw) * (1.0 + scale) + shift


def _sigmoid(x):
    return 1.0 / (1.0 + jnp.exp(-x))


def _softplus(x):
    return jnp.maximum(x, 0.0) + jnp.log(1.0 + jnp.exp(-jnp.abs(x)))


def _row_block(t_len):
    return 512 if t_len % 512 == 0 else 256


def ffn_fwd(x, vec, wgu, wd):
    t_len = x.shape[0]
    tm = _row_block(t_len)

    def body(x_ref, vec_ref, wg_ref, wu_ref, wd_ref, x1_ref, h_ref, g_ref, u_ref, f_ref, hs, acc):
        j = pl.program_id(1)

        @pl.when(j == 0)
        def _():
            hb = _normmod(x_ref[...], vec_ref[0:1, :], vec_ref[1:2, :], vec_ref[2:3, :]).astype(BF16)
            hs[...] = hb
            h_ref[...] = hb
            acc[...] = jnp.zeros_like(acc)

        hb = hs[...]
        g = jnp.dot(hb, wg_ref[...], preferred_element_type=F32)
        u = jnp.dot(hb, wu_ref[...], preferred_element_type=F32)
        g_ref[...] = g.astype(BF16)
        u_ref[...] = u.astype(BF16)
        act = g * _sigmoid(g) * u
        acc[...] += jnp.dot(act.astype(BF16), wd_ref[...], preferred_element_type=F32)

        @pl.when(j == 1)
        def _():
            f = acc[...]
            f_ref[...] = f.astype(BF16)
            x1_ref[...] = x_ref[...] + 0.5 * vec_ref[3:4, :] * f

    return pl.pallas_call(
        body, name="ffn_fwd", grid=(t_len // tm, 2),
        in_specs=[
            pl.BlockSpec((tm, D), lambda i, j: (i, 0)),
            pl.BlockSpec((8, D), lambda i, j: (0, 0)),
            pl.BlockSpec((D, FFN_HB), lambda i, j: (0, j)),
            pl.BlockSpec((D, FFN_HB), lambda i, j: (0, j + 2)),
            pl.BlockSpec((FFN_HB, D), lambda i, j: (j, 0)),
        ],
        out_specs=[
            pl.BlockSpec((tm, D), lambda i, j: (i, 0)),
            pl.BlockSpec((tm, D), lambda i, j: (i, 0)),
            pl.BlockSpec((tm, FFN_HB), lambda i, j: (i, j)),
            pl.BlockSpec((tm, FFN_HB), lambda i, j: (i, j)),
            pl.BlockSpec((tm, D), lambda i, j: (i, 0)),
        ],
        out_shape=[
            jax.ShapeDtypeStruct((t_len, D), F32),
            jax.ShapeDtypeStruct((t_len, D), BF16),
            jax.ShapeDtypeStruct((t_len, FFN_H), BF16),
            jax.ShapeDtypeStruct((t_len, FFN_H), BF16),
            jax.ShapeDtypeStruct((t_len, D), BF16),
        ],
        scratch_shapes=[pltpu.VMEM((tm, D), BF16), pltpu.VMEM((tm, D), F32)],
        compiler_params=_params(2),
    )(x, vec, wgu, wgu, wd)


def ffn_bwd_act(dx1, vec, g, u, f, wd):
    t_len = dx1.shape[0]
    tm = _row_block(t_len)

    def body(dx_ref, vec_ref, g_ref, u_ref, f_ref, wd_ref, act_ref, dg_ref, du_ref, dy_ref, pg_ref, dys):
        i, j = pl.program_id(0), pl.program_id(1)

        @pl.when((i == 0) & (j == 0))
        def _():
            pg_ref[...] = jnp.zeros_like(pg_ref)

        @pl.when(j == 0)
        def _():
            dx = dx_ref[...]
            dyb = (0.5 * vec_ref[3:4, :] * dx).astype(BF16)
            dys[...] = dyb
            dy_ref[...] = dyb
            pg_ref[3:4, :] += jnp.sum(0.5 * dx * f_ref[...].astype(F32), axis=0, keepdims=True)

        dact = _dot(dys[...], wd_ref[...], NT)
        gg = g_ref[...].astype(F32)
        uu = u_ref[...].astype(F32)
        sg = _sigmoid(gg)
        sl = gg * sg
        act_ref[...] = (sl * uu).astype(BF16)
        du_ref[...] = (dact * sl).astype(BF16)
        dg_ref[...] = (dact * uu * (sg + sl * (1.0 - sg))).astype(BF16)

    return pl.pallas_call(
        body, name="ffn_bwd_act", grid=(t_len // tm, 2),
        in_specs=[
            pl.BlockSpec((tm, D), lambda i, j: (i, 0)),
            pl.BlockSpec((8, D), lambda i, j: (0, 0)),
            pl.BlockSpec((tm, FFN_HB), lambda i, j: (i, j)),
            pl.BlockSpec((tm, FFN_HB), lambda i, j: (i, j)),
            pl.BlockSpec((tm, D), lambda i, j: (i, 0)),
            pl.BlockSpec((FFN_HB, D), lambda i, j: (j, 0)),
        ],
        out_specs=[
            pl.BlockSpec((tm, FFN_HB), lambda i, j: (i, j)),
            pl.BlockSpec((tm, FFN_HB), lambda i, j: (i, j)),
            pl.BlockSpec((tm, FFN_HB), lambda i, j: (i, j)),
            pl.BlockSpec((tm, D), lambda i, j: (i, 0)),
            pl.BlockSpec((8, D), lambda i, j: (0, 0)),
        ],
        out_shape=[
            jax.ShapeDtypeStruct((t_len, FFN_H), BF16),
            jax.ShapeDtypeStruct((t_len, FFN_H), BF16),
            jax.ShapeDtypeStruct((t_len, FFN_H), BF16),
            jax.ShapeDtypeStruct((t_len, D), BF16),
            jax.ShapeDtypeStruct((8, D), F32),
        ],
        scratch_shapes=[pltpu.VMEM((tm, D), BF16)],
        compiler_params=_params(2),
    )(dx1, vec, g, u, f, wd)


def nt_norm_bwd(dys, w, col_offsets, tk, n_steps, x_in, dres, vec, name):
    t_len = x_in.shape[0]
    tm = _row_block(t_len)
    n_seg = len(dys)

    def body(*refs):
        dy_refs = refs[:n_seg]
        w_refs = refs[n_seg:2 * n_seg]
        x_ref, dres_ref, vec_ref, dx_ref, pg_ref, acc = refs[2 * n_seg:]
        i, j = pl.program_id(0), pl.program_id(1)

        @pl.when((i == 0) & (j == 0))
        def _():
            pg_ref[...] = jnp.zeros_like(pg_ref)

        @pl.when(j == 0)
        def _():
            acc[...] = jnp.zeros_like(acc)

        part = _dot(dy_refs[0][...], w_refs[0][...], NT)
        for s in range(1, n_seg):
            part += _dot(dy_refs[s][...], w_refs[s][...], NT)
        acc[...] += part

        @pl.when(j == n_steps - 1)
        def _():
            _, vjp = jax.vjp(_normmod, x_ref[...], vec_ref[0:1, :], vec_ref[1:2, :], vec_ref[2:3, :])
            dxn, dnw, dsh, dsc = vjp(acc[...])
            dx_ref[...] = dres_ref[...] + dxn
            pg_ref[0:1, :] += dnw
            pg_ref[1:2, :] += dsh
            pg_ref[2:3, :] += dsc

    in_specs = [pl.BlockSpec((tm, tk), lambda i, j: (i, j)) for _ in range(n_seg)]
    in_specs += [pl.BlockSpec((D, tk), functools.partial(lambda i, j, off: (0, off + j), off=off))
                 for off in col_offsets]
    in_specs += [pl.BlockSpec((tm, D), lambda i, j: (i, 0)), pl.BlockSpec((tm, D), lambda i, j: (i, 0)),
                 pl.BlockSpec((8, D), lambda i, j: (0, 0))]
    return pl.pallas_call(
        body, name=name, grid=(t_len // tm, n_steps),
        in_specs=in_specs,
        out_specs=[pl.BlockSpec((tm, D), lambda i, j: (i, 0)), pl.BlockSpec((8, D), lambda i, j: (0, 0))],
        out_shape=[jax.ShapeDtypeStruct((t_len, D), F32), jax.ShapeDtypeStruct((8, D), F32)],
        scratch_shapes=[pltpu.VMEM((tm, D), F32)],
        compiler_params=_params(2),
    )(*dys, *([w] * n_seg), x_in, dres, vec)


def tn_matmul(a_x, b_x, a_c, b_c, tn, name):
    t_len, k_dim = a_x.shape
    n_dim = b_x.shape[1]
    t_ctx = a_c.shape[0]
    tt = 512
    n_t = t_len // tt

    def body(ax_ref, bx_ref, ac_ref, bc_ref, o_ref):
        t = pl.program_id(1)

        @pl.when(t == 0)
        def _():
            o_ref[...] = _dot(ac_ref[...], bc_ref[...], TN)

        o_ref[...] += _dot(ax_ref[...], bx_ref[...], TN)

    return pl.pallas_call(
        body, name=name, grid=(n_dim // tn, n_t),
        in_specs=[
            pl.BlockSpec((tt, k_dim), lambda n, t: (t, 0)),
            pl.BlockSpec((tt, tn), lambda n, t: (t, n)),
            pl.BlockSpec((t_ctx, k_dim), lambda n, t: (0, 0)),
            pl.BlockSpec((t_ctx, tn), lambda n, t: (0, n)),
        ],
        out_specs=pl.BlockSpec((k_dim, tn), lambda n, t: (0, n)),
        out_shape=jax.ShapeDtypeStruct((k_dim, n_dim), F32),
        compiler_params=_params(2),
    )(a_x, b_x, a_c, b_c)


def proj_fwd(x, vec, w_in_p):
    t_len = x.shape[0]
    tm = _row_block(t_len)

    def body(x_ref, vec_ref, w_ref, p_ref, h_ref, hs):
        @pl.when(pl.program_id(1) == 0)
        def _():
            hb = _normmod(x_ref[...], vec_ref[0:1, :], vec_ref[1:2, :], vec_ref[2:3, :]).astype(BF16)
            hs[...] = hb
            h_ref[...] = hb

        p_ref[...] = jnp.dot(hs[...], w_ref[...], preferred_element_type=F32)

    return pl.pallas_call(
        body, name="proj_fwd", grid=(t_len // tm, PW // PROJ_TN),
        in_specs=[
            pl.BlockSpec((tm, D), lambda i, j: (i, 0)),
            pl.BlockSpec((8, D), lambda i, j: (0, 0)),
            pl.BlockSpec((D, PROJ_TN), lambda i, j: (0, j)),
        ],
        out_specs=[pl.BlockSpec((tm, PROJ_TN), lambda i, j: (i, j)), pl.BlockSpec((tm, D), lambda i, j: (i, 0))],
        out_shape=[jax.ShapeDtypeStruct((t_len, PW), F32), jax.ShapeDtypeStruct((t_len, D), BF16)],
        scratch_shapes=[pltpu.VMEM((tm, D), BF16)],
        compiler_params=_params(2),
    )(x, vec, w_in_p)


def _shift_rows(e, s):
    n = e.shape[0]
    return pltpu.roll(e, (-s) % n, 0)


def _swap_halves(t):
    return pltpu.roll(t, DK // 2, 1)


def _halo_specs(tm, t_len, width, col, lead=False):
    per = tm // HALO
    last = t_len // HALO - 1
    if lead:
        return (pl.BlockSpec((2, HALO, width), lambda i: (0, jnp.maximum(i * per - 1, 0), col)),
                pl.BlockSpec((2, HALO, width), lambda i: (0, jnp.minimum((i + 1) * per, last), col)))
    return (pl.BlockSpec((HALO, width), lambda i: (jnp.maximum(i * per - 1, 0), col)),
            pl.BlockSpec((HALO, width), lambda i: (jnp.minimum((i + 1) * per, last), col)))


def _gate_cols(ba, gp_ref):
    lane = lax.broadcasted_iota(jnp.int32, ba.shape, 1)
    z = ba + gp_ref[1:2, :]
    return jnp.where(lane < 8, _sigmoid(ba), gp_ref[0:1, :] * _softplus(z))


def _conv_chunk(win, w_ref, c0, width):
    y = w_ref[0:1, c0:c0 + width] * _shift_rows(win, -2)
    for j in range(1, CONV_K):
        y += w_ref[j:j + 1, c0:c0 + width] * _shift_rows(win, j - 2)
    return y


def feat_fwd(p, cos2, sin2, conv_w8, gparams):
    t_len = p.shape[0]
    tm = _row_block(t_len)
    n_t = t_len // tm
    cw = 512

    def body(rqk_ref, dqk_ref, dqk_p, dqk_n, dvv_ref, dvv_p, dvv_n, dba_ref, cos_ref, sin_ref, cw_ref, gp_ref,
             o_rqk, o_dqkv, o_gbc):
        i = pl.program_id(0)
        cos, sin = cos_ref[...], sin_ref[...]
        for s in range(2 * NH):
            t = rqk_ref[:, s * DK:(s + 1) * DK]
            if s >= NH:
                t = t * (DK ** -0.5)
            o_rqk[:, s * DK:(s + 1) * DK] = t * cos + _swap_halves(t) * sin
        o_gbc[...] = _gate_cols(dba_ref[...], gp_ref)
        first, last = i == 0, i == n_t - 1
        for ci in range(4):
            src, sp, sn = (dqk_ref, dqk_p, dqk_n) if ci < 2 else (dvv_ref, dvv_p, dvv_n)
            c0 = (ci % 2) * cw
            win = jnp.concatenate([
                jnp.where(first, 0.0, sp[:, c0:c0 + cw]), src[:, c0:c0 + cw],
                jnp.where(last, 0.0, sn[:, c0:c0 + cw])], axis=0)
            y = _conv_chunk(win, cw_ref, ci * cw, cw)[HALO:HALO + tm]
            sv = y * _sigmoid(y)
            if ci < 2:
                scale = DK ** -0.5 if ci == 0 else 1.0
                for hh in range(NH):
                    sh = sv[:, hh * DK:(hh + 1) * DK]
                    nrm = lax.rsqrt(jnp.sum(sh * sh, axis=-1, keepdims=True) + EPS)
                    o_dqkv[:, ci * cw + hh * DK:ci * cw + (hh + 1) * DK] = sh * nrm * scale
            else:
                o_dqkv[:, ci * cw:(ci + 1) * cw] = sv

    hq = _halo_specs(tm, t_len, D, C_DQK)
    hv = _halo_specs(tm, t_len, D, C_DVV)
    return pl.pallas_call(
        body, name="feat_fwd", grid=(n_t,),
        in_specs=[
            pl.BlockSpec((tm, D), lambda i: (i, C_RQK)),
            pl.BlockSpec((tm, D), lambda i: (i, C_DQK)), hq[0], hq[1],
            pl.BlockSpec((tm, D), lambda i: (i, C_DVV)), hv[0], hv[1],
            pl.BlockSpec((tm, LANE), lambda i: (i, C_DBA)),
            pl.BlockSpec((tm, LANE), lambda i: (i, 0)),
            pl.BlockSpec((tm, LANE), lambda i: (i, 0)),
            pl.BlockSpec((8, 2 * D), lambda i: (0, 0)),
            pl.BlockSpec((8, LANE), lambda i: (0, 0)),
        ],
        out_specs=[pl.BlockSpec((tm, D), lambda i: (i, 0)), pl.BlockSpec((tm, 2 * D), lambda i: (i, 0)),
                   pl.BlockSpec((tm, LANE), lambda i: (i, 0))],
        out_shape=[jax.ShapeDtypeStruct((t_len, D), F32), jax.ShapeDtypeStruct((t_len, 2 * D), F32),
                   jax.ShapeDtypeStruct((t_len, LANE), F32)],
        compiler_params=_params(1),
    )(p, p, p, p, p, p, p, p, cos2, sin2, conv_w8, gparams)


def feat_bwd(p, cos2, sin2, conv_w8, gparams, d_rqk, d_rv, d_dqk, d_dvv, d_gbc):
    t_len = p.shape[0]
    tm = 256
    n_t = t_len // tm
    cw = 512

    def body(dqk_ref, dqk_p, dqk_n, dvv_ref, dvv_p, dvv_n, dba_ref, cos_ref, sin_ref, cw_ref, gp_ref,
             g_rqk, g_rv, g_dqk, g_dqk_p, g_dqk_n, g_dvv, g_dvv_p, g_dvv_n, g_gbc,
             o_rqk, o_rv, o_dqk, o_dvv, o_dba, o_cw, o_gp):
        i = pl.program_id(0)
        first, last = i == 0, i == n_t - 1

        @pl.when(first)
        def _():
            o_cw[...] = jnp.zeros_like(o_cw)
            o_gp[...] = jnp.zeros_like(o_gp)

        cos, sin = cos_ref[...], sin_ref[...]
        for s in range(2 * NH):
            gsl = g_rqk[0, :, s * DK:(s + 1) * DK] + g_rqk[1, :, s * DK:(s + 1) * DK]
            dt = gsl * cos + _swap_halves(gsl * sin)
            if s >= NH:
                dt = dt * (DK ** -0.5)
            o_rqk[:, s * DK:(s + 1) * DK] = dt.astype(BF16)
        o_rv[...] = (g_rv[0] + g_rv[1]).astype(BF16)

        ba = dba_ref[...]
        gg = g_gbc[0] + g_gbc[1]
        lane = lax.broadcasted_iota(jnp.int32, ba.shape, 1)
        sb = _sigmoid(ba)
        z = ba + gp_ref[1:2, :]
        a_row = gp_ref[0:1, :]
        dz = gg * a_row * _sigmoid(z)
        o_dba[...] = jnp.where(lane < 8, gg * sb * (1.0 - sb), dz).astype(BF16)
        is_g = (lane >= 8) & (lane < 16)
        o_gp[0:1, :] += jnp.sum(jnp.where(is_g, gg * a_row * _softplus(z), 0.0), axis=0, keepdims=True)
        o_gp[1:2, :] += jnp.sum(jnp.where(is_g, dz, 0.0), axis=0, keepdims=True)

        for ci in range(4):
            src, sp, sn = (dqk_ref, dqk_p, dqk_n) if ci < 2 else (dvv_ref, dvv_p, dvv_n)
            c0 = (ci % 2) * cw
            gc0 = ci * cw
            win = jnp.concatenate([
                jnp.where(first, 0.0, sp[:, c0:c0 + cw]), src[:, c0:c0 + cw],
                jnp.where(last, 0.0, sn[:, c0:c0 + cw])], axis=0)
            gs, gsp, gsn = (g_dqk, g_dqk_p, g_dqk_n) if ci < 2 else (g_dvv, g_dvv_p, g_dvv_n)
            gext = jnp.concatenate([
                jnp.where(first, 0.0, gsp[0, :, c0:c0 + cw] + gsp[1, :, c0:c0 + cw]),
                gs[0, :, c0:c0 + cw] + gs[1, :, c0:c0 + cw],
                jnp.where(last, 0.0, gsn[0, :, c0:c0 + cw] + gsn[1, :, c0:c0 + cw])], axis=0)
            y = _conv_chunk(win, cw_ref, gc0, cw)
            sg = _sigmoid(y)
            sv = y * sg
            if ci < 2:
                scale = DK ** -0.5 if ci == 0 else 1.0
                parts = []
                for hh in range(NH):
                    sh = sv[:, hh * DK:(hh + 1) * DK]
                    gh = gext[:, hh * DK:(hh + 1) * DK]
                    nrm = lax.rsqrt(jnp.sum(sh * sh, axis=-1, keepdims=True) + EPS)
                    dot = jnp.sum(gh * sh, axis=-1, keepdims=True)
                    parts.append(scale * nrm * (gh - sh * (nrm * nrm) * dot))
                ds = jnp.concatenate(parts, axis=1)
            else:
                ds = gext
            dy = ds * (sg + sv * (1.0 - sg))
            dpe = cw_ref[0:1, gc0:gc0 + cw] * _shift_rows(dy, 2)
            for j in range(1, CONV_K):
                dpe += cw_ref[j:j + 1, gc0:gc0 + cw] * _shift_rows(dy, 2 - j)
            dst = o_dqk if ci < 2 else o_dvv
            dst[:, c0:c0 + cw] = dpe[HALO:HALO + tm].astype(BF16)
            dyc = dy[HALO:HALO + tm]
            for j in range(CONV_K):
                o_cw[j:j + 1, gc0:gc0 + cw] += jnp.sum(dyc * _shift_rows(win, j - 2)[HALO:HALO + tm], axis=0,
                                                      keepdims=True)

    hq = _halo_specs(tm, t_len, D, C_DQK)
    hv = _halo_specs(tm, t_len, D, C_DVV)
    hg = _halo_specs(tm, t_len, D, 0, lead=True)
    outs = pl.pallas_call(
        body, name="feat_bwd", grid=(n_t,),
        in_specs=[
            pl.BlockSpec((tm, D), lambda i: (i, C_DQK)), hq[0], hq[1],
            pl.BlockSpec((tm, D), lambda i: (i, C_DVV)), hv[0], hv[1],
            pl.BlockSpec((tm, LANE), lambda i: (i, C_DBA)),
            pl.BlockSpec((tm, LANE), lambda i: (i, 0)),
            pl.BlockSpec((tm, LANE), lambda i: (i, 0)),
            pl.BlockSpec((8, 2 * D), lambda i: (0, 0)),
            pl.BlockSpec((8, LANE), lambda i: (0, 0)),
            pl.BlockSpec((2, tm, D), lambda i: (0, i, 0)),
            pl.BlockSpec((2, tm, D), lambda i: (0, i, 0)),
            pl.BlockSpec((2, tm, D), lambda i: (0, i, 0)), hg[0], hg[1],
            pl.BlockSpec((2, tm, D), lambda i: (0, i, 0)), hg[0], hg[1],
            pl.BlockSpec((2, tm, LANE), lambda i: (0, i, 0)),
        ],
        out_specs=[pl.BlockSpec((tm, D), lambda i: (i, 0))] * 4
        + [pl.BlockSpec((tm, LANE), lambda i: (i, 0)),
           pl.BlockSpec((8, 2 * D), lambda i: (0, 0)),
           pl.BlockSpec((8, LANE), lambda i: (0, 0))],
        out_shape=[jax.ShapeDtypeStruct((t_len, D), BF16)] * 4
        + [jax.ShapeDtypeStruct((t_len, LANE), BF16),
           jax.ShapeDtypeStruct((8, 2 * D), F32), jax.ShapeDtypeStruct((8, LANE), F32)],
        compiler_params=_params(1),
    )(p, p, p, p, p, p, p, cos2, sin2, conv_w8, gparams, d_rqk, d_rv, d_dqk, d_dqk, d_dqk, d_dvv, d_dvv, d_dvv,
      d_gbc)
    return outs


def _ret_chunk(q, k, v, s, lg, rev):
    c = q.shape[0]
    ii = lax.broadcasted_iota(jnp.int32, (c, c), 0).astype(F32)
    jj = lax.broadcasted_iota(jnp.int32, (c, c), 1).astype(F32)
    diff = jnp.where(rev, jj - ii, ii - jj)
    mask = diff >= jnp.where(rev, 1.0, 0.0)
    dec =jnp.where(mask, jnp.exp(lg * jnp.where(mask, diff, 0.0)), 0.0)
    idx = lax.broadcasted_iota(jnp.int32, (c, 1), 0).astype(F32)
    idx = jnp.where(rev, c - 1.0 - idx, idx)
    kdec = k * jnp.exp(lg * (c - 1.0 - idx))
    qdec = q * jnp.exp(lg * (idx + 1.0))
    o = mm_nn(mm_nt(q, k) * dec, v) + mm_nn(qdec, s)
    s2 = s * math.exp(lg * c) + mm_tn(kdec, v)
    return o, s2


def _log_gamma(h):
    return math.log1p(-(2.0 ** (-5.0 - h)))


def _dn_chunk(q, k, v, g, beta, s, rev):
    c = q.shape[0]
    ii = lax.broadcasted_iota(jnp.int32, (c, c), 0)
    jj = lax.broadcasted_iota(jnp.int32, (c, c), 1)
    dd = jnp.where(rev, jj - ii, ii - jj)
    incl = dd >= 0
    strict = dd > 0
    gb = jnp.broadcast_to(g, (c, LANE))
    gcb = _dot_hi(incl.astype(F32), gb)
    gc = jnp.max(gcb, axis=1, keepdims=True)
    gcr = _dot_hi(jnp.full((c, LANE), 1.0 / LANE, F32), gcb, NT)
    decay = jnp.where(incl, jnp.exp(jnp.where(incl, gc - gcr, 0.0)), 0.0)
    kb = k * beta
    a = jnp.where(strict, mm_nt(kb, k) * decay, 0.0)
    eye = (ii == jj).astype(F32)
    x = -a
    tinv = eye + x
    pw = x
    for _ in range(5):
        pw = _dot_hi(pw, pw)
        tinv = tinv + _dot_hi(tinv, pw)
    u = mm_nn(tinv, v * beta)
    w = mm_nn(tinv, kb * jnp.exp(gc))
    attn = jnp.where(incl, mm_nt(q, k) * decay, 0.0)
    qg = q * jnp.exp(gc)
    glast = jnp.sum(g, axis=0, keepdims=True)
    kd = k * jnp.exp(glast - gc)
    v_new = u - mm_nn(w, s)
    o = mm_nn(qg, s) + mm_nn(attn, v_new)
    s2 = s * jnp.exp(glast) + mm_tn(kd, v_new)
    return o, s2


def _pick_lane(x, lane_idx):
    lane = lax.broadcasted_iota(jnp.int32, x.shape, 1)
    return jnp.sum(jnp.where(lane == lane_idx, x, 0.0), axis=1, keepdims=True)


def _put_lane(col, lane_idx, width):
    lane = lax.broadcasted_iota(jnp.int32, (col.shape[0], width), 1)
    return jnp.where(lane == lane_idx, col, 0.0)


def _state_spec():
    return pl.BlockSpec((1, NH, DK, DV), lambda d, t: (d, 0, 0, 0))


def mixer_fwd(kind, qk_arr, v_arr, v_col, gbc, s0):
    is_dn = kind == "dn"
    c = DN_C if is_dn else RET_C
    t_len = qk_arr.shape[0]
    n_c = t_len // c

    def chunk_of(d, t):
        return jnp.where(d == 0, t, n_c - 1 - t)

    def body(*refs):
        if is_dn:
            qk_ref, v_ref, gb_ref, s0_ref, o_ref, ss_ref, sf_ref, st = refs
        else:
            qk_ref, v_ref, s0_ref, o_ref, ss_ref, sf_ref, st = refs
        d, t = pl.program_id(0), pl.program_id(1)
        rev = d == 1

        @pl.when(t == 0)
        def _():
            st[...] = s0_ref[0]

        for h in range(NH):
            q = qk_ref[:, h * DK:(h + 1) * DK]
            k = qk_ref[:, NH * DK + h * DK:NH * DK + (h + 1) * DK]
            v = v_ref[:, h * DV:(h + 1) * DV]
            sh = st[h]
            ss_ref[0, 0, h] = sh.astype(BF16)
            if is_dn:
                gbv = gb_ref[...]
                o, s2 = _dn_chunk(q, k, v, _pick_lane(gbv, 8 + 4 * d + h), _pick_lane(gbv, 4 * d + h), sh, rev)
            else:
                o, s2 = _ret_chunk(q, k, v, sh, _log_gamma(h), rev)
            o_ref[0, :, h * DV:(h + 1) * DV] = o
            st[h] = s2

        @pl.when(t == n_c - 1)
        def _():
            sf_ref[0] = st[...]

    in_specs = [pl.BlockSpec((c, D), lambda d, t: (chunk_of(d, t), 0)),
                pl.BlockSpec((c, D), lambda d, t: (chunk_of(d, t), v_col))]
    args = [qk_arr, v_arr]
    if is_dn:
        in_specs.append(pl.BlockSpec((c, LANE), lambda d, t: (chunk_of(d, t), 0)))
        args.append(gbc)
    in_specs.append(_state_spec())
    args.append(s0)
    return pl.pallas_call(
        body, name=kind + "_fwd", grid=(2, n_c),
        in_specs=in_specs,
        out_specs=[pl.BlockSpec((1, c, D), lambda d, t: (d, chunk_of(d, t), 0)),
                   pl.BlockSpec((1, 1, NH, DK, DV), lambda d, t: (d, chunk_of(d, t), 0, 0, 0)),
                   _state_spec()],
        out_shape=[jax.ShapeDtypeStruct((2, t_len, D), F32),
                   jax.ShapeDtypeStruct((2, n_c, NH, DK, DV), BF16),
                   jax.ShapeDtypeStruct((2, NH, DK, DV), F32)],
        scratch_shapes=[pltpu.VMEM((NH, DK, DV), F32)],
        compiler_params=_params(2),
    )(*args)


def mixer_bwd(kind, qk_arr, v_arr, v_col, gbc, ssave, do, ds_fin):
    is_dn = kind == "dn"
    c = DN_C if is_dn else RET_C
    t_len = qk_arr.shape[0]
    n_c = t_len // c

    def chunk_of(d, t):
        return jnp.where(d == 0, n_c - 1 - t, t)

    def body(*refs):
        if is_dn:
            qk_ref, v_ref, gb_ref, ss_ref, do_ref, dsf_ref, dqk_ref, dv_ref, dgb_ref, ds0_ref, dst = refs
        else:
            qk_ref, v_ref, ss_ref, do_ref, dsf_ref, dqk_ref, dv_ref, ds0_ref, dst = refs
        d, t = pl.program_id(0), pl.program_id(1)
        rev = d == 1

        @pl.when(t == 0)
        def _():
            dst[...] = dsf_ref[0]

        dgb = None
        for h in range(NH):
            q = qk_ref[:, h * DK:(h + 1) * DK]
            k = qk_ref[:, NH * DK + h * DK:NH * DK + (h + 1) * DK]
            v = v_ref[:, h * DV:(h + 1) * DV]
            sh = ss_ref[0, 0, h].astype(F32)
            doh = do_ref[:, h * DV:(h + 1) * DV]
            if is_dn:
                gbv = gb_ref[...]
                g_col = _pick_lane(gbv, 8 + 4 * d + h)
                b_col = _pick_lane(gbv, 4 * d + h)
                _, vjp = jax.vjp(functools.partial(_dn_chunk, rev=rev), q, k, v, g_col, b_col, sh)
                dq, dk, dv, dg, db, ds = vjp((doh, dst[h]))
                part = _put_lane(dg, 8 + 4 * d + h, LANE) + _put_lane(db, 4 * d + h, LANE)
                dgb = part if dgb is None else dgb + part
            else:
                _, vjp = jax.vjp(functools.partial(_ret_chunk, lg=_log_gamma(h), rev=rev), q, k, v, sh)
                dq, dk, dv, ds = vjp((doh, dst[h]))
            dqk_ref[0, :, h * DK:(h + 1) * DK] = dq
            dqk_ref[0, :, NH * DK + h * DK:NH * DK + (h + 1) * DK] = dk
            dv_ref[0, :, h * DV:(h + 1) * DV] = dv
            dst[h] = ds
        if is_dn:
            dgb_ref[0] = dgb

        @pl.when(t == n_c - 1)
        def _():
            ds0_ref[0] = dst[...]

    in_specs = [pl.BlockSpec((c, D), lambda d, t: (chunk_of(d, t), 0)),
                pl.BlockSpec((c, D), lambda d, t: (chunk_of(d, t), v_col))]
    args = [qk_arr, v_arr]
    if is_dn:
        in_specs.append(pl.BlockSpec((c, LANE), lambda d, t: (chunk_of(d, t), 0)))
        args.append(gbc)
    in_specs += [pl.BlockSpec((1, 1, NH, DK, DV), lambda d, t: (d, chunk_of(d, t), 0, 0, 0)),
                 pl.BlockSpec((c, D), lambda d, t: (chunk_of(d, t), 0)),
                 _state_spec()]
    args += [ssave, do, ds_fin]
    seq_spec = pl.BlockSpec((1, c, D), lambda d, t: (d, chunk_of(d, t), 0))
    out_specs = [seq_spec, seq_spec]
    out_shape = [jax.ShapeDtypeStruct((2, t_len, D), F32)] * 2
    if is_dn:
        out_specs.append(pl.BlockSpec((1, c, LANE), lambda d, t: (d, chunk_of(d, t), 0)))
        out_shape.append(jax.ShapeDtypeStruct((2, t_len, LANE), F32))
    out_specs.append(_state_spec())
    out_shape.append(jax.ShapeDtypeStruct((2, NH, DK, DV), F32))
    return pl.pallas_call(
        body, name=kind + "_bwd", grid=(2, n_c),
        in_specs=in_specs, out_specs=out_specs, out_shape=out_shape,
        scratch_shapes=[pltpu.VMEM((NH, DK, DV), F32)],
        compiler_params=_params(2),
    )(*args)


def _head_gate(o_ret, o_dn, rg, dz, nw):
    ret = o_ret * lax.rsqrt(jnp.mean(o_ret * o_ret, axis=-1, keepdims=True) + EPS) * (rg * _sigmoid(rg))
    dn = o_dn * lax.rsqrt(jnp.mean(o_dn * o_dn, axis=-1, keepdims=True) + EPS) * nw * (dz * _sigmoid(dz))
    return ret, dn


MIX_TM = 256


def _mix_specs():
    seq = lambda col: pl.BlockSpec((MIX_TM, D), functools.partial(lambda i, col: (i, col), col=col))
    pair = pl.BlockSpec((2, MIX_TM, D), lambda i: (0, i, 0))
    wfull = pl.BlockSpec((D, D), lambda i: (0, 0))
    vec = pl.BlockSpec((8, D), lambda i: (0, 0))
    return seq, pair, wfull, vec


def mixout_fwd(x1, vec, o_ret, o_dn, p, w_r, w_d, w_o):
    t_len = x1.shape[0]

    def body(x_ref, vec_ref, or_ref, od_ref, rg_ref, dz_ref, ga_ref, gb_ref, wr_ref, wd_ref, wo_ref,
             x2_ref, ret_ref, dn_ref, y_ref, yr_ref, yd_ref, z_ref):
        for h in range(NH):
            sl = slice(h * DV, (h + 1) * DV)
            ret, dn = _head_gate(or_ref[0, :, sl] + or_ref[1, :, sl], od_ref[0, :, sl] + od_ref[1, :, sl],
                                 rg_ref[:, sl], dz_ref[:, sl], vec_ref[1:2, sl])
            ret_ref[:, sl] = ret.astype(BF16)
            dn_ref[:, sl] = dn.astype(BF16)
        yr = jnp.dot(ret_ref[...], wr_ref[...], preferred_element_type=F32)
        yd = jnp.dot(dn_ref[...], wd_ref[...], preferred_element_type=F32)
        yr_ref[...] = yr.astype(BF16)
        yd_ref[...] = yd.astype(BF16)
        y = (_sigmoid(ga_ref[...]) * yr + _sigmoid(gb_ref[...]) * yd).astype(BF16)
        y_ref[...] = y
        z = jnp.dot(y, wo_ref[...], preferred_element_type=F32)
        z_ref[...] = z.astype(BF16)
        x2_ref[...] = x_ref[...] + vec_ref[0:1, :] * z

    seq, pair, wfull, vecs = _mix_specs()
    return pl.pallas_call(
        body, name="mixout_fwd", grid=(t_len // MIX_TM,),
        in_specs=[seq(0), vecs, pair, pair, seq(C_RG), seq(C_DZ), seq(C_GA), seq(C_GB), wfull, wfull, wfull],
        out_specs=[seq(0)] * 7,
        out_shape=[jax.ShapeDtypeStruct((t_len, D), F32)] + [jax.ShapeDtypeStruct((t_len, D), BF16)] * 6,
        compiler_params=_params(1),
    )(x1, vec, o_ret, o_dn, p, p, p, p, w_r, w_d, w_o)


def mixout_bwd(dx2, vec, o_ret, o_dn, p, yr, yd, z, w_r, w_d, w_o):
    t_len = dx2.shape[0]

    def body(dx_ref, vec_ref, or_ref, od_ref, rg_ref, dz_ref, ga_ref, gb_ref, yr_ref, yd_ref, z_ref,
             wr_ref, wd_ref, wo_ref,
             dor_ref, dod_ref, drg_ref, ddz_ref, dga_ref, dgb_ref, dyr_ref, dyd_ref, dzz_ref, pg_ref):
        @pl.when(pl.program_id(0) == 0)
        def _():
            pg_ref[...] = jnp.zeros_like(pg_ref)

        dx = dx_ref[...]
        pg_ref[0:1, :] += jnp.sum(dx * z_ref[...].astype(F32), axis=0, keepdims=True)
        dzz = (vec_ref[0:1, :] * dx).astype(BF16)
        dzz_ref[...] = dzz
        dy = _dot(dzz, wo_ref[...], NT)
        sa = _sigmoid(ga_ref[...])
        sb = _sigmoid(gb_ref[...])
        dyr = (dy * sa).astype(BF16)
        dyd = (dy * sb).astype(BF16)
        dyr_ref[...] = dyr
        dyd_ref[...] = dyd
        dga_ref[...] = (dy * yr_ref[...].astype(F32) * sa * (1.0 - sa)).astype(BF16)
        dgb_ref[...] = (dy * yd_ref[...].astype(F32) * sb * (1.0 - sb)).astype(BF16)
        dret = _dot(dyr, wr_ref[...], NT)
        ddn = _dot(dyd, wd_ref[...], NT)
        for h in range(NH):
            sl = slice(h * DV, (h + 1) * DV)
            _, vjp = jax.vjp(_head_gate, or_ref[0, :, sl] + or_ref[1, :, sl], od_ref[0, :, sl] + od_ref[1, :, sl],
                             rg_ref[:, sl], dz_ref[:, sl], vec_ref[1:2, sl])
            d_or, d_od, d_rg, d_dz, d_nw = vjp((dret[:, sl], ddn[:, sl]))
            dor_ref[:, sl] = d_or
            dod_ref[:, sl] = d_od
            drg_ref[:, sl] = d_rg.astype(BF16)
            ddz_ref[:, sl] = d_dz.astype(BF16)
            pg_ref[1:2, sl] += d_nw

    seq, pair, wfull, vecs = _mix_specs()
    return pl.pallas_call(
        body, name="mixout_bwd", grid=(t_len // MIX_TM,),
        in_specs=[seq(0), vecs, pair, pair, seq(C_RG), seq(C_DZ), seq(C_GA), seq(C_GB), seq(0), seq(0), seq(0),
                  wfull, wfull, wfull],
        out_specs=[seq(0)] * 9 + [vecs],
        out_shape=[jax.ShapeDtypeStruct((t_len, D), F32)] * 2 + [jax.ShapeDtypeStruct((t_len, D), BF16)] * 7
        + [jax.ShapeDtypeStruct((8, D), F32)],
        compiler_params=_params(1),
    )(dx2, vec, o_ret, o_dn, p, p, p, p, yr, yd, z, w_r, w_d, w_o)


def _final_loss(x, w, target):
    y = x * lax.rsqrt(jnp.mean(x * x, axis=-1, keepdims=True) + EPS) * w
    err = y - target
    return 0.5 * jnp.sum(jnp.mean(err * err, axis=-1, keepdims=True), axis=0, keepdims=True)


def final_fwd_bwd(x, vec, target):
    t_len = x.shape[0]
    tm = _row_block(t_len)

    def body(x_ref, vec_ref, t_ref, dx_ref, pg_ref, loss_ref):
        @pl.when(pl.program_id(0) == 0)
        def _():
            pg_ref[...] = jnp.zeros_like(pg_ref)
            loss_ref[...] = jnp.zeros_like(loss_ref)

        loss, vjp = jax.vjp(functools.partial(_final_loss, target=t_ref[...]), x_ref[...], vec_ref[0:1, :])
        dx, dw = vjp(jnp.ones((1, 1), F32))
        dx_ref[...] = dx
        pg_ref[0:1, :] += dw
        loss_ref[...] += jnp.broadcast_to(loss, loss_ref.shape)

    return pl.pallas_call(
        body, name="final_fwd_bwd", grid=(t_len // tm,),
        in_specs=[pl.BlockSpec((tm, D), lambda i: (i, 0)), pl.BlockSpec((8, D), lambda i: (0, 0)),
                  pl.BlockSpec((tm, D), lambda i: (i, 0))],
        out_specs=[pl.BlockSpec((tm, D), lambda i: (i, 0)), pl.BlockSpec((8, D), lambda i: (0, 0)),
                   pl.BlockSpec((8, LANE), lambda i: (0, 0))],
        out_shape=[jax.ShapeDtypeStruct((t_len, D), F32), jax.ShapeDtypeStruct((8, D), F32),
                   jax.ShapeDtypeStruct((8, LANE), F32)],
        compiler_params=_params(1),
    )(x, vec, target)


def _vec8(*rows):
    rows = list(rows) + [jnp.zeros((D,), F32)] * (8 - len(rows))
    return jnp.stack(rows)


def _layer_vecs(lw, m):
    nw = lw["norm_w"]
    return dict(ffn1=_vec8(nw[0], m[0], m[1], m[2]), proj=_vec8(nw[1], m[3], m[4]),
                mix=_vec8(m[5], lw["dn_norm_t"]), ffn2=_vec8(nw[2], m[6], m[7], m[8]))


def _rope_tables(t_len, grid_w=64, base=10000.0):
    n_freq = DK // 4
    inv = base ** (-jnp.arange(n_freq, dtype=F32) / n_freq)
    tok = jnp.arange(t_len)
    ang = jnp.concatenate([(tok // grid_w).astype(F32)[:, None] * inv, (tok % grid_w).astype(F32)[:, None] * inv],
                          axis=-1)
    cos, sin = jnp.cos(ang), jnp.sin(ang)
    return jnp.concatenate([cos, cos], axis=-1), jnp.concatenate([-sin, sin], axis=-1)


def _stream_fwd_a(x0, lw, vecs, rope):
    x1, h1, g1, u1, f1 = ffn_fwd(x0, vecs["ffn1"], lw["wgu1"], lw["wd1"])
    p, h2 = proj_fwd(x1, vecs["proj"], lw["w_in"])
    rqk, dqkv, gbc = feat_fwd(p, rope[0], rope[1], lw["conv_w8"], lw["gparams"])
    return dict(x0=x0, h1=h1, g1=g1, u1=u1, f1=f1, x1=x1, h2=h2, p=p, rqk=rqk, dqkv=dqkv, gbc=gbc)


def _stream_mix(sv, s0_ret, s0_dn):
    sv["o_ret"], sv["ss_ret"], sf_ret = mixer_fwd("ret", sv["rqk"], sv["p"], C_RV, None, s0_ret)
    sv["o_dn"], sv["ss_dn"], sf_dn = mixer_fwd("dn", sv["dqkv"], sv["dqkv"], 1, sv["gbc"], s0_dn)
    return sf_ret, sf_dn


def _stream_fwd_b(sv, lw, vecs):
    x2, ret, dn, y, yr, yd, z = mixout_fwd(sv["x1"], vecs["mix"], sv["o_ret"], sv["o_dn"], sv["p"],
                                           lw["w_r"], lw["w_d"], lw["w_o"])
    x3, h3, g3, u3, f3 = ffn_fwd(x2, vecs["ffn2"], lw["wgu2"], lw["wd2"])
    sv.update(ret=ret, dn=dn, y=y, yr=yr, yd=yd, z=z, x2=x2, h3=h3, g3=g3, u3=u3, f3=f3)
    return x3


def layer_fwd(xs, cs, lw, ropes):
    vx, vc = _layer_vecs(lw, lw["mx"]), _layer_vecs(lw, lw["mc"])
    sx = _stream_fwd_a(xs, lw, vx, ropes[0])
    sc = _stream_fwd_a(cs, lw, vc, ropes[1])
    zero = jnp.zeros((2, NH, DK, DV), F32)
    sf_ret, sf_dn = _stream_mix(sc, zero, zero)
    _stream_mix(sx, sf_ret, sf_dn)
    x3 = _stream_fwd_b(sx, lw, vx)
    c3 = _stream_fwd_b(sc, lw, vc)
    return x3, c3, (sx, sc)


def _stream_bwd_a(dx3, sv, lw, vecs):
    act3, dg3, du3, dy3, pg_a = ffn_bwd_act(dx3, vecs["ffn2"], sv["g3"], sv["u3"], sv["f3"], lw["wd2"])
    dx2, pg_b = nt_norm_bwd([dg3, du3], lw["wgu2"], [0, 2], FFN_HB, 2, sv["x2"], dx3, vecs["ffn2"], "ffn_bwd_in")
    (dor, dod, drg, ddz, dga, dgb, dyr, dyd, dzz, pg_m) = mixout_bwd(
        dx2, vecs["mix"], sv["o_ret"], sv["o_dn"], sv["p"], sv["yr"], sv["yd"], sv["z"], lw["w_r"], lw["w_d"], lw["w_o"])
    return dict(act3=act3, dg3=dg3, du3=du3, dy3=dy3, pg_a2=pg_a, pg_b2=pg_b, dx2=dx2, dor=dor, dod=dod,
                drg=drg, ddz=ddz, dga=dga, dgb=dgb, dyr=dyr, dyd=dyd, dzz=dzz, pg_m=pg_m)


def _stream_bwd_mix(bw, sv, dsf_ret, dsf_dn):
    bw["dqk_r"], bw["drv"], ds0_ret = mixer_bwd("ret", sv["rqk"], sv["p"], C_RV, None, sv["ss_ret"], bw["dor"], dsf_ret)
    bw["dqk_d"], bw["dvv_d"], bw["dgbc"], ds0_dn = mixer_bwd("dn", sv["dqkv"], sv["dqkv"], 1, sv["gbc"], sv["ss_dn"],
                                                             bw["dod"], dsf_dn)
    return ds0_ret, ds0_dn


def _stream_bwd_b(bw, sv, lw, vecs, rope):
    dp_rqk, dp_rv, dp_dqk, dp_dvv, dp_dba, bw["dcw"], bw["dgp"] = feat_bwd(
        sv["p"], rope[0], rope[1], lw["conv_w8"], lw["gparams"], bw["dqk_r"], bw["drv"], bw["dqk_d"], bw["dvv_d"],
        bw["dgbc"])
    bw["dp"] = jnp.concatenate([dp_rqk, dp_rv, bw["drg"], dp_dqk, dp_dvv, bw["ddz"], bw["dga"], bw["dgb"], dp_dba],
                               axis=1)
    dx1, bw["pg_p"] = nt_norm_bwd([bw["dp"]], lw["w_in"], [0], PROJ_TN, PW // PROJ_TN, sv["x1"], bw["dx2"],
                                  vecs["proj"], "proj_bwd_in")
    bw["act1"], bw["dg1"], bw["du1"], bw["dy1"], bw["pg_a1"] = ffn_bwd_act(dx1, vecs["ffn1"], sv["g1"], sv["u1"],
                                                                          sv["f1"], lw["wd1"])
    dx0, bw["pg_b1"] = nt_norm_bwd([bw["dg1"], bw["du1"]], lw["wgu1"], [0, 2], FFN_HB, 2, sv["x0"], dx1,
                                   vecs["ffn1"], "ffn_bwd_in")
    return dx0


def _stream_pgrads(bw):
    a1, b1, pp, pm, a2, b2 = bw["pg_a1"], bw["pg_b1"], bw["pg_p"], bw["pg_m"], bw["pg_a2"], bw["pg_b2"]
    dmod = jnp.stack([b1[1], b1[2], a1[3], pp[1], pp[2], pm[0], b2[1], b2[2], a2[3]])
    dnorm = jnp.stack([b1[0], pp[0], b2[0]])
    ddnw = pm[1].reshape(NH, DV).sum(axis=0)
    return dmod, dnorm, ddnw, bw["dcw"][:CONV_K], bw["dgp"][0, 8:16], bw["dgp"][1, 8:16]


def layer_bwd(dx3, dc3, lw, saved, ropes):
    sx, sc = saved
    vx, vc = _layer_vecs(lw, lw["mx"]), _layer_vecs(lw, lw["mc"])
    bx = _stream_bwd_a(dx3, sx, lw, vx)
    bc = _stream_bwd_a(dc3, sc, lw, vc)
    zero = jnp.zeros((2, NH, DK, DV), F32)
    ds0_ret, ds0_dn = _stream_bwd_mix(bx, sx, zero, zero)
    _stream_bwd_mix(bc, sc, ds0_ret, ds0_dn)
    dx0 = _stream_bwd_b(bx, sx, lw, vx, ropes[0])
    dc0 = _stream_bwd_b(bc, sc, lw, vc, ropes[1])

    def wgrad(a, b, tn, name):
        ax = sx[a] if a in sx else bx[a]
        ac = sc[a] if a in sc else bc[a]
        return tn_matmul(ax, bx[b], ac, bc[b], tn, name)

    gw = dict(
        wg1=wgrad("h1", "dg1", FFN_HB, "dw_ffn_gu"), wu1=wgrad("h1", "du1", FFN_HB, "dw_ffn_gu"),
        wd1=wgrad("act1", "dy1", 512, "dw_ffn_d"),
        w_in=wgrad("h2", "dp", PROJ_TN, "dw_in"),
        w_r=wgrad("ret", "dyr", 512, "dw_sq"), w_d=wgrad("dn", "dyd", 512, "dw_sq"), w_o=wgrad("y", "dzz", 512, "dw_sq"),
        wg2=wgrad("h3", "dg3", FFN_HB, "dw_ffn_gu"), wu2=wgrad("h3", "du3", FFN_HB, "dw_ffn_gu"),
        wd2=wgrad("act3", "dy3", 512, "dw_ffn_d"),
    )
    px, pc = _stream_pgrads(bx), _stream_pgrads(bc)
    small = dict(dmx=px[0], dmc=pc[0], norm_w=px[1] + pc[1], dn_norm_w=px[2] + pc[2], conv_w=px[3] + pc[3],
                 a_log=px[4] + pc[4], dt_bias=px[5] + pc[5])
    return dx0, dc0, gw, small


def local_step(x, ctx, target, final_norm_w, lws):
    t_len, t_ctx = x.shape[0], ctx.shape[0]
    ropes = (_rope_tables(t_len), (jnp.ones((t_ctx, LANE), F32), jnp.zeros((t_ctx, LANE), F32)))

    def fwd_body(carry, lw):
        x3, c3, saved = layer_fwd(carry[0], carry[1], lw, ropes)
        return (x3, c3), saved

    (xf, _), saved = lax.scan(fwd_body, (x, ctx), lws)
    dx, pg_f, loss = final_fwd_bwd(xf, _vec8(final_norm_w), target)

    def bwd_body(carry, inp):
        dx0, dc0, gw, small = layer_bwd(carry[0], carry[1], inp[0], inp[1], ropes)
        return (dx0, dc0), (gw, small)

    (gx, _), (gws, smalls) = lax.scan(bwd_body, (dx, jnp.zeros_like(ctx)), (lws, saved), reverse=True)
    return loss[0, 0], gx, gws, smalls, pg_f[0]


ADA_ROWS = 16
ADA_COLS = N_MOD * D // N_DEV


def ada_fwd(cc, ada_w, ada_b_cols):
    def body(cc_ref, w_ref, b_ref, o_ref):
        cv = cc_ref[...]
        o_ref[0] = _dot_hi(cv * _sigmoid(cv), w_ref[0]) + b_ref[0]

    return pl.pallas_call(
        body, name="ada_fwd", grid=(DEPTH,),
        in_specs=[pl.BlockSpec((ADA_ROWS, D), lambda l: (0, 0)), pl.BlockSpec((1, D, ADA_COLS), lambda l: (l, 0, 0)),
                  pl.BlockSpec((1, 1, ADA_COLS), lambda l: (l, 0, 0))],
        out_specs=pl.BlockSpec((1, ADA_ROWS, ADA_COLS), lambda l: (l, 0, 0)),
        out_shape=jax.ShapeDtypeStruct((DEPTH, ADA_ROWS, ADA_COLS), F32),
        compiler_params=_params(1),
    )(cc, ada_w, ada_b_cols)


def ada_bwd(cc, dmod, ada_w):
    def body(cc_ref, d_ref, w_ref, gw_ref, ds_ref):
        cv = cc_ref[...]
        gw_ref[0] = _dot_hi(cv * _sigmoid(cv), d_ref[0], TN)
        ds_ref[0] = _dot_hi(d_ref[0], w_ref[0], NT)

    return pl.pallas_call(
        body, name="ada_bwd", grid=(DEPTH,),
        in_specs=[pl.BlockSpec((ADA_ROWS, D), lambda l: (0, 0)),
                  pl.BlockSpec((1, ADA_ROWS, ADA_COLS), lambda l: (l, 0, 0)),
                  pl.BlockSpec((1, D, ADA_COLS), lambda l: (l, 0, 0))],
        out_specs=[pl.BlockSpec((1, D, ADA_COLS), lambda l: (l, 0, 0)), pl.BlockSpec((1, ADA_ROWS, D), lambda l: (l, 0, 0))],
        out_shape=[jax.ShapeDtypeStruct((DEPTH, D, ADA_COLS), F32), jax.ShapeDtypeStruct((DEPTH, ADA_ROWS, D), F32)],
        compiler_params=_params(1),
    )(cc, dmod, ada_w)


def c_ctx_grad(parts, c_ctx8):
    def body(p_ref, c_ref, o_ref):
        acc = p_ref[0]
        for j in range(1, N_DEV):
            acc = acc + p_ref[j]
        cv = c_ref[...]
        sg = _sigmoid(cv)
        o_ref[...] = acc * (sg + cv * sg * (1.0 - sg))

    return pl.pallas_call(body, name="c_ctx_grad", out_shape=jax.ShapeDtypeStruct((8, LANE), F32))(parts, c_ctx8)


def sum_slots(parts):
    n_slot, rows, _ = parts.shape
    tr = 8
    for cand in (1024, 512, 256, 128, 64, 32, 16, 8):
        if rows % cand == 0:
            tr = cand
            break

    def body(p_ref, o_ref):
        acc = p_ref[0]
        for j in range(1, n_slot):
            acc = acc + p_ref[j]
        o_ref[...] = acc

    return pl.pallas_call(
        body, name="sum_slots", grid=(rows // tr,),
        in_specs=[pl.BlockSpec((n_slot, tr, LANE), lambda i: (0, i, 0))],
        out_specs=pl.BlockSpec((tr, LANE), lambda i: (i, 0)),
        out_shape=jax.ShapeDtypeStruct((rows, LANE), F32),
        compiler_params=_params(1),
    )(parts)


ADAM_ROWS = 1024


def adamw(parts, w, m, v, name):
    n_slot, rows, _ = parts.shape

    def body(p_ref, w_ref, m_ref, v_ref, g_ref, d_ref, m2_ref, v2_ref):
        g = p_ref[0].astype(F32)
        for j in range(1, n_slot):
            g = g + p_ref[j].astype(F32)
        m2 = ADAM_B1 * m_ref[...] + (1.0 - ADAM_B1) * g
        v2 = ADAM_B2 * v_ref[...] + (1.0 - ADAM_B2) * (g * g)
        m_hat = m2 / (1.0 - ADAM_B1 ** ADAM_STEP)
        v_hat = v2 / (1.0 - ADAM_B2 ** ADAM_STEP)
        g_ref[...] = g
        m2_ref[...] = m2
        v2_ref[...] = v2
        d_ref[...] = -ADAM_LR * (m_hat / (jnp.sqrt(v_hat) + ADAM_EPS) + ADAM_WD * w_ref[...])

    flat = pl.BlockSpec((ADAM_ROWS, LANE), lambda i: (i, 0))
    return pl.pallas_call(
        body, name=name, grid=(rows // ADAM_ROWS,),
        in_specs=[pl.BlockSpec((n_slot, ADAM_ROWS, LANE), lambda i: (0, i, 0)), flat, flat, flat],
        out_specs=[flat] * 4,
        out_shape=[jax.ShapeDtypeStruct((rows, LANE), F32)] * 4,
        compiler_params=_params(1),
    )(parts, w, m, v)


MESH = pl.DeviceIdType.MESH


def _my_index():
    return 4 * lax.axis_index("x") + 2 * lax.axis_index("y") + lax.axis_index("c")


def all_gather(shard, name):
    rows, lanes = shard.shape

    def body(x_ref, out_ref, send_sems, recv_sems, local_sem):
        x, y, c = lax.axis_index("x"), lax.axis_index("y"), lax.axis_index("c")
        me, sibling = (x, y, c), (x, y, 1 - c)
        chips = [(1 - x, y), (x, 1 - y), (1 - x, 1 - y)]

        def slot(px, py, pc):
            return out_ref.at[4 * px + 2 * py + pc]

        def copy(k, block, to, src=None):
            return pltpu.make_async_remote_copy(
                src_ref=slot(*block) if src is None else src, dst_ref=slot(*block),
                send_sem=send_sems.at[k], recv_sem=recv_sems.at[k], device_id=to, device_id_type=MESH)

        mine = pltpu.make_async_copy(x_ref, slot(*me), local_sem)
        mine.start()
        first = [copy(0, me, sibling, src=x_ref)]
        first += [copy(1 + j, me, (*chip, c), src=x_ref) for j, chip in enumerate(chips)]
        for cp in first:
            cp.start()
        passed = [copy(4 + j, (*chip, c), sibling) for j, chip in enumerate(chips)]
        for j, chip in enumerate(chips):
            copy(1 + j, (*chip, c), me).wait_recv()
            passed[j].start()
        copy(0, sibling, me).wait_recv()
        for j, chip in enumerate(chips):
            copy(4 + j, (*chip, 1 - c), me).wait_recv()
        for cp in first + passed:
            cp.wait_send()
        mine.wait()

    return pl.pallas_call(
        body, name=name,
        in_specs=[pl.BlockSpec(memory_space=pl.ANY)],
        out_specs=pl.BlockSpec(memory_space=pl.ANY),
        out_shape=jax.ShapeDtypeStruct((N_DEV, rows, lanes), shard.dtype),
        scratch_shapes=[pltpu.SemaphoreType.DMA((7,)), pltpu.SemaphoreType.DMA((7,)), pltpu.SemaphoreType.DMA],
    )(shard)


def all_to_all(parts, name):
    n_slot, rows, lanes = parts.shape

    def body(p_ref, q_ref, send_sems, recv_sems, local_sem):
        x, y, c = lax.axis_index("x"), lax.axis_index("y"), lax.axis_index("c")
        me = 4 * x + 2 * y + c
        mine = pltpu.make_async_copy(p_ref.at[me], q_ref.at[me], local_sem)
        mine.start()
        copies = []
        for r in range(1, N_DEV):
            px, py, pc = x ^ (r >> 2), y ^ ((r >> 1) & 1), c ^ (r & 1)
            peer = 4 * px + 2 * py + pc
            cp = pltpu.make_async_remote_copy(
                src_ref=p_ref.at[peer], dst_ref=q_ref.at[me], send_sem=send_sems.at[r - 1], recv_sem=recv_sems.at[r - 1],
                device_id=(px, py, pc), device_id_type=MESH)
            cp.start()
            copies.append((cp, peer, r))
        for cp, peer, r in copies:
            pltpu.make_async_remote_copy(
                src_ref=p_ref.at[peer], dst_ref=q_ref.at[peer], send_sem=send_sems.at[r - 1],
                recv_sem=recv_sems.at[r - 1], device_id=(x, y, c), device_id_type=MESH).wait_recv()
        for cp, _, _ in copies:
            cp.wait_send()
        mine.wait()

    return pl.pallas_call(
        body, name=name,
        in_specs=[pl.BlockSpec(memory_space=pl.ANY)],
        out_specs=pl.BlockSpec(memory_space=pl.ANY),
        out_shape=jax.ShapeDtypeStruct((n_slot, rows, lanes), parts.dtype),
        scratch_shapes=[pltpu.SemaphoreType.DMA((7,)), pltpu.SemaphoreType.DMA((7,)), pltpu.SemaphoreType.DMA],
    )(parts)


WEIGHT_ORDER = ("c_ctx", "ada_w", "ada_b", "norm_w", "ffn1_wgu", "ffn1_wd", "w_in", "dn_conv_w", "dn_a_log",
                "dn_dt_bias", "dn_norm_w", "w_ret_out", "w_dn_out", "w_o", "ffn2_wgu", "ffn2_wd", "final_norm_w")
BIG = (("ffn1_wgu", "col"), ("ffn1_wd", "row"), ("w_in", "col"), ("w_ret_out", "row"), ("w_dn_out", "row"),
       ("w_o", "row"), ("ffn2_wgu", "col"), ("ffn2_wd", "row"))
LOCAL = ("ada_w", "c_ctx", "ada_b", "norm_w", "dn_conv_w", "dn_a_log", "dn_dt_bias", "dn_norm_w", "final_norm_w")


def _pack(arrs, row_mult, lead=None):
    if lead is None:
        flat = jnp.concatenate([a.reshape(-1) for a in arrs])
        n = flat.shape[0]
    else:
        flat = jnp.concatenate([a.reshape(lead, -1) for a in arrs], axis=1)
        n = flat.shape[1]
    unit = LANE * row_mult
    total = -(-n // unit) * unit
    if lead is None:
        return jnp.pad(flat, (0, total - n)).reshape(total // LANE, LANE)
    return jnp.pad(flat, ((0, 0), (0, total - n))).reshape(lead, total // LANE, LANE)


def _unpack(packed, shapes, lead=None):
    flat = packed.reshape(-1) if lead is None else packed.reshape(lead, -1)
    out, off = [], 0
    for shp in shapes:
        n = math.prod(shp)
        if lead is None:
            out.append(flat[off:off + n].reshape(shp))
        else:
            out.append(flat[:, off:off + n].reshape((lead,) + tuple(shp)))
        off += n
    return out


def _join_shards(g, kind):
    if kind == "col":
        return g.transpose(1, 2, 0, 3).reshape(g.shape[1], g.shape[2], N_DEV * g.shape[3])
    return g.transpose(1, 0, 2, 3).reshape(g.shape[1], N_DEV * g.shape[2], g.shape[3])


def _split_shards(full, kind):
    l, a, b = full.shape
    if kind == "col":
        return full.reshape(l, a, N_DEV, b // N_DEV).transpose(2, 0, 1, 3)
    return full.reshape(l, N_DEV, a // N_DEV, b).transpose(1, 0, 2, 3)


def _pad_w_in(w):
    return jnp.concatenate([w[..., :6144], w[..., 6160:8208], w[..., 6144:6160],
                            jnp.zeros(w.shape[:-1] + (PW - 8208,), w.dtype)], axis=-1)


def _unpad_w_in(g):
    return jnp.concatenate([g[..., :6144], g[..., 8192:8208], g[..., 6144:8192]], axis=-1)


def kernel(x, c, ctx, c_ctx, ada_w, ada_b, norm_w, ffn1_wgu, ffn1_wd, w_in, dn_conv_w, dn_a_log, dn_dt_bias, dn_norm_w, w_ret_out, w_dn_out, w_o, ffn2_wgu, ffn2_wd, final_norm_w, loss_target, m_c_ctx, m_ada_w, m_ada_b, m_norm_w, m_ffn1_wgu, m_ffn1_wd, m_w_in, m_dn_conv_w, m_dn_a_log, m_dn_dt_bias, m_dn_norm_w, m_w_ret_out, m_w_dn_out, m_w_o, m_ffn2_wgu, m_ffn2_wd, m_final_norm_w, v_c_ctx, v_ada_w, v_ada_b, v_norm_w, v_ffn1_wgu, v_ffn1_wd, v_w_in, v_dn_conv_w, v_dn_a_log, v_dn_dt_bias, v_dn_norm_w, v_w_ret_out, v_w_dn_out, v_w_o, v_ffn2_wgu, v_ffn2_wd, v_final_norm_w):
    w = dict(c_ctx=c_ctx, ada_w=ada_w, ada_b=ada_b, norm_w=norm_w, ffn1_wgu=ffn1_wgu, ffn1_wd=ffn1_wd, w_in=w_in, dn_conv_w=dn_conv_w, dn_a_log=dn_a_log, dn_dt_bias=dn_dt_bias, dn_norm_w=dn_norm_w, w_ret_out=w_ret_out, w_dn_out=w_dn_out, w_o=w_o, ffn2_wgu=ffn2_wgu, ffn2_wd=ffn2_wd, final_norm_w=final_norm_w)
    m = dict(c_ctx=m_c_ctx, ada_w=m_ada_w, ada_b=m_ada_b, norm_w=m_norm_w, ffn1_wgu=m_ffn1_wgu, ffn1_wd=m_ffn1_wd, w_in=m_w_in, dn_conv_w=m_dn_conv_w, dn_a_log=m_dn_a_log, dn_dt_bias=m_dn_dt_bias, dn_norm_w=m_dn_norm_w, w_ret_out=m_w_ret_out, w_dn_out=m_w_dn_out, w_o=m_w_o, ffn2_wgu=m_ffn2_wgu, ffn2_wd=m_ffn2_wd, final_norm_w=m_final_norm_w)
    v = dict(c_ctx=v_c_ctx, ada_w=v_ada_w, ada_b=v_ada_b, norm_w=v_norm_w, ffn1_wgu=v_ffn1_wgu, ffn1_wd=v_ffn1_wd, w_in=v_w_in, dn_conv_w=v_dn_conv_w, dn_a_log=v_dn_a_log, dn_dt_bias=v_dn_dt_bias, dn_norm_w=v_dn_norm_w, w_ret_out=v_w_ret_out, w_dn_out=v_w_dn_out, w_o=v_w_o, ffn2_wgu=v_ffn2_wgu, ffn2_wd=v_ffn2_wd, final_norm_w=v_final_norm_w)
    me = _my_index()
    big_names = [n for n, _ in BIG]
    big_shapes = [w[n].shape for n in big_names]

    w_big = _pack([w[n] for n in big_names], ADAM_ROWS)
    g_big = all_gather(w_big.astype(BF16), "ag_weights")
    full = {n: _join_shards(s, kind) for (n, kind), s in zip(BIG, _unpack(g_big, big_shapes, lead=N_DEV))}
    small_shapes = [norm_w.shape, dn_conv_w.shape, c.shape]
    g_small = all_gather(_pack([norm_w, dn_conv_w, c], 8), "ag_small")
    norm_s, conv_s, c_all = _unpack(g_small, small_shapes, lead=N_DEV)
    norm_full, conv_full = _join_shards(norm_s, "col"), _join_shards(conv_s, "col")

    cc = jnp.concatenate([c_all.reshape(N_DEV, D), c_ctx[None], jnp.zeros((ADA_ROWS - N_DEV - 1, D), F32)])
    ada_b_cols = lax.dynamic_slice_in_dim(ada_b, me * ADA_COLS, ADA_COLS, axis=1)[:, None, :]
    mods_part = ada_fwd(cc, ada_w, ada_b_cols)
    g_mods = all_gather(mods_part.reshape(-1, LANE), "ag_mods").reshape(N_DEV, DEPTH, ADA_ROWS, ADA_COLS)
    mods_all = g_mods.transpose(1, 2, 0, 3).reshape(DEPTH, ADA_ROWS, N_MOD * D)
    mx = lax.dynamic_index_in_dim(mods_all, me, axis=1, keepdims=False).reshape(DEPTH, N_MOD, D)
    mc = mods_all[:, N_DEV].reshape(DEPTH, N_MOD, D)

    gparams = jnp.pad(jnp.stack([-jnp.exp(dn_a_log).reshape(DEPTH, 8), dn_dt_bias.reshape(DEPTH, 8)], axis=1),
                      ((0, 0), (0, 6), (8, LANE - 16)))
    lws = dict(
        wgu1=full["ffn1_wgu"], wd1=full["ffn1_wd"], w_in=_pad_w_in(full["w_in"]), w_r=full["w_ret_out"],
        w_d=full["w_dn_out"], w_o=full["w_o"], wgu2=full["ffn2_wgu"], wd2=full["ffn2_wd"], norm_w=norm_full,
        conv_w8=jnp.pad(conv_full, ((0, 0), (0, 8 - CONV_K), (0, 0))), gparams=gparams,
        dn_norm_t=jnp.tile(dn_norm_w, (1, NH)), mx=mx, mc=mc)
    loss_l, gx, gws, smalls, d_fnw = local_step(x[0], ctx[0], loss_target[0], final_norm_w, lws)

    sm_list = [smalls["dmx"], smalls["dmc"], smalls["norm_w"], smalls["conv_w"], smalls["a_log"], smalls["dt_bias"],
               smalls["dn_norm_w"], d_fnw, loss_l.reshape(1)]
    sm_shapes = [a.shape for a in sm_list]
    g_sm = all_gather(_pack(sm_list, 8), "ag_small_grads")
    (dmx_sum, dmc_sum, g_norm, g_conv, g_alog, g_dtb, g_dnw, g_fnw, loss) = _unpack(sum_slots(g_sm), sm_shapes)
    dmx_all = _unpack(g_sm, sm_shapes[:1], lead=N_DEV)[0].reshape(N_DEV, DEPTH, N_MOD * D)
    dmc_sum = dmc_sum.reshape(DEPTH, N_MOD * D)
    dmx_sum = dmx_sum.reshape(DEPTH, N_MOD * D)
    dmod = jnp.concatenate([
        lax.dynamic_slice_in_dim(dmx_all, me * ADA_COLS, ADA_COLS, axis=2).transpose(1, 0, 2),
        lax.dynamic_slice_in_dim(dmc_sum, me * ADA_COLS, ADA_COLS, axis=1)[:, None, :],
        jnp.zeros((DEPTH, ADA_ROWS - N_DEV - 1, ADA_COLS), F32)], axis=1)
    g_ada_w, d_sil = ada_bwd(cc, dmod, ada_w)
    g_cc = all_gather(d_sil[:, N_DEV].sum(axis=0).reshape(8, LANE), "ag_c_ctx")
    grads = dict(
        ada_w=g_ada_w, c_ctx=c_ctx_grad(g_cc, c_ctx.reshape(8, LANE)).reshape(D), ada_b=dmx_sum + dmc_sum,
        norm_w=lax.dynamic_slice_in_dim(g_norm, me * (D // N_DEV), D // N_DEV, axis=2),
        dn_conv_w=lax.dynamic_slice_in_dim(g_conv, me * (2 * D // N_DEV), 2 * D // N_DEV, axis=2),
        dn_a_log=g_alog.reshape(dn_a_log.shape), dn_dt_bias=g_dtb.reshape(dn_dt_bias.shape), dn_norm_w=g_dnw,
        final_norm_w=g_fnw)

    full_g = dict(
        ffn1_wgu=jnp.concatenate([gws["wg1"], gws["wu1"]], axis=-1), ffn1_wd=gws["wd1"], w_in=_unpad_w_in(gws["w_in"]),
        w_ret_out=gws["w_r"], w_dn_out=gws["w_d"], w_o=gws["w_o"],
        ffn2_wgu=jnp.concatenate([gws["wg2"], gws["wu2"]], axis=-1), ffn2_wd=gws["wd2"])
    parts = _pack([_split_shards(full_g[n], kind) for n, kind in BIG], ADAM_ROWS, lead=N_DEV)
    recv = all_to_all(parts, "a2a_grads")
    res_big = adamw(recv, w_big, _pack([m[n] for n in big_names], ADAM_ROWS),
                    _pack([v[n] for n in big_names], ADAM_ROWS), "adamw_big")
    res_loc = adamw(_pack([grads[n] for n in LOCAL], ADAM_ROWS)[None], _pack([w[n] for n in LOCAL], ADAM_ROWS),
                    _pack([m[n] for n in LOCAL], ADAM_ROWS), _pack([v[n] for n in LOCAL], ADAM_ROWS), "adamw_local")
    out = []
    loc_shapes = [w[n].shape for n in LOCAL]
    for rb, rl in zip(res_big, res_loc):
        by_name = dict(zip(big_names, _unpack(rb, big_shapes)))
        by_name.update(zip(LOCAL, _unpack(rl, loc_shapes)))
        out.append([by_name[n] for n in WEIGHT_ORDER])
    return (loss.reshape(()), gx[None], *out[0], *out[1], *out[2], *out[3])
```

```python
import functools
import math

import jax
import jax.numpy as jnp
from jax import lax
from jax.experimental import pallas as pl
from jax.experimental.pallas import tpu as pltpu

F32 = jnp.float32
BF16 = jnp.bfloat16

D = 1024
NH = 4
DK = 128
DV = 256
RET_C = 128
DN_C = 64
FFN_H = 2816
FFN_HB = 1408
N_MOD = 9
DEPTH = 4
N_DEV = 8
EPS = 1e-6
CONV_K = 5
HALO = 8

PW = 8320
C_RQK, C_RV, C_RG, C_DQK, C_DVV, C_DZ, C_GA, C_GB = 0, 1, 2, 3, 4, 5, 6, 7
C_DBA = 64
PROJ_TN = 1664

LANE = 128
VMEM_LIMIT = 56 * 1024 * 1024

ADAM_LR, ADAM_B1, ADAM_B2, ADAM_EPS, ADAM_WD, ADAM_STEP = 0.001, 0.9, 0.999, 1e-08, 0.01, 10

NN = ((1,), (0,))
NT = ((1,), (1,))
TN = ((0,), (0,))
HI = lax.Precision.HIGHEST


def _params(n_grid):
    return pltpu.CompilerParams(dimension_semantics=("arbitrary",) * n_grid, vmem_limit_bytes=VMEM_LIMIT)


def _dot(a, b, dims):
    return lax.dot_general(a.astype(BF16), b.astype(BF16), (dims, ((), ())), preferred_element_type=F32)


def _dot_hi(a, b, dims=NN):
    return lax.dot_general(a, b, (dims, ((), ())), precision=HI, preferred_element_type=F32)


@jax.custom_vjp
def mm_nn(a, b):
    return _dot(a, b, NN)


mm_nn.defvjp(lambda a, b: (_dot(a, b, NN), (a, b)),
             lambda r, g: (_dot(g, r[1], NT), _dot(r[0], g, TN)))


@jax.custom_vjp
def mm_nt(a, b):
    return _dot(a, b, NT)


mm_nt.defvjp(lambda a, b: (_dot(a, b, NT), (a, b)),
             lambda r, g: (_dot(g, r[1], NN), _dot(g, r[0], TN)))


@jax.custom_vjp
def mm_tn(a, b):
    return _dot(a, b, TN)


mm_tn.defvjp(lambda a, b: (_dot(a, b, TN), (a, b)),
             lambda r, g: (_dot(r[1], g, NT), _dot(r[0], g, NN)))


def _normmod(x, nw, shift, scale):
    r = lax.rsqrt(jnp.mean(x * x, axis=-1, keepdims=True) + EPS)
    return (x * r * nw) * (1.0 + scale) + shift


def _sigmoid(x):
    return 1.0 / (1.0 + jnp.exp(-x))


def _softplus(x):
    return jnp.maximum(x, 0.0) + jnp.log(1.0 + jnp.exp(-jnp.abs(x)))


def _row_block(t_len):
    return 512 if t_len % 512 == 0 else 256


def ffn_fwd(x, vec, wgu, wd):
    t_len = x.shape[0]
    tm = _row_block(t_len)

    def body(x_ref, vec_ref, wg_ref, wu_ref, wd_ref, x1_ref, h_ref, g_ref, u_ref, f_ref, hs, acc):
        j = pl.program_id(1)

        @pl.when(j == 0)
        def _():
            hb = _normmod(x_ref[...], vec_ref[0:1, :], vec_ref[1:2, :], vec_ref[2:3, :]).astype(BF16)
            hs[...] = hb
            h_ref[...] = hb
            acc[...] = jnp.zeros_like(acc)

        hb = hs[...]
        g = jnp.dot(hb, wg_ref[...], preferred_element_type=F32)
        u = jnp.dot(hb, wu_ref[...], preferred_element_type=F32)
        g_ref[...] = g.astype(BF16)
        u_ref[...] = u.astype(BF16)
        act = g * _sigmoid(g) * u
        acc[...] += jnp.dot(act.astype(BF16), wd_ref[...], preferred_element_type=F32)

        @pl.when(j == 1)
        def _():
            f = acc[...]
            f_ref[...] = f.astype(BF16)
            x1_ref[...] = x_ref[...] + 0.5 * vec_ref[3:4, :] * f

    return pl.pallas_call(
        body, name="ffn_fwd", grid=(t_len // tm, 2),
        in_specs=[
            pl.BlockSpec((tm, D), lambda i, j: (i, 0)),
            pl.BlockSpec((8, D), lambda i, j: (0, 0)),
            pl.BlockSpec((D, FFN_HB), lambda i, j: (0, j)),
            pl.BlockSpec((D, FFN_HB), lambda i, j: (0, j + 2)),
            pl.BlockSpec((FFN_HB, D), lambda i, j: (j, 0)),
        ],
        out_specs=[
            pl.BlockSpec((tm, D), lambda i, j: (i, 0)),
            pl.BlockSpec((tm, D), lambda i, j: (i, 0)),
            pl.BlockSpec((tm, FFN_HB), lambda i, j: (i, j)),
            pl.BlockSpec((tm, FFN_HB), lambda i, j: (i, j)),
            pl.BlockSpec((tm, D), lambda i, j: (i, 0)),
        ],
        out_shape=[
            jax.ShapeDtypeStruct((t_len, D), F32),
            jax.ShapeDtypeStruct((t_len, D), BF16),
            jax.ShapeDtypeStruct((t_len, FFN_H), BF16),
            jax.ShapeDtypeStruct((t_len, FFN_H), BF16),
            jax.ShapeDtypeStruct((t_len, D), BF16),
        ],
        scratch_shapes=[pltpu.VMEM((tm, D), BF16), pltpu.VMEM((tm, D), F32)],
        compiler_params=_params(2),
    )(x, vec, wgu, wgu, wd)


def ffn_bwd_act(dx1, vec, g, u, f, wd):
    t_len = dx1.shape[0]
    tm = _row_block(t_len)

    def body(dx_ref, vec_ref, g_ref, u_ref, f_ref, wd_ref, act_ref, dg_ref, du_ref, dy_ref, pg_ref, dys):
        i, j = pl.program_id(0), pl.program_id(1)

        @pl.when((i == 0) & (j == 0))
        def _():
            pg_ref[...] = jnp.zeros_like(pg_ref)

        @pl.when(j == 0)
        def _():
            dx = dx_ref[...]
            dyb = (0.5 * vec_ref[3:4, :] * dx).astype(BF16)
            dys[...] = dyb
            dy_ref[...] = dyb
            pg_ref[3:4, :] += jnp.sum(0.5 * dx * f_ref[...].astype(F32), axis=0, keepdims=True)

        dact = _dot(dys[...], wd_ref[...], NT)
        gg = g_ref[...].astype(F32)
        uu = u_ref[...].astype(F32)
        sg = _sigmoid(gg)
        sl = gg * sg
        act_ref[...] = (sl * uu).astype(BF16)
        du_ref[...] = (dact * sl).astype(BF16)
        dg_ref[...] = (dact * uu * (sg + sl * (1.0 - sg))).astype(BF16)

    return pl.pallas_call(
        body, name="ffn_bwd_act", grid=(t_len // tm, 2),
        in_specs=[
            pl.BlockSpec((tm, D), lambda i, j: (i, 0)),
            pl.BlockSpec((8, D), lambda i, j: (0, 0)),
            pl.BlockSpec((tm, FFN_HB), lambda i, j: (i, j)),
            pl.BlockSpec((tm, FFN_HB), lambda i, j: (i, j)),
            pl.BlockSpec((tm, D), lambda i, j: (i, 0)),
            pl.BlockSpec((FFN_HB, D), lambda i, j: (j, 0)),
        ],
        out_specs=[
            pl.BlockSpec((tm, FFN_HB), lambda i, j: (i, j)),
            pl.BlockSpec((tm, FFN_HB), lambda i, j: (i, j)),
            pl.BlockSpec((tm, FFN_HB), lambda i, j: (i, j)),
            pl.BlockSpec((tm, D), lambda i, j: (i, 0)),
            pl.BlockSpec((8, D), lambda i, j: (0, 0)),
        ],
        out_shape=[
            jax.ShapeDtypeStruct((t_len, FFN_H), BF16),
            jax.ShapeDtypeStruct((t_len, FFN_H), BF16),
            jax.ShapeDtypeStruct((t_len, FFN_H), BF16),
            jax.ShapeDtypeStruct((t_len, D), BF16),
            jax.ShapeDtypeStruct((8, D), F32),
        ],
        scratch_shapes=[pltpu.VMEM((tm, D), BF16)],
        compiler_params=_params(2),
    )(dx1, vec, g, u, f, wd)


def nt_norm_bwd(dys, w, col_offsets, tk, n_steps, x_in, dres, vec, name):
    t_len = x_in.shape[0]
    tm = _row_block(t_len)
    n_seg = len(dys)

    def body(*refs):
        dy_refs = refs[:n_seg]
        w_refs = refs[n_seg:2 * n_seg]
        x_ref, dres_ref, vec_ref, dx_ref, pg_ref, acc = refs[2 * n_seg:]
        i, j = pl.program_id(0), pl.program_id(1)

        @pl.when((i == 0) & (j == 0))
        def _():
            pg_ref[...] = jnp.zeros_like(pg_ref)

        @pl.when(j == 0)
        def _():
            acc[...] = jnp.zeros_like(acc)

        part = _dot(dy_refs[0][...], w_refs[0][...], NT)
        for s in range(1, n_seg):
            part += _dot(dy_refs[s][...], w_refs[s][...], NT)
        acc[...] += part

        @pl.when(j == n_steps - 1)
        def _():
            _, vjp = jax.vjp(_normmod, x_ref[...], vec_ref[0:1, :], vec_ref[1:2, :], vec_ref[2:3, :])
            dxn, dnw, dsh, dsc = vjp(acc[...])
            dx_ref[...] = dres_ref[...] + dxn
            pg_ref[0:1, :] += dnw
            pg_ref[1:2, :] += dsh
            pg_ref[2:3, :] += dsc

    in_specs = [pl.BlockSpec((tm, tk), lambda i, j: (i, j)) for _ in range(n_seg)]
    in_specs += [pl.BlockSpec((D, tk), functools.partial(lambda i, j, off: (0, off + j), off=off))
                 for off in col_offsets]
    in_specs += [pl.BlockSpec((tm, D), lambda i, j: (i, 0)), pl.BlockSpec((tm, D), lambda i, j: (i, 0)),
                 pl.BlockSpec((8, D), lambda i, j: (0, 0))]
    return pl.pallas_call(
        body, name=name, grid=(t_len // tm, n_steps),
        in_specs=in_specs,
        out_specs=[pl.BlockSpec((tm, D), lambda i, j: (i, 0)), pl.BlockSpec((8, D), lambda i, j: (0, 0))],
        out_shape=[jax.ShapeDtypeStruct((t_len, D), F32), jax.ShapeDtypeStruct((8, D), F32)],
        scratch_shapes=[pltpu.VMEM((tm, D), F32)],
        compiler_params=_params(2),
    )(*dys, *([w] * n_seg), x_in, dres, vec)


def tn_matmul(a_x, b_x, a_c, b_c, tn, name):
    t_len, k_dim = a_x.shape
    n_dim = b_x.shape[1]
    t_ctx = a_c.shape[0]
    tt = 512
    n_t = t_len // tt

    def body(ax_ref, bx_ref, ac_ref, bc_ref, o_ref):
        t = pl.program_id(1)

        @pl.when(t == 0)
        def _():
            o_ref[...] = _dot(ac_ref[...], bc_ref[...], TN)

        o_ref[...] += _dot(ax_ref[...], bx_ref[...], TN)

    return pl.pallas_call(
        body, name=name, grid=(n_dim // tn, n_t),
        in_specs=[
            pl.BlockSpec((tt, k_dim), lambda n, t: (t, 0)),
            pl.BlockSpec((tt, tn), lambda n, t: (t, n)),
            pl.BlockSpec((t_ctx, k_dim), lambda n, t: (0, 0)),
            pl.BlockSpec((t_ctx, tn), lambda n, t: (0, n)),
        ],
        out_specs=pl.BlockSpec((k_dim, tn), lambda n, t: (0, n)),
        out_shape=jax.ShapeDtypeStruct((k_dim, n_dim), F32),
        compiler_params=_params(2),
    )(a_x, b_x, a_c, b_c)


def proj_fwd(x, vec, w_in_p):
    t_len = x.shape[0]
    tm = _row_block(t_len)

    def body(x_ref, vec_ref, w_ref, p_ref, h_ref, hs):
        @pl.when(pl.program_id(1) == 0)
        def _():
            hb = _normmod(x_ref[...], vec_ref[0:1, :], vec_ref[1:2, :], vec_ref[2:3, :]).astype(BF16)
            hs[...] = hb
            h_ref[...] = hb

        p_ref[...] = jnp.dot(hs[...], w_ref[...], preferred_element_type=F32)

    return pl.pallas_call(
        body, name="proj_fwd", grid=(t_len // tm, PW // PROJ_TN),
        in_specs=[
            pl.BlockSpec((tm, D), lambda i, j: (i, 0)),
            pl.BlockSpec((8, D), lambda i, j: (0, 0)),
            pl.BlockSpec((D, PROJ_TN), lambda i, j: (0, j)),
        ],
        out_specs=[pl.BlockSpec((tm, PROJ_TN), lambda i, j: (i, j)), pl.BlockSpec((tm, D), lambda i, j: (i, 0))],
        out_shape=[jax.ShapeDtypeStruct((t_len, PW), F32), jax.ShapeDtypeStruct((t_len, D), BF16)],
        scratch_shapes=[pltpu.VMEM((tm, D), BF16)],
        compiler_params=_params(2),
    )(x, vec, w_in_p)


def _shift_rows(e, s):
    n = e.shape[0]
    return pltpu.roll(e, (-s) % n, 0)


def _swap_halves(t):
    return pltpu.roll(t, DK // 2, 1)


def _halo_specs(tm, t_len, width, col, lead=False):
    per = tm // HALO
    last = t_len // HALO - 1
    if lead:
        return (pl.BlockSpec((2, HALO, width), lambda i: (0, jnp.maximum(i * per - 1, 0), col)),
                pl.BlockSpec((2, HALO, width), lambda i: (0, jnp.minimum((i + 1) * per, last), col)))
    return (pl.BlockSpec((HALO, width), lambda i: (jnp.maximum(i * per - 1, 0), col)),
            pl.BlockSpec((HALO, width), lambda i: (jnp.minimum((i + 1) * per, last), col)))


def _gate_cols(ba, gp_ref):
    lane = lax.broadcasted_iota(jnp.int32, ba.shape, 1)
    z = ba + gp_ref[1:2, :]
    return jnp.where(lane < 8, _sigmoid(ba), gp_ref[0:1, :] * _softplus(z))


def _conv_chunk(win, w_ref, c0, width):
    y = w_ref[0:1, c0:c0 + width] * _shift_rows(win, -2)
    for j in range(1, CONV_K):
        y += w_ref[j:j + 1, c0:c0 + width] * _shift_rows(win, j - 2)
    return y


def feat_fwd(p, cos2, sin2, conv_w8, gparams):
    t_len = p.shape[0]
    tm = _row_block(t_len)
    n_t = t_len // tm
    cw = 512

    def body(rqk_ref, dqk_ref, dqk_p, dqk_n, dvv_ref, dvv_p, dvv_n, dba_ref, cos_ref, sin_ref, cw_ref, gp_ref,
             o_rqk, o_dqkv, o_gbc):
        i = pl.program_id(0)
        cos, sin = cos_ref[...], sin_ref[...]
        for s in range(2 * NH):
            t = rqk_ref[:, s * DK:(s + 1) * DK]
            if s >= NH:
                t = t * (DK ** -0.5)
            o_rqk[:, s * DK:(s + 1) * DK] = t * cos + _swap_halves(t) * sin
        o_gbc[...] = _gate_cols(dba_ref[...], gp_ref)
        first, last = i == 0, i == n_t - 1
        for ci in range(4):
            src, sp, sn = (dqk_ref, dqk_p, dqk_n) if ci < 2 else (dvv_ref, dvv_p, dvv_n)
            c0 = (ci % 2) * cw
            win = jnp.concatenate([
                jnp.where(first, 0.0, sp[:, c0:c0 + cw]), src[:, c0:c0 + cw],
                jnp.where(last, 0.0, sn[:, c0:c0 + cw])], axis=0)
            y = _conv_chunk(win, cw_ref, ci * cw, cw)[HALO:HALO + tm]
            sv = y * _sigmoid(y)
            if ci < 2:
                scale = DK ** -0.5 if ci == 0 else 1.0
                for hh in range(NH):
                    sh = sv[:, hh * DK:(hh + 1) * DK]
                    nrm = lax.rsqrt(jnp.sum(sh * sh, axis=-1, keepdims=True) + EPS)
                    o_dqkv[:, ci * cw + hh * DK:ci * cw + (hh + 1) * DK] = sh * nrm * scale
            else:
                o_dqkv[:, ci * cw:(ci + 1) * cw] = sv

    hq = _halo_specs(tm, t_len, D, C_DQK)
    hv = _halo_specs(tm, t_len, D, C_DVV)
    return pl.pallas_call(
        body, name="feat_fwd", grid=(n_t,),
        in_specs=[
            pl.BlockSpec((tm, D), lambda i: (i, C_RQK)),
            pl.BlockSpec((tm, D), lambda i: (i, C_DQK)), hq[0], hq[1],
            pl.BlockSpec((tm, D), lambda i: (i, C_DVV)), hv[0], hv[1],
            pl.BlockSpec((tm, LANE), lambda i: (i, C_DBA)),
            pl.BlockSpec((tm, LANE), lambda i: (i, 0)),
            pl.BlockSpec((tm, LANE), lambda i: (i, 0)),
            pl.BlockSpec((8, 2 * D), lambda i: (0, 0)),
            pl.BlockSpec((8, LANE), lambda i: (0, 0)),
        ],
        out_specs=[pl.BlockSpec((tm, D), lambda i: (i, 0)), pl.BlockSpec((tm, 2 * D), lambda i: (i, 0)),
                   pl.BlockSpec((tm, LANE), lambda i: (i, 0))],
        out_shape=[jax.ShapeDtypeStruct((t_len, D), F32), jax.ShapeDtypeStruct((t_len, 2 * D), F32),
                   jax.ShapeDtypeStruct((t_len, LANE), F32)],
        compiler_params=_params(1),
    )(p, p, p, p, p, p, p, p, cos2, sin2, conv_w8, gparams)


def feat_bwd(p, cos2, sin2, conv_w8, gparams, d_rqk, d_rv, d_dqk, d_dvv, d_gbc):
    t_len = p.shape[0]
    tm = 256
    n_t = t_len // tm
    cw = 512

    def body(dqk_ref, dqk_p, dqk_n, dvv_ref, dvv_p, dvv_n, dba_ref, cos_ref, sin_ref, cw_ref, gp_ref,
             g_rqk, g_rv, g_dqk, g_dqk_p, g_dqk_n, g_dvv, g_dvv_p, g_dvv_n, g_gbc,
             o_rqk, o_rv, o_dqk, o_dvv, o_dba, o_cw, o_gp):
        i = pl.program_id(0)
        first, last = i == 0, i == n_t - 1

        @pl.when(first)
        def _():
            o_cw[...] = jnp.zeros_like(o_cw)
            o_gp[...] = jnp.zeros_like(o_gp)

        cos, sin = cos_ref[...], sin_ref[...]
        for s in range(2 * NH):
            gsl = g_rqk[0, :, s * DK:(s + 1) * DK] + g_rqk[1, :, s * DK:(s + 1) * DK]
            dt = gsl * cos + _swap_halves(gsl * sin)
            if s >= NH:
                dt = dt * (DK ** -0.5)
            o_rqk[:, s * DK:(s + 1) * DK] = dt.astype(BF16)
        o_rv[...] = (g_rv[0] + g_rv[1]).astype(BF16)

        ba = dba_ref[...]
        gg = g_gbc[0] + g_gbc[1]
        lane = lax.broadcasted_iota(jnp.int32, ba.shape, 1)
        sb = _sigmoid(ba)
        z = ba + gp_ref[1:2, :]
        a_row = gp_ref[0:1, :]
        dz = gg * a_row * _sigmoid(z)
        o_dba[...] = jnp.where(lane < 8, gg * sb * (1.0 - sb), dz).astype(BF16)
        is_g = (lane >= 8) & (lane < 16)
        o_gp[0:1, :] += jnp.sum(jnp.where(is_g, gg * a_row * _softplus(z), 0.0), axis=0, keepdims=True)
        o_gp[1:2, :] += jnp.sum(jnp.where(is_g, dz, 0.0), axis=0, keepdims=True)

        for ci in range(4):
            src, sp, sn = (dqk_ref, dqk_p, dqk_n) if ci < 2 else (dvv_ref, dvv_p, dvv_n)
            c0 = (ci % 2) * cw
            gc0 = ci * cw
            win = jnp.concatenate([
                jnp.where(first, 0.0, sp[:, c0:c0 + cw]), src[:, c0:c0 + cw],
                jnp.where(last, 0.0, sn[:, c0:c0 + cw])], axis=0)
            gs, gsp, gsn = (g_dqk, g_dqk_p, g_dqk_n) if ci < 2 else (g_dvv, g_dvv_p, g_dvv_n)
            gext = jnp.concatenate([
                jnp.where(first, 0.0, gsp[0, :, c0:c0 + cw] + gsp[1, :, c0:c0 + cw]),
                gs[0, :, c0:c0 + cw] + gs[1, :, c0:c0 + cw],
                jnp.where(last, 0.0, gsn[0, :, c0:c0 + cw] + gsn[1, :, c0:c0 + cw])], axis=0)
            y = _conv_chunk(win, cw_ref, gc0, cw)
            sg = _sigmoid(y)
            sv = y * sg
            if ci < 2:
                scale = DK ** -0.5 if ci == 0 else 1.0
                parts = []
                for hh in range(NH):
                    sh = sv[:, hh * DK:(hh + 1) * DK]
                    gh = gext[:, hh * DK:(hh + 1) * DK]
                    nrm = lax.rsqrt(jnp.sum(sh * sh, axis=-1, keepdims=True) + EPS)
                    dot = jnp.sum(gh * sh, axis=-1, keepdims=True)
                    parts.append(scale * nrm * (gh - sh * (nrm * nrm) * dot))
                ds = jnp.concatenate(parts, axis=1)
            else:
                ds = gext
            dy = ds * (sg + sv * (1.0 - sg))
            dpe = cw_ref[0:1, gc0:gc0 + cw] * _shift_rows(dy, 2)
            for j in range(1, CONV_K):
                dpe += cw_ref[j:j + 1, gc0:gc0 + cw] * _shift_rows(dy, 2 - j)
            dst = o_dqk if ci < 2 else o_dvv
            dst[:, c0:c0 + cw] = dpe[HALO:HALO + tm].astype(BF16)
            dyc = dy[HALO:HALO + tm]
            for j in range(CONV_K):
                o_cw[j:j + 1, gc0:gc0 + cw] += jnp.sum(dyc * _shift_rows(win, j - 2)[HALO:HALO + tm], axis=0,
                                                      keepdims=True)

    hq = _halo_specs(tm, t_len, D, C_DQK)
    hv = _halo_specs(tm, t_len, D, C_DVV)
    hg = _halo_specs(tm, t_len, D, 0, lead=True)
    outs = pl.pallas_call(
        body, name="feat_bwd", grid=(n_t,),
        in_specs=[
            pl.BlockSpec((tm, D), lambda i: (i, C_DQK)), hq[0], hq[1],
            pl.BlockSpec((tm, D), lambda i: (i, C_DVV)), hv[0], hv[1],
            pl.BlockSpec((tm, LANE), lambda i: (i, C_DBA)),
            pl.BlockSpec((tm, LANE), lambda i: (i, 0)),
            pl.BlockSpec((tm, LANE), lambda i: (i, 0)),
            pl.BlockSpec((8, 2 * D), lambda i: (0, 0)),
            pl.BlockSpec((8, LANE), lambda i: (0, 0)),
            pl.BlockSpec((2, tm, D), lambda i: (0, i, 0)),
            pl.BlockSpec((2, tm, D), lambda i: (0, i, 0)),
            pl.BlockSpec((2, tm, D), lambda i: (0, i, 0)), hg[0], hg[1],
            pl.BlockSpec((2, tm, D), lambda i: (0, i, 0)), hg[0], hg[1],
            pl.BlockSpec((2, tm, LANE), lambda i: (0, i, 0)),
        ],
        out_specs=[pl.BlockSpec((tm, D), lambda i: (i, 0))] * 4
        + [pl.BlockSpec((tm, LANE), lambda i: (i, 0)),
           pl.BlockSpec((8, 2 * D), lambda i: (0, 0)),
           pl.BlockSpec((8, LANE), lambda i: (0, 0))],
        out_shape=[jax.ShapeDtypeStruct((t_len, D), BF16)] * 4
        + [jax.ShapeDtypeStruct((t_len, LANE), BF16),
           jax.ShapeDtypeStruct((8, 2 * D), F32), jax.ShapeDtypeStruct((8, LANE), F32)],
        compiler_params=_params(1),
    )(p, p, p, p, p, p, p, cos2, sin2, conv_w8, gparams, d_rqk, d_rv, d_dqk, d_dqk, d_dqk, d_dvv, d_dvv, d_dvv,
      d_gbc)
    return outs


def _ret_chunk(q, k, v, s, lg, rev):
    c = q.shape[0]
    ii = lax.broadcasted_iota(jnp.int32, (c, c), 0).astype(F32)
    jj = lax.broadcasted_iota(jnp.int32, (c, c), 1).astype(F32)
    diff = jnp.where(rev, jj - ii, ii - jj)
    mask = diff >= jnp.where(rev, 1.0, 0.0)
    dec =jnp.where(mask, jnp.exp(lg * jnp.where(mask, diff, 0.0)), 0.0)
    idx = lax.broadcasted_iota(jnp.int32, (c, 1), 0).astype(F32)
    idx = jnp.where(rev, c - 1.0 - idx, idx)
    kdec = k * jnp.exp(lg * (c - 1.0 - idx))
    qdec = q * jnp.exp(lg * (idx + 1.0))
    o = mm_nn(mm_nt(q, k) * dec, v) + mm_nn(qdec, s)
    s2 = s * math.exp(lg * c) + mm_tn(kdec, v)
    return o, s2


def _log_gamma(h):
    return math.log1p(-(2.0 ** (-5.0 - h)))


def _dn_intra(q, k, v, g, beta, rev):
    c = q.shape[0]
    ii = lax.broadcasted_iota(jnp.int32, (c, c), 0)
    jj = lax.broadcasted_iota(jnp.int32, (c, c), 1)
    dd = jnp.where(rev, jj - ii, ii - jj)
    incl = dd >= 0
    strict = dd > 0
    gb = jnp.broadcast_to(g, (c, LANE))
    gcb = _dot_hi(incl.astype(F32), gb)
    gc = jnp.max(gcb, axis=1, keepdims=True)
    gcr = _dot_hi(jnp.full((c, LANE), 1.0 / LANE, F32), gcb, NT)
    decay = jnp.where(incl, jnp.exp(jnp.where(incl, gc - gcr, 0.0)), 0.0)
    kb = k * beta
    a = jnp.where(strict, mm_nt(kb, k) * decay, 0.0)
    eye = (ii == jj).astype(F32)
    x = -a
    tinv = eye + x
    pw = x
    for _ in range(5):
        pw = _dot_hi(pw, pw)
        tinv = tinv + _dot_hi(tinv, pw)
    u = mm_nn(tinv, v * beta)
    w = mm_nn(tinv, kb * jnp.exp(gc))
    attn = jnp.where(incl, mm_nt(q, k) * decay, 0.0)
    qg = q * jnp.exp(gc)
    glast = jnp.sum(g, axis=0, keepdims=True)
    kd = k * jnp.exp(glast - gc)
    return u, w, qg, kd, attn


def _dn_seq(u, w, qg, kd, attn, glast, s):
    v_new = u - mm_nn(w, s)
    o = mm_nn(qg, s) + mm_nn(attn, v_new)
    s2 = s * jnp.exp(glast) + mm_tn(kd, v_new)
    return o, s2


DN_W = 3 * NH * DK


def _dn_gate_cols(gbv, d, h):
    return _pick_lane(gbv, 8 + 4 * d + h), _pick_lane(gbv, 4 * d + h)


def dn_intra_fwd(dqkv, gbc):
    t_len = dqkv.shape[0]
    rb = 256
    n_g = rb // DN_C

    def body(qk_ref, v_ref, gb_ref, u_ref, wqk_ref, at_ref):
        d = pl.program_id(0)
        rev = d == 1
        for gi in range(n_g):
            rows = slice(gi * DN_C, (gi + 1) * DN_C)
            gbv = gb_ref[rows, :]
            for h in range(NH):
                g_col, b_col = _dn_gate_cols(gbv, d, h)
                u, w, qg, kd, attn = _dn_intra(qk_ref[rows, h * DK:(h + 1) * DK],
                                               qk_ref[rows, NH * DK + h * DK:NH * DK + (h + 1) * DK],
                                               v_ref[rows, h * DV:(h + 1) * DV], g_col, b_col, rev)
                u_ref[0, rows, h * DV:(h + 1) * DV] = u
                wqk_ref[0, rows, h * DK:(h + 1) * DK] = w.astype(BF16)
                wqk_ref[0, rows, NH * DK + h * DK:NH * DK + (h + 1) * DK] = qg.astype(BF16)
                wqk_ref[0, rows, 2 * NH * DK + h * DK:2 * NH * DK + (h + 1) * DK] = kd.astype(BF16)
                at_ref[0, h, rows, :] = attn.astype(BF16)

    return pl.pallas_call(
        body, name="dn_intra_fwd", grid=(2, t_len // rb),
        in_specs=[pl.BlockSpec((rb, D), lambda d, i: (i, 0)), pl.BlockSpec((rb, D), lambda d, i: (i, 1)),
                  pl.BlockSpec((rb, LANE), lambda d, i: (i, 0))],
        out_specs=[pl.BlockSpec((1, rb, D), lambda d, i: (d, i, 0)),
                   pl.BlockSpec((1, rb, DN_W), lambda d, i: (d, i, 0)),
                   pl.BlockSpec((1, NH, rb, DN_C), lambda d, i: (d, 0, i, 0))],
        out_shape=[jax.ShapeDtypeStruct((2, t_len, D), F32), jax.ShapeDtypeStruct((2, t_len, DN_W), BF16),
                   jax.ShapeDtypeStruct((2, NH, t_len, DN_C), BF16)],
        compiler_params=_params(2),
    )(dqkv, dqkv, gbc)


def _dn_seq_inputs(u_ref, wqk_ref, at_ref, gbv, d, h):
    u = u_ref[0, :, h * DV:(h + 1) * DV]
    w = wqk_ref[0, :, h * DK:(h + 1) * DK].astype(F32)
    qg = wqk_ref[0, :, NH * DK + h * DK:NH * DK + (h + 1) * DK].astype(F32)
    kd = wqk_ref[0, :, 2 * NH * DK + h * DK:2 * NH * DK + (h + 1) * DK].astype(F32)
    attn = at_ref[0, h].astype(F32)
    glast = jnp.sum(_pick_lane(gbv, 8 + 4 * d + h), axis=0, keepdims=True)
    return u, w, qg, kd, attn, glast


def dn_seq_fwd(u, wqk, attn, gbc, s0):
    t_len = u.shape[1]
    n_c = t_len // DN_C

    def chunk_of(d, t):
        return jnp.where(d == 0, t, n_c - 1 - t)

    def body(u_ref, wqk_ref, at_ref, gb_ref, s0_ref, o_ref, ss_ref, sf_ref, st):
        d, t = pl.program_id(0), pl.program_id(1)

        @pl.when(t == 0)
        def _():
            st[...] = s0_ref[0]

        gbv = gb_ref[...]
        for h in range(NH):
            sh = st[h]
            ss_ref[0, 0, h] = sh.astype(BF16)
            o, s2 = _dn_seq(*_dn_seq_inputs(u_ref, wqk_ref, at_ref, gbv, d, h), sh)
            o_ref[0, :, h * DV:(h + 1) * DV] = o
            st[h] = s2

        @pl.when(t == n_c - 1)
        def _():
            sf_ref[0] = st[...]

    return pl.pallas_call(
        body, name="dn_seq_fwd", grid=(2, n_c),
        in_specs=[pl.BlockSpec((1, DN_C, D), lambda d, t: (d, chunk_of(d, t), 0)),
                  pl.BlockSpec((1, DN_C, DN_W), lambda d, t: (d, chunk_of(d, t), 0)),
                  pl.BlockSpec((1, NH, DN_C, DN_C), lambda d, t: (d, 0, chunk_of(d, t), 0)),
                  pl.BlockSpec((DN_C, LANE), lambda d, t: (chunk_of(d, t), 0)),
                  _state_spec()],
        out_specs=[pl.BlockSpec((1, DN_C, D), lambda d, t: (d, chunk_of(d, t), 0)),
                   pl.BlockSpec((1, 1, NH, DK, DV), lambda d, t: (d, chunk_of(d, t), 0, 0, 0)),
                   _state_spec()],
        out_shape=[jax.ShapeDtypeStruct((2, t_len, D), F32),
                   jax.ShapeDtypeStruct((2, n_c, NH, DK, DV), BF16),
                   jax.ShapeDtypeStruct((2, NH, DK, DV), F32)],
        scratch_shapes=[pltpu.VMEM((NH, DK, DV), F32)],
        compiler_params=_params(2),
    )(u, wqk, attn, gbc, s0)


def dn_seq_bwd(u, wqk, attn, gbc, ssave, do, ds_fin):
    t_len = u.shape[1]
    n_c = t_len // DN_C

    def chunk_of(d, t):
        return jnp.where(d == 0, n_c - 1 - t, t)

    def body(u_ref, wqk_ref, at_ref, gb_ref, ss_ref, do_ref, dsf_ref, du_ref, dwqk_ref, dat_ref, dgl_ref, ds0_ref, dst):
        d, t = pl.program_id(0), pl.program_id(1)

        @pl.when(t == 0)
        def _():
            dst[...] = dsf_ref[0]

        gbv = gb_ref[...]
        rows8 = lax.broadcasted_iota(jnp.int32, (8, LANE), 0)
        dgl_tile = jnp.zeros((8, LANE), F32)
        for h in range(NH):
            sh = ss_ref[0, 0, h].astype(F32)
            _, vjp = jax.vjp(_dn_seq, *_dn_seq_inputs(u_ref, wqk_ref, at_ref, gbv, d, h), sh)
            du, dw, dqg, dkd, dat, dgl, ds = vjp((do_ref[:, h * DV:(h + 1) * DV], dst[h]))
            du_ref[0, :, h * DV:(h + 1) * DV] = du
            dwqk_ref[0, :, h * DK:(h + 1) * DK] = dw
            dwqk_ref[0, :, NH * DK + h * DK:NH * DK + (h + 1) * DK] = dqg
            dwqk_ref[0, :, 2 * NH * DK + h * DK:2 * NH * DK + (h + 1) * DK] = dkd
            dat_ref[0, h] = dat
            dgl_tile = jnp.where(rows8 == h, dgl, dgl_tile)
            dst[h] = ds
        dgl_ref[0, 0] = dgl_tile

        @pl.when(t == n_c - 1)
        def _():
            ds0_ref[0] = dst[...]

    seq = lambda width: pl.BlockSpec((1, DN_C, width), lambda d, t: (d, chunk_of(d, t), 0))
    att = pl.BlockSpec((1, NH, DN_C, DN_C), lambda d, t: (d, 0, chunk_of(d, t), 0))
    return pl.pallas_call(
        body, name="dn_seq_bwd", grid=(2, n_c),
        in_specs=[seq(D), seq(DN_W), att, pl.BlockSpec((DN_C, LANE), lambda d, t: (chunk_of(d, t), 0)),
                  pl.BlockSpec((1, 1, NH, DK, DV), lambda d, t: (d, chunk_of(d, t), 0, 0, 0)),
                  pl.BlockSpec((DN_C, D), lambda d, t: (chunk_of(d, t), 0)), _state_spec()],
        out_specs=[seq(D), seq(DN_W), att, pl.BlockSpec((1, 1, 8, LANE), lambda d, t: (d, chunk_of(d, t), 0, 0)),
                   _state_spec()],
        out_shape=[jax.ShapeDtypeStruct((2, t_len, D), F32), jax.ShapeDtypeStruct((2, t_len, DN_W), F32),
                   jax.ShapeDtypeStruct((2, NH, t_len, DN_C), F32), jax.ShapeDtypeStruct((2, n_c, 8, LANE), F32),
                   jax.ShapeDtypeStruct((2, NH, DK, DV), F32)],
        scratch_shapes=[pltpu.VMEM((NH, DK, DV), F32)],
        compiler_params=_params(2),
    )(u, wqk, attn, gbc, ssave, do, ds_fin)


def dn_intra_bwd(dqkv, gbc, du, dwqk, dattn, dgl):
    t_len = dqkv.shape[0]
    rb = 128
    n_g = rb // DN_C

    def body(qk_ref, v_ref, gb_ref, du_ref, dwqk_ref, dat_ref, dgl_ref, dqk_ref, dv_ref, dgb_ref):
        d = pl.program_id(0)
        rev = d == 1
        for gi in range(n_g):
            rows = slice(gi * DN_C, (gi + 1) * DN_C)
            gbv = gb_ref[rows, :]
            dgb = jnp.zeros((DN_C, LANE), F32)
            for h in range(NH):
                g_col, b_col = _dn_gate_cols(gbv, d, h)
                _, vjp = jax.vjp(functools.partial(_dn_intra, rev=rev), qk_ref[rows, h * DK:(h + 1) * DK],
                                 qk_ref[rows, NH * DK + h * DK:NH * DK + (h + 1) * DK],
                                 v_ref[rows, h * DV:(h + 1) * DV], g_col, b_col)
                dq, dk, dv, dg, db = vjp((du_ref[0, rows, h * DV:(h + 1) * DV],
                                          dwqk_ref[0, rows, h * DK:(h + 1) * DK],
                                          dwqk_ref[0, rows, NH * DK + h * DK:NH * DK + (h + 1) * DK],
                                          dwqk_ref[0, rows, 2 * NH * DK + h * DK:2 * NH * DK + (h + 1) * DK],
                                          dat_ref[0, h, rows, :]))
                dg = dg + dgl_ref[0, gi, h:h + 1, 0:1]
                dqk_ref[0, rows, h * DK:(h + 1) * DK] = dq
                dqk_ref[0, rows, NH * DK + h * DK:NH * DK + (h + 1) * DK] = dk
                dv_ref[0, rows, h * DV:(h + 1) * DV] = dv
                dgb = dgb + _put_lane(dg, 8 + 4 * d + h, LANE) + _put_lane(db, 4 * d + h, LANE)
            dgb_ref[0, rows, :] = dgb

    seq = lambda width: pl.BlockSpec((1, rb, width), lambda d, i: (d, i, 0))
    return pl.pallas_call(
        body, name="dn_intra_bwd", grid=(2, t_len // rb),
        in_specs=[pl.BlockSpec((rb, D), lambda d, i: (i, 0)), pl.BlockSpec((rb, D), lambda d, i: (i, 1)),
                  pl.BlockSpec((rb, LANE), lambda d, i: (i, 0)), seq(D), seq(DN_W),
                  pl.BlockSpec((1, NH, rb, DN_C), lambda d, i: (d, 0, i, 0)),
                  pl.BlockSpec((1, n_g, 8, LANE), lambda d, i: (d, i, 0, 0))],
        out_specs=[seq(D), seq(D), seq(LANE)],
        out_shape=[jax.ShapeDtypeStruct((2, t_len, D), F32), jax.ShapeDtypeStruct((2, t_len, D), F32),
                   jax.ShapeDtypeStruct((2, t_len, LANE), F32)],
        compiler_params=_params(2),
    )(dqkv, dqkv, gbc, du, dwqk, dattn, dgl)


def _pick_lane(x, lane_idx):
    lane = lax.broadcasted_iota(jnp.int32, x.shape, 1)
    return jnp.sum(jnp.where(lane == lane_idx, x, 0.0), axis=1, keepdims=True)


def _put_lane(col, lane_idx, width):
    lane = lax.broadcasted_iota(jnp.int32, (col.shape[0], width), 1)
    return jnp.where(lane == lane_idx, col, 0.0)


def _state_spec():
    return pl.BlockSpec((1, NH, DK, DV), lambda d, t: (d, 0, 0, 0))


def ret_fwd(rqk, p, s0):
    c = RET_C
    t_len = rqk.shape[0]
    n_c = t_len // c

    def chunk_of(d, t):
        return jnp.where(d == 0, t, n_c - 1 - t)

    def body(qk_ref, v_ref, s0_ref, o_ref, ss_ref, sf_ref, st):
        d, t = pl.program_id(0), pl.program_id(1)
        rev = d == 1

        @pl.when(t == 0)
        def _():
            st[...] = s0_ref[0]

        for h in range(NH):
            sh = st[h]
            ss_ref[0, 0, h] = sh.astype(BF16)
            o, s2 = _ret_chunk(qk_ref[:, h * DK:(h + 1) * DK], qk_ref[:, NH * DK + h * DK:NH * DK + (h + 1) * DK],
                               v_ref[:, h * DV:(h + 1) * DV], sh, _log_gamma(h), rev)
            o_ref[0, :, h * DV:(h + 1) * DV] = o
            st[h] = s2

        @pl.when(t == n_c - 1)
        def _():
            sf_ref[0] = st[...]

    return pl.pallas_call(
        body, name="ret_fwd", grid=(2, n_c),
        in_specs=[pl.BlockSpec((c, D), lambda d, t: (chunk_of(d, t), 0)),
                  pl.BlockSpec((c, D), lambda d, t: (chunk_of(d, t), C_RV)),
                  _state_spec()],
        out_specs=[pl.BlockSpec((1, c, D), lambda d, t: (d, chunk_of(d, t), 0)),
                   pl.BlockSpec((1, 1, NH, DK, DV), lambda d, t: (d, chunk_of(d, t), 0, 0, 0)),
                   _state_spec()],
        out_shape=[jax.ShapeDtypeStruct((2, t_len, D), F32),
                   jax.ShapeDtypeStruct((2, n_c, NH, DK, DV), BF16),
                   jax.ShapeDtypeStruct((2, NH, DK, DV), F32)],
        scratch_shapes=[pltpu.VMEM((NH, DK, DV), F32)],
        compiler_params=_params(2),
    )(rqk, p, s0)


def ret_bwd(rqk, p, ssave, do, ds_fin):
    c = RET_C
    t_len = rqk.shape[0]
    n_c = t_len // c

    def chunk_of(d, t):
        return jnp.where(d == 0, n_c - 1 - t, t)

    def body(qk_ref, v_ref, ss_ref, do_ref, dsf_ref, dqk_ref, dv_ref, ds0_ref, dst):
        d, t = pl.program_id(0), pl.program_id(1)
        rev = d == 1

        @pl.when(t == 0)
        def _():
            dst[...] = dsf_ref[0]

        for h in range(NH):
            _, vjp = jax.vjp(functools.partial(_ret_chunk, lg=_log_gamma(h), rev=rev),
                             qk_ref[:, h * DK:(h + 1) * DK], qk_ref[:, NH * DK + h * DK:NH * DK + (h + 1) * DK],
                             v_ref[:, h * DV:(h + 1) * DV], ss_ref[0, 0, h].astype(F32))
            dq, dk, dv, ds = vjp((do_ref[:, h * DV:(h + 1) * DV], dst[h]))
            dqk_ref[0, :, h * DK:(h + 1) * DK] = dq
            dqk_ref[0, :, NH * DK + h * DK:NH * DK + (h + 1) * DK] = dk
            dv_ref[0, :, h * DV:(h + 1) * DV] = dv
            dst[h] = ds

        @pl.when(t == n_c - 1)
        def _():
            ds0_ref[0] = dst[...]

    seq_spec = pl.BlockSpec((1, c, D), lambda d, t: (d, chunk_of(d, t), 0))
    return pl.pallas_call(
        body, name="ret_bwd", grid=(2, n_c),
        in_specs=[pl.BlockSpec((c, D), lambda d, t: (chunk_of(d, t), 0)),
                  pl.BlockSpec((c, D), lambda d, t: (chunk_of(d, t), C_RV)),
                  pl.BlockSpec((1, 1, NH, DK, DV), lambda d, t: (d, chunk_of(d, t), 0, 0, 0)),
                  pl.BlockSpec((c, D), lambda d, t: (chunk_of(d, t), 0)),
                  _state_spec()],
        out_specs=[seq_spec, seq_spec, _state_spec()],
        out_shape=[jax.ShapeDtypeStruct((2, t_len, D), F32)] * 2 + [jax.ShapeDtypeStruct((2, NH, DK, DV), F32)],
        scratch_shapes=[pltpu.VMEM((NH, DK, DV), F32)],
        compiler_params=_params(2),
    )(rqk, p, ssave, do, ds_fin)


def _head_gate(o_ret, o_dn, rg, dz, nw):
    ret = o_ret * lax.rsqrt(jnp.mean(o_ret * o_ret, axis=-1, keepdims=True) + EPS) * (rg * _sigmoid(rg))
    dn = o_dn * lax.rsqrt(jnp.mean(o_dn * o_dn, axis=-1, keepdims=True) + EPS) * nw * (dz * _sigmoid(dz))
    return ret, dn


MIX_TM = 256


def _mix_specs():
    seq = lambda col: pl.BlockSpec((MIX_TM, D), functools.partial(lambda i, col: (i, col), col=col))
    pair = pl.BlockSpec((2, MIX_TM, D), lambda i: (0, i, 0))
    wfull = pl.BlockSpec((D, D), lambda i: (0, 0))
    vec = pl.BlockSpec((8, D), lambda i: (0, 0))
    return seq, pair, wfull, vec


def mixout_fwd(x1, vec, o_ret, o_dn, p, w_r, w_d, w_o):
    t_len = x1.shape[0]

    def body(x_ref, vec_ref, or_ref, od_ref, rg_ref, dz_ref, ga_ref, gb_ref, wr_ref, wd_ref, wo_ref,
             x2_ref, ret_ref, dn_ref, y_ref, yr_ref, yd_ref, z_ref):
        for h in range(NH):
            sl = slice(h * DV, (h + 1) * DV)
            ret, dn = _head_gate(or_ref[0, :, sl] + or_ref[1, :, sl], od_ref[0, :, sl] + od_ref[1, :, sl],
                                 rg_ref[:, sl], dz_ref[:, sl], vec_ref[1:2, sl])
            ret_ref[:, sl] = ret.astype(BF16)
            dn_ref[:, sl] = dn.astype(BF16)
        yr = jnp.dot(ret_ref[...], wr_ref[...], preferred_element_type=F32)
        yd = jnp.dot(dn_ref[...], wd_ref[...], preferred_element_type=F32)
        yr_ref[...] = yr.astype(BF16)
        yd_ref[...] = yd.astype(BF16)
        y = (_sigmoid(ga_ref[...]) * yr + _sigmoid(gb_ref[...]) * yd).astype(BF16)
        y_ref[...] = y
        z = jnp.dot(y, wo_ref[...], preferred_element_type=F32)
        z_ref[...] = z.astype(BF16)
        x2_ref[...] = x_ref[...] + vec_ref[0:1, :] * z

    seq, pair, wfull, vecs = _mix_specs()
    return pl.pallas_call(
        body, name="mixout_fwd", grid=(t_len // MIX_TM,),
        in_specs=[seq(0), vecs, pair, pair, seq(C_RG), seq(C_DZ), seq(C_GA), seq(C_GB), wfull, wfull, wfull],
        out_specs=[seq(0)] * 7,
        out_shape=[jax.ShapeDtypeStruct((t_len, D), F32)] + [jax.ShapeDtypeStruct((t_len, D), BF16)] * 6,
        compiler_params=_params(1),
    )(x1, vec, o_ret, o_dn, p, p, p, p, w_r, w_d, w_o)


def mixout_bwd(dx2, vec, o_ret, o_dn, p, yr, yd, z, w_r, w_d, w_o):
    t_len = dx2.shape[0]

    def body(dx_ref, vec_ref, or_ref, od_ref, rg_ref, dz_ref, ga_ref, gb_ref, yr_ref, yd_ref, z_ref,
             wr_ref, wd_ref, wo_ref,
             dor_ref, dod_ref, drg_ref, ddz_ref, dga_ref, dgb_ref, dyr_ref, dyd_ref, dzz_ref, pg_ref):
        @pl.when(pl.program_id(0) == 0)
        def _():
            pg_ref[...] = jnp.zeros_like(pg_ref)

        dx = dx_ref[...]
        pg_ref[0:1, :] += jnp.sum(dx * z_ref[...].astype(F32), axis=0, keepdims=True)
        dzz = (vec_ref[0:1, :] * dx).astype(BF16)
        dzz_ref[...] = dzz
        dy = _dot(dzz, wo_ref[...], NT)
        sa = _sigmoid(ga_ref[...])
        sb = _sigmoid(gb_ref[...])
        dyr = (dy * sa).astype(BF16)
        dyd = (dy * sb).astype(BF16)
        dyr_ref[...] = dyr
        dyd_ref[...] = dyd
        dga_ref[...] = (dy * yr_ref[...].astype(F32) * sa * (1.0 - sa)).astype(BF16)
        dgb_ref[...] = (dy * yd_ref[...].astype(F32) * sb * (1.0 - sb)).astype(BF16)
        dret = _dot(dyr, wr_ref[...], NT)
        ddn = _dot(dyd, wd_ref[...], NT)
        for h in range(NH):
            sl = slice(h * DV, (h + 1) * DV)
            _, vjp = jax.vjp(_head_gate, or_ref[0, :, sl] + or_ref[1, :, sl], od_ref[0, :, sl] + od_ref[1, :, sl],
                             rg_ref[:, sl], dz_ref[:, sl], vec_ref[1:2, sl])
            d_or, d_od, d_rg, d_dz, d_nw = vjp((dret[:, sl], ddn[:, sl]))
            dor_ref[:, sl] = d_or
            dod_ref[:, sl] = d_od
            drg_ref[:, sl] = d_rg.astype(BF16)
            ddz_ref[:, sl] = d_dz.astype(BF16)
            pg_ref[1:2, sl] += d_nw

    seq, pair, wfull, vecs = _mix_specs()
    return pl.pallas_call(
        body, name="mixout_bwd", grid=(t_len // MIX_TM,),
        in_specs=[seq(0), vecs, pair, pair, seq(C_RG), seq(C_DZ), seq(C_GA), seq(C_GB), seq(0), seq(0), seq(0),
                  wfull, wfull, wfull],
        out_specs=[seq(0)] * 9 + [vecs],
        out_shape=[jax.ShapeDtypeStruct((t_len, D), F32)] * 2 + [jax.ShapeDtypeStruct((t_len, D), BF16)] * 7
        + [jax.ShapeDtypeStruct((8, D), F32)],
        compiler_params=_params(1),
    )(dx2, vec, o_ret, o_dn, p, p, p, p, yr, yd, z, w_r, w_d, w_o)


def _final_loss(x, w, target):
    y = x * lax.rsqrt(jnp.mean(x * x, axis=-1, keepdims=True) + EPS) * w
    err = y - target
    return 0.5 * jnp.sum(jnp.mean(err * err, axis=-1, keepdims=True), axis=0, keepdims=True)


def final_fwd_bwd(x, vec, target):
    t_len = x.shape[0]
    tm = _row_block(t_len)

    def body(x_ref, vec_ref, t_ref, dx_ref, pg_ref, loss_ref):
        @pl.when(pl.program_id(0) == 0)
        def _():
            pg_ref[...] = jnp.zeros_like(pg_ref)
            loss_ref[...] = jnp.zeros_like(loss_ref)

        loss, vjp = jax.vjp(functools.partial(_final_loss, target=t_ref[...]), x_ref[...], vec_ref[0:1, :])
        dx, dw = vjp(jnp.ones((1, 1), F32))
        dx_ref[...] = dx
        pg_ref[0:1, :] += dw
        loss_ref[...] += jnp.broadcast_to(loss, loss_ref.shape)

    return pl.pallas_call(
        body, name="final_fwd_bwd", grid=(t_len // tm,),
        in_specs=[pl.BlockSpec((tm, D), lambda i: (i, 0)), pl.BlockSpec((8, D), lambda i: (0, 0)),
                  pl.BlockSpec((tm, D), lambda i: (i, 0))],
        out_specs=[pl.BlockSpec((tm, D), lambda i: (i, 0)), pl.BlockSpec((8, D), lambda i: (0, 0)),
                   pl.BlockSpec((8, LANE), lambda i: (0, 0))],
        out_shape=[jax.ShapeDtypeStruct((t_len, D), F32), jax.ShapeDtypeStruct((8, D), F32),
                   jax.ShapeDtypeStruct((8, LANE), F32)],
        compiler_params=_params(1),
    )(x, vec, target)


def _vec8(*rows):
    rows = list(rows) + [jnp.zeros((D,), F32)] * (8 - len(rows))
    return jnp.stack(rows)


def _layer_vecs(lw, m):
    nw = lw["norm_w"]
    return dict(ffn1=_vec8(nw[0], m[0], m[1], m[2]), proj=_vec8(nw[1], m[3], m[4]),
                mix=_vec8(m[5], lw["dn_norm_t"]), ffn2=_vec8(nw[2], m[6], m[7], m[8]))


def _rope_tables(t_len, grid_w=64, base=10000.0):
    n_freq = DK // 4
    inv = base ** (-jnp.arange(n_freq, dtype=F32) / n_freq)
    tok = jnp.arange(t_len)
    ang = jnp.concatenate([(tok // grid_w).astype(F32)[:, None] * inv, (tok % grid_w).astype(F32)[:, None] * inv],
                          axis=-1)
    cos, sin = jnp.cos(ang), jnp.sin(ang)
    return jnp.concatenate([cos, cos], axis=-1), jnp.concatenate([-sin, sin], axis=-1)


def _stream_fwd_a(x0, lw, vecs, rope):
    x1, h1, g1, u1, f1 = ffn_fwd(x0, vecs["ffn1"], lw["wgu1"], lw["wd1"])
    p, h2 = proj_fwd(x1, vecs["proj"], lw["w_in"])
    rqk, dqkv, gbc = feat_fwd(p, rope[0], rope[1], lw["conv_w8"], lw["gparams"])
    return dict(x0=x0, h1=h1, g1=g1, u1=u1, f1=f1, x1=x1, h2=h2, p=p, rqk=rqk, dqkv=dqkv, gbc=gbc)


def _stream_mix(sv, s0_ret, s0_dn):
    sv["o_ret"], sv["ss_ret"], sf_ret = ret_fwd(sv["rqk"], sv["p"], s0_ret)
    sv["dn_u"], sv["dn_wqk"], sv["dn_attn"] = dn_intra_fwd(sv["dqkv"], sv["gbc"])
    sv["o_dn"], sv["ss_dn"], sf_dn = dn_seq_fwd(sv["dn_u"], sv["dn_wqk"], sv["dn_attn"], sv["gbc"], s0_dn)
    return sf_ret, sf_dn


def _stream_fwd_b(sv, lw, vecs):
    x2, ret, dn, y, yr, yd, z = mixout_fwd(sv["x1"], vecs["mix"], sv["o_ret"], sv["o_dn"], sv["p"],
                                           lw["w_r"], lw["w_d"], lw["w_o"])
    x3, h3, g3, u3, f3 = ffn_fwd(x2, vecs["ffn2"], lw["wgu2"], lw["wd2"])
    sv.update(ret=ret, dn=dn, y=y, yr=yr, yd=yd, z=z, x2=x2, h3=h3, g3=g3, u3=u3, f3=f3)
    return x3


def layer_fwd(xs, cs, lw, ropes):
    vx, vc = _layer_vecs(lw, lw["mx"]), _layer_vecs(lw, lw["mc"])
    sx = _stream_fwd_a(xs, lw, vx, ropes[0])
    sc = _stream_fwd_a(cs, lw, vc, ropes[1])
    zero = jnp.zeros((2, NH, DK, DV), F32)
    sf_ret, sf_dn = _stream_mix(sc, zero, zero)
    _stream_mix(sx, sf_ret, sf_dn)
    x3 = _stream_fwd_b(sx, lw, vx)
    c3 = _stream_fwd_b(sc, lw, vc)
    return x3, c3, (sx, sc)


def _stream_bwd_a(dx3, sv, lw, vecs):
    act3, dg3, du3, dy3, pg_a = ffn_bwd_act(dx3, vecs["ffn2"], sv["g3"], sv["u3"], sv["f3"], lw["wd2"])
    dx2, pg_b = nt_norm_bwd([dg3, du3], lw["wgu2"], [0, 2], FFN_HB, 2, sv["x2"], dx3, vecs["ffn2"], "ffn_bwd_in")
    (dor, dod, drg, ddz, dga, dgb, dyr, dyd, dzz, pg_m) = mixout_bwd(
        dx2, vecs["mix"], sv["o_ret"], sv["o_dn"], sv["p"], sv["yr"], sv["yd"], sv["z"], lw["w_r"], lw["w_d"], lw["w_o"])
    return dict(act3=act3, dg3=dg3, du3=du3, dy3=dy3, pg_a2=pg_a, pg_b2=pg_b, dx2=dx2, dor=dor, dod=dod,
                drg=drg, ddz=ddz, dga=dga, dgb=dgb, dyr=dyr, dyd=dyd, dzz=dzz, pg_m=pg_m)


def _stream_bwd_mix(bw, sv, dsf_ret, dsf_dn):
    bw["dqk_r"], bw["drv"], ds0_ret = ret_bwd(sv["rqk"], sv["p"], sv["ss_ret"], bw["dor"], dsf_ret)
    du, dwqk, dattn, dgl, ds0_dn = dn_seq_bwd(sv["dn_u"], sv["dn_wqk"], sv["dn_attn"], sv["gbc"], sv["ss_dn"],
                                              bw["dod"], dsf_dn)
    bw["dqk_d"], bw["dvv_d"], bw["dgbc"] = dn_intra_bwd(sv["dqkv"], sv["gbc"], du, dwqk, dattn, dgl)
    return ds0_ret, ds0_dn


def _stream_bwd_b(bw, sv, lw, vecs, rope):
    dp_rqk, dp_rv, dp_dqk, dp_dvv, dp_dba, bw["dcw"], bw["dgp"] = feat_bwd(
        sv["p"], rope[0], rope[1], lw["conv_w8"], lw["gparams"], bw["dqk_r"], bw["drv"], bw["dqk_d"], bw["dvv_d"],
        bw["dgbc"])
    bw["dp"] = jnp.concatenate([dp_rqk, dp_rv, bw["drg"], dp_dqk, dp_dvv, bw["ddz"], bw["dga"], bw["dgb"], dp_dba],
                               axis=1)
    dx1, bw["pg_p"] = nt_norm_bwd([bw["dp"]], lw["w_in"], [0], PROJ_TN, PW // PROJ_TN, sv["x1"], bw["dx2"],
                                  vecs["proj"], "proj_bwd_in")
    bw["act1"], bw["dg1"], bw["du1"], bw["dy1"], bw["pg_a1"] = ffn_bwd_act(dx1, vecs["ffn1"], sv["g1"], sv["u1"],
                                                                          sv["f1"], lw["wd1"])
    dx0, bw["pg_b1"] = nt_norm_bwd([bw["dg1"], bw["du1"]], lw["wgu1"], [0, 2], FFN_HB, 2, sv["x0"], dx1,
                                   vecs["ffn1"], "ffn_bwd_in")
    return dx0


def _stream_pgrads(bw):
    a1, b1, pp, pm, a2, b2 = bw["pg_a1"], bw["pg_b1"], bw["pg_p"], bw["pg_m"], bw["pg_a2"], bw["pg_b2"]
    dmod = jnp.stack([b1[1], b1[2], a1[3], pp[1], pp[2], pm[0], b2[1], b2[2], a2[3]])
    dnorm = jnp.stack([b1[0], pp[0], b2[0]])
    ddnw = pm[1].reshape(NH, DV).sum(axis=0)
    return dmod, dnorm, ddnw, bw["dcw"][:CONV_K], bw["dgp"][0, 8:16], bw["dgp"][1, 8:16]


def layer_bwd(dx3, dc3, lw, saved, ropes):
    sx, sc = saved
    vx, vc = _layer_vecs(lw, lw["mx"]), _layer_vecs(lw, lw["mc"])
    bx = _stream_bwd_a(dx3, sx, lw, vx)
    bc = _stream_bwd_a(dc3, sc, lw, vc)
    zero = jnp.zeros((2, NH, DK, DV), F32)
    ds0_ret, ds0_dn = _stream_bwd_mix(bx, sx, zero, zero)
    _stream_bwd_mix(bc, sc, ds0_ret, ds0_dn)
    dx0 = _stream_bwd_b(bx, sx, lw, vx, ropes[0])
    dc0 = _stream_bwd_b(bc, sc, lw, vc, ropes[1])

    def wgrad(a, b, tn, name):
        ax = sx[a] if a in sx else bx[a]
        ac = sc[a] if a in sc else bc[a]
        return tn_matmul(ax, bx[b], ac, bc[b], tn, name)

    gw = dict(
        wg1=wgrad("h1", "dg1", FFN_HB, "dw_ffn_gu"), wu1=wgrad("h1", "du1", FFN_HB, "dw_ffn_gu"),
        wd1=wgrad("act1", "dy1", 512, "dw_ffn_d"),
        w_in=wgrad("h2", "dp", PROJ_TN, "dw_in"),
        w_r=wgrad("ret", "dyr", 512, "dw_sq"), w_d=wgrad("dn", "dyd", 512, "dw_sq"), w_o=wgrad("y", "dzz", 512, "dw_sq"),
        wg2=wgrad("h3", "dg3", FFN_HB, "dw_ffn_gu"), wu2=wgrad("h3", "du3", FFN_HB, "dw_ffn_gu"),
        wd2=wgrad("act3", "dy3", 512, "dw_ffn_d"),
    )
    px, pc = _stream_pgrads(bx), _stream_pgrads(bc)
    small = dict(dmx=px[0], dmc=pc[0], norm_w=px[1] + pc[1], dn_norm_w=px[2] + pc[2], conv_w=px[3] + pc[3],
                 a_log=px[4] + pc[4], dt_bias=px[5] + pc[5])
    return dx0, dc0, gw, small


def local_step(x, ctx, target, final_norm_w, lws):
    t_len, t_ctx = x.shape[0], ctx.shape[0]
    ropes = (_rope_tables(t_len), (jnp.ones((t_ctx, LANE), F32), jnp.zeros((t_ctx, LANE), F32)))
    xs, cs, saved = x, ctx, []
    for lw in lws:
        xs, cs, sv = layer_fwd(xs, cs, lw, ropes)
        saved.append(sv)
    dx, pg_f, loss = final_fwd_bwd(xs, _vec8(final_norm_w), target)
    dc = jnp.zeros_like(ctx)
    gws, smalls = [None] * len(lws), [None] * len(lws)
    for l in reversed(range(len(lws))):
        dx, dc, gws[l], smalls[l] = layer_bwd(dx, dc, lws[l], saved[l], ropes)
    return loss[0, 0], dx, gws, smalls, pg_f[0]


ADA_ROWS = 16
ADA_COLS = N_MOD * D // N_DEV


def ada_fwd(cc, ada_w, ada_b_cols):
    def body(cc_ref, w_ref, b_ref, o_ref):
        cv = cc_ref[...]
        o_ref[0] = _dot_hi(cv * _sigmoid(cv), w_ref[0]) + b_ref[0]

    return pl.pallas_call(
        body, name="ada_fwd", grid=(DEPTH,),
        in_specs=[pl.BlockSpec((ADA_ROWS, D), lambda l: (0, 0)), pl.BlockSpec((1, D, ADA_COLS), lambda l: (l, 0, 0)),
                  pl.BlockSpec((1, 1, ADA_COLS), lambda l: (l, 0, 0))],
        out_specs=pl.BlockSpec((1, ADA_ROWS, ADA_COLS), lambda l: (l, 0, 0)),
        out_shape=jax.ShapeDtypeStruct((DEPTH, ADA_ROWS, ADA_COLS), F32),
        compiler_params=_params(1),
    )(cc, ada_w, ada_b_cols)


def ada_bwd(cc, dmod, ada_w):
    def body(cc_ref, d_ref, w_ref, gw_ref, ds_ref):
        cv = cc_ref[...]
        gw_ref[0] = _dot_hi(cv * _sigmoid(cv), d_ref[0], TN)
        ds_ref[0] = _dot_hi(d_ref[0], w_ref[0], NT)

    return pl.pallas_call(
        body, name="ada_bwd", grid=(DEPTH,),
        in_specs=[pl.BlockSpec((ADA_ROWS, D), lambda l: (0, 0)),
                  pl.BlockSpec((1, ADA_ROWS, ADA_COLS), lambda l: (l, 0, 0)),
                  pl.BlockSpec((1, D, ADA_COLS), lambda l: (l, 0, 0))],
        out_specs=[pl.BlockSpec((1, D, ADA_COLS), lambda l: (l, 0, 0)), pl.BlockSpec((1, ADA_ROWS, D), lambda l: (l, 0, 0))],
        out_shape=[jax.ShapeDtypeStruct((DEPTH, D, ADA_COLS), F32), jax.ShapeDtypeStruct((DEPTH, ADA_ROWS, D), F32)],
        compiler_params=_params(1),
    )(cc, dmod, ada_w)


def c_ctx_grad(parts, c_ctx8):
    def body(p_ref, c_ref, o_ref):
        acc = p_ref[0]
        for j in range(1, N_DEV):
            acc = acc + p_ref[j]
        cv = c_ref[...]
        sg = _sigmoid(cv)
        o_ref[...] = acc * (sg + cv * sg * (1.0 - sg))

    return pl.pallas_call(body, name="c_ctx_grad", out_shape=jax.ShapeDtypeStruct((8, LANE), F32))(parts, c_ctx8)


def sum_slots(parts):
    n_slot, rows, _ = parts.shape
    tr = 8
    for cand in (1024, 512, 256, 128, 64, 32, 16, 8):
        if rows % cand == 0:
            tr = cand
            break

    def body(p_ref, o_ref):
        acc = p_ref[0]
        for j in range(1, n_slot):
            acc = acc + p_ref[j]
        o_ref[...] = acc

    return pl.pallas_call(
        body, name="sum_slots", grid=(rows // tr,),
        in_specs=[pl.BlockSpec((n_slot, tr, LANE), lambda i: (0, i, 0))],
        out_specs=pl.BlockSpec((tr, LANE), lambda i: (i, 0)),
        out_shape=jax.ShapeDtypeStruct((rows, LANE), F32),
        compiler_params=_params(1),
    )(parts)


def adamw(parts, w, m, v, rows_blk, name):
    n_slot, n_l, n_a, n_b = parts.shape

    def body(p_ref, w_ref, m_ref, v_ref, g_ref, d_ref, m2_ref, v2_ref):
        g = p_ref[0].astype(F32)
        for j in range(1, n_slot):
            g = g + p_ref[j].astype(F32)
        m2 = ADAM_B1 * m_ref[...] + (1.0 - ADAM_B1) * g
        v2 = ADAM_B2 * v_ref[...] + (1.0 - ADAM_B2) * (g * g)
        m_hat = m2 / (1.0 - ADAM_B1 ** ADAM_STEP)
        v_hat = v2 / (1.0 - ADAM_B2 ** ADAM_STEP)
        g_ref[...] = g
        m2_ref[...] = m2
        v2_ref[...] = v2
        d_ref[...] = -ADAM_LR * (m_hat / (jnp.sqrt(v_hat) + ADAM_EPS) + ADAM_WD * w_ref[...])

    blk = pl.BlockSpec((1, rows_blk, n_b), lambda l, i: (l, i, 0))
    return pl.pallas_call(
        body, name=name, grid=(n_l, n_a // rows_blk),
        in_specs=[pl.BlockSpec((n_slot, 1, rows_blk, n_b), lambda l, i: (0, l, i, 0)), blk, blk, blk],
        out_specs=[blk] * 4,
        out_shape=[jax.ShapeDtypeStruct((n_l, n_a, n_b), F32)] * 4,
        compiler_params=_params(2),
    )(parts, w, m, v)


MESH = pl.DeviceIdType.MESH


def _my_index():
    return 4 * lax.axis_index("x") + 2 * lax.axis_index("y") + lax.axis_index("c")


def all_gather(shards, name):
    n = len(shards)

    def body(*refs):
        x_refs, out_refs = refs[:n], refs[n:2 * n]
        send_sems, recv_sems, local_sems = refs[2 * n:]
        x, y, c = lax.axis_index("x"), lax.axis_index("y"), lax.axis_index("c")
        me, sibling = (x, y, c), (x, y, 1 - c)
        chips = [(1 - x, y), (x, 1 - y), (1 - x, 1 - y)]

        def slot(a, px, py, pc):
            return out_refs[a].at[4 * px + 2 * py + pc]

        def copy(a, k, block, to, src=None):
            return pltpu.make_async_remote_copy(
                src_ref=slot(a, *block) if src is None else src, dst_ref=slot(a, *block),
                send_sem=send_sems.at[7 * a + k], recv_sem=recv_sems.at[7 * a + k], device_id=to, device_id_type=MESH)

        mine = [pltpu.make_async_copy(x_refs[a], slot(a, *me), local_sems.at[a]) for a in range(n)]
        for cp in mine:
            cp.start()
        first = []
        for a in range(n):
            first.append(copy(a, 0, me, sibling, src=x_refs[a]))
            first += [copy(a, 1 + j, me, (*chip, c), src=x_refs[a]) for j, chip in enumerate(chips)]
        for cp in first:
            cp.start()
        passed = []
        for j, chip in enumerate(chips):
            for a in range(n):
                copy(a, 1 + j, (*chip, c), me).wait_recv()
                fwd = copy(a, 4 + j, (*chip, c), sibling)
                fwd.start()
                passed.append(fwd)
        for a in range(n):
            copy(a, 0, sibling, me).wait_recv()
            for j, chip in enumerate(chips):
                copy(a, 4 + j, (*chip, 1 - c), me).wait_recv()
        for cp in first + passed:
            cp.wait_send()
        for cp in mine:
            cp.wait()

    return pl.pallas_call(
        body, name=name,
        in_specs=[pl.BlockSpec(memory_space=pl.ANY)] * n,
        out_specs=[pl.BlockSpec(memory_space=pl.ANY)] * n,
        out_shape=[jax.ShapeDtypeStruct((N_DEV,) + s.shape, s.dtype) for s in shards],
        scratch_shapes=[pltpu.SemaphoreType.DMA((7 * n,)), pltpu.SemaphoreType.DMA((7 * n,)),
                        pltpu.SemaphoreType.DMA((n,))],
    )(*shards)


def all_to_all(parts, name):
    n = len(parts)

    def body(*refs):
        p_refs, q_refs = refs[:n], refs[n:2 * n]
        send_sems, recv_sems, local_sems = refs[2 * n:]
        x, y, c = lax.axis_index("x"), lax.axis_index("y"), lax.axis_index("c")
        me = 4 * x + 2 * y + c
        mine = [pltpu.make_async_copy(p_refs[a].at[me], q_refs[a].at[me], local_sems.at[a]) for a in range(n)]
        for cp in mine:
            cp.start()
        copies = []
        for r in range(1, N_DEV):
            px, py, pc = x ^ (r >> 2), y ^ ((r >> 1) & 1), c ^ (r & 1)
            peer = 4 * px + 2 * py + pc
            for a in range(n):
                k = 7 * a + r - 1
                cp = pltpu.make_async_remote_copy(
                    src_ref=p_refs[a].at[peer], dst_ref=q_refs[a].at[me], send_sem=send_sems.at[k],
                    recv_sem=recv_sems.at[k], device_id=(px, py, pc), device_id_type=MESH)
                cp.start()
                copies.append((cp, a, peer, k))
        for cp, a, peer, k in copies:
            pltpu.make_async_remote_copy(
                src_ref=p_refs[a].at[peer], dst_ref=q_refs[a].at[peer], send_sem=send_sems.at[k],
                recv_sem=recv_sems.at[k], device_id=(x, y, c), device_id_type=MESH).wait_recv()
        for cp, _, _, _ in copies:
            cp.wait_send()
        for cp in mine:
            cp.wait()

    return pl.pallas_call(
        body, name=name,
        in_specs=[pl.BlockSpec(memory_space=pl.ANY)] * n,
        out_specs=[pl.BlockSpec(memory_space=pl.ANY)] * n,
        out_shape=[jax.ShapeDtypeStruct(p.shape, p.dtype) for p in parts],
        scratch_shapes=[pltpu.SemaphoreType.DMA((7 * n,)), pltpu.SemaphoreType.DMA((7 * n,)),
                        pltpu.SemaphoreType.DMA((n,))],
    )(*parts)


WEIGHT_ORDER = ("c_ctx", "ada_w", "ada_b", "norm_w", "ffn1_wgu", "ffn1_wd", "w_in", "dn_conv_w", "dn_a_log",
                "dn_dt_bias", "dn_norm_w", "w_ret_out", "w_dn_out", "w_o", "ffn2_wgu", "ffn2_wd", "final_norm_w")
BIG = (("ffn1_wgu", "col"), ("ffn1_wd", "row"), ("w_in", "col"), ("w_ret_out", "row"), ("w_dn_out", "row"),
       ("w_o", "row"), ("ffn2_wgu", "col"), ("ffn2_wd", "row"))
LOCAL = ("ada_w", "c_ctx", "ada_b", "norm_w", "dn_conv_w", "dn_a_log", "dn_dt_bias", "dn_norm_w", "final_norm_w")


def _pack(arrs, row_mult, lead=None):
    if lead is None:
        flat = jnp.concatenate([a.reshape(-1) for a in arrs])
        n = flat.shape[0]
    else:
        flat = jnp.concatenate([a.reshape(lead, -1) for a in arrs], axis=1)
        n = flat.shape[1]
    unit = LANE * row_mult
    total = -(-n // unit) * unit
    if lead is None:
        return jnp.pad(flat, (0, total - n)).reshape(total // LANE, LANE)
    return jnp.pad(flat, ((0, 0), (0, total - n))).reshape(lead, total // LANE, LANE)


def _unpack(packed, shapes, lead=None):
    flat = packed.reshape(-1) if lead is None else packed.reshape(lead, -1)
    out, off = [], 0
    for shp in shapes:
        n = math.prod(shp)
        if lead is None:
            out.append(flat[off:off + n].reshape(shp))
        else:
            out.append(flat[:, off:off + n].reshape((lead,) + tuple(shp)))
        off += n
    return out


def _join_shards(g, kind):
    lead = tuple(range(1, g.ndim - 2))
    a, b = g.shape[-2:]
    if kind == "col":
        return g.transpose(*lead, g.ndim - 2, 0, g.ndim - 1).reshape(g.shape[1:-2] + (a, N_DEV * b))
    return g.transpose(*lead, 0, g.ndim - 2, g.ndim - 1).reshape(g.shape[1:-2] + (N_DEV * a, b))


def _split_shards(full, kind):
    a, b = full.shape
    if kind == "col":
        return full.reshape(a, N_DEV, b // N_DEV).transpose(1, 0, 2)
    return full.reshape(N_DEV, a // N_DEV, b)


def _pad_w_in(w):
    return jnp.concatenate([w[..., :6144], w[..., 6160:8208], w[..., 6144:6160],
                            jnp.zeros(w.shape[:-1] + (PW - 8208,), w.dtype)], axis=-1)


def _unpad_w_in(g):
    return jnp.concatenate([g[..., :6144], g[..., 8192:8208], g[..., 6144:8192]], axis=-1)


def kernel(x, c, ctx, c_ctx, ada_w, ada_b, norm_w, ffn1_wgu, ffn1_wd, w_in, dn_conv_w, dn_a_log, dn_dt_bias, dn_norm_w, w_ret_out, w_dn_out, w_o, ffn2_wgu, ffn2_wd, final_norm_w, loss_target, m_c_ctx, m_ada_w, m_ada_b, m_norm_w, m_ffn1_wgu, m_ffn1_wd, m_w_in, m_dn_conv_w, m_dn_a_log, m_dn_dt_bias, m_dn_norm_w, m_w_ret_out, m_w_dn_out, m_w_o, m_ffn2_wgu, m_ffn2_wd, m_final_norm_w, v_c_ctx, v_ada_w, v_ada_b, v_norm_w, v_ffn1_wgu, v_ffn1_wd, v_w_in, v_dn_conv_w, v_dn_a_log, v_dn_dt_bias, v_dn_norm_w, v_w_ret_out, v_w_dn_out, v_w_o, v_ffn2_wgu, v_ffn2_wd, v_final_norm_w):
    w = dict(c_ctx=c_ctx, ada_w=ada_w, ada_b=ada_b, norm_w=norm_w, ffn1_wgu=ffn1_wgu, ffn1_wd=ffn1_wd, w_in=w_in, dn_conv_w=dn_conv_w, dn_a_log=dn_a_log, dn_dt_bias=dn_dt_bias, dn_norm_w=dn_norm_w, w_ret_out=w_ret_out, w_dn_out=w_dn_out, w_o=w_o, ffn2_wgu=ffn2_wgu, ffn2_wd=ffn2_wd, final_norm_w=final_norm_w)
    m = dict(c_ctx=m_c_ctx, ada_w=m_ada_w, ada_b=m_ada_b, norm_w=m_norm_w, ffn1_wgu=m_ffn1_wgu, ffn1_wd=m_ffn1_wd, w_in=m_w_in, dn_conv_w=m_dn_conv_w, dn_a_log=m_dn_a_log, dn_dt_bias=m_dn_dt_bias, dn_norm_w=m_dn_norm_w, w_ret_out=m_w_ret_out, w_dn_out=m_w_dn_out, w_o=m_w_o, ffn2_wgu=m_ffn2_wgu, ffn2_wd=m_ffn2_wd, final_norm_w=m_final_norm_w)
    v = dict(c_ctx=v_c_ctx, ada_w=v_ada_w, ada_b=v_ada_b, norm_w=v_norm_w, ffn1_wgu=v_ffn1_wgu, ffn1_wd=v_ffn1_wd, w_in=v_w_in, dn_conv_w=v_dn_conv_w, dn_a_log=v_dn_a_log, dn_dt_bias=v_dn_dt_bias, dn_norm_w=v_dn_norm_w, w_ret_out=v_w_ret_out, w_dn_out=v_w_dn_out, w_o=v_w_o, ffn2_wgu=v_ffn2_wgu, ffn2_wd=v_ffn2_wd, final_norm_w=v_final_norm_w)
    me = _my_index()
    big_names = [n for n, _ in BIG]

    g_big = all_gather([w[n].astype(BF16) for n in big_names], "ag_weights")
    full = {n: _join_shards(s, kind) for (n, kind), s in zip(BIG, g_big)}
    small_shapes = [norm_w.shape, dn_conv_w.shape, c.shape]
    g_small = all_gather([_pack([norm_w, dn_conv_w, c], 8)], "ag_small")[0]
    norm_s, conv_s, c_all = _unpack(g_small, small_shapes, lead=N_DEV)
    norm_full, conv_full = _join_shards(norm_s, "col"), _join_shards(conv_s, "col")

    cc = jnp.concatenate([c_all.reshape(N_DEV, D), c_ctx[None], jnp.zeros((ADA_ROWS - N_DEV - 1, D), F32)])
    ada_b_cols = lax.dynamic_slice_in_dim(ada_b, me * ADA_COLS, ADA_COLS, axis=1)[:, None, :]
    mods_part = ada_fwd(cc, ada_w, ada_b_cols)
    g_mods = all_gather([mods_part], "ag_mods")[0]
    mods_all = g_mods.transpose(1, 2, 0, 3).reshape(DEPTH, ADA_ROWS, N_MOD * D)
    mx = lax.dynamic_index_in_dim(mods_all, me, axis=1, keepdims=False).reshape(DEPTH, N_MOD, D)
    mc = mods_all[:, N_DEV].reshape(DEPTH, N_MOD, D)

    gparams = jnp.pad(jnp.stack([-jnp.exp(dn_a_log).reshape(DEPTH, 8), dn_dt_bias.reshape(DEPTH, 8)], axis=1),
                      ((0, 0), (0, 6), (8, LANE - 16)))
    stacked = dict(
        wgu1=full["ffn1_wgu"], wd1=full["ffn1_wd"], w_in=_pad_w_in(full["w_in"]), w_r=full["w_ret_out"],
        w_d=full["w_dn_out"], w_o=full["w_o"], wgu2=full["ffn2_wgu"], wd2=full["ffn2_wd"], norm_w=norm_full,
        conv_w8=jnp.pad(conv_full, ((0, 0), (0, 8 - CONV_K), (0, 0))), gparams=gparams,
        dn_norm_t=jnp.tile(dn_norm_w, (1, NH)), mx=mx, mc=mc)
    lws = [{k: a[l] for k, a in stacked.items()} for l in range(DEPTH)]
    loss_l, gx, gws, smalls, d_fnw = local_step(x[0], ctx[0], loss_target[0], final_norm_w, lws)

    sm_list = [jnp.stack([s[k] for s in smalls]) for k in ("dmx", "dmc", "norm_w", "conv_w", "a_log", "dt_bias",
                                                            "dn_norm_w")] + [d_fnw, loss_l.reshape(1)]
    sm_shapes = [a.shape for a in sm_list]
    g_sm = all_gather([_pack(sm_list, 8)], "ag_small_grads")[0]
    (dmx_sum, dmc_sum, g_norm, g_conv, g_alog, g_dtb, g_dnw, g_fnw, loss) = _unpack(sum_slots(g_sm), sm_shapes)
    dmx_all = _unpack(g_sm, sm_shapes[:1], lead=N_DEV)[0].reshape(N_DEV, DEPTH, N_MOD * D)
    dmc_sum = dmc_sum.reshape(DEPTH, N_MOD * D)
    dmx_sum = dmx_sum.reshape(DEPTH, N_MOD * D)
    dmod = jnp.concatenate([
        lax.dynamic_slice_in_dim(dmx_all, me * ADA_COLS, ADA_COLS, axis=2).transpose(1, 0, 2),
        lax.dynamic_slice_in_dim(dmc_sum, me * ADA_COLS, ADA_COLS, axis=1)[:, None, :],
        jnp.zeros((DEPTH, ADA_ROWS - N_DEV - 1, ADA_COLS), F32)], axis=1)
    g_ada_w, d_sil = ada_bwd(cc, dmod, ada_w)
    g_cc = all_gather([d_sil[:, N_DEV].sum(axis=0).reshape(8, LANE)], "ag_c_ctx")[0]
    grads = dict(
        ada_w=g_ada_w, c_ctx=c_ctx_grad(g_cc, c_ctx.reshape(8, LANE)).reshape(D), ada_b=dmx_sum + dmc_sum,
        norm_w=lax.dynamic_slice_in_dim(g_norm, me * (D // N_DEV), D // N_DEV, axis=2),
        dn_conv_w=lax.dynamic_slice_in_dim(g_conv, me * (2 * D // N_DEV), 2 * D // N_DEV, axis=2),
        dn_a_log=g_alog.reshape(dn_a_log.shape), dn_dt_bias=g_dtb.reshape(dn_dt_bias.shape), dn_norm_w=g_dnw,
        final_norm_w=g_fnw)

    def layer_full(gw):
        return dict(ffn1_wgu=jnp.concatenate([gw["wg1"], gw["wu1"]], axis=-1), ffn1_wd=gw["wd1"],
                    w_in=_unpad_w_in(gw["w_in"]), w_ret_out=gw["w_r"], w_dn_out=gw["w_d"], w_o=gw["w_o"],
                    ffn2_wgu=jnp.concatenate([gw["wg2"], gw["wu2"]], axis=-1), ffn2_wd=gw["wd2"])

    full_g = [layer_full(gw) for gw in gws]
    parts = [jnp.stack([_split_shards(fg[n], kind) for fg in full_g], axis=1).astype(BF16) for n, kind in BIG]
    recv = all_to_all(parts, "a2a_grads")
    res = {}
    for n, r in zip(big_names, recv):
        rows_blk = 256 if w[n].shape[1] % 256 == 0 else w[n].shape[1]
        res[n] = adamw(r, w[n], m[n], v[n], rows_blk, "adamw_" + n)
    res["ada_w"] = adamw(g_ada_w[None], ada_w, m["ada_w"], v["ada_w"], 256, "adamw_ada_w")
    small_names = [n for n in LOCAL if n != "ada_w"]
    small_pack = lambda d: _pack([d[n] for n in small_names], 8)[None]
    res_small = adamw(small_pack(grads)[None], small_pack(w), small_pack(m), small_pack(v),
                      small_pack(w).shape[1], "adamw_small")
    unpacked = [_unpack(r, [w[n].shape for n in small_names]) for r in res_small]
    for i, n in enumerate(small_names):
        res[n] = tuple(u[i] for u in unpacked)
    outs = [[res[n][k] for n in WEIGHT_ORDER] for k in range(4)]
    return (loss.reshape(()), gx[None], *outs[0], *outs[1], *outs[2], *outs[3])
```

```python
import functools
import math

import jax
import jax.numpy as jnp
from jax import lax
from jax.experimental import pallas as pl
from jax.experimental.pallas import tpu as pltpu

F32 = jnp.float32
BF16 = jnp.bfloat16

D = 1024
NH = 4
DK = 128
DV = 256
RET_C = 128
DN_C = 64
FFN_H = 2816
FFN_HB = 1408
N_MOD = 9
DEPTH = 4
N_DEV = 8
EPS = 1e-6
CONV_K = 5
HALO = 8

PW = 8320
C_RQK, C_RV, C_RG, C_DQK, C_DVV, C_DZ, C_GA, C_GB = 0, 1, 2, 3, 4, 5, 6, 7
C_DBA = 64
PROJ_TN = 1664

LANE = 128
VMEM_LIMIT = 56 * 1024 * 1024

ADAM_LR, ADAM_B1, ADAM_B2, ADAM_EPS, ADAM_WD, ADAM_STEP = 0.001, 0.9, 0.999, 1e-08, 0.01, 10

NN = ((1,), (0,))
NT = ((1,), (1,))
TN = ((0,), (0,))
HI = lax.Precision.HIGHEST


def _params(n_grid):
    return pltpu.CompilerParams(dimension_semantics=("arbitrary",) * n_grid, vmem_limit_bytes=VMEM_LIMIT)


def _dot(a, b, dims):
    return lax.dot_general(a.astype(BF16), b.astype(BF16), (dims, ((), ())), preferred_element_type=F32)


def _dot_hi(a, b, dims=NN):
    return lax.dot_general(a, b, (dims, ((), ())), precision=HI, preferred_element_type=F32)


@jax.custom_vjp
def mm_nn(a, b):
    return _dot(a, b, NN)


mm_nn.defvjp(lambda a, b: (_dot(a, b, NN), (a, b)),
             lambda r, g: (_dot(g, r[1], NT), _dot(r[0], g, TN)))


@jax.custom_vjp
def mm_nt(a, b):
    return _dot(a, b, NT)


mm_nt.defvjp(lambda a, b: (_dot(a, b, NT), (a, b)),
             lambda r, g: (_dot(g, r[1], NN), _dot(g, r[0], TN)))


@jax.custom_vjp
def mm_tn(a, b):
    return _dot(a, b, TN)


mm_tn.defvjp(lambda a, b: (_dot(a, b, TN), (a, b)),
             lambda r, g: (_dot(r[1], g, NT), _dot(r[0], g, NN)))


def _split_bf16(x, n):
    parts = []
    for _ in range(n):
        p = x.astype(BF16)
        parts.append(p)
        x = x - p.astype(F32)
    return parts


def _dot_f32(a, b, dims, exact=None):
    mm = lambda p, q: lax.dot_general(p, q, (dims, ((), ())), preferred_element_type=F32)
    if exact == "a":
        ab = a.astype(BF16)
        b1, b2, b3 = _split_bf16(b, 3)
        return mm(ab, b1) + (mm(ab, b2) + mm(ab, b3))
    if exact == "b":
        bb = b.astype(BF16)
        a1, a2, a3 = _split_bf16(a, 3)
        return mm(a1, bb) + (mm(a2, bb) + mm(a3, bb))
    a1, a2 = _split_bf16(a, 2)
    b1, b2 = _split_bf16(b, 2)
    return mm(a1, b1) + (mm(a1, b2) + mm(a2, b1))


@jax.custom_vjp
def _cum_rows(mask, x):
    return _dot_f32(mask, x, NN, exact="a")


_cum_rows.defvjp(lambda mask, x: (_dot_f32(mask, x, NN, exact="a"), mask),
                 lambda mask, g: (jnp.zeros_like(mask), _dot_f32(mask, g, TN, exact="a")))


@jax.custom_vjp
def _row_bcast(xb):
    return _dot_f32(jnp.full(xb.shape, 1.0 / LANE, F32), xb, NT, exact="a")


_row_bcast.defvjp(lambda xb: (_row_bcast(xb), None),
                  lambda _, g: (_dot_f32(g, jnp.full((g.shape[0], LANE), 1.0 / LANE, F32), TN, exact="b"),))


def _tri_inv_fwd(a):
    c = a.shape[0]
    eye = (lax.broadcasted_iota(jnp.int32, (c, c), 0) == lax.broadcasted_iota(jnp.int32, (c, c), 1)).astype(F32)
    pw = -a
    tinv = eye + pw
    m = 2
    while m < DN_C:
        pw = _dot_f32(pw, pw, NN)
        tinv = tinv + _dot_f32(tinv, pw, NN)
        m *= 2
    return tinv


@jax.custom_vjp
def _tri_inv(a):
    return _tri_inv_fwd(a)


def _tri_inv_bwd(tinv, g):
    return (-_dot_f32(_dot_f32(tinv, g, TN), tinv, NT),)


_tri_inv.defvjp(lambda a: (lambda t: (t, t))(_tri_inv_fwd(a)), _tri_inv_bwd)


def _normmod(x, nw, shift, scale):
    r = lax.rsqrt(jnp.mean(x * x, axis=-1, keepdims=True) + EPS)
    return (x * r * nw) * (1.0 + scale) + shift


def _sigmoid(x):
    return 1.0 / (1.0 + jnp.exp(-x))


def _softplus(x):
    return jnp.maximum(x, 0.0) + jnp.log(1.0 + jnp.exp(-jnp.abs(x)))


def _row_block(t_len):
    return 512 if t_len % 512 == 0 else 256


def ffn_fwd(x, vec, wgu, wd):
    t_len = x.shape[0]
    tm = _row_block(t_len)

    def body(x_ref, vec_ref, wg_ref, wu_ref, wd_ref, x1_ref, h_ref, g_ref, u_ref, f_ref, hs, acc):
        j = pl.program_id(1)

        @pl.when(j == 0)
        def _():
            hb = _normmod(x_ref[...], vec_ref[0:1, :], vec_ref[1:2, :], vec_ref[2:3, :]).astype(BF16)
            hs[...] = hb
            h_ref[...] = hb
            acc[...] = jnp.zeros_like(acc)

        hb = hs[...]
        g = jnp.dot(hb, wg_ref[...], preferred_element_type=F32)
        u = jnp.dot(hb, wu_ref[...], preferred_element_type=F32)
        g_ref[...] = g.astype(BF16)
        u_ref[...] = u.astype(BF16)
        act = g * _sigmoid(g) * u
        acc[...] += jnp.dot(act.astype(BF16), wd_ref[...], preferred_element_type=F32)

        @pl.when(j == 1)
        def _():
            f = acc[...]
            f_ref[...] = f.astype(BF16)
            x1_ref[...] = x_ref[...] + 0.5 * vec_ref[3:4, :] * f

    return pl.pallas_call(
        body, name="ffn_fwd", grid=(t_len // tm, 2),
        in_specs=[
            pl.BlockSpec((tm, D), lambda i, j: (i, 0)),
            pl.BlockSpec((8, D), lambda i, j: (0, 0)),
            pl.BlockSpec((D, FFN_HB), lambda i, j: (0, j)),
            pl.BlockSpec((D, FFN_HB), lambda i, j: (0, j + 2)),
            pl.BlockSpec((FFN_HB, D), lambda i, j: (j, 0)),
        ],
        out_specs=[
            pl.BlockSpec((tm, D), lambda i, j: (i, 0)),
            pl.BlockSpec((tm, D), lambda i, j: (i, 0)),
            pl.BlockSpec((tm, FFN_HB), lambda i, j: (i, j)),
            pl.BlockSpec((tm, FFN_HB), lambda i, j: (i, j)),
            pl.BlockSpec((tm, D), lambda i, j: (i, 0)),
        ],
        out_shape=[
            jax.ShapeDtypeStruct((t_len, D), F32),
            jax.ShapeDtypeStruct((t_len, D), BF16),
            jax.ShapeDtypeStruct((t_len, FFN_H), BF16),
            jax.ShapeDtypeStruct((t_len, FFN_H), BF16),
            jax.ShapeDtypeStruct((t_len, D), BF16),
        ],
        scratch_shapes=[pltpu.VMEM((tm, D), BF16), pltpu.VMEM((tm, D), F32)],
        compiler_params=_params(2),
    )(x, vec, wgu, wgu, wd)


def ffn_bwd_act(dx1, vec, g, u, f, wd):
    t_len = dx1.shape[0]
    tm = _row_block(t_len)

    def body(dx_ref, vec_ref, g_ref, u_ref, f_ref, wd_ref, act_ref, dg_ref, du_ref, dy_ref, pg_ref, dys):
        i, j = pl.program_id(0), pl.program_id(1)

        @pl.when((i == 0) & (j == 0))
        def _():
            pg_ref[...] = jnp.zeros_like(pg_ref)

        @pl.when(j == 0)
        def _():
            dx = dx_ref[...]
            dyb = (0.5 * vec_ref[3:4, :] * dx).astype(BF16)
            dys[...] = dyb
            dy_ref[...] = dyb
            pg_ref[3:4, :] += jnp.sum(0.5 * dx * f_ref[...].astype(F32), axis=0, keepdims=True)

        dact = _dot(dys[...], wd_ref[...], NT)
        gg = g_ref[...].astype(F32)
        uu = u_ref[...].astype(F32)
        sg = _sigmoid(gg)
        sl = gg * sg
        act_ref[...] = (sl * uu).astype(BF16)
        du_ref[...] = (dact * sl).astype(BF16)
        dg_ref[...] = (dact * uu * (sg + sl * (1.0 - sg))).astype(BF16)

    return pl.pallas_call(
        body, name="ffn_bwd_act", grid=(t_len // tm, 2),
        in_specs=[
            pl.BlockSpec((tm, D), lambda i, j: (i, 0)),
            pl.BlockSpec((8, D), lambda i, j: (0, 0)),
            pl.BlockSpec((tm, FFN_HB), lambda i, j: (i, j)),
            pl.BlockSpec((tm, FFN_HB), lambda i, j: (i, j)),
            pl.BlockSpec((tm, D), lambda i, j: (i, 0)),
            pl.BlockSpec((FFN_HB, D), lambda i, j: (j, 0)),
        ],
        out_specs=[
            pl.BlockSpec((tm, FFN_HB), lambda i, j: (i, j)),
            pl.BlockSpec((tm, FFN_HB), lambda i, j: (i, j)),
            pl.BlockSpec((tm, FFN_HB), lambda i, j: (i, j)),
            pl.BlockSpec((tm, D), lambda i, j: (i, 0)),
            pl.BlockSpec((8, D), lambda i, j: (0, 0)),
        ],
        out_shape=[
            jax.ShapeDtypeStruct((t_len, FFN_H), BF16),
            jax.ShapeDtypeStruct((t_len, FFN_H), BF16),
            jax.ShapeDtypeStruct((t_len, FFN_H), BF16),
            jax.ShapeDtypeStruct((t_len, D), BF16),
            jax.ShapeDtypeStruct((8, D), F32),
        ],
        scratch_shapes=[pltpu.VMEM((tm, D), BF16)],
        compiler_params=_params(2),
    )(dx1, vec, g, u, f, wd)


def nt_norm_bwd(dys, w, col_offsets, tk, n_steps, x_in, dres, vec, name):
    t_len = x_in.shape[0]
    tm = _row_block(t_len)
    n_seg = len(dys)

    def body(*refs):
        dy_refs = refs[:n_seg]
        w_refs = refs[n_seg:2 * n_seg]
        x_ref, dres_ref, vec_ref, dx_ref, pg_ref, acc = refs[2 * n_seg:]
        i, j = pl.program_id(0), pl.program_id(1)

        @pl.when((i == 0) & (j == 0))
        def _():
            pg_ref[...] = jnp.zeros_like(pg_ref)

        @pl.when(j == 0)
        def _():
            acc[...] = jnp.zeros_like(acc)

        part = _dot(dy_refs[0][...], w_refs[0][...], NT)
        for s in range(1, n_seg):
            part += _dot(dy_refs[s][...], w_refs[s][...], NT)
        acc[...] += part

        @pl.when(j == n_steps - 1)
        def _():
            _, vjp = jax.vjp(_normmod, x_ref[...], vec_ref[0:1, :], vec_ref[1:2, :], vec_ref[2:3, :])
            dxn, dnw, dsh, dsc = vjp(acc[...])
            dx_ref[...] = dres_ref[...] + dxn
            pg_ref[0:1, :] += dnw
            pg_ref[1:2, :] += dsh
            pg_ref[2:3, :] += dsc

    in_specs = [pl.BlockSpec((tm, tk), lambda i, j: (i, j)) for _ in range(n_seg)]
    in_specs += [pl.BlockSpec((D, tk), functools.partial(lambda i, j, off: (0, off + j), off=off))
                 for off in col_offsets]
    in_specs += [pl.BlockSpec((tm, D), lambda i, j: (i, 0)), pl.BlockSpec((tm, D), lambda i, j: (i, 0)),
                 pl.BlockSpec((8, D), lambda i, j: (0, 0))]
    return pl.pallas_call(
        body, name=name, grid=(t_len // tm, n_steps),
        in_specs=in_specs,
        out_specs=[pl.BlockSpec((tm, D), lambda i, j: (i, 0)), pl.BlockSpec((8, D), lambda i, j: (0, 0))],
        out_shape=[jax.ShapeDtypeStruct((t_len, D), F32), jax.ShapeDtypeStruct((8, D), F32)],
        scratch_shapes=[pltpu.VMEM((tm, D), F32)],
        compiler_params=_params(2),
    )(*dys, *([w] * n_seg), x_in, dres, vec)


def tn_matmul(a_x, b_x, a_c, b_c, tn, name):
    t_len, k_dim = a_x.shape
    n_dim = b_x.shape[1]
    t_ctx = a_c.shape[0]
    tt = 512
    n_t = t_len // tt

    def body(ax_ref, bx_ref, ac_ref, bc_ref, o_ref):
        t = pl.program_id(1)

        @pl.when(t == 0)
        def _():
            o_ref[...] = _dot(ac_ref[...], bc_ref[...], TN)

        o_ref[...] += _dot(ax_ref[...], bx_ref[...], TN)

    return pl.pallas_call(
        body, name=name, grid=(n_dim // tn, n_t),
        in_specs=[
            pl.BlockSpec((tt, k_dim), lambda n, t: (t, 0)),
            pl.BlockSpec((tt, tn), lambda n, t: (t, n)),
            pl.BlockSpec((t_ctx, k_dim), lambda n, t: (0, 0)),
            pl.BlockSpec((t_ctx, tn), lambda n, t: (0, n)),
        ],
        out_specs=pl.BlockSpec((k_dim, tn), lambda n, t: (0, n)),
        out_shape=jax.ShapeDtypeStruct((k_dim, n_dim), F32),
        compiler_params=_params(2),
    )(a_x, b_x, a_c, b_c)


def proj_fwd(x, vec, w_in_p):
    t_len = x.shape[0]
    tm = _row_block(t_len)

    def body(x_ref, vec_ref, w_ref, p_ref, h_ref, hs):
        @pl.when(pl.program_id(1) == 0)
        def _():
            hb = _normmod(x_ref[...], vec_ref[0:1, :], vec_ref[1:2, :], vec_ref[2:3, :]).astype(BF16)
            hs[...] = hb
            h_ref[...] = hb

        p_ref[...] = jnp.dot(hs[...], w_ref[...], preferred_element_type=F32)

    return pl.pallas_call(
        body, name="proj_fwd", grid=(t_len // tm, PW // PROJ_TN),
        in_specs=[
            pl.BlockSpec((tm, D), lambda i, j: (i, 0)),
            pl.BlockSpec((8, D), lambda i, j: (0, 0)),
            pl.BlockSpec((D, PROJ_TN), lambda i, j: (0, j)),
        ],
        out_specs=[pl.BlockSpec((tm, PROJ_TN), lambda i, j: (i, j)), pl.BlockSpec((tm, D), lambda i, j: (i, 0))],
        out_shape=[jax.ShapeDtypeStruct((t_len, PW), F32), jax.ShapeDtypeStruct((t_len, D), BF16)],
        scratch_shapes=[pltpu.VMEM((tm, D), BF16)],
        compiler_params=_params(2),
    )(x, vec, w_in_p)


def _shift_rows(e, s):
    n = e.shape[0]
    return pltpu.roll(e, (-s) % n, 0)


def _swap_halves(t):
    return pltpu.roll(t, DK // 2, 1)


def _halo_specs(tm, t_len, width, col, lead=False):
    per = tm // HALO
    last = t_len // HALO - 1
    if lead:
        return (pl.BlockSpec((2, HALO, width), lambda i: (0, jnp.maximum(i * per - 1, 0), col)),
                pl.BlockSpec((2, HALO, width), lambda i: (0, jnp.minimum((i + 1) * per, last), col)))
    return (pl.BlockSpec((HALO, width), lambda i: (jnp.maximum(i * per - 1, 0), col)),
            pl.BlockSpec((HALO, width), lambda i: (jnp.minimum((i + 1) * per, last), col)))


def _gate_cols(ba, gp_ref):
    lane = lax.broadcasted_iota(jnp.int32, ba.shape, 1)
    z = ba + gp_ref[1:2, :]
    return jnp.where(lane < 8, _sigmoid(ba), gp_ref[0:1, :] * _softplus(z))


def _conv_chunk(win, w_ref, c0, width):
    y = w_ref[0:1, c0:c0 + width] * _shift_rows(win, -2)
    for j in range(1, CONV_K):
        y += w_ref[j:j + 1, c0:c0 + width] * _shift_rows(win, j - 2)
    return y


def feat_fwd(p, cos2, sin2, conv_w8, gparams):
    t_len = p.shape[0]
    tm = _row_block(t_len)
    n_t = t_len // tm
    cw = 512

    def body(rqk_ref, dqk_ref, dqk_p, dqk_n, dvv_ref, dvv_p, dvv_n, dba_ref, cos_ref, sin_ref, cw_ref, gp_ref,
             o_rqk, o_dqkv, o_gbc):
        i = pl.program_id(0)
        cos, sin = cos_ref[...], sin_ref[...]
        for s in range(2 * NH):
            t = rqk_ref[:, s * DK:(s + 1) * DK]
            if s >= NH:
                t = t * (DK ** -0.5)
            o_rqk[:, s * DK:(s + 1) * DK] = t * cos + _swap_halves(t) * sin
        o_gbc[...] = _gate_cols(dba_ref[...], gp_ref)
        first, last = i == 0, i == n_t - 1
        for ci in range(4):
            src, sp, sn = (dqk_ref, dqk_p, dqk_n) if ci < 2 else (dvv_ref, dvv_p, dvv_n)
            c0 = (ci % 2) * cw
            win = jnp.concatenate([
                jnp.where(first, 0.0, sp[:, c0:c0 + cw]), src[:, c0:c0 + cw],
                jnp.where(last, 0.0, sn[:, c0:c0 + cw])], axis=0)
            y = _conv_chunk(win, cw_ref, ci * cw, cw)[HALO:HALO + tm]
            sv = y * _sigmoid(y)
            if ci < 2:
                scale = DK ** -0.5 if ci == 0 else 1.0
                for hh in range(NH):
                    sh = sv[:, hh * DK:(hh + 1) * DK]
                    nrm = lax.rsqrt(jnp.sum(sh * sh, axis=-1, keepdims=True) + EPS)
                    o_dqkv[:, ci * cw + hh * DK:ci * cw + (hh + 1) * DK] = sh * nrm * scale
            else:
                o_dqkv[:, ci * cw:(ci + 1) * cw] = sv

    hq = _halo_specs(tm, t_len, D, C_DQK)
    hv = _halo_specs(tm, t_len, D, C_DVV)
    return pl.pallas_call(
        body, name="feat_fwd", grid=(n_t,),
        in_specs=[
            pl.BlockSpec((tm, D), lambda i: (i, C_RQK)),
            pl.BlockSpec((tm, D), lambda i: (i, C_DQK)), hq[0], hq[1],
            pl.BlockSpec((tm, D), lambda i: (i, C_DVV)), hv[0], hv[1],
            pl.BlockSpec((tm, LANE), lambda i: (i, C_DBA)),
            pl.BlockSpec((tm, LANE), lambda i: (i, 0)),
            pl.BlockSpec((tm, LANE), lambda i: (i, 0)),
            pl.BlockSpec((8, 2 * D), lambda i: (0, 0)),
            pl.BlockSpec((8, LANE), lambda i: (0, 0)),
        ],
        out_specs=[pl.BlockSpec((tm, D), lambda i: (i, 0)), pl.BlockSpec((tm, 2 * D), lambda i: (i, 0)),
                   pl.BlockSpec((tm, LANE), lambda i: (i, 0))],
        out_shape=[jax.ShapeDtypeStruct((t_len, D), F32), jax.ShapeDtypeStruct((t_len, 2 * D), F32),
                   jax.ShapeDtypeStruct((t_len, LANE), F32)],
        compiler_params=_params(1),
    )(p, p, p, p, p, p, p, p, cos2, sin2, conv_w8, gparams)


def feat_bwd(p, cos2, sin2, conv_w8, gparams, d_rqk, d_rv, d_dqk, d_dvv, d_gbc):
    t_len = p.shape[0]
    tm = 256
    n_t = t_len // tm
    cw = 512

    def body(dqk_ref, dqk_p, dqk_n, dvv_ref, dvv_p, dvv_n, dba_ref, cos_ref, sin_ref, cw_ref, gp_ref,
             g_rqk, g_rv, g_dqk, g_dqk_p, g_dqk_n, g_dvv, g_dvv_p, g_dvv_n, g_gbc,
             o_rqk, o_rv, o_dqk, o_dvv, o_dba, o_cw, o_gp):
        i = pl.program_id(0)
        first, last = i == 0, i == n_t - 1

        @pl.when(first)
        def _():
            o_cw[...] = jnp.zeros_like(o_cw)
            o_gp[...] = jnp.zeros_like(o_gp)

        cos, sin = cos_ref[...], sin_ref[...]
        for s in range(2 * NH):
            gsl = g_rqk[0, :, s * DK:(s + 1) * DK] + g_rqk[1, :, s * DK:(s + 1) * DK]
            dt = gsl * cos + _swap_halves(gsl * sin)
            if s >= NH:
                dt = dt * (DK ** -0.5)
            o_rqk[:, s * DK:(s + 1) * DK] = dt.astype(BF16)
        o_rv[...] = (g_rv[0] + g_rv[1]).astype(BF16)

        ba = dba_ref[...]
        gg = g_gbc[0] + g_gbc[1]
        lane = lax.broadcasted_iota(jnp.int32, ba.shape, 1)
        sb = _sigmoid(ba)
        z = ba + gp_ref[1:2, :]
        a_row = gp_ref[0:1, :]
        dz = gg * a_row * _sigmoid(z)
        o_dba[...] = jnp.where(lane < 8, gg * sb * (1.0 - sb), dz).astype(BF16)
        is_g = (lane >= 8) & (lane < 16)
        o_gp[0:1, :] += jnp.sum(jnp.where(is_g, gg * a_row * _softplus(z), 0.0), axis=0, keepdims=True)
        o_gp[1:2, :] += jnp.sum(jnp.where(is_g, dz, 0.0), axis=0, keepdims=True)

        for ci in range(4):
            src, sp, sn = (dqk_ref, dqk_p, dqk_n) if ci < 2 else (dvv_ref, dvv_p, dvv_n)
            c0 = (ci % 2) * cw
            gc0 = ci * cw
            win = jnp.concatenate([
                jnp.where(first, 0.0, sp[:, c0:c0 + cw]), src[:, c0:c0 + cw],
                jnp.where(last, 0.0, sn[:, c0:c0 + cw])], axis=0)
            gs, gsp, gsn = (g_dqk, g_dqk_p, g_dqk_n) if ci < 2 else (g_dvv, g_dvv_p, g_dvv_n)
            gext = jnp.concatenate([
                jnp.where(first, 0.0, gsp[0, :, c0:c0 + cw] + gsp[1, :, c0:c0 + cw]),
                gs[0, :, c0:c0 + cw] + gs[1, :, c0:c0 + cw],
                jnp.where(last, 0.0, gsn[0, :, c0:c0 + cw] + gsn[1, :, c0:c0 + cw])], axis=0)
            y = _conv_chunk(win, cw_ref, gc0, cw)
            sg = _sigmoid(y)
            sv = y * sg
            if ci < 2:
                scale = DK ** -0.5 if ci == 0 else 1.0
                parts = []
                for hh in range(NH):
                    sh = sv[:, hh * DK:(hh + 1) * DK]
                    gh = gext[:, hh * DK:(hh + 1) * DK]
                    nrm = lax.rsqrt(jnp.sum(sh * sh, axis=-1, keepdims=True) + EPS)
                    dot = jnp.sum(gh * sh, axis=-1, keepdims=True)
                    parts.append(scale * nrm * (gh - sh * (nrm * nrm) * dot))
                ds = jnp.concatenate(parts, axis=1)
            else:
                ds = gext
            dy = ds * (sg + sv * (1.0 - sg))
            dpe = cw_ref[0:1, gc0:gc0 + cw] * _shift_rows(dy, 2)
            for j in range(1, CONV_K):
                dpe += cw_ref[j:j + 1, gc0:gc0 + cw] * _shift_rows(dy, 2 - j)
            dst = o_dqk if ci < 2 else o_dvv
            dst[:, c0:c0 + cw] = dpe[HALO:HALO + tm].astype(BF16)
            dyc = dy[HALO:HALO + tm]
            for j in range(CONV_K):
                o_cw[j:j + 1, gc0:gc0 + cw] += jnp.sum(dyc * _shift_rows(win, j - 2)[HALO:HALO + tm], axis=0,
                                                      keepdims=True)

    hq = _halo_specs(tm, t_len, D, C_DQK)
    hv = _halo_specs(tm, t_len, D, C_DVV)
    hg = _halo_specs(tm, t_len, D, 0, lead=True)
    outs = pl.pallas_call(
        body, name="feat_bwd", grid=(n_t,),
        in_specs=[
            pl.BlockSpec((tm, D), lambda i: (i, C_DQK)), hq[0], hq[1],
            pl.BlockSpec((tm, D), lambda i: (i, C_DVV)), hv[0], hv[1],
            pl.BlockSpec((tm, LANE), lambda i: (i, C_DBA)),
            pl.BlockSpec((tm, LANE), lambda i: (i, 0)),
            pl.BlockSpec((tm, LANE), lambda i: (i, 0)),
            pl.BlockSpec((8, 2 * D), lambda i: (0, 0)),
            pl.BlockSpec((8, LANE), lambda i: (0, 0)),
            pl.BlockSpec((2, tm, D), lambda i: (0, i, 0)),
            pl.BlockSpec((2, tm, D), lambda i: (0, i, 0)),
            pl.BlockSpec((2, tm, D), lambda i: (0, i, 0)), hg[0], hg[1],
            pl.BlockSpec((2, tm, D), lambda i: (0, i, 0)), hg[0], hg[1],
            pl.BlockSpec((2, tm, LANE), lambda i: (0, i, 0)),
        ],
        out_specs=[pl.BlockSpec((tm, D), lambda i: (i, 0))] * 4
        + [pl.BlockSpec((tm, LANE), lambda i: (i, 0)),
           pl.BlockSpec((8, 2 * D), lambda i: (0, 0)),
           pl.BlockSpec((8, LANE), lambda i: (0, 0))],
        out_shape=[jax.ShapeDtypeStruct((t_len, D), BF16)] * 4
        + [jax.ShapeDtypeStruct((t_len, LANE), BF16),
           jax.ShapeDtypeStruct((8, 2 * D), F32), jax.ShapeDtypeStruct((8, LANE), F32)],
        compiler_params=_params(1),
    )(p, p, p, p, p, p, p, cos2, sin2, conv_w8, gparams, d_rqk, d_rv, d_dqk, d_dqk, d_dqk, d_dvv, d_dvv, d_dvv,
      d_gbc)
    return outs


def _ret_chunk(q, k, v, s, lg, rev):
    c = q.shape[0]
    ii = lax.broadcasted_iota(jnp.int32, (c, c), 0).astype(F32)
    jj = lax.broadcasted_iota(jnp.int32, (c, c), 1).astype(F32)
    diff = jnp.where(rev, jj - ii, ii - jj)
    mask = diff >= jnp.where(rev, 1.0, 0.0)
    dec =jnp.where(mask, jnp.exp(lg * jnp.where(mask, diff, 0.0)), 0.0)
    idx = lax.broadcasted_iota(jnp.int32, (c, 1), 0).astype(F32)
    idx = jnp.where(rev, c - 1.0 - idx, idx)
    kdec = k * jnp.exp(lg * (c - 1.0 - idx))
    qdec = q * jnp.exp(lg * (idx + 1.0))
    o = mm_nn(mm_nt(q, k) * dec, v) + mm_nn(qdec, s)
    s2 = s * math.exp(lg * c) + mm_tn(kdec, v)
    return o, s2


def _log_gamma(h):
    return math.log1p(-(2.0 ** (-5.0 - h)))


DN_W = 3 * NH * DK
HS = NH * DN_C


def _dn_intra4(q4, k4, v4, g4, beta4, rev):
    ii = lax.broadcasted_iota(jnp.int32, (HS, HS), 0)
    jj = lax.broadcasted_iota(jnp.int32, (HS, HS), 1)
    blk = (ii // DN_C) == (jj // DN_C)
    dd = jnp.where(rev, jj - ii, ii - jj)
    incl = blk & (dd >= 0)
    strict = blk & (dd > 0)
    gb = jnp.broadcast_to(g4, (HS, LANE))
    gcb = _cum_rows(incl.astype(F32), gb)
    gc = jnp.max(gcb, axis=1, keepdims=True)
    gcr = _row_bcast(gcb)
    glast = jnp.max(_cum_rows(blk.astype(F32), gb), axis=1, keepdims=True)
    decay = jnp.where(incl, jnp.exp(jnp.where(incl, gc - gcr, 0.0)), 0.0)
    kb = k4 * beta4
    a = jnp.where(strict, mm_nt(kb, k4) * decay, 0.0)
    tinv = _tri_inv(a)
    u = mm_nn(tinv, v4 * beta4)
    w = mm_nn(tinv, kb * jnp.exp(gc))
    attn = jnp.where(incl, mm_nt(q4, k4) * decay, 0.0)
    return u, w, q4 * jnp.exp(gc), k4 * jnp.exp(glast - gc), attn


def _dn_seq4(u, w, qg, kd, attn, glast, s):
    v_new = [uh - mm_nn(wh, sh) for uh, wh, sh in zip(u, w, s)]
    s2 = [sh * jnp.exp(gh) + mm_tn(kh, vh) for sh, gh, kh, vh in zip(s, glast, kd, v_new)]
    o = jnp.concatenate([mm_nn(qh, sh) for qh, sh in zip(qg, s)], axis=0) + mm_nn(attn, jnp.concatenate(v_new, axis=0))
    return o, s2


def _stack_heads(ref_rows, width, col0=0):
    return jnp.concatenate([ref_rows(slice(col0 + h * width, col0 + (h + 1) * width)) for h in range(NH)], axis=0)


def _head_rows(h):
    return slice(h * DN_C, (h + 1) * DN_C)


def _dn_gates4(gbv, d):
    g4 = jnp.concatenate([_pick_lane(gbv, 8 + 4 * d + h) for h in range(NH)], axis=0)
    b4 = jnp.concatenate([_pick_lane(gbv, 4 * d + h) for h in range(NH)], axis=0)
    return g4, b4


def dn_intra_fwd(dqkv, gbc):
    t_len = dqkv.shape[0]
    rb = 256
    n_g = rb // DN_C

    def body(qk_ref, v_ref, gb_ref, u_ref, wqk_ref, at_ref):
        d = pl.program_id(0)
        rev = d == 1
        for gi in range(n_g):
            rows = slice(gi * DN_C, (gi + 1) * DN_C)
            g4, b4 = _dn_gates4(gb_ref[rows, :], d)
            u, w, qg, kd, attn = _dn_intra4(_stack_heads(lambda cs: qk_ref[rows, cs], DK),
                                            _stack_heads(lambda cs: qk_ref[rows, cs], DK, NH * DK),
                                            _stack_heads(lambda cs: v_ref[rows, cs], DV), g4, b4, rev)
            at_ref[0, gi] = attn.astype(BF16)
            for h in range(NH):
                hr = _head_rows(h)
                u_ref[0, rows, h * DV:(h + 1) * DV] = u[hr]
                wqk_ref[0, rows, h * DK:(h + 1) * DK] = w[hr].astype(BF16)
                wqk_ref[0, rows, NH * DK + h * DK:NH * DK + (h + 1) * DK] = qg[hr].astype(BF16)
                wqk_ref[0, rows, 2 * NH * DK + h * DK:2 * NH * DK + (h + 1) * DK] = kd[hr].astype(BF16)

    return pl.pallas_call(
        body, name="dn_intra_fwd", grid=(2, t_len // rb),
        in_specs=[pl.BlockSpec((rb, D), lambda d, i: (i, 0)), pl.BlockSpec((rb, D), lambda d, i: (i, 1)),
                  pl.BlockSpec((rb, LANE), lambda d, i: (i, 0))],
        out_specs=[pl.BlockSpec((1, rb, D), lambda d, i: (d, i, 0)),
                   pl.BlockSpec((1, rb, DN_W), lambda d, i: (d, i, 0)),
                   pl.BlockSpec((1, n_g, HS, HS), lambda d, i: (d, i, 0, 0))],
        out_shape=[jax.ShapeDtypeStruct((2, t_len, D), F32), jax.ShapeDtypeStruct((2, t_len, DN_W), BF16),
                   jax.ShapeDtypeStruct((2, t_len // DN_C, HS, HS), BF16)],
        compiler_params=_params(2),
    )(dqkv, dqkv, gbc)


def _dn_seq_inputs(u_ref, wqk_ref, at_ref, gbv, d):
    u = [u_ref[0, :, h * DV:(h + 1) * DV] for h in range(NH)]
    w = [wqk_ref[0, :, h * DK:(h + 1) * DK].astype(F32) for h in range(NH)]
    qg = [wqk_ref[0, :, NH * DK + h * DK:NH * DK + (h + 1) * DK].astype(F32) for h in range(NH)]
    kd = [wqk_ref[0, :, 2 * NH * DK + h * DK:2 * NH * DK + (h + 1) * DK].astype(F32) for h in range(NH)]
    glast = [jnp.sum(_pick_lane(gbv, 8 + 4 * d + h), axis=0, keepdims=True) for h in range(NH)]
    return u, w, qg, kd, at_ref[0, 0].astype(F32), glast


def dn_seq_fwd(u, wqk, attn, gbc, s0):
    t_len = u.shape[1]
    n_c = t_len // DN_C

    def chunk_of(d, t):
        return jnp.where(d == 0, t, n_c - 1 - t)

    def body(u_ref, wqk_ref, at_ref, gb_ref, s0_ref, o_ref, ss_ref, sf_ref, st):
        d, t = pl.program_id(0), pl.program_id(1)

        @pl.when(t == 0)
        def _():
            st[...] = s0_ref[0]

        s_in = [st[h] for h in range(NH)]
        ss_ref[0, 0] = st[...].astype(BF16)
        o, s2 = _dn_seq4(*_dn_seq_inputs(u_ref, wqk_ref, at_ref, gb_ref[...], d), s_in)
        for h in range(NH):
            o_ref[0, :, h * DV:(h + 1) * DV] = o[_head_rows(h)]
            st[h] = s2[h]

        @pl.when(t == n_c - 1)
        def _():
            sf_ref[0] = st[...]

    return pl.pallas_call(
        body, name="dn_seq_fwd", grid=(2, n_c),
        in_specs=[pl.BlockSpec((1, DN_C, D), lambda d, t: (d, chunk_of(d, t), 0)),
                  pl.BlockSpec((1, DN_C, DN_W), lambda d, t: (d, chunk_of(d, t), 0)),
                  pl.BlockSpec((1, 1, HS, HS), lambda d, t: (d, chunk_of(d, t), 0, 0)),
                  pl.BlockSpec((DN_C, LANE), lambda d, t: (chunk_of(d, t), 0)),
                  _state_spec()],
        out_specs=[pl.BlockSpec((1, DN_C, D), lambda d, t: (d, chunk_of(d, t), 0)),
                   pl.BlockSpec((1, 1, NH, DK, DV), lambda d, t: (d, chunk_of(d, t), 0, 0, 0)),
                   _state_spec()],
        out_shape=[jax.ShapeDtypeStruct((2, t_len, D), F32),
                   jax.ShapeDtypeStruct((2, n_c, NH, DK, DV), BF16),
                   jax.ShapeDtypeStruct((2, NH, DK, DV), F32)],
        scratch_shapes=[pltpu.VMEM((NH, DK, DV), F32)],
        compiler_params=_params(2),
    )(u, wqk, attn, gbc, s0)


def dn_seq_bwd(u, wqk, attn, gbc, ssave, do, ds_fin):
    t_len = u.shape[1]
    n_c = t_len // DN_C

    def chunk_of(d, t):
        return jnp.where(d == 0, n_c - 1 - t, t)

    def body(u_ref, wqk_ref, at_ref, gb_ref, ss_ref, do_ref, dsf_ref, du_ref, dwqk_ref, dat_ref, dgl_ref, ds0_ref, dst):
        d, t = pl.program_id(0), pl.program_id(1)

        @pl.when(t == 0)
        def _():
            dst[...] = dsf_ref[0]

        s_in = [ss_ref[0, 0, h].astype(F32) for h in range(NH)]
        _, vjp = jax.vjp(_dn_seq4, *_dn_seq_inputs(u_ref, wqk_ref, at_ref, gb_ref[...], d), s_in)
        do4 = _stack_heads(lambda cs: do_ref[:, cs], DV)
        du, dw, dqg, dkd, dat, dgl, ds = vjp((do4, [dst[h] for h in range(NH)]))
        dat_ref[0, 0] = dat
        rows8 = lax.broadcasted_iota(jnp.int32, (8, LANE), 0)
        dgl_tile = jnp.zeros((8, LANE), F32)
        for h in range(NH):
            du_ref[0, :, h * DV:(h + 1) * DV] = du[h]
            dwqk_ref[0, :, h * DK:(h + 1) * DK] = dw[h]
            dwqk_ref[0, :, NH * DK + h * DK:NH * DK + (h + 1) * DK] = dqg[h]
            dwqk_ref[0, :, 2 * NH * DK + h * DK:2 * NH * DK + (h + 1) * DK] = dkd[h]
            dgl_tile = jnp.where(rows8 == h, dgl[h], dgl_tile)
            dst[h] = ds[h]
        dgl_ref[0, 0] = dgl_tile

        @pl.when(t == n_c - 1)
        def _():
            ds0_ref[0] = dst[...]

    seq = lambda width: pl.BlockSpec((1, DN_C, width), lambda d, t: (d, chunk_of(d, t), 0))
    att = pl.BlockSpec((1, 1, HS, HS), lambda d, t: (d, chunk_of(d, t), 0, 0))
    return pl.pallas_call(
        body, name="dn_seq_bwd", grid=(2, n_c),
        in_specs=[seq(D), seq(DN_W), att, pl.BlockSpec((DN_C, LANE), lambda d, t: (chunk_of(d, t), 0)),
                  pl.BlockSpec((1, 1, NH, DK, DV), lambda d, t: (d, chunk_of(d, t), 0, 0, 0)),
                  pl.BlockSpec((DN_C, D), lambda d, t: (chunk_of(d, t), 0)), _state_spec()],
        out_specs=[seq(D), seq(DN_W), att, pl.BlockSpec((1, 1, 8, LANE), lambda d, t: (d, chunk_of(d, t), 0, 0)),
                   _state_spec()],
        out_shape=[jax.ShapeDtypeStruct((2, t_len, D), F32), jax.ShapeDtypeStruct((2, t_len, DN_W), F32),
                   jax.ShapeDtypeStruct((2, n_c, HS, HS), F32), jax.ShapeDtypeStruct((2, n_c, 8, LANE), F32),
                   jax.ShapeDtypeStruct((2, NH, DK, DV), F32)],
        scratch_shapes=[pltpu.VMEM((NH, DK, DV), F32)],
        compiler_params=_params(2),
    )(u, wqk, attn, gbc, ssave, do, ds_fin)


def dn_intra_bwd(dqkv, gbc, du, dwqk, dattn, dgl):
    t_len = dqkv.shape[0]
    rb = 128
    n_g = rb // DN_C

    def body(qk_ref, v_ref, gb_ref, du_ref, dwqk_ref, dat_ref, dgl_ref, dqk_ref, dv_ref, dgb_ref):
        d = pl.program_id(0)
        rev = d == 1
        for gi in range(n_g):
            rows = slice(gi * DN_C, (gi + 1) * DN_C)
            g4, b4 = _dn_gates4(gb_ref[rows, :], d)
            _, vjp = jax.vjp(functools.partial(_dn_intra4, rev=rev),
                             _stack_heads(lambda cs: qk_ref[rows, cs], DK),
                             _stack_heads(lambda cs: qk_ref[rows, cs], DK, NH * DK),
                             _stack_heads(lambda cs: v_ref[rows, cs], DV), g4, b4)
            dq, dk, dv, dg, db = vjp((_stack_heads(lambda cs: du_ref[0, rows, cs], DV),
                                      _stack_heads(lambda cs: dwqk_ref[0, rows, cs], DK),
                                      _stack_heads(lambda cs: dwqk_ref[0, rows, cs], DK, NH * DK),
                                      _stack_heads(lambda cs: dwqk_ref[0, rows, cs], DK, 2 * NH * DK),
                                      dat_ref[0, gi]))
            dgb = jnp.zeros((DN_C, LANE), F32)
            for h in range(NH):
                hr = _head_rows(h)
                dqk_ref[0, rows, h * DK:(h + 1) * DK] = dq[hr]
                dqk_ref[0, rows, NH * DK + h * DK:NH * DK + (h + 1) * DK] = dk[hr]
                dv_ref[0, rows, h * DV:(h + 1) * DV] = dv[hr]
                dgb = (dgb + _put_lane(dg[hr] + dgl_ref[0, gi, h:h + 1, 0:1], 8 + 4 * d + h, LANE)
                       + _put_lane(db[hr], 4 * d + h, LANE))
            dgb_ref[0, rows, :] = dgb

    seq = lambda width: pl.BlockSpec((1, rb, width), lambda d, i: (d, i, 0))
    return pl.pallas_call(
        body, name="dn_intra_bwd", grid=(2, t_len // rb),
        in_specs=[pl.BlockSpec((rb, D), lambda d, i: (i, 0)), pl.BlockSpec((rb, D), lambda d, i: (i, 1)),
                  pl.BlockSpec((rb, LANE), lambda d, i: (i, 0)), seq(D), seq(DN_W),
                  pl.BlockSpec((1, n_g, HS, HS), lambda d, i: (d, i, 0, 0)),
                  pl.BlockSpec((1, n_g, 8, LANE), lambda d, i: (d, i, 0, 0))],
        out_specs=[seq(D), seq(D), seq(LANE)],
        out_shape=[jax.ShapeDtypeStruct((2, t_len, D), F32), jax.ShapeDtypeStruct((2, t_len, D), F32),
                   jax.ShapeDtypeStruct((2, t_len, LANE), F32)],
        compiler_params=_params(2),
    )(dqkv, dqkv, gbc, du, dwqk, dattn, dgl)


def _pick_lane(x, lane_idx):
    lane = lax.broadcasted_iota(jnp.int32, x.shape, 1)
    return jnp.sum(jnp.where(lane == lane_idx, x, 0.0), axis=1, keepdims=True)


def _put_lane(col, lane_idx, width):
    lane = lax.broadcasted_iota(jnp.int32, (col.shape[0], width), 1)
    return jnp.where(lane == lane_idx, col, 0.0)


def _state_spec():
    return pl.BlockSpec((1, NH, DK, DV), lambda d, t: (d, 0, 0, 0))


def ret_fwd(rqk, p, s0):
    c = RET_C
    t_len = rqk.shape[0]
    n_c = t_len // c

    def chunk_of(d, t):
        return jnp.where(d == 0, t, n_c - 1 - t)

    def body(qk_ref, v_ref, s0_ref, o_ref, ss_ref, sf_ref, st):
        d, t = pl.program_id(0), pl.program_id(1)
        rev = d == 1

        @pl.when(t == 0)
        def _():
            st[...] = s0_ref[0]

        for h in range(NH):
            sh = st[h]
            ss_ref[0, 0, h] = sh.astype(BF16)
            o, s2 = _ret_chunk(qk_ref[:, h * DK:(h + 1) * DK], qk_ref[:, NH * DK + h * DK:NH * DK + (h + 1) * DK],
                               v_ref[:, h * DV:(h + 1) * DV], sh, _log_gamma(h), rev)
            o_ref[0, :, h * DV:(h + 1) * DV] = o
            st[h] = s2

        @pl.when(t == n_c - 1)
        def _():
            sf_ref[0] = st[...]

    return pl.pallas_call(
        body, name="ret_fwd", grid=(2, n_c),
        in_specs=[pl.BlockSpec((c, D), lambda d, t: (chunk_of(d, t), 0)),
                  pl.BlockSpec((c, D), lambda d, t: (chunk_of(d, t), C_RV)),
                  _state_spec()],
        out_specs=[pl.BlockSpec((1, c, D), lambda d, t: (d, chunk_of(d, t), 0)),
                   pl.BlockSpec((1, 1, NH, DK, DV), lambda d, t: (d, chunk_of(d, t), 0, 0, 0)),
                   _state_spec()],
        out_shape=[jax.ShapeDtypeStruct((2, t_len, D), F32),
                   jax.ShapeDtypeStruct((2, n_c, NH, DK, DV), BF16),
                   jax.ShapeDtypeStruct((2, NH, DK, DV), F32)],
        scratch_shapes=[pltpu.VMEM((NH, DK, DV), F32)],
        compiler_params=_params(2),
    )(rqk, p, s0)


def ret_bwd(rqk, p, ssave, do, ds_fin):
    c = RET_C
    t_len = rqk.shape[0]
    n_c = t_len // c

    def chunk_of(d, t):
        return jnp.where(d == 0, n_c - 1 - t, t)

    def body(qk_ref, v_ref, ss_ref, do_ref, dsf_ref, dqk_ref, dv_ref, ds0_ref, dst):
        d, t = pl.program_id(0), pl.program_id(1)
        rev = d == 1

        @pl.when(t == 0)
        def _():
            dst[...] = dsf_ref[0]

        for h in range(NH):
            _, vjp = jax.vjp(functools.partial(_ret_chunk, lg=_log_gamma(h), rev=rev),
                             qk_ref[:, h * DK:(h + 1) * DK], qk_ref[:, NH * DK + h * DK:NH * DK + (h + 1) * DK],
                             v_ref[:, h * DV:(h + 1) * DV], ss_ref[0, 0, h].astype(F32))
            dq, dk, dv, ds = vjp((do_ref[:, h * DV:(h + 1) * DV], dst[h]))
            dqk_ref[0, :, h * DK:(h + 1) * DK] = dq
            dqk_ref[0, :, NH * DK + h * DK:NH * DK + (h + 1) * DK] = dk
            dv_ref[0, :, h * DV:(h + 1) * DV] = dv
            dst[h] = ds

        @pl.when(t == n_c - 1)
        def _():
            ds0_ref[0] = dst[...]

    seq_spec = pl.BlockSpec((1, c, D), lambda d, t: (d, chunk_of(d, t), 0))
    return pl.pallas_call(
        body, name="ret_bwd", grid=(2, n_c),
        in_specs=[pl.BlockSpec((c, D), lambda d, t: (chunk_of(d, t), 0)),
                  pl.BlockSpec((c, D), lambda d, t: (chunk_of(d, t), C_RV)),
                  pl.BlockSpec((1, 1, NH, DK, DV), lambda d, t: (d, chunk_of(d, t), 0, 0, 0)),
                  pl.BlockSpec((c, D), lambda d, t: (chunk_of(d, t), 0)),
                  _state_spec()],
        out_specs=[seq_spec, seq_spec, _state_spec()],
        out_shape=[jax.ShapeDtypeStruct((2, t_len, D), F32)] * 2 + [jax.ShapeDtypeStruct((2, NH, DK, DV), F32)],
        scratch_shapes=[pltpu.VMEM((NH, DK, DV), F32)],
        compiler_params=_params(2),
    )(rqk, p, ssave, do, ds_fin)


def _head_gate(o_ret, o_dn, rg, dz, nw):
    ret = o_ret * lax.rsqrt(jnp.mean(o_ret * o_ret, axis=-1, keepdims=True) + EPS) * (rg * _sigmoid(rg))
    dn = o_dn * lax.rsqrt(jnp.mean(o_dn * o_dn, axis=-1, keepdims=True) + EPS) * nw * (dz * _sigmoid(dz))
    return ret, dn


MIX_TM = 256


def _mix_specs():
    seq = lambda col: pl.BlockSpec((MIX_TM, D), functools.partial(lambda i, col: (i, col), col=col))
    pair = pl.BlockSpec((2, MIX_TM, D), lambda i: (0, i, 0))
    wfull = pl.BlockSpec((D, D), lambda i: (0, 0))
    vec = pl.BlockSpec((8, D), lambda i: (0, 0))
    return seq, pair, wfull, vec


def mixout_fwd(x1, vec, o_ret, o_dn, p, w_r, w_d, w_o):
    t_len = x1.shape[0]

    def body(x_ref, vec_ref, or_ref, od_ref, rg_ref, dz_ref, ga_ref, gb_ref, wr_ref, wd_ref, wo_ref,
             x2_ref, ret_ref, dn_ref, y_ref, yr_ref, yd_ref, z_ref):
        for h in range(NH):
            sl = slice(h * DV, (h + 1) * DV)
            ret, dn = _head_gate(or_ref[0, :, sl] + or_ref[1, :, sl], od_ref[0, :, sl] + od_ref[1, :, sl],
                                 rg_ref[:, sl], dz_ref[:, sl], vec_ref[1:2, sl])
            ret_ref[:, sl] = ret.astype(BF16)
            dn_ref[:, sl] = dn.astype(BF16)
        yr = jnp.dot(ret_ref[...], wr_ref[...], preferred_element_type=F32)
        yd = jnp.dot(dn_ref[...], wd_ref[...], preferred_element_type=F32)
        yr_ref[...] = yr.astype(BF16)
        yd_ref[...] = yd.astype(BF16)
        y = (_sigmoid(ga_ref[...]) * yr + _sigmoid(gb_ref[...]) * yd).astype(BF16)
        y_ref[...] = y
        z = jnp.dot(y, wo_ref[...], preferred_element_type=F32)
        z_ref[...] = z.astype(BF16)
        x2_ref[...] = x_ref[...] + vec_ref[0:1, :] * z

    seq, pair, wfull, vecs = _mix_specs()
    return pl.pallas_call(
        body, name="mixout_fwd", grid=(t_len // MIX_TM,),
        in_specs=[seq(0), vecs, pair, pair, seq(C_RG), seq(C_DZ), seq(C_GA), seq(C_GB), wfull, wfull, wfull],
        out_specs=[seq(0)] * 7,
        out_shape=[jax.ShapeDtypeStruct((t_len, D), F32)] + [jax.ShapeDtypeStruct((t_len, D), BF16)] * 6,
        compiler_params=_params(1),
    )(x1, vec, o_ret, o_dn, p, p, p, p, w_r, w_d, w_o)


def mixout_bwd(dx2, vec, o_ret, o_dn, p, yr, yd, z, w_r, w_d, w_o):
    t_len = dx2.shape[0]

    def body(dx_ref, vec_ref, or_ref, od_ref, rg_ref, dz_ref, ga_ref, gb_ref, yr_ref, yd_ref, z_ref,
             wr_ref, wd_ref, wo_ref,
             dor_ref, dod_ref, drg_ref, ddz_ref, dga_ref, dgb_ref, dyr_ref, dyd_ref, dzz_ref, pg_ref):
        @pl.when(pl.program_id(0) == 0)
        def _():
            pg_ref[...] = jnp.zeros_like(pg_ref)

        dx = dx_ref[...]
        pg_ref[0:1, :] += jnp.sum(dx * z_ref[...].astype(F32), axis=0, keepdims=True)
        dzz = (vec_ref[0:1, :] * dx).astype(BF16)
        dzz_ref[...] = dzz
        dy = _dot(dzz, wo_ref[...], NT)
        sa = _sigmoid(ga_ref[...])
        sb = _sigmoid(gb_ref[...])
        dyr = (dy * sa).astype(BF16)
        dyd = (dy * sb).astype(BF16)
        dyr_ref[...] = dyr
        dyd_ref[...] = dyd
        dga_ref[...] = (dy * yr_ref[...].astype(F32) * sa * (1.0 - sa)).astype(BF16)
        dgb_ref[...] = (dy * yd_ref[...].astype(F32) * sb * (1.0 - sb)).astype(BF16)
        dret = _dot(dyr, wr_ref[...], NT)
        ddn = _dot(dyd, wd_ref[...], NT)
        for h in range(NH):
            sl = slice(h * DV, (h + 1) * DV)
            _, vjp = jax.vjp(_head_gate, or_ref[0, :, sl] + or_ref[1, :, sl], od_ref[0, :, sl] + od_ref[1, :, sl],
                             rg_ref[:, sl], dz_ref[:, sl], vec_ref[1:2, sl])
            d_or, d_od, d_rg, d_dz, d_nw = vjp((dret[:, sl], ddn[:, sl]))
            dor_ref[:, sl] = d_or
            dod_ref[:, sl] = d_od
            drg_ref[:, sl] = d_rg.astype(BF16)
            ddz_ref[:, sl] = d_dz.astype(BF16)
            pg_ref[1:2, sl] += d_nw

    seq, pair, wfull, vecs = _mix_specs()
    return pl.pallas_call(
        body, name="mixout_bwd", grid=(t_len // MIX_TM,),
        in_specs=[seq(0), vecs, pair, pair, seq(C_RG), seq(C_DZ), seq(C_GA), seq(C_GB), seq(0), seq(0), seq(0),
                  wfull, wfull, wfull],
        out_specs=[seq(0)] * 9 + [vecs],
        out_shape=[jax.ShapeDtypeStruct((t_len, D), F32)] * 2 + [jax.ShapeDtypeStruct((t_len, D), BF16)] * 7
        + [jax.ShapeDtypeStruct((8, D), F32)],
        compiler_params=_params(1),
    )(dx2, vec, o_ret, o_dn, p, p, p, p, yr, yd, z, w_r, w_d, w_o)


def _final_loss(x, w, target):
    y = x * lax.rsqrt(jnp.mean(x * x, axis=-1, keepdims=True) + EPS) * w
    err = y - target
    return 0.5 * jnp.sum(jnp.mean(err * err, axis=-1, keepdims=True), axis=0, keepdims=True)


def final_fwd_bwd(x, vec, target):
    t_len = x.shape[0]
    tm = _row_block(t_len)

    def body(x_ref, vec_ref, t_ref, dx_ref, pg_ref, loss_ref):
        @pl.when(pl.program_id(0) == 0)
        def _():
            pg_ref[...] = jnp.zeros_like(pg_ref)
            loss_ref[...] = jnp.zeros_like(loss_ref)

        loss, vjp = jax.vjp(functools.partial(_final_loss, target=t_ref[...]), x_ref[...], vec_ref[0:1, :])
        dx, dw = vjp(jnp.ones((1, 1), F32))
        dx_ref[...] = dx
        pg_ref[0:1, :] += dw
        loss_ref[...] += jnp.broadcast_to(loss, loss_ref.shape)

    return pl.pallas_call(
        body, name="final_fwd_bwd", grid=(t_len // tm,),
        in_specs=[pl.BlockSpec((tm, D), lambda i: (i, 0)), pl.BlockSpec((8, D), lambda i: (0, 0)),
                  pl.BlockSpec((tm, D), lambda i: (i, 0))],
        out_specs=[pl.BlockSpec((tm, D), lambda i: (i, 0)), pl.BlockSpec((8, D), lambda i: (0, 0)),
                   pl.BlockSpec((8, LANE), lambda i: (0, 0))],
        out_shape=[jax.ShapeDtypeStruct((t_len, D), F32), jax.ShapeDtypeStruct((8, D), F32),
                   jax.ShapeDtypeStruct((8, LANE), F32)],
        compiler_params=_params(1),
    )(x, vec, target)


def _vec8(*rows):
    rows = list(rows) + [jnp.zeros((D,), F32)] * (8 - len(rows))
    return jnp.stack(rows)


def _layer_vecs(lw, m):
    nw = lw["norm_w"]
    return dict(ffn1=_vec8(nw[0], m[0], m[1], m[2]), proj=_vec8(nw[1], m[3], m[4]),
                mix=_vec8(m[5], lw["dn_norm_t"]), ffn2=_vec8(nw[2], m[6], m[7], m[8]))


def _rope_tables(t_len, grid_w=64, base=10000.0):
    n_freq = DK // 4
    inv = base ** (-jnp.arange(n_freq, dtype=F32) / n_freq)
    tok = jnp.arange(t_len)
    ang = jnp.concatenate([(tok // grid_w).astype(F32)[:, None] * inv, (tok % grid_w).astype(F32)[:, None] * inv],
                          axis=-1)
    cos, sin = jnp.cos(ang), jnp.sin(ang)
    return jnp.concatenate([cos, cos], axis=-1), jnp.concatenate([-sin, sin], axis=-1)


def _stream_fwd_a(x0, lw, vecs, rope):
    x1, h1, g1, u1, f1 = ffn_fwd(x0, vecs["ffn1"], lw["wgu1"], lw["wd1"])
    p, h2 = proj_fwd(x1, vecs["proj"], lw["w_in"])
    rqk, dqkv, gbc = feat_fwd(p, rope[0], rope[1], lw["conv_w8"], lw["gparams"])
    return dict(x0=x0, h1=h1, g1=g1, u1=u1, f1=f1, x1=x1, h2=h2, p=p, rqk=rqk, dqkv=dqkv, gbc=gbc)


def _stream_mix(sv, s0_ret, s0_dn):
    sv["o_ret"], sv["ss_ret"], sf_ret = ret_fwd(sv["rqk"], sv["p"], s0_ret)
    sv["dn_u"], sv["dn_wqk"], sv["dn_attn"] = dn_intra_fwd(sv["dqkv"], sv["gbc"])
    sv["o_dn"], sv["ss_dn"], sf_dn = dn_seq_fwd(sv["dn_u"], sv["dn_wqk"], sv["dn_attn"], sv["gbc"], s0_dn)
    return sf_ret, sf_dn


def _stream_fwd_b(sv, lw, vecs):
    x2, ret, dn, y, yr, yd, z = mixout_fwd(sv["x1"], vecs["mix"], sv["o_ret"], sv["o_dn"], sv["p"],
                                           lw["w_r"], lw["w_d"], lw["w_o"])
    x3, h3, g3, u3, f3 = ffn_fwd(x2, vecs["ffn2"], lw["wgu2"], lw["wd2"])
    sv.update(ret=ret, dn=dn, y=y, yr=yr, yd=yd, z=z, x2=x2, h3=h3, g3=g3, u3=u3, f3=f3)
    return x3


def layer_fwd(xs, cs, lw, ropes):
    vx, vc = _layer_vecs(lw, lw["mx"]), _layer_vecs(lw, lw["mc"])
    sx = _stream_fwd_a(xs, lw, vx, ropes[0])
    sc = _stream_fwd_a(cs, lw, vc, ropes[1])
    zero = jnp.zeros((2, NH, DK, DV), F32)
    sf_ret, sf_dn = _stream_mix(sc, zero, zero)
    _stream_mix(sx, sf_ret, sf_dn)
    x3 = _stream_fwd_b(sx, lw, vx)
    c3 = _stream_fwd_b(sc, lw, vc)
    return x3, c3, (sx, sc)


def _stream_bwd_a(dx3, sv, lw, vecs):
    act3, dg3, du3, dy3, pg_a = ffn_bwd_act(dx3, vecs["ffn2"], sv["g3"], sv["u3"], sv["f3"], lw["wd2"])
    dx2, pg_b = nt_norm_bwd([dg3, du3], lw["wgu2"], [0, 2], FFN_HB, 2, sv["x2"], dx3, vecs["ffn2"], "ffn_bwd_in")
    (dor, dod, drg, ddz, dga, dgb, dyr, dyd, dzz, pg_m) = mixout_bwd(
        dx2, vecs["mix"], sv["o_ret"], sv["o_dn"], sv["p"], sv["yr"], sv["yd"], sv["z"], lw["w_r"], lw["w_d"], lw["w_o"])
    return dict(act3=act3, dg3=dg3, du3=du3, dy3=dy3, pg_a2=pg_a, pg_b2=pg_b, dx2=dx2, dor=dor, dod=dod,
                drg=drg, ddz=ddz, dga=dga, dgb=dgb, dyr=dyr, dyd=dyd, dzz=dzz, pg_m=pg_m)


def _stream_bwd_mix(bw, sv, dsf_ret, dsf_dn):
    bw["dqk_r"], bw["drv"], ds0_ret = ret_bwd(sv["rqk"], sv["p"], sv["ss_ret"], bw["dor"], dsf_ret)
    du, dwqk, dattn, dgl, ds0_dn = dn_seq_bwd(sv["dn_u"], sv["dn_wqk"], sv["dn_attn"], sv["gbc"], sv["ss_dn"],
                                              bw["dod"], dsf_dn)
    bw["dqk_d"], bw["dvv_d"], bw["dgbc"] = dn_intra_bwd(sv["dqkv"], sv["gbc"], du, dwqk, dattn, dgl)
    return ds0_ret, ds0_dn


def _stream_bwd_b(bw, sv, lw, vecs, rope):
    dp_rqk, dp_rv, dp_dqk, dp_dvv, dp_dba, bw["dcw"], bw["dgp"] = feat_bwd(
        sv["p"], rope[0], rope[1], lw["conv_w8"], lw["gparams"], bw["dqk_r"], bw["drv"], bw["dqk_d"], bw["dvv_d"],
        bw["dgbc"])
    bw["dp"] = jnp.concatenate([dp_rqk, dp_rv, bw["drg"], dp_dqk, dp_dvv, bw["ddz"], bw["dga"], bw["dgb"], dp_dba],
                               axis=1)
    dx1, bw["pg_p"] = nt_norm_bwd([bw["dp"]], lw["w_in"], [0], PROJ_TN, PW // PROJ_TN, sv["x1"], bw["dx2"],
                                  vecs["proj"], "proj_bwd_in")
    bw["act1"], bw["dg1"], bw["du1"], bw["dy1"], bw["pg_a1"] = ffn_bwd_act(dx1, vecs["ffn1"], sv["g1"], sv["u1"],
                                                                          sv["f1"], lw["wd1"])
    dx0, bw["pg_b1"] = nt_norm_bwd([bw["dg1"], bw["du1"]], lw["wgu1"], [0, 2], FFN_HB, 2, sv["x0"], dx1,
                                   vecs["ffn1"], "ffn_bwd_in")
    return dx0


def _stream_pgrads(bw):
    a1, b1, pp, pm, a2, b2 = bw["pg_a1"], bw["pg_b1"], bw["pg_p"], bw["pg_m"], bw["pg_a2"], bw["pg_b2"]
    dmod = jnp.stack([b1[1], b1[2], a1[3], pp[1], pp[2], pm[0], b2[1], b2[2], a2[3]])
    dnorm = jnp.stack([b1[0], pp[0], b2[0]])
    ddnw = pm[1].reshape(NH, DV).sum(axis=0)
    return dmod, dnorm, ddnw, bw["dcw"][:CONV_K], bw["dgp"][0, 8:16], bw["dgp"][1, 8:16]


def layer_bwd(dx3, dc3, lw, saved, ropes):
    sx, sc = saved
    vx, vc = _layer_vecs(lw, lw["mx"]), _layer_vecs(lw, lw["mc"])
    bx = _stream_bwd_a(dx3, sx, lw, vx)
    bc = _stream_bwd_a(dc3, sc, lw, vc)
    zero = jnp.zeros((2, NH, DK, DV), F32)
    ds0_ret, ds0_dn = _stream_bwd_mix(bx, sx, zero, zero)
    _stream_bwd_mix(bc, sc, ds0_ret, ds0_dn)
    dx0 = _stream_bwd_b(bx, sx, lw, vx, ropes[0])
    dc0 = _stream_bwd_b(bc, sc, lw, vc, ropes[1])

    def wgrad(a, b, tn, name):
        ax = sx[a] if a in sx else bx[a]
        ac = sc[a] if a in sc else bc[a]
        return tn_matmul(ax, bx[b], ac, bc[b], tn, name)

    gw = dict(
        wg1=wgrad("h1", "dg1", FFN_HB, "dw_ffn_gu"), wu1=wgrad("h1", "du1", FFN_HB, "dw_ffn_gu"),
        wd1=wgrad("act1", "dy1", 512, "dw_ffn_d"),
        w_in=wgrad("h2", "dp", PROJ_TN, "dw_in"),
        w_r=wgrad("ret", "dyr", 512, "dw_sq"), w_d=wgrad("dn", "dyd", 512, "dw_sq"), w_o=wgrad("y", "dzz", 512, "dw_sq"),
        wg2=wgrad("h3", "dg3", FFN_HB, "dw_ffn_gu"), wu2=wgrad("h3", "du3", FFN_HB, "dw_ffn_gu"),
        wd2=wgrad("act3", "dy3", 512, "dw_ffn_d"),
    )
    px, pc = _stream_pgrads(bx), _stream_pgrads(bc)
    small = dict(dmx=px[0], dmc=pc[0], norm_w=px[1] + pc[1], dn_norm_w=px[2] + pc[2], conv_w=px[3] + pc[3],
                 a_log=px[4] + pc[4], dt_bias=px[5] + pc[5])
    return dx0, dc0, gw, small


def local_step(x, ctx, target, final_norm_w, lws):
    t_len, t_ctx = x.shape[0], ctx.shape[0]
    ropes = (_rope_tables(t_len), (jnp.ones((t_ctx, LANE), F32), jnp.zeros((t_ctx, LANE), F32)))
    xs, cs, saved = x, ctx, []
    for lw in lws:
        xs, cs, sv = layer_fwd(xs, cs, lw, ropes)
        saved.append(sv)
    dx, pg_f, loss = final_fwd_bwd(xs, _vec8(final_norm_w), target)
    dc = jnp.zeros_like(ctx)
    gws, smalls = [None] * len(lws), [None] * len(lws)
    for l in reversed(range(len(lws))):
        dx, dc, gws[l], smalls[l] = layer_bwd(dx, dc, lws[l], saved[l], ropes)
    return loss[0, 0], dx, gws, smalls, pg_f[0]


ADA_ROWS = 16
ADA_COLS = N_MOD * D // N_DEV


def ada_fwd(cc, ada_w, ada_b_cols):
    def body(cc_ref, w_ref, b_ref, o_ref):
        cv = cc_ref[...]
        o_ref[0] = _dot_hi(cv * _sigmoid(cv), w_ref[0]) + b_ref[0]

    return pl.pallas_call(
        body, name="ada_fwd", grid=(DEPTH,),
        in_specs=[pl.BlockSpec((ADA_ROWS, D), lambda l: (0, 0)), pl.BlockSpec((1, D, ADA_COLS), lambda l: (l, 0, 0)),
                  pl.BlockSpec((1, 1, ADA_COLS), lambda l: (l, 0, 0))],
        out_specs=pl.BlockSpec((1, ADA_ROWS, ADA_COLS), lambda l: (l, 0, 0)),
        out_shape=jax.ShapeDtypeStruct((DEPTH, ADA_ROWS, ADA_COLS), F32),
        compiler_params=_params(1),
    )(cc, ada_w, ada_b_cols)


def ada_bwd(cc, dmod, ada_w):
    def body(cc_ref, d_ref, w_ref, gw_ref, ds_ref):
        cv = cc_ref[...]
        gw_ref[0] = _dot_hi(cv * _sigmoid(cv), d_ref[0], TN)
        ds_ref[0] = _dot_hi(d_ref[0], w_ref[0], NT)

    return pl.pallas_call(
        body, name="ada_bwd", grid=(DEPTH,),
        in_specs=[pl.BlockSpec((ADA_ROWS, D), lambda l: (0, 0)),
                  pl.BlockSpec((1, ADA_ROWS, ADA_COLS), lambda l: (l, 0, 0)),
                  pl.BlockSpec((1, D, ADA_COLS), lambda l: (l, 0, 0))],
        out_specs=[pl.BlockSpec((1, D, ADA_COLS), lambda l: (l, 0, 0)), pl.BlockSpec((1, ADA_ROWS, D), lambda l: (l, 0, 0))],
        out_shape=[jax.ShapeDtypeStruct((DEPTH, D, ADA_COLS), F32), jax.ShapeDtypeStruct((DEPTH, ADA_ROWS, D), F32)],
        compiler_params=_params(1),
    )(cc, dmod, ada_w)


def c_ctx_grad(parts, c_ctx8):
    def body(p_ref, c_ref, o_ref):
        acc = p_ref[0]
        for j in range(1, N_DEV):
            acc = acc + p_ref[j]
        cv = c_ref[...]
        sg = _sigmoid(cv)
        o_ref[...] = acc * (sg + cv * sg * (1.0 - sg))

    return pl.pallas_call(body, name="c_ctx_grad", out_shape=jax.ShapeDtypeStruct((8, LANE), F32))(parts, c_ctx8)


def sum_slots(parts):
    n_slot, rows, _ = parts.shape
    tr = 8
    for cand in (1024, 512, 256, 128, 64, 32, 16, 8):
        if rows % cand == 0:
            tr = cand
            break

    def body(p_ref, o_ref):
        acc = p_ref[0]
        for j in range(1, n_slot):
            acc = acc + p_ref[j]
        o_ref[...] = acc

    return pl.pallas_call(
        body, name="sum_slots", grid=(rows // tr,),
        in_specs=[pl.BlockSpec((n_slot, tr, LANE), lambda i: (0, i, 0))],
        out_specs=pl.BlockSpec((tr, LANE), lambda i: (i, 0)),
        out_shape=jax.ShapeDtypeStruct((rows, LANE), F32),
        compiler_params=_params(1),
    )(parts)


def adamw(parts, w, m, v, rows_blk, name):
    n_slot, n_l, n_a, n_b = parts.shape

    def body(p_ref, w_ref, m_ref, v_ref, g_ref, d_ref, m2_ref, v2_ref):
        g = p_ref[0].astype(F32)
        for j in range(1, n_slot):
            g = g + p_ref[j].astype(F32)
        m2 = ADAM_B1 * m_ref[...] + (1.0 - ADAM_B1) * g
        v2 = ADAM_B2 * v_ref[...] + (1.0 - ADAM_B2) * (g * g)
        m_hat = m2 / (1.0 - ADAM_B1 ** ADAM_STEP)
        v_hat = v2 / (1.0 - ADAM_B2 ** ADAM_STEP)
        g_ref[...] = g
        m2_ref[...] = m2
        v2_ref[...] = v2
        d_ref[...] = -ADAM_LR * (m_hat / (jnp.sqrt(v_hat) + ADAM_EPS) + ADAM_WD * w_ref[...])

    blk = pl.BlockSpec((1, rows_blk, n_b), lambda l, i: (l, i, 0))
    return pl.pallas_call(
        body, name=name, grid=(n_l, n_a // rows_blk),
        in_specs=[pl.BlockSpec((n_slot, 1, rows_blk, n_b), lambda l, i: (0, l, i, 0)), blk, blk, blk],
        out_specs=[blk] * 4,
        out_shape=[jax.ShapeDtypeStruct((n_l, n_a, n_b), F32)] * 4,
        compiler_params=_params(2),
    )(parts, w, m, v)


MESH = pl.DeviceIdType.MESH


def _my_index():
    return 4 * lax.axis_index("x") + 2 * lax.axis_index("y") + lax.axis_index("c")


def all_gather(shards, name):
    n = len(shards)

    def body(*refs):
        x_refs, out_refs = refs[:n], refs[n:2 * n]
        send_sems, recv_sems, local_sems = refs[2 * n:]
        x, y, c = lax.axis_index("x"), lax.axis_index("y"), lax.axis_index("c")
        me, sibling = (x, y, c), (x, y, 1 - c)
        chips = [(1 - x, y), (x, 1 - y), (1 - x, 1 - y)]

        def slot(a, px, py, pc):
            return out_refs[a].at[4 * px + 2 * py + pc]

        def copy(a, k, block, to, src=None):
            return pltpu.make_async_remote_copy(
                src_ref=slot(a, *block) if src is None else src, dst_ref=slot(a, *block),
                send_sem=send_sems.at[7 * a + k], recv_sem=recv_sems.at[7 * a + k], device_id=to, device_id_type=MESH)

        mine = [pltpu.make_async_copy(x_refs[a], slot(a, *me), local_sems.at[a]) for a in range(n)]
        for cp in mine:
            cp.start()
        first = []
        for a in range(n):
            first.append(copy(a, 0, me, sibling, src=x_refs[a]))
            first += [copy(a, 1 + j, me, (*chip, c), src=x_refs[a]) for j, chip in enumerate(chips)]
        for cp in first:
            cp.start()
        passed = []
        for j, chip in enumerate(chips):
            for a in range(n):
                copy(a, 1 + j, (*chip, c), me).wait_recv()
                fwd = copy(a, 4 + j, (*chip, c), sibling)
                fwd.start()
                passed.append(fwd)
        for a in range(n):
            copy(a, 0, sibling, me).wait_recv()
            for j, chip in enumerate(chips):
                copy(a, 4 + j, (*chip, 1 - c), me).wait_recv()
        for cp in first + passed:
            cp.wait_send()
        for cp in mine:
            cp.wait()

    return pl.pallas_call(
        body, name=name,
        in_specs=[pl.BlockSpec(memory_space=pl.ANY)] * n,
        out_specs=[pl.BlockSpec(memory_space=pl.ANY)] * n,
        out_shape=[jax.ShapeDtypeStruct((N_DEV,) + s.shape, s.dtype) for s in shards],
        scratch_shapes=[pltpu.SemaphoreType.DMA((7 * n,)), pltpu.SemaphoreType.DMA((7 * n,)),
                        pltpu.SemaphoreType.DMA((n,))],
    )(*shards)


def all_to_all(parts, name):
    n = len(parts)

    def body(*refs):
        p_refs, q_refs = refs[:n], refs[n:2 * n]
        send_sems, recv_sems, local_sems = refs[2 * n:]
        x, y, c = lax.axis_index("x"), lax.axis_index("y"), lax.axis_index("c")
        me = 4 * x + 2 * y + c
        mine = [pltpu.make_async_copy(p_refs[a].at[me], q_refs[a].at[me], local_sems.at[a]) for a in range(n)]
        for cp in mine:
            cp.start()
        copies = []
        for r in range(1, N_DEV):
            px, py, pc = x ^ (r >> 2), y ^ ((r >> 1) & 1), c ^ (r & 1)
            peer = 4 * px + 2 * py + pc
            for a in range(n):
                k = 7 * a + r - 1
                cp = pltpu.make_async_remote_copy(
                    src_ref=p_refs[a].at[peer], dst_ref=q_refs[a].at[me], send_sem=send_sems.at[k],
                    recv_sem=recv_sems.at[k], device_id=(px, py, pc), device_id_type=MESH)
                cp.start()
                copies.append((cp, a, peer, k))
        for cp, a, peer, k in copies:
            pltpu.make_async_remote_copy(
                src_ref=p_refs[a].at[peer], dst_ref=q_refs[a].at[peer], send_sem=send_sems.at[k],
                recv_sem=recv_sems.at[k], device_id=(x, y, c), device_id_type=MESH).wait_recv()
        for cp, _, _, _ in copies:
            cp.wait_send()
        for cp in mine:
            cp.wait()

    return pl.pallas_call(
        body, name=name,
        in_specs=[pl.BlockSpec(memory_space=pl.ANY)] * n,
        out_specs=[pl.BlockSpec(memory_space=pl.ANY)] * n,
        out_shape=[jax.ShapeDtypeStruct(p.shape, p.dtype) for p in parts],
        scratch_shapes=[pltpu.SemaphoreType.DMA((7 * n,)), pltpu.SemaphoreType.DMA((7 * n,)),
                        pltpu.SemaphoreType.DMA((n,))],
    )(*parts)


WEIGHT_ORDER = ("c_ctx", "ada_w", "ada_b", "norm_w", "ffn1_wgu", "ffn1_wd", "w_in", "dn_conv_w", "dn_a_log",
                "dn_dt_bias", "dn_norm_w", "w_ret_out", "w_dn_out", "w_o", "ffn2_wgu", "ffn2_wd", "final_norm_w")
BIG = (("ffn1_wgu", "col"), ("ffn1_wd", "row"), ("w_in", "col"), ("w_ret_out", "row"), ("w_dn_out", "row"),
       ("w_o", "row"), ("ffn2_wgu", "col"), ("ffn2_wd", "row"))
LOCAL = ("ada_w", "c_ctx", "ada_b", "norm_w", "dn_conv_w", "dn_a_log", "dn_dt_bias", "dn_norm_w", "final_norm_w")


def _pack(arrs, row_mult, lead=None):
    if lead is None:
        flat = jnp.concatenate([a.reshape(-1) for a in arrs])
        n = flat.shape[0]
    else:
        flat = jnp.concatenate([a.reshape(lead, -1) for a in arrs], axis=1)
        n = flat.shape[1]
    unit = LANE * row_mult
    total = -(-n // unit) * unit
    if lead is None:
        return jnp.pad(flat, (0, total - n)).reshape(total // LANE, LANE)
    return jnp.pad(flat, ((0, 0), (0, total - n))).reshape(lead, total // LANE, LANE)


def _unpack(packed, shapes, lead=None):
    flat = packed.reshape(-1) if lead is None else packed.reshape(lead, -1)
    out, off = [], 0
    for shp in shapes:
        n = math.prod(shp)
        if lead is None:
            out.append(flat[off:off + n].reshape(shp))
        else:
            out.append(flat[:, off:off + n].reshape((lead,) + tuple(shp)))
        off += n
    return out


def _join_shards(g, kind):
    lead = tuple(range(1, g.ndim - 2))
    a, b = g.shape[-2:]
    if kind == "col":
        return g.transpose(*lead, g.ndim - 2, 0, g.ndim - 1).reshape(g.shape[1:-2] + (a, N_DEV * b))
    return g.transpose(*lead, 0, g.ndim - 2, g.ndim - 1).reshape(g.shape[1:-2] + (N_DEV * a, b))


def _split_shards(full, kind):
    a, b = full.shape
    if kind == "col":
        return full.reshape(a, N_DEV, b // N_DEV).transpose(1, 0, 2)
    return full.reshape(N_DEV, a // N_DEV, b)


def _pad_w_in(w):
    return jnp.concatenate([w[..., :6144], w[..., 6160:8208], w[..., 6144:6160],
                            jnp.zeros(w.shape[:-1] + (PW - 8208,), w.dtype)], axis=-1)


def _unpad_w_in(g):
    return jnp.concatenate([g[..., :6144], g[..., 8192:8208], g[..., 6144:8192]], axis=-1)


def kernel(x, c, ctx, c_ctx, ada_w, ada_b, norm_w, ffn1_wgu, ffn1_wd, w_in, dn_conv_w, dn_a_log, dn_dt_bias, dn_norm_w, w_ret_out, w_dn_out, w_o, ffn2_wgu, ffn2_wd, final_norm_w, loss_target, m_c_ctx, m_ada_w, m_ada_b, m_norm_w, m_ffn1_wgu, m_ffn1_wd, m_w_in, m_dn_conv_w, m_dn_a_log, m_dn_dt_bias, m_dn_norm_w, m_w_ret_out, m_w_dn_out, m_w_o, m_ffn2_wgu, m_ffn2_wd, m_final_norm_w, v_c_ctx, v_ada_w, v_ada_b, v_norm_w, v_ffn1_wgu, v_ffn1_wd, v_w_in, v_dn_conv_w, v_dn_a_log, v_dn_dt_bias, v_dn_norm_w, v_w_ret_out, v_w_dn_out, v_w_o, v_ffn2_wgu, v_ffn2_wd, v_final_norm_w):
    w = dict(c_ctx=c_ctx, ada_w=ada_w, ada_b=ada_b, norm_w=norm_w, ffn1_wgu=ffn1_wgu, ffn1_wd=ffn1_wd, w_in=w_in, dn_conv_w=dn_conv_w, dn_a_log=dn_a_log, dn_dt_bias=dn_dt_bias, dn_norm_w=dn_norm_w, w_ret_out=w_ret_out, w_dn_out=w_dn_out, w_o=w_o, ffn2_wgu=ffn2_wgu, ffn2_wd=ffn2_wd, final_norm_w=final_norm_w)
    m = dict(c_ctx=m_c_ctx, ada_w=m_ada_w, ada_b=m_ada_b, norm_w=m_norm_w, ffn1_wgu=m_ffn1_wgu, ffn1_wd=m_ffn1_wd, w_in=m_w_in, dn_conv_w=m_dn_conv_w, dn_a_log=m_dn_a_log, dn_dt_bias=m_dn_dt_bias, dn_norm_w=m_dn_norm_w, w_ret_out=m_w_ret_out, w_dn_out=m_w_dn_out, w_o=m_w_o, ffn2_wgu=m_ffn2_wgu, ffn2_wd=m_ffn2_wd, final_norm_w=m_final_norm_w)
    v = dict(c_ctx=v_c_ctx, ada_w=v_ada_w, ada_b=v_ada_b, norm_w=v_norm_w, ffn1_wgu=v_ffn1_wgu, ffn1_wd=v_ffn1_wd, w_in=v_w_in, dn_conv_w=v_dn_conv_w, dn_a_log=v_dn_a_log, dn_dt_bias=v_dn_dt_bias, dn_norm_w=v_dn_norm_w, w_ret_out=v_w_ret_out, w_dn_out=v_w_dn_out, w_o=v_w_o, ffn2_wgu=v_ffn2_wgu, ffn2_wd=v_ffn2_wd, final_norm_w=v_final_norm_w)
    me = _my_index()
    big_names = [n for n, _ in BIG]

    g_big = all_gather([w[n].astype(BF16) for n in big_names], "ag_weights")
    full = {n: _join_shards(s, kind) for (n, kind), s in zip(BIG, g_big)}
    small_shapes = [norm_w.shape, dn_conv_w.shape, c.shape]
    g_small = all_gather([_pack([norm_w, dn_conv_w, c], 8)], "ag_small")[0]
    norm_s, conv_s, c_all = _unpack(g_small, small_shapes, lead=N_DEV)
    norm_full, conv_full = _join_shards(norm_s, "col"), _join_shards(conv_s, "col")

    cc = jnp.concatenate([c_all.reshape(N_DEV, D), c_ctx[None], jnp.zeros((ADA_ROWS - N_DEV - 1, D), F32)])
    ada_b_cols = lax.dynamic_slice_in_dim(ada_b, me * ADA_COLS, ADA_COLS, axis=1)[:, None, :]
    mods_part = ada_fwd(cc, ada_w, ada_b_cols)
    g_mods = all_gather([mods_part], "ag_mods")[0]
    mods_all = g_mods.transpose(1, 2, 0, 3).reshape(DEPTH, ADA_ROWS, N_MOD * D)
    mx = lax.dynamic_index_in_dim(mods_all, me, axis=1, keepdims=False).reshape(DEPTH, N_MOD, D)
    mc = mods_all[:, N_DEV].reshape(DEPTH, N_MOD, D)

    gparams = jnp.pad(jnp.stack([-jnp.exp(dn_a_log).reshape(DEPTH, 8), dn_dt_bias.reshape(DEPTH, 8)], axis=1),
                      ((0, 0), (0, 6), (8, LANE - 16)))
    stacked = dict(
        wgu1=full["ffn1_wgu"], wd1=full["ffn1_wd"], w_in=_pad_w_in(full["w_in"]), w_r=full["w_ret_out"],
        w_d=full["w_dn_out"], w_o=full["w_o"], wgu2=full["ffn2_wgu"], wd2=full["ffn2_wd"], norm_w=norm_full,
        conv_w8=jnp.pad(conv_full, ((0, 0), (0, 8 - CONV_K), (0, 0))), gparams=gparams,
        dn_norm_t=jnp.tile(dn_norm_w, (1, NH)), mx=mx, mc=mc)
    lws = [{k: a[l] for k, a in stacked.items()} for l in range(DEPTH)]
    loss_l, gx, gws, smalls, d_fnw = local_step(x[0], ctx[0], loss_target[0], final_norm_w, lws)

    sm_list = [jnp.stack([s[k] for s in smalls]) for k in ("dmx", "dmc", "norm_w", "conv_w", "a_log", "dt_bias",
                                                            "dn_norm_w")] + [d_fnw, loss_l.reshape(1)]
    sm_shapes = [a.shape for a in sm_list]
    g_sm = all_gather([_pack(sm_list, 8)], "ag_small_grads")[0]
    (dmx_sum, dmc_sum, g_norm, g_conv, g_alog, g_dtb, g_dnw, g_fnw, loss) = _unpack(sum_slots(g_sm), sm_shapes)
    dmx_all = _unpack(g_sm, sm_shapes[:1], lead=N_DEV)[0].reshape(N_DEV, DEPTH, N_MOD * D)
    dmc_sum = dmc_sum.reshape(DEPTH, N_MOD * D)
    dmx_sum = dmx_sum.reshape(DEPTH, N_MOD * D)
    dmod = jnp.concatenate([
        lax.dynamic_slice_in_dim(dmx_all, me * ADA_COLS, ADA_COLS, axis=2).transpose(1, 0, 2),
        lax.dynamic_slice_in_dim(dmc_sum, me * ADA_COLS, ADA_COLS, axis=1)[:, None, :],
        jnp.zeros((DEPTH, ADA_ROWS - N_DEV - 1, ADA_COLS), F32)], axis=1)
    g_ada_w, d_sil = ada_bwd(cc, dmod, ada_w)
    g_cc = all_gather([d_sil[:, N_DEV].sum(axis=0).reshape(8, LANE)], "ag_c_ctx")[0]
    grads = dict(
        ada_w=g_ada_w, c_ctx=c_ctx_grad(g_cc, c_ctx.reshape(8, LANE)).reshape(D), ada_b=dmx_sum + dmc_sum,
        norm_w=lax.dynamic_slice_in_dim(g_norm, me * (D // N_DEV), D // N_DEV, axis=2),
        dn_conv_w=lax.dynamic_slice_in_dim(g_conv, me * (2 * D // N_DEV), 2 * D // N_DEV, axis=2),
        dn_a_log=g_alog.reshape(dn_a_log.shape), dn_dt_bias=g_dtb.reshape(dn_dt_bias.shape), dn_norm_w=g_dnw,
        final_norm_w=g_fnw)

    def layer_full(gw):
        return dict(ffn1_wgu=jnp.concatenate([gw["wg1"], gw["wu1"]], axis=-1), ffn1_wd=gw["wd1"],
                    w_in=_unpad_w_in(gw["w_in"]), w_ret_out=gw["w_r"], w_dn_out=gw["w_d"], w_o=gw["w_o"],
                    ffn2_wgu=jnp.concatenate([gw["wg2"], gw["wu2"]], axis=-1), ffn2_wd=gw["wd2"])

    full_g = [layer_full(gw) for gw in gws]
    parts = [jnp.stack([_split_shards(fg[n], kind) for fg in full_g], axis=1).astype(BF16) for n, kind in BIG]
    recv = all_to_all(parts, "a2a_grads")
    res = {}
    for n, r in zip(big_names, recv):
        rows_blk = 256 if w[n].shape[1] % 256 == 0 else w[n].shape[1]
        res[n] = adamw(r, w[n], m[n], v[n], rows_blk, "adamw_" + n)
    res["ada_w"] = adamw(g_ada_w[None], ada_w, m["ada_w"], v["ada_w"], 256, "adamw_ada_w")
    small_names = [n for n in LOCAL if n != "ada_w"]
    small_pack = lambda d: _pack([d[n] for n in small_names], 8)[None]
    res_small = adamw(small_pack(grads)[None], small_pack(w), small_pack(m), small_pack(v),
                      small_pack(w).shape[1], "adamw_small")
    unpacked = [_unpack(r, [w[n].shape for n in small_names]) for r in res_small]
    for i, n in enumerate(small_names):
        res[n] = tuple(u[i] for u in unpacked)
    outs = [[res[n][k] for n in WEIGHT_ORDER] for k in range(4)]
    return (loss.reshape(()), gx[None], *outs[0], *outs[1], *outs[2], *outs[3])
```

```python
import functools
import math

import jax
import jax.numpy as jnp
from jax import lax
from jax.experimental import pallas as pl
from jax.experimental.pallas import tpu as pltpu

F32 = jnp.float32
BF16 = jnp.bfloat16

D = 1024
NH = 4
DK = 128
DV = 256
RET_C = 128
DN_C = 64
FFN_H = 2816
FFN_HB = 1408
N_MOD = 9
DEPTH = 4
N_DEV = 8
EPS = 1e-6
CONV_K = 5
HALO = 8

PW = 8320
C_RQK, C_RV, C_RG, C_DQK, C_DVV, C_DZ, C_GA, C_GB = 0, 1, 2, 3, 4, 5, 6, 7
C_DBA = 64
PROJ_TN = 1664

LANE = 128
VMEM_LIMIT = 56 * 1024 * 1024

ADAM_LR, ADAM_B1, ADAM_B2, ADAM_EPS, ADAM_WD, ADAM_STEP = 0.001, 0.9, 0.999, 1e-08, 0.01, 10

NN = ((1,), (0,))
NT = ((1,), (1,))
TN = ((0,), (0,))
HI = lax.Precision.HIGHEST


def _params(n_grid):
    return pltpu.CompilerParams(dimension_semantics=("arbitrary",) * n_grid, vmem_limit_bytes=VMEM_LIMIT)


def _dot(a, b, dims):
    return lax.dot_general(a.astype(BF16), b.astype(BF16), (dims, ((), ())), preferred_element_type=F32)


def _dot_hi(a, b, dims=NN):
    return lax.dot_general(a, b, (dims, ((), ())), precision=HI, preferred_element_type=F32)


@jax.custom_vjp
def mm_nn(a, b):
    return _dot(a, b, NN)


mm_nn.defvjp(lambda a, b: (_dot(a, b, NN), (a, b)),
             lambda r, g: (_dot(g, r[1], NT), _dot(r[0], g, TN)))


@jax.custom_vjp
def mm_nt(a, b):
    return _dot(a, b, NT)


mm_nt.defvjp(lambda a, b: (_dot(a, b, NT), (a, b)),
             lambda r, g: (_dot(g, r[1], NN), _dot(g, r[0], TN)))


@jax.custom_vjp
def mm_tn(a, b):
    return _dot(a, b, TN)


mm_tn.defvjp(lambda a, b: (_dot(a, b, TN), (a, b)),
             lambda r, g: (_dot(r[1], g, NT), _dot(r[0], g, NN)))


def _split_bf16(x, n):
    parts = []
    for _ in range(n):
        p = x.astype(BF16)
        parts.append(p)
        x = x - p.astype(F32)
    return parts


def _dot_f32(a, b, dims, exact=None):
    mm = lambda p, q: lax.dot_general(p, q, (dims, ((), ())), preferred_element_type=F32)
    if exact == "a":
        ab = a.astype(BF16)
        b1, b2, b3 = _split_bf16(b, 3)
        return mm(ab, b1) + (mm(ab, b2) + mm(ab, b3))
    if exact == "b":
        bb = b.astype(BF16)
        a1, a2, a3 = _split_bf16(a, 3)
        return mm(a1, bb) + (mm(a2, bb) + mm(a3, bb))
    a1, a2 = _split_bf16(a, 2)
    b1, b2 = _split_bf16(b, 2)
    return mm(a1, b1) + (mm(a1, b2) + mm(a2, b1))


@jax.custom_vjp
def _cum_rows(mask, x):
    return _dot_f32(mask, x, NN, exact="a")


_cum_rows.defvjp(lambda mask, x: (_dot_f32(mask, x, NN, exact="a"), mask),
                 lambda mask, g: (jnp.zeros_like(mask), _dot_f32(mask, g, TN, exact="a")))


@jax.custom_vjp
def _row_bcast(xb):
    return _dot_f32(jnp.full(xb.shape, 1.0 / LANE, F32), xb, NT, exact="a")


_row_bcast.defvjp(lambda xb: (_row_bcast(xb), None),
                  lambda _, g: (_dot_f32(g, jnp.full((g.shape[0], LANE), 1.0 / LANE, F32), TN, exact="b"),))


def _tri_inv_fwd(a):
    c = a.shape[0]
    eye = (lax.broadcasted_iota(jnp.int32, (c, c), 0) == lax.broadcasted_iota(jnp.int32, (c, c), 1)).astype(F32)
    pw = -a
    tinv = eye + pw
    m = 2
    while m < DN_C:
        pw = _dot_f32(pw, pw, NN)
        tinv = tinv + _dot_f32(tinv, pw, NN)
        m *= 2
    return tinv


@jax.custom_vjp
def _tri_inv(a):
    return _tri_inv_fwd(a)


def _tri_inv_bwd(tinv, g):
    return (-_dot_f32(_dot_f32(tinv, g, TN), tinv, NT),)


_tri_inv.defvjp(lambda a: (lambda t: (t, t))(_tri_inv_fwd(a)), _tri_inv_bwd)


@jax.custom_vjp
def _tri_inv_known(a, tinv):
    del a
    return tinv


_tri_inv_known.defvjp(lambda a, tinv: (tinv, tinv),
                      lambda tinv, g: (_tri_inv_bwd(tinv, g)[0], jnp.zeros_like(tinv)))


def _normmod(x, nw, shift, scale):
    r = lax.rsqrt(jnp.mean(x * x, axis=-1, keepdims=True) + EPS)
    return (x * r * nw) * (1.0 + scale) + shift


def _sigmoid(x):
    return 0.5 * jnp.tanh(0.5 * x) + 0.5


def _softplus(x):
    return jnp.maximum(x, 0.0) + jnp.log(1.0 + jnp.exp(-jnp.abs(x)))


def _row_block(t_len):
    return 512 if t_len % 512 == 0 else 256


def ffn_fwd(x, vec, wgu, wd):
    t_len = x.shape[0]
    tm = _row_block(t_len)

    def body(x_ref, vec_ref, wg_ref, wu_ref, wd_ref, x1_ref, h_ref, g_ref, u_ref, f_ref, hs, acc):
        j = pl.program_id(1)

        @pl.when(j == 0)
        def _():
            hb = _normmod(x_ref[...], vec_ref[0:1, :], vec_ref[1:2, :], vec_ref[2:3, :]).astype(BF16)
            hs[...] = hb
            h_ref[...] = hb
            acc[...] = jnp.zeros_like(acc)

        hb = hs[...]
        g = jnp.dot(hb, wg_ref[...], preferred_element_type=F32)
        u = jnp.dot(hb, wu_ref[...], preferred_element_type=F32)
        g_ref[...] = g.astype(BF16)
        u_ref[...] = u.astype(BF16)
        act = g * _sigmoid(g) * u
        acc[...] += jnp.dot(act.astype(BF16), wd_ref[...], preferred_element_type=F32)

        @pl.when(j == 1)
        def _():
            f = acc[...]
            f_ref[...] = f.astype(BF16)
            x1_ref[...] = x_ref[...] + 0.5 * vec_ref[3:4, :] * f

    return pl.pallas_call(
        body, name="ffn_fwd", grid=(t_len // tm, 2),
        in_specs=[
            pl.BlockSpec((tm, D), lambda i, j: (i, 0)),
            pl.BlockSpec((8, D), lambda i, j: (0, 0)),
            pl.BlockSpec((D, FFN_HB), lambda i, j: (0, j)),
            pl.BlockSpec((D, FFN_HB), lambda i, j: (0, j + 2)),
            pl.BlockSpec((FFN_HB, D), lambda i, j: (j, 0)),
        ],
        out_specs=[
            pl.BlockSpec((tm, D), lambda i, j: (i, 0)),
            pl.BlockSpec((tm, D), lambda i, j: (i, 0)),
            pl.BlockSpec((tm, FFN_HB), lambda i, j: (i, j)),
            pl.BlockSpec((tm, FFN_HB), lambda i, j: (i, j)),
            pl.BlockSpec((tm, D), lambda i, j: (i, 0)),
        ],
        out_shape=[
            jax.ShapeDtypeStruct((t_len, D), F32),
            jax.ShapeDtypeStruct((t_len, D), BF16),
            jax.ShapeDtypeStruct((t_len, FFN_H), BF16),
            jax.ShapeDtypeStruct((t_len, FFN_H), BF16),
            jax.ShapeDtypeStruct((t_len, D), BF16),
        ],
        scratch_shapes=[pltpu.VMEM((tm, D), BF16), pltpu.VMEM((tm, D), F32)],
        compiler_params=_params(2),
    )(x, vec, wgu, wgu, wd)


def ffn_bwd_act(dx1, vec, g, u, f, wd):
    t_len = dx1.shape[0]
    tm = _row_block(t_len)

    def body(dx_ref, vec_ref, g_ref, u_ref, f_ref, wd_ref, act_ref, dg_ref, du_ref, dy_ref, pg_ref, dys):
        i, j = pl.program_id(0), pl.program_id(1)

        @pl.when((i == 0) & (j == 0))
        def _():
            pg_ref[...] = jnp.zeros_like(pg_ref)

        @pl.when(j == 0)
        def _():
            dx = dx_ref[...]
            dyb = (0.5 * vec_ref[3:4, :] * dx).astype(BF16)
            dys[...] = dyb
            dy_ref[...] = dyb
            pg_ref[3:4, :] += jnp.sum(0.5 * dx * f_ref[...].astype(F32), axis=0, keepdims=True)

        dact = _dot(dys[...], wd_ref[...], NT)
        gg = g_ref[...].astype(F32)
        uu = u_ref[...].astype(F32)
        sg = _sigmoid(gg)
        sl = gg * sg
        act_ref[...] = (sl * uu).astype(BF16)
        du_ref[...] = (dact * sl).astype(BF16)
        dg_ref[...] = (dact * uu * (sg + sl * (1.0 - sg))).astype(BF16)

    return pl.pallas_call(
        body, name="ffn_bwd_act", grid=(t_len // tm, 2),
        in_specs=[
            pl.BlockSpec((tm, D), lambda i, j: (i, 0)),
            pl.BlockSpec((8, D), lambda i, j: (0, 0)),
            pl.BlockSpec((tm, FFN_HB), lambda i, j: (i, j)),
            pl.BlockSpec((tm, FFN_HB), lambda i, j: (i, j)),
            pl.BlockSpec((tm, D), lambda i, j: (i, 0)),
            pl.BlockSpec((FFN_HB, D), lambda i, j: (j, 0)),
        ],
        out_specs=[
            pl.BlockSpec((tm, FFN_HB), lambda i, j: (i, j)),
            pl.BlockSpec((tm, FFN_HB), lambda i, j: (i, j)),
            pl.BlockSpec((tm, FFN_HB), lambda i, j: (i, j)),
            pl.BlockSpec((tm, D), lambda i, j: (i, 0)),
            pl.BlockSpec((8, D), lambda i, j: (0, 0)),
        ],
        out_shape=[
            jax.ShapeDtypeStruct((t_len, FFN_H), BF16),
            jax.ShapeDtypeStruct((t_len, FFN_H), BF16),
            jax.ShapeDtypeStruct((t_len, FFN_H), BF16),
            jax.ShapeDtypeStruct((t_len, D), BF16),
            jax.ShapeDtypeStruct((8, D), F32),
        ],
        scratch_shapes=[pltpu.VMEM((tm, D), BF16)],
        compiler_params=_params(2),
    )(dx1, vec, g, u, f, wd)


def nt_norm_bwd(dys, w, col_offsets, tk, n_steps, x_in, dres, vec, name):
    t_len = x_in.shape[0]
    tm = _row_block(t_len)
    n_seg = len(dys)

    def body(*refs):
        dy_refs = refs[:n_seg]
        w_refs = refs[n_seg:2 * n_seg]
        x_ref, dres_ref, vec_ref, dx_ref, pg_ref, acc = refs[2 * n_seg:]
        i, j = pl.program_id(0), pl.program_id(1)

        @pl.when((i == 0) & (j == 0))
        def _():
            pg_ref[...] = jnp.zeros_like(pg_ref)

        @pl.when(j == 0)
        def _():
            acc[...] = jnp.zeros_like(acc)

        part = _dot(dy_refs[0][...], w_refs[0][...], NT)
        for s in range(1, n_seg):
            part += _dot(dy_refs[s][...], w_refs[s][...], NT)
        acc[...] += part

        @pl.when(j == n_steps - 1)
        def _():
            _, vjp = jax.vjp(_normmod, x_ref[...], vec_ref[0:1, :], vec_ref[1:2, :], vec_ref[2:3, :])
            dxn, dnw, dsh, dsc = vjp(acc[...])
            dx_ref[...] = dres_ref[...] + dxn
            pg_ref[0:1, :] += dnw
            pg_ref[1:2, :] += dsh
            pg_ref[2:3, :] += dsc

    in_specs = [pl.BlockSpec((tm, tk), lambda i, j: (i, j)) for _ in range(n_seg)]
    in_specs += [pl.BlockSpec((D, tk), functools.partial(lambda i, j, off: (0, off + j), off=off))
                 for off in col_offsets]
    in_specs += [pl.BlockSpec((tm, D), lambda i, j: (i, 0)), pl.BlockSpec((tm, D), lambda i, j: (i, 0)),
                 pl.BlockSpec((8, D), lambda i, j: (0, 0))]
    return pl.pallas_call(
        body, name=name, grid=(t_len // tm, n_steps),
        in_specs=in_specs,
        out_specs=[pl.BlockSpec((tm, D), lambda i, j: (i, 0)), pl.BlockSpec((8, D), lambda i, j: (0, 0))],
        out_shape=[jax.ShapeDtypeStruct((t_len, D), F32), jax.ShapeDtypeStruct((8, D), F32)],
        scratch_shapes=[pltpu.VMEM((tm, D), F32)],
        compiler_params=_params(2),
    )(*dys, *([w] * n_seg), x_in, dres, vec)


def tn_matmul(a_x, b_x, a_c, b_c, tn, name):
    t_len, k_dim = a_x.shape
    n_dim = b_x.shape[1]
    t_ctx = a_c.shape[0]
    tt = 512
    n_t = t_len // tt

    def body(ax_ref, bx_ref, ac_ref, bc_ref, o_ref):
        t = pl.program_id(1)

        @pl.when(t == 0)
        def _():
            o_ref[...] = _dot(ac_ref[...], bc_ref[...], TN)

        o_ref[...] += _dot(ax_ref[...], bx_ref[...], TN)

    return pl.pallas_call(
        body, name=name, grid=(n_dim // tn, n_t),
        in_specs=[
            pl.BlockSpec((tt, k_dim), lambda n, t: (t, 0)),
            pl.BlockSpec((tt, tn), lambda n, t: (t, n)),
            pl.BlockSpec((t_ctx, k_dim), lambda n, t: (0, 0)),
            pl.BlockSpec((t_ctx, tn), lambda n, t: (0, n)),
        ],
        out_specs=pl.BlockSpec((k_dim, tn), lambda n, t: (0, n)),
        out_shape=jax.ShapeDtypeStruct((k_dim, n_dim), F32),
        compiler_params=_params(2),
    )(a_x, b_x, a_c, b_c)


def proj_fwd(x, vec, w_in_p):
    t_len = x.shape[0]
    tm = _row_block(t_len)

    def body(x_ref, vec_ref, w_ref, p_ref, h_ref, hs):
        @pl.when(pl.program_id(1) == 0)
        def _():
            hb = _normmod(x_ref[...], vec_ref[0:1, :], vec_ref[1:2, :], vec_ref[2:3, :]).astype(BF16)
            hs[...] = hb
            h_ref[...] = hb

        p_ref[...] = jnp.dot(hs[...], w_ref[...], preferred_element_type=F32)

    return pl.pallas_call(
        body, name="proj_fwd", grid=(t_len // tm, PW // PROJ_TN),
        in_specs=[
            pl.BlockSpec((tm, D), lambda i, j: (i, 0)),
            pl.BlockSpec((8, D), lambda i, j: (0, 0)),
            pl.BlockSpec((D, PROJ_TN), lambda i, j: (0, j)),
        ],
        out_specs=[pl.BlockSpec((tm, PROJ_TN), lambda i, j: (i, j)), pl.BlockSpec((tm, D), lambda i, j: (i, 0))],
        out_shape=[jax.ShapeDtypeStruct((t_len, PW), F32), jax.ShapeDtypeStruct((t_len, D), BF16)],
        scratch_shapes=[pltpu.VMEM((tm, D), BF16)],
        compiler_params=_params(2),
    )(x, vec, w_in_p)


def _shift_rows(e, s):
    n = e.shape[0]
    return pltpu.roll(e, (-s) % n, 0)


def _swap_halves(t):
    return pltpu.roll(t, DK // 2, 1)


def _halo_specs(tm, t_len, width, col, lead=False):
    per = tm // HALO
    last = t_len // HALO - 1
    if lead:
        return (pl.BlockSpec((2, HALO, width), lambda i: (0, jnp.maximum(i * per - 1, 0), col)),
                pl.BlockSpec((2, HALO, width), lambda i: (0, jnp.minimum((i + 1) * per, last), col)))
    return (pl.BlockSpec((HALO, width), lambda i: (jnp.maximum(i * per - 1, 0), col)),
            pl.BlockSpec((HALO, width), lambda i: (jnp.minimum((i + 1) * per, last), col)))


def _gate_cols(ba, gp_ref):
    lane = lax.broadcasted_iota(jnp.int32, ba.shape, 1)
    z = ba + gp_ref[1:2, :]
    return jnp.where(lane < 8, _sigmoid(ba), gp_ref[0:1, :] * _softplus(z))


def _conv_chunk(win, w_ref, c0, width):
    y = w_ref[0:1, c0:c0 + width] * _shift_rows(win, -2)
    for j in range(1, CONV_K):
        y += w_ref[j:j + 1, c0:c0 + width] * _shift_rows(win, j - 2)
    return y


def feat_fwd(p, cos2, sin2, conv_w8, gparams):
    t_len = p.shape[0]
    tm = _row_block(t_len)
    n_t = t_len // tm
    cw = 512

    def body(rqk_ref, dqk_ref, dqk_p, dqk_n, dvv_ref, dvv_p, dvv_n, dba_ref, cos_ref, sin_ref, cw_ref, gp_ref,
             o_rqk, o_dqkv, o_gbc):
        i = pl.program_id(0)
        cos, sin = cos_ref[...], sin_ref[...]
        for s in range(2 * NH):
            t = rqk_ref[:, s * DK:(s + 1) * DK]
            if s >= NH:
                t = t * (DK ** -0.5)
            o_rqk[:, s * DK:(s + 1) * DK] = t * cos + _swap_halves(t) * sin
        o_gbc[...] = _gate_cols(dba_ref[...], gp_ref)
        first, last = i == 0, i == n_t - 1
        for ci in range(4):
            src, sp, sn = (dqk_ref, dqk_p, dqk_n) if ci < 2 else (dvv_ref, dvv_p, dvv_n)
            c0 = (ci % 2) * cw
            win = jnp.concatenate([
                jnp.where(first, 0.0, sp[:, c0:c0 + cw]), src[:, c0:c0 + cw],
                jnp.where(last, 0.0, sn[:, c0:c0 + cw])], axis=0)
            y = _conv_chunk(win, cw_ref, ci * cw, cw)[HALO:HALO + tm]
            sv = y * _sigmoid(y)
            if ci < 2:
                scale = DK ** -0.5 if ci == 0 else 1.0
                for hh in range(NH):
                    sh = sv[:, hh * DK:(hh + 1) * DK]
                    nrm = lax.rsqrt(jnp.sum(sh * sh, axis=-1, keepdims=True) + EPS)
                    o_dqkv[:, ci * cw + hh * DK:ci * cw + (hh + 1) * DK] = sh * nrm * scale
            else:
                o_dqkv[:, ci * cw:(ci + 1) * cw] = sv

    hq = _halo_specs(tm, t_len, D, C_DQK)
    hv = _halo_specs(tm, t_len, D, C_DVV)
    return pl.pallas_call(
        body, name="feat_fwd", grid=(n_t,),
        in_specs=[
            pl.BlockSpec((tm, D), lambda i: (i, C_RQK)),
            pl.BlockSpec((tm, D), lambda i: (i, C_DQK)), hq[0], hq[1],
            pl.BlockSpec((tm, D), lambda i: (i, C_DVV)), hv[0], hv[1],
            pl.BlockSpec((tm, LANE), lambda i: (i, C_DBA)),
            pl.BlockSpec((tm, LANE), lambda i: (i, 0)),
            pl.BlockSpec((tm, LANE), lambda i: (i, 0)),
            pl.BlockSpec((8, 2 * D), lambda i: (0, 0)),
            pl.BlockSpec((8, LANE), lambda i: (0, 0)),
        ],
        out_specs=[pl.BlockSpec((tm, D), lambda i: (i, 0)), pl.BlockSpec((tm, 2 * D), lambda i: (i, 0)),
                   pl.BlockSpec((tm, LANE), lambda i: (i, 0))],
        out_shape=[jax.ShapeDtypeStruct((t_len, D), F32), jax.ShapeDtypeStruct((t_len, 2 * D), F32),
                   jax.ShapeDtypeStruct((t_len, LANE), F32)],
        compiler_params=_params(1),
    )(p, p, p, p, p, p, p, p, cos2, sin2, conv_w8, gparams)


def feat_bwd(p, cos2, sin2, conv_w8, gparams, d_rqk, d_rv, d_dqk, d_dvv, d_gbc):
    t_len = p.shape[0]
    tm = 256
    n_t = t_len // tm
    cw = 512

    def body(dqk_ref, dqk_p, dqk_n, dvv_ref, dvv_p, dvv_n, dba_ref, cos_ref, sin_ref, cw_ref, gp_ref,
             g_rqk, g_rv, g_dqk, g_dqk_p, g_dqk_n, g_dvv, g_dvv_p, g_dvv_n, g_gbc,
             o_rqk, o_rv, o_dqk, o_dvv, o_dba, o_cw, o_gp):
        i = pl.program_id(0)
        first, last = i == 0, i == n_t - 1

        @pl.when(first)
        def _():
            o_cw[...] = jnp.zeros_like(o_cw)
            o_gp[...] = jnp.zeros_like(o_gp)

        cos, sin = cos_ref[...], sin_ref[...]
        for s in range(2 * NH):
            gsl = g_rqk[0, :, s * DK:(s + 1) * DK] + g_rqk[1, :, s * DK:(s + 1) * DK]
            dt = gsl * cos + _swap_halves(gsl * sin)
            if s >= NH:
                dt = dt * (DK ** -0.5)
            o_rqk[:, s * DK:(s + 1) * DK] = dt.astype(BF16)
        o_rv[...] = (g_rv[0] + g_rv[1]).astype(BF16)

        ba = dba_ref[...]
        gg = g_gbc[0] + g_gbc[1]
        lane = lax.broadcasted_iota(jnp.int32, ba.shape, 1)
        sb = _sigmoid(ba)
        z = ba + gp_ref[1:2, :]
        a_row = gp_ref[0:1, :]
        dz = gg * a_row * _sigmoid(z)
        o_dba[...] = jnp.where(lane < 8, gg * sb * (1.0 - sb), dz).astype(BF16)
        is_g = (lane >= 8) & (lane < 16)
        o_gp[0:1, :] += jnp.sum(jnp.where(is_g, gg * a_row * _softplus(z), 0.0), axis=0, keepdims=True)
        o_gp[1:2, :] += jnp.sum(jnp.where(is_g, dz, 0.0), axis=0, keepdims=True)

        for ci in range(4):
            src, sp, sn = (dqk_ref, dqk_p, dqk_n) if ci < 2 else (dvv_ref, dvv_p, dvv_n)
            c0 = (ci % 2) * cw
            gc0 = ci * cw
            win = jnp.concatenate([
                jnp.where(first, 0.0, sp[:, c0:c0 + cw]), src[:, c0:c0 + cw],
                jnp.where(last, 0.0, sn[:, c0:c0 + cw])], axis=0)
            gs, gsp, gsn = (g_dqk, g_dqk_p, g_dqk_n) if ci < 2 else (g_dvv, g_dvv_p, g_dvv_n)
            gext = jnp.concatenate([
                jnp.where(first, 0.0, gsp[0, :, c0:c0 + cw] + gsp[1, :, c0:c0 + cw]),
                gs[0, :, c0:c0 + cw] + gs[1, :, c0:c0 + cw],
                jnp.where(last, 0.0, gsn[0, :, c0:c0 + cw] + gsn[1, :, c0:c0 + cw])], axis=0)
            y = _conv_chunk(win, cw_ref, gc0, cw)
            sg = _sigmoid(y)
            sv = y * sg
            if ci < 2:
                scale = DK ** -0.5 if ci == 0 else 1.0
                parts = []
                for hh in range(NH):
                    sh = sv[:, hh * DK:(hh + 1) * DK]
                    gh = gext[:, hh * DK:(hh + 1) * DK]
                    nrm = lax.rsqrt(jnp.sum(sh * sh, axis=-1, keepdims=True) + EPS)
                    dot = jnp.sum(gh * sh, axis=-1, keepdims=True)
                    parts.append(scale * nrm * (gh - sh * (nrm * nrm) * dot))
                ds = jnp.concatenate(parts, axis=1)
            else:
                ds = gext
            dy = ds * (sg + sv * (1.0 - sg))
            dpe = cw_ref[0:1, gc0:gc0 + cw] * _shift_rows(dy, 2)
            for j in range(1, CONV_K):
                dpe += cw_ref[j:j + 1, gc0:gc0 + cw] * _shift_rows(dy, 2 - j)
            dst = o_dqk if ci < 2 else o_dvv
            dst[:, c0:c0 + cw] = dpe[HALO:HALO + tm].astype(BF16)
            dyc = dy[HALO:HALO + tm]
            for j in range(CONV_K):
                o_cw[j:j + 1, gc0:gc0 + cw] += jnp.sum(dyc * _shift_rows(win, j - 2)[HALO:HALO + tm], axis=0,
                                                      keepdims=True)

    hq = _halo_specs(tm, t_len, D, C_DQK)
    hv = _halo_specs(tm, t_len, D, C_DVV)
    hg = _halo_specs(tm, t_len, D, 0, lead=True)
    outs = pl.pallas_call(
        body, name="feat_bwd", grid=(n_t,),
        in_specs=[
            pl.BlockSpec((tm, D), lambda i: (i, C_DQK)), hq[0], hq[1],
            pl.BlockSpec((tm, D), lambda i: (i, C_DVV)), hv[0], hv[1],
            pl.BlockSpec((tm, LANE), lambda i: (i, C_DBA)),
            pl.BlockSpec((tm, LANE), lambda i: (i, 0)),
            pl.BlockSpec((tm, LANE), lambda i: (i, 0)),
            pl.BlockSpec((8, 2 * D), lambda i: (0, 0)),
            pl.BlockSpec((8, LANE), lambda i: (0, 0)),
            pl.BlockSpec((2, tm, D), lambda i: (0, i, 0)),
            pl.BlockSpec((2, tm, D), lambda i: (0, i, 0)),
            pl.BlockSpec((2, tm, D), lambda i: (0, i, 0)), hg[0], hg[1],
            pl.BlockSpec((2, tm, D), lambda i: (0, i, 0)), hg[0], hg[1],
            pl.BlockSpec((2, tm, LANE), lambda i: (0, i, 0)),
        ],
        out_specs=[pl.BlockSpec((tm, D), lambda i: (i, 0))] * 4
        + [pl.BlockSpec((tm, LANE), lambda i: (i, 0)),
           pl.BlockSpec((8, 2 * D), lambda i: (0, 0)),
           pl.BlockSpec((8, LANE), lambda i: (0, 0))],
        out_shape=[jax.ShapeDtypeStruct((t_len, D), BF16)] * 4
        + [jax.ShapeDtypeStruct((t_len, LANE), BF16),
           jax.ShapeDtypeStruct((8, 2 * D), F32), jax.ShapeDtypeStruct((8, LANE), F32)],
        compiler_params=_params(1),
    )(p, p, p, p, p, p, p, cos2, sin2, conv_w8, gparams, d_rqk, d_rv, d_dqk, d_dqk, d_dqk, d_dvv, d_dvv, d_dvv,
      d_gbc)
    return outs


def _ret_chunk(q, k, v, s, lg, rev):
    c = q.shape[0]
    ii = lax.broadcasted_iota(jnp.int32, (c, c), 0).astype(F32)
    jj = lax.broadcasted_iota(jnp.int32, (c, c), 1).astype(F32)
    diff = jnp.where(rev, jj - ii, ii - jj)
    mask = diff >= jnp.where(rev, 1.0, 0.0)
    dec =jnp.where(mask, jnp.exp(lg * jnp.where(mask, diff, 0.0)), 0.0)
    idx = lax.broadcasted_iota(jnp.int32, (c, 1), 0).astype(F32)
    idx = jnp.where(rev, c - 1.0 - idx, idx)
    kdec = k * jnp.exp(lg * (c - 1.0 - idx))
    qdec = q * jnp.exp(lg * (idx + 1.0))
    o = mm_nn(mm_nt(q, k) * dec, v) + mm_nn(qdec, s)
    s2 = s * math.exp(lg * c) + mm_tn(kdec, v)
    return o, s2


def _log_gamma(h):
    return math.log1p(-(2.0 ** (-5.0 - h)))


DN_W = 3 * NH * DK
HS = NH * DN_C


def _dn_intra4(q4, k4, v4, g4, beta4, rev, tinv_known=None):
    ii = lax.broadcasted_iota(jnp.int32, (HS, HS), 0)
    jj = lax.broadcasted_iota(jnp.int32, (HS, HS), 1)
    blk = (ii // DN_C) == (jj // DN_C)
    dd = jnp.where(rev, jj - ii, ii - jj)
    incl = blk & (dd >= 0)
    strict = blk & (dd > 0)
    gb = jnp.broadcast_to(g4, (HS, LANE))
    gcb = _cum_rows(incl.astype(F32), gb)
    gc = jnp.max(gcb, axis=1, keepdims=True)
    gcr = _row_bcast(gcb)
    at_end = blk & ((jj % DN_C) == jnp.where(rev, 0, DN_C - 1))
    glast = jnp.sum(jnp.where(at_end, gcr, 0.0), axis=1, keepdims=True)
    decay = jnp.where(incl, jnp.exp(jnp.where(incl, gc - gcr, 0.0)), 0.0)
    kb = k4 * beta4
    a = jnp.where(strict, mm_nt(kb, k4) * decay, 0.0)
    tinv = _tri_inv(a) if tinv_known is None else _tri_inv_known(a, tinv_known)
    u = mm_nn(tinv, v4 * beta4)
    w = mm_nn(tinv, kb * jnp.exp(gc))
    attn = jnp.where(incl, mm_nt(q4, k4) * decay, 0.0)
    return (u, w, q4 * jnp.exp(gc), k4 * jnp.exp(glast - gc), attn), tinv


def _dn_seq4(u, w, qg, kd, attn, glast, s):
    v_new = [uh - mm_nn(wh, sh) for uh, wh, sh in zip(u, w, s)]
    s2 = [sh * jnp.exp(gh) + mm_tn(kh, vh) for sh, gh, kh, vh in zip(s, glast, kd, v_new)]
    o = jnp.concatenate([mm_nn(qh, sh) for qh, sh in zip(qg, s)], axis=0) + mm_nn(attn, jnp.concatenate(v_new, axis=0))
    return o, s2


def _stack_heads(ref_rows, width, col0=0):
    return jnp.concatenate([ref_rows(slice(col0 + h * width, col0 + (h + 1) * width)) for h in range(NH)], axis=0)


def _head_rows(h):
    return slice(h * DN_C, (h + 1) * DN_C)


def _dn_gates4(gbv, d):
    g4 = jnp.concatenate([_pick_lane(gbv, 8 + 4 * d + h) for h in range(NH)], axis=0)
    b4 = jnp.concatenate([_pick_lane(gbv, 4 * d + h) for h in range(NH)], axis=0)
    return g4, b4


def dn_intra_fwd(dqkv, gbc):
    t_len = dqkv.shape[0]
    rb = 256
    n_g = rb // DN_C

    def body(qk_ref, v_ref, gb_ref, u_ref, wqk_ref, at_ref, ti_ref):
        d = pl.program_id(0)
        rev = d == 1
        for gi in range(n_g):
            rows = slice(gi * DN_C, (gi + 1) * DN_C)
            g4, b4 = _dn_gates4(gb_ref[rows, :], d)
            (u, w, qg, kd, attn), tinv = _dn_intra4(_stack_heads(lambda cs: qk_ref[rows, cs], DK),
                                                    _stack_heads(lambda cs: qk_ref[rows, cs], DK, NH * DK),
                                                    _stack_heads(lambda cs: v_ref[rows, cs], DV), g4, b4, rev)
            at_ref[0, gi] = attn.astype(BF16)
            ti_ref[0, gi] = tinv
            for h in range(NH):
                hr = _head_rows(h)
                u_ref[0, rows, h * DV:(h + 1) * DV] = u[hr]
                wqk_ref[0, rows, h * DK:(h + 1) * DK] = w[hr].astype(BF16)
                wqk_ref[0, rows, NH * DK + h * DK:NH * DK + (h + 1) * DK] = qg[hr].astype(BF16)
                wqk_ref[0, rows, 2 * NH * DK + h * DK:2 * NH * DK + (h + 1) * DK] = kd[hr].astype(BF16)

    return pl.pallas_call(
        body, name="dn_intra_fwd", grid=(2, t_len // rb),
        in_specs=[pl.BlockSpec((rb, D), lambda d, i: (i, 0)), pl.BlockSpec((rb, D), lambda d, i: (i, 1)),
                  pl.BlockSpec((rb, LANE), lambda d, i: (i, 0))],
        out_specs=[pl.BlockSpec((1, rb, D), lambda d, i: (d, i, 0)),
                   pl.BlockSpec((1, rb, DN_W), lambda d, i: (d, i, 0)),
                   pl.BlockSpec((1, n_g, HS, HS), lambda d, i: (d, i, 0, 0)),
                   pl.BlockSpec((1, n_g, HS, HS), lambda d, i: (d, i, 0, 0))],
        out_shape=[jax.ShapeDtypeStruct((2, t_len, D), F32), jax.ShapeDtypeStruct((2, t_len, DN_W), BF16),
                   jax.ShapeDtypeStruct((2, t_len // DN_C, HS, HS), BF16),
                   jax.ShapeDtypeStruct((2, t_len // DN_C, HS, HS), F32)],
        compiler_params=_params(2),
    )(dqkv, dqkv, gbc)


def _dn_seq_inputs(u_ref, wqk_ref, at_ref, gbv, d):
    u = [u_ref[0, :, h * DV:(h + 1) * DV] for h in range(NH)]
    w = [wqk_ref[0, :, h * DK:(h + 1) * DK].astype(F32) for h in range(NH)]
    qg = [wqk_ref[0, :, NH * DK + h * DK:NH * DK + (h + 1) * DK].astype(F32) for h in range(NH)]
    kd = [wqk_ref[0, :, 2 * NH * DK + h * DK:2 * NH * DK + (h + 1) * DK].astype(F32) for h in range(NH)]
    glast = [jnp.sum(_pick_lane(gbv, 8 + 4 * d + h), axis=0, keepdims=True) for h in range(NH)]
    return u, w, qg, kd, at_ref[0, 0].astype(F32), glast


def dn_seq_fwd(u, wqk, attn, gbc, s0):
    t_len = u.shape[1]
    n_c = t_len // DN_C

    def chunk_of(d, t):
        return jnp.where(d == 0, t, n_c - 1 - t)

    def body(u_ref, wqk_ref, at_ref, gb_ref, s0_ref, o_ref, ss_ref, sf_ref, st):
        d, t = pl.program_id(0), pl.program_id(1)

        @pl.when(t == 0)
        def _():
            st[...] = s0_ref[0]

        s_in = [st[h] for h in range(NH)]
        ss_ref[0, 0] = st[...].astype(BF16)
        o, s2 = _dn_seq4(*_dn_seq_inputs(u_ref, wqk_ref, at_ref, gb_ref[...], d), s_in)
        for h in range(NH):
            o_ref[0, :, h * DV:(h + 1) * DV] = o[_head_rows(h)]
            st[h] = s2[h]

        @pl.when(t == n_c - 1)
        def _():
            sf_ref[0] = st[...]

    return pl.pallas_call(
        body, name="dn_seq_fwd", grid=(2, n_c),
        in_specs=[pl.BlockSpec((1, DN_C, D), lambda d, t: (d, chunk_of(d, t), 0)),
                  pl.BlockSpec((1, DN_C, DN_W), lambda d, t: (d, chunk_of(d, t), 0)),
                  pl.BlockSpec((1, 1, HS, HS), lambda d, t: (d, chunk_of(d, t), 0, 0)),
                  pl.BlockSpec((DN_C, LANE), lambda d, t: (chunk_of(d, t), 0)),
                  _state_spec()],
        out_specs=[pl.BlockSpec((1, DN_C, D), lambda d, t: (d, chunk_of(d, t), 0)),
                   pl.BlockSpec((1, 1, NH, DK, DV), lambda d, t: (d, chunk_of(d, t), 0, 0, 0)),
                   _state_spec()],
        out_shape=[jax.ShapeDtypeStruct((2, t_len, D), F32),
                   jax.ShapeDtypeStruct((2, n_c, NH, DK, DV), BF16),
                   jax.ShapeDtypeStruct((2, NH, DK, DV), F32)],
        scratch_shapes=[pltpu.VMEM((NH, DK, DV), F32)],
        compiler_params=_params(2),
    )(u, wqk, attn, gbc, s0)


def dn_seq_bwd(u, wqk, attn, gbc, ssave, do, ds_fin):
    t_len = u.shape[1]
    n_c = t_len // DN_C

    def chunk_of(d, t):
        return jnp.where(d == 0, n_c - 1 - t, t)

    def body(u_ref, wqk_ref, at_ref, gb_ref, ss_ref, do_ref, dsf_ref, du_ref, dwqk_ref, dat_ref, dgl_ref, ds0_ref, dst):
        d, t = pl.program_id(0), pl.program_id(1)

        @pl.when(t == 0)
        def _():
            dst[...] = dsf_ref[0]

        s_in = [ss_ref[0, 0, h].astype(F32) for h in range(NH)]
        _, vjp = jax.vjp(_dn_seq4, *_dn_seq_inputs(u_ref, wqk_ref, at_ref, gb_ref[...], d), s_in)
        do4 = _stack_heads(lambda cs: do_ref[:, cs], DV)
        du, dw, dqg, dkd, dat, dgl, ds = vjp((do4, [dst[h] for h in range(NH)]))
        dat_ref[0, 0] = dat
        rows8 = lax.broadcasted_iota(jnp.int32, (8, LANE), 0)
        dgl_tile = jnp.zeros((8, LANE), F32)
        for h in range(NH):
            du_ref[0, :, h * DV:(h + 1) * DV] = du[h]
            dwqk_ref[0, :, h * DK:(h + 1) * DK] = dw[h]
            dwqk_ref[0, :, NH * DK + h * DK:NH * DK + (h + 1) * DK] = dqg[h]
            dwqk_ref[0, :, 2 * NH * DK + h * DK:2 * NH * DK + (h + 1) * DK] = dkd[h]
            dgl_tile = jnp.where(rows8 == h, dgl[h], dgl_tile)
            dst[h] = ds[h]
        dgl_ref[0, 0] = dgl_tile

        @pl.when(t == n_c - 1)
        def _():
            ds0_ref[0] = dst[...]

    seq = lambda width: pl.BlockSpec((1, DN_C, width), lambda d, t: (d, chunk_of(d, t), 0))
    att = pl.BlockSpec((1, 1, HS, HS), lambda d, t: (d, chunk_of(d, t), 0, 0))
    return pl.pallas_call(
        body, name="dn_seq_bwd", grid=(2, n_c),
        in_specs=[seq(D), seq(DN_W), att, pl.BlockSpec((DN_C, LANE), lambda d, t: (chunk_of(d, t), 0)),
                  pl.BlockSpec((1, 1, NH, DK, DV), lambda d, t: (d, chunk_of(d, t), 0, 0, 0)),
                  pl.BlockSpec((DN_C, D), lambda d, t: (chunk_of(d, t), 0)), _state_spec()],
        out_specs=[seq(D), seq(DN_W), att, pl.BlockSpec((1, 1, 8, LANE), lambda d, t: (d, chunk_of(d, t), 0, 0)),
                   _state_spec()],
        out_shape=[jax.ShapeDtypeStruct((2, t_len, D), F32), jax.ShapeDtypeStruct((2, t_len, DN_W), F32),
                   jax.ShapeDtypeStruct((2, n_c, HS, HS), F32), jax.ShapeDtypeStruct((2, n_c, 8, LANE), F32),
                   jax.ShapeDtypeStruct((2, NH, DK, DV), F32)],
        scratch_shapes=[pltpu.VMEM((NH, DK, DV), F32)],
        compiler_params=_params(2),
    )(u, wqk, attn, gbc, ssave, do, ds_fin)


def dn_intra_bwd(dqkv, gbc, tinv, du, dwqk, dattn, dgl):
    t_len = dqkv.shape[0]
    rb = 128
    n_g = rb // DN_C

    def body(qk_ref, v_ref, gb_ref, ti_ref, du_ref, dwqk_ref, dat_ref, dgl_ref, dqk_ref, dv_ref, dgb_ref):
        d = pl.program_id(0)
        rev = d == 1
        for gi in range(n_g):
            rows = slice(gi * DN_C, (gi + 1) * DN_C)
            g4, b4 = _dn_gates4(gb_ref[rows, :], d)
            _, vjp, _ = jax.vjp(functools.partial(_dn_intra4, rev=rev, tinv_known=ti_ref[0, gi]),
                                _stack_heads(lambda cs: qk_ref[rows, cs], DK),
                                _stack_heads(lambda cs: qk_ref[rows, cs], DK, NH * DK),
                                _stack_heads(lambda cs: v_ref[rows, cs], DV), g4, b4, has_aux=True)
            dq, dk, dv, dg, db = vjp((_stack_heads(lambda cs: du_ref[0, rows, cs], DV),
                                      _stack_heads(lambda cs: dwqk_ref[0, rows, cs], DK),
                                      _stack_heads(lambda cs: dwqk_ref[0, rows, cs], DK, NH * DK),
                                      _stack_heads(lambda cs: dwqk_ref[0, rows, cs], DK, 2 * NH * DK),
                                      dat_ref[0, gi]))
            dgb = jnp.zeros((DN_C, LANE), F32)
            for h in range(NH):
                hr = _head_rows(h)
                dqk_ref[0, rows, h * DK:(h + 1) * DK] = dq[hr]
                dqk_ref[0, rows, NH * DK + h * DK:NH * DK + (h + 1) * DK] = dk[hr]
                dv_ref[0, rows, h * DV:(h + 1) * DV] = dv[hr]
                dgb = (dgb + _put_lane(dg[hr] + dgl_ref[0, gi, h:h + 1, 0:1], 8 + 4 * d + h, LANE)
                       + _put_lane(db[hr], 4 * d + h, LANE))
            dgb_ref[0, rows, :] = dgb

    seq = lambda width: pl.BlockSpec((1, rb, width), lambda d, i: (d, i, 0))
    return pl.pallas_call(
        body, name="dn_intra_bwd", grid=(2, t_len // rb),
        in_specs=[pl.BlockSpec((rb, D), lambda d, i: (i, 0)), pl.BlockSpec((rb, D), lambda d, i: (i, 1)),
                  pl.BlockSpec((rb, LANE), lambda d, i: (i, 0)),
                  pl.BlockSpec((1, n_g, HS, HS), lambda d, i: (d, i, 0, 0)), seq(D), seq(DN_W),
                  pl.BlockSpec((1, n_g, HS, HS), lambda d, i: (d, i, 0, 0)),
                  pl.BlockSpec((1, n_g, 8, LANE), lambda d, i: (d, i, 0, 0))],
        out_specs=[seq(D), seq(D), seq(LANE)],
        out_shape=[jax.ShapeDtypeStruct((2, t_len, D), F32), jax.ShapeDtypeStruct((2, t_len, D), F32),
                   jax.ShapeDtypeStruct((2, t_len, LANE), F32)],
        compiler_params=_params(2),
    )(dqkv, dqkv, gbc, tinv, du, dwqk, dattn, dgl)


def _pick_lane(x, lane_idx):
    lane = lax.broadcasted_iota(jnp.int32, x.shape, 1)
    return jnp.sum(jnp.where(lane == lane_idx, x, 0.0), axis=1, keepdims=True)


def _put_lane(col, lane_idx, width):
    lane = lax.broadcasted_iota(jnp.int32, (col.shape[0], width), 1)
    return jnp.where(lane == lane_idx, col, 0.0)


def _state_spec():
    return pl.BlockSpec((1, NH, DK, DV), lambda d, t: (d, 0, 0, 0))


def ret_fwd(rqk, p, s0):
    c = RET_C
    t_len = rqk.shape[0]
    n_c = t_len // c

    def chunk_of(d, t):
        return jnp.where(d == 0, t, n_c - 1 - t)

    def body(qk_ref, v_ref, s0_ref, o_ref, ss_ref, sf_ref, st):
        d, t = pl.program_id(0), pl.program_id(1)
        rev = d == 1

        @pl.when(t == 0)
        def _():
            st[...] = s0_ref[0]

        for h in range(NH):
            sh = st[h]
            ss_ref[0, 0, h] = sh.astype(BF16)
            o, s2 = _ret_chunk(qk_ref[:, h * DK:(h + 1) * DK], qk_ref[:, NH * DK + h * DK:NH * DK + (h + 1) * DK],
                               v_ref[:, h * DV:(h + 1) * DV], sh, _log_gamma(h), rev)
            o_ref[0, :, h * DV:(h + 1) * DV] = o
            st[h] = s2

        @pl.when(t == n_c - 1)
        def _():
            sf_ref[0] = st[...]

    return pl.pallas_call(
        body, name="ret_fwd", grid=(2, n_c),
        in_specs=[pl.BlockSpec((c, D), lambda d, t: (chunk_of(d, t), 0)),
                  pl.BlockSpec((c, D), lambda d, t: (chunk_of(d, t), C_RV)),
                  _state_spec()],
        out_specs=[pl.BlockSpec((1, c, D), lambda d, t: (d, chunk_of(d, t), 0)),
                   pl.BlockSpec((1, 1, NH, DK, DV), lambda d, t: (d, chunk_of(d, t), 0, 0, 0)),
                   _state_spec()],
        out_shape=[jax.ShapeDtypeStruct((2, t_len, D), F32),
                   jax.ShapeDtypeStruct((2, n_c, NH, DK, DV), BF16),
                   jax.ShapeDtypeStruct((2, NH, DK, DV), F32)],
        scratch_shapes=[pltpu.VMEM((NH, DK, DV), F32)],
        compiler_params=_params(2),
    )(rqk, p, s0)


def ret_bwd(rqk, p, ssave, do, ds_fin):
    c = RET_C
    t_len = rqk.shape[0]
    n_c = t_len // c

    def chunk_of(d, t):
        return jnp.where(d == 0, n_c - 1 - t, t)

    def body(qk_ref, v_ref, ss_ref, do_ref, dsf_ref, dqk_ref, dv_ref, ds0_ref, dst):
        d, t = pl.program_id(0), pl.program_id(1)
        rev = d == 1

        @pl.when(t == 0)
        def _():
            dst[...] = dsf_ref[0]

        for h in range(NH):
            _, vjp = jax.vjp(functools.partial(_ret_chunk, lg=_log_gamma(h), rev=rev),
                             qk_ref[:, h * DK:(h + 1) * DK], qk_ref[:, NH * DK + h * DK:NH * DK + (h + 1) * DK],
                             v_ref[:, h * DV:(h + 1) * DV], ss_ref[0, 0, h].astype(F32))
            dq, dk, dv, ds = vjp((do_ref[:, h * DV:(h + 1) * DV], dst[h]))
            dqk_ref[0, :, h * DK:(h + 1) * DK] = dq
            dqk_ref[0, :, NH * DK + h * DK:NH * DK + (h + 1) * DK] = dk
            dv_ref[0, :, h * DV:(h + 1) * DV] = dv
            dst[h] = ds

        @pl.when(t == n_c - 1)
        def _():
            ds0_ref[0] = dst[...]

    seq_spec = pl.BlockSpec((1, c, D), lambda d, t: (d, chunk_of(d, t), 0))
    return pl.pallas_call(
        body, name="ret_bwd", grid=(2, n_c),
        in_specs=[pl.BlockSpec((c, D), lambda d, t: (chunk_of(d, t), 0)),
                  pl.BlockSpec((c, D), lambda d, t: (chunk_of(d, t), C_RV)),
                  pl.BlockSpec((1, 1, NH, DK, DV), lambda d, t: (d, chunk_of(d, t), 0, 0, 0)),
                  pl.BlockSpec((c, D), lambda d, t: (chunk_of(d, t), 0)),
                  _state_spec()],
        out_specs=[seq_spec, seq_spec, _state_spec()],
        out_shape=[jax.ShapeDtypeStruct((2, t_len, D), F32)] * 2 + [jax.ShapeDtypeStruct((2, NH, DK, DV), F32)],
        scratch_shapes=[pltpu.VMEM((NH, DK, DV), F32)],
        compiler_params=_params(2),
    )(rqk, p, ssave, do, ds_fin)


def _head_gate(o_ret, o_dn, rg, dz, nw):
    ret = o_ret * lax.rsqrt(jnp.mean(o_ret * o_ret, axis=-1, keepdims=True) + EPS) * (rg * _sigmoid(rg))
    dn = o_dn * lax.rsqrt(jnp.mean(o_dn * o_dn, axis=-1, keepdims=True) + EPS) * nw * (dz * _sigmoid(dz))
    return ret, dn


MIX_TM = 256


def _mix_specs():
    seq = lambda col: pl.BlockSpec((MIX_TM, D), functools.partial(lambda i, col: (i, col), col=col))
    pair = pl.BlockSpec((2, MIX_TM, D), lambda i: (0, i, 0))
    wfull = pl.BlockSpec((D, D), lambda i: (0, 0))
    vec = pl.BlockSpec((8, D), lambda i: (0, 0))
    return seq, pair, wfull, vec


def mixout_fwd(x1, vec, o_ret, o_dn, p, w_r, w_d, w_o):
    t_len = x1.shape[0]

    def body(x_ref, vec_ref, or_ref, od_ref, rg_ref, dz_ref, ga_ref, gb_ref, wr_ref, wd_ref, wo_ref,
             x2_ref, ret_ref, dn_ref, y_ref, yr_ref, yd_ref, z_ref):
        for h in range(NH):
            sl = slice(h * DV, (h + 1) * DV)
            ret, dn = _head_gate(or_ref[0, :, sl] + or_ref[1, :, sl], od_ref[0, :, sl] + od_ref[1, :, sl],
                                 rg_ref[:, sl], dz_ref[:, sl], vec_ref[1:2, sl])
            ret_ref[:, sl] = ret.astype(BF16)
            dn_ref[:, sl] = dn.astype(BF16)
        yr = jnp.dot(ret_ref[...], wr_ref[...], preferred_element_type=F32)
        yd = jnp.dot(dn_ref[...], wd_ref[...], preferred_element_type=F32)
        yr_ref[...] = yr.astype(BF16)
        yd_ref[...] = yd.astype(BF16)
        y = (_sigmoid(ga_ref[...]) * yr + _sigmoid(gb_ref[...]) * yd).astype(BF16)
        y_ref[...] = y
        z = jnp.dot(y, wo_ref[...], preferred_element_type=F32)
        z_ref[...] = z.astype(BF16)
        x2_ref[...] = x_ref[...] + vec_ref[0:1, :] * z

    seq, pair, wfull, vecs = _mix_specs()
    return pl.pallas_call(
        body, name="mixout_fwd", grid=(t_len // MIX_TM,),
        in_specs=[seq(0), vecs, pair, pair, seq(C_RG), seq(C_DZ), seq(C_GA), seq(C_GB), wfull, wfull, wfull],
        out_specs=[seq(0)] * 7,
        out_shape=[jax.ShapeDtypeStruct((t_len, D), F32)] + [jax.ShapeDtypeStruct((t_len, D), BF16)] * 6,
        compiler_params=_params(1),
    )(x1, vec, o_ret, o_dn, p, p, p, p, w_r, w_d, w_o)


def mixout_bwd(dx2, vec, o_ret, o_dn, p, yr, yd, z, w_r, w_d, w_o):
    t_len = dx2.shape[0]

    def body(dx_ref, vec_ref, or_ref, od_ref, rg_ref, dz_ref, ga_ref, gb_ref, yr_ref, yd_ref, z_ref,
             wr_ref, wd_ref, wo_ref,
             dor_ref, dod_ref, drg_ref, ddz_ref, dga_ref, dgb_ref, dyr_ref, dyd_ref, dzz_ref, pg_ref):
        @pl.when(pl.program_id(0) == 0)
        def _():
            pg_ref[...] = jnp.zeros_like(pg_ref)

        dx = dx_ref[...]
        pg_ref[0:1, :] += jnp.sum(dx * z_ref[...].astype(F32), axis=0, keepdims=True)
        dzz = (vec_ref[0:1, :] * dx).astype(BF16)
        dzz_ref[...] = dzz
        dy = _dot(dzz, wo_ref[...], NT)
        sa = _sigmoid(ga_ref[...])
        sb = _sigmoid(gb_ref[...])
        dyr = (dy * sa).astype(BF16)
        dyd = (dy * sb).astype(BF16)
        dyr_ref[...] = dyr
        dyd_ref[...] = dyd
        dga_ref[...] = (dy * yr_ref[...].astype(F32) * sa * (1.0 - sa)).astype(BF16)
        dgb_ref[...] = (dy * yd_ref[...].astype(F32) * sb * (1.0 - sb)).astype(BF16)
        dret = _dot(dyr, wr_ref[...], NT)
        ddn = _dot(dyd, wd_ref[...], NT)
        for h in range(NH):
            sl = slice(h * DV, (h + 1) * DV)
            _, vjp = jax.vjp(_head_gate, or_ref[0, :, sl] + or_ref[1, :, sl], od_ref[0, :, sl] + od_ref[1, :, sl],
                             rg_ref[:, sl], dz_ref[:, sl], vec_ref[1:2, sl])
            d_or, d_od, d_rg, d_dz, d_nw = vjp((dret[:, sl], ddn[:, sl]))
            dor_ref[:, sl] = d_or
            dod_ref[:, sl] = d_od
            drg_ref[:, sl] = d_rg.astype(BF16)
            ddz_ref[:, sl] = d_dz.astype(BF16)
            pg_ref[1:2, sl] += d_nw

    seq, pair, wfull, vecs = _mix_specs()
    return pl.pallas_call(
        body, name="mixout_bwd", grid=(t_len // MIX_TM,),
        in_specs=[seq(0), vecs, pair, pair, seq(C_RG), seq(C_DZ), seq(C_GA), seq(C_GB), seq(0), seq(0), seq(0),
                  wfull, wfull, wfull],
        out_specs=[seq(0)] * 9 + [vecs],
        out_shape=[jax.ShapeDtypeStruct((t_len, D), F32)] * 2 + [jax.ShapeDtypeStruct((t_len, D), BF16)] * 7
        + [jax.ShapeDtypeStruct((8, D), F32)],
        compiler_params=_params(1),
    )(dx2, vec, o_ret, o_dn, p, p, p, p, yr, yd, z, w_r, w_d, w_o)


def _final_loss(x, w, target):
    y = x * lax.rsqrt(jnp.mean(x * x, axis=-1, keepdims=True) + EPS) * w
    err = y - target
    return 0.5 * jnp.sum(jnp.mean(err * err, axis=-1, keepdims=True), axis=0, keepdims=True)


def final_fwd_bwd(x, vec, target):
    t_len = x.shape[0]
    tm = _row_block(t_len)

    def body(x_ref, vec_ref, t_ref, dx_ref, pg_ref, loss_ref):
        @pl.when(pl.program_id(0) == 0)
        def _():
            pg_ref[...] = jnp.zeros_like(pg_ref)
            loss_ref[...] = jnp.zeros_like(loss_ref)

        loss, vjp = jax.vjp(functools.partial(_final_loss, target=t_ref[...]), x_ref[...], vec_ref[0:1, :])
        dx, dw = vjp(jnp.ones((1, 1), F32))
        dx_ref[...] = dx
        pg_ref[0:1, :] += dw
        loss_ref[...] += jnp.broadcast_to(loss, loss_ref.shape)

    return pl.pallas_call(
        body, name="final_fwd_bwd", grid=(t_len // tm,),
        in_specs=[pl.BlockSpec((tm, D), lambda i: (i, 0)), pl.BlockSpec((8, D), lambda i: (0, 0)),
                  pl.BlockSpec((tm, D), lambda i: (i, 0))],
        out_specs=[pl.BlockSpec((tm, D), lambda i: (i, 0)), pl.BlockSpec((8, D), lambda i: (0, 0)),
                   pl.BlockSpec((8, LANE), lambda i: (0, 0))],
        out_shape=[jax.ShapeDtypeStruct((t_len, D), F32), jax.ShapeDtypeStruct((8, D), F32),
                   jax.ShapeDtypeStruct((8, LANE), F32)],
        compiler_params=_params(1),
    )(x, vec, target)


def _vec8(*rows):
    rows = list(rows) + [jnp.zeros((D,), F32)] * (8 - len(rows))
    return jnp.stack(rows)


def _layer_vecs(lw, m):
    nw = lw["norm_w"]
    return dict(ffn1=_vec8(nw[0], m[0], m[1], m[2]), proj=_vec8(nw[1], m[3], m[4]),
                mix=_vec8(m[5], lw["dn_norm_t"]), ffn2=_vec8(nw[2], m[6], m[7], m[8]))


def _rope_tables(t_len, grid_w=64, base=10000.0):
    n_freq = DK // 4
    inv = base ** (-jnp.arange(n_freq, dtype=F32) / n_freq)
    tok = jnp.arange(t_len)
    ang = jnp.concatenate([(tok // grid_w).astype(F32)[:, None] * inv, (tok % grid_w).astype(F32)[:, None] * inv],
                          axis=-1)
    cos, sin = jnp.cos(ang), jnp.sin(ang)
    return jnp.concatenate([cos, cos], axis=-1), jnp.concatenate([-sin, sin], axis=-1)


def _stream_fwd_a(x0, lw, vecs, rope):
    x1, h1, g1, u1, f1 = ffn_fwd(x0, vecs["ffn1"], lw["wgu1"], lw["wd1"])
    p, h2 = proj_fwd(x1, vecs["proj"], lw["w_in"])
    rqk, dqkv, gbc = feat_fwd(p, rope[0], rope[1], lw["conv_w8"], lw["gparams"])
    return dict(x0=x0, h1=h1, g1=g1, u1=u1, f1=f1, x1=x1, h2=h2, p=p, rqk=rqk, dqkv=dqkv, gbc=gbc)


def _stream_mix(sv, s0_ret, s0_dn):
    sv["o_ret"], sv["ss_ret"], sf_ret = ret_fwd(sv["rqk"], sv["p"], s0_ret)
    sv["dn_u"], sv["dn_wqk"], sv["dn_attn"], sv["dn_tinv"] = dn_intra_fwd(sv["dqkv"], sv["gbc"])
    sv["o_dn"], sv["ss_dn"], sf_dn = dn_seq_fwd(sv["dn_u"], sv["dn_wqk"], sv["dn_attn"], sv["gbc"], s0_dn)
    return sf_ret, sf_dn


def _stream_fwd_b(sv, lw, vecs):
    x2, ret, dn, y, yr, yd, z = mixout_fwd(sv["x1"], vecs["mix"], sv["o_ret"], sv["o_dn"], sv["p"],
                                           lw["w_r"], lw["w_d"], lw["w_o"])
    x3, h3, g3, u3, f3 = ffn_fwd(x2, vecs["ffn2"], lw["wgu2"], lw["wd2"])
    sv.update(ret=ret, dn=dn, y=y, yr=yr, yd=yd, z=z, x2=x2, h3=h3, g3=g3, u3=u3, f3=f3)
    return x3


def layer_fwd(xs, cs, lw, ropes):
    vx, vc = _layer_vecs(lw, lw["mx"]), _layer_vecs(lw, lw["mc"])
    sx = _stream_fwd_a(xs, lw, vx, ropes[0])
    sc = _stream_fwd_a(cs, lw, vc, ropes[1])
    zero = jnp.zeros((2, NH, DK, DV), F32)
    sf_ret, sf_dn = _stream_mix(sc, zero, zero)
    _stream_mix(sx, sf_ret, sf_dn)
    x3 = _stream_fwd_b(sx, lw, vx)
    c3 = _stream_fwd_b(sc, lw, vc)
    return x3, c3, (sx, sc)


def _stream_bwd_a(dx3, sv, lw, vecs):
    act3, dg3, du3, dy3, pg_a = ffn_bwd_act(dx3, vecs["ffn2"], sv["g3"], sv["u3"], sv["f3"], lw["wd2"])
    dx2, pg_b = nt_norm_bwd([dg3, du3], lw["wgu2"], [0, 2], FFN_HB, 2, sv["x2"], dx3, vecs["ffn2"], "ffn_bwd_in")
    (dor, dod, drg, ddz, dga, dgb, dyr, dyd, dzz, pg_m) = mixout_bwd(
        dx2, vecs["mix"], sv["o_ret"], sv["o_dn"], sv["p"], sv["yr"], sv["yd"], sv["z"], lw["w_r"], lw["w_d"], lw["w_o"])
    return dict(act3=act3, dg3=dg3, du3=du3, dy3=dy3, pg_a2=pg_a, pg_b2=pg_b, dx2=dx2, dor=dor, dod=dod,
                drg=drg, ddz=ddz, dga=dga, dgb=dgb, dyr=dyr, dyd=dyd, dzz=dzz, pg_m=pg_m)


def _stream_bwd_mix(bw, sv, dsf_ret, dsf_dn):
    bw["dqk_r"], bw["drv"], ds0_ret = ret_bwd(sv["rqk"], sv["p"], sv["ss_ret"], bw["dor"], dsf_ret)
    du, dwqk, dattn, dgl, ds0_dn = dn_seq_bwd(sv["dn_u"], sv["dn_wqk"], sv["dn_attn"], sv["gbc"], sv["ss_dn"],
                                              bw["dod"], dsf_dn)
    bw["dqk_d"], bw["dvv_d"], bw["dgbc"] = dn_intra_bwd(sv["dqkv"], sv["gbc"], sv["dn_tinv"], du, dwqk, dattn, dgl)
    return ds0_ret, ds0_dn


def _stream_bwd_b(bw, sv, lw, vecs, rope):
    dp_rqk, dp_rv, dp_dqk, dp_dvv, dp_dba, bw["dcw"], bw["dgp"] = feat_bwd(
        sv["p"], rope[0], rope[1], lw["conv_w8"], lw["gparams"], bw["dqk_r"], bw["drv"], bw["dqk_d"], bw["dvv_d"],
        bw["dgbc"])
    bw["dp"] = jnp.concatenate([dp_rqk, dp_rv, bw["drg"], dp_dqk, dp_dvv, bw["ddz"], bw["dga"], bw["dgb"], dp_dba],
                               axis=1)
    dx1, bw["pg_p"] = nt_norm_bwd([bw["dp"]], lw["w_in"], [0], PROJ_TN, PW // PROJ_TN, sv["x1"], bw["dx2"],
                                  vecs["proj"], "proj_bwd_in")
    bw["act1"], bw["dg1"], bw["du1"], bw["dy1"], bw["pg_a1"] = ffn_bwd_act(dx1, vecs["ffn1"], sv["g1"], sv["u1"],
                                                                          sv["f1"], lw["wd1"])
    dx0, bw["pg_b1"] = nt_norm_bwd([bw["dg1"], bw["du1"]], lw["wgu1"], [0, 2], FFN_HB, 2, sv["x0"], dx1,
                                   vecs["ffn1"], "ffn_bwd_in")
    return dx0


def _stream_pgrads(bw):
    a1, b1, pp, pm, a2, b2 = bw["pg_a1"], bw["pg_b1"], bw["pg_p"], bw["pg_m"], bw["pg_a2"], bw["pg_b2"]
    dmod = jnp.stack([b1[1], b1[2], a1[3], pp[1], pp[2], pm[0], b2[1], b2[2], a2[3]])
    dnorm = jnp.stack([b1[0], pp[0], b2[0]])
    ddnw = pm[1].reshape(NH, DV).sum(axis=0)
    return dmod, dnorm, ddnw, bw["dcw"][:CONV_K], bw["dgp"][0, 8:16], bw["dgp"][1, 8:16]


def layer_bwd(dx3, dc3, lw, saved, ropes):
    sx, sc = saved
    vx, vc = _layer_vecs(lw, lw["mx"]), _layer_vecs(lw, lw["mc"])
    bx = _stream_bwd_a(dx3, sx, lw, vx)
    bc = _stream_bwd_a(dc3, sc, lw, vc)
    zero = jnp.zeros((2, NH, DK, DV), F32)
    ds0_ret, ds0_dn = _stream_bwd_mix(bx, sx, zero, zero)
    _stream_bwd_mix(bc, sc, ds0_ret, ds0_dn)
    dx0 = _stream_bwd_b(bx, sx, lw, vx, ropes[0])
    dc0 = _stream_bwd_b(bc, sc, lw, vc, ropes[1])

    def wgrad(a, b, tn, name):
        ax = sx[a] if a in sx else bx[a]
        ac = sc[a] if a in sc else bc[a]
        return tn_matmul(ax, bx[b], ac, bc[b], tn, name)

    gw = dict(
        wg1=wgrad("h1", "dg1", FFN_HB, "dw_ffn_gu"), wu1=wgrad("h1", "du1", FFN_HB, "dw_ffn_gu"),
        wd1=wgrad("act1", "dy1", 512, "dw_ffn_d"),
        w_in=wgrad("h2", "dp", PROJ_TN, "dw_in"),
        w_r=wgrad("ret", "dyr", 512, "dw_sq"), w_d=wgrad("dn", "dyd", 512, "dw_sq"), w_o=wgrad("y", "dzz", 512, "dw_sq"),
        wg2=wgrad("h3", "dg3", FFN_HB, "dw_ffn_gu"), wu2=wgrad("h3", "du3", FFN_HB, "dw_ffn_gu"),
        wd2=wgrad("act3", "dy3", 512, "dw_ffn_d"),
    )
    px, pc = _stream_pgrads(bx), _stream_pgrads(bc)
    small = dict(dmx=px[0], dmc=pc[0], norm_w=px[1] + pc[1], dn_norm_w=px[2] + pc[2], conv_w=px[3] + pc[3],
                 a_log=px[4] + pc[4], dt_bias=px[5] + pc[5])
    return dx0, dc0, gw, small


def local_step(x, ctx, target, final_norm_w, lws):
    t_len, t_ctx = x.shape[0], ctx.shape[0]
    ropes = (_rope_tables(t_len), (jnp.ones((t_ctx, LANE), F32), jnp.zeros((t_ctx, LANE), F32)))
    xs, cs, saved = x, ctx, []
    for lw in lws:
        xs, cs, sv = layer_fwd(xs, cs, lw, ropes)
        saved.append(sv)
    dx, pg_f, loss = final_fwd_bwd(xs, _vec8(final_norm_w), target)
    dc = jnp.zeros_like(ctx)
    gws, smalls = [None] * len(lws), [None] * len(lws)
    for l in reversed(range(len(lws))):
        dx, dc, gws[l], smalls[l] = layer_bwd(dx, dc, lws[l], saved[l], ropes)
    return loss[0, 0], dx, gws, smalls, pg_f[0]


ADA_ROWS = 16
ADA_COLS = N_MOD * D // N_DEV


def ada_fwd(cc, ada_w, ada_b_cols):
    def body(cc_ref, w_ref, b_ref, o_ref):
        cv = cc_ref[...]
        o_ref[0] = _dot_hi(cv * _sigmoid(cv), w_ref[0]) + b_ref[0]

    return pl.pallas_call(
        body, name="ada_fwd", grid=(DEPTH,),
        in_specs=[pl.BlockSpec((ADA_ROWS, D), lambda l: (0, 0)), pl.BlockSpec((1, D, ADA_COLS), lambda l: (l, 0, 0)),
                  pl.BlockSpec((1, 1, ADA_COLS), lambda l: (l, 0, 0))],
        out_specs=pl.BlockSpec((1, ADA_ROWS, ADA_COLS), lambda l: (l, 0, 0)),
        out_shape=jax.ShapeDtypeStruct((DEPTH, ADA_ROWS, ADA_COLS), F32),
        compiler_params=_params(1),
    )(cc, ada_w, ada_b_cols)


def ada_bwd(cc, dmod, ada_w):
    def body(cc_ref, d_ref, w_ref, gw_ref, ds_ref):
        cv = cc_ref[...]
        gw_ref[0] = _dot_hi(cv * _sigmoid(cv), d_ref[0], TN)
        ds_ref[0] = _dot_hi(d_ref[0], w_ref[0], NT)

    return pl.pallas_call(
        body, name="ada_bwd", grid=(DEPTH,),
        in_specs=[pl.BlockSpec((ADA_ROWS, D), lambda l: (0, 0)),
                  pl.BlockSpec((1, ADA_ROWS, ADA_COLS), lambda l: (l, 0, 0)),
                  pl.BlockSpec((1, D, ADA_COLS), lambda l: (l, 0, 0))],
        out_specs=[pl.BlockSpec((1, D, ADA_COLS), lambda l: (l, 0, 0)), pl.BlockSpec((1, ADA_ROWS, D), lambda l: (l, 0, 0))],
        out_shape=[jax.ShapeDtypeStruct((DEPTH, D, ADA_COLS), F32), jax.ShapeDtypeStruct((DEPTH, ADA_ROWS, D), F32)],
        compiler_params=_params(1),
    )(cc, dmod, ada_w)


def c_ctx_grad(parts, c_ctx8):
    def body(p_ref, c_ref, o_ref):
        acc = p_ref[0]
        for j in range(1, N_DEV):
            acc = acc + p_ref[j]
        cv = c_ref[...]
        sg = _sigmoid(cv)
        o_ref[...] = acc * (sg + cv * sg * (1.0 - sg))

    return pl.pallas_call(body, name="c_ctx_grad", out_shape=jax.ShapeDtypeStruct((8, LANE), F32))(parts, c_ctx8)


def sum_slots(parts):
    n_slot, rows, _ = parts.shape
    tr = 8
    for cand in (1024, 512, 256, 128, 64, 32, 16, 8):
        if rows % cand == 0:
            tr = cand
            break

    def body(p_ref, o_ref):
        acc = p_ref[0]
        for j in range(1, n_slot):
            acc = acc + p_ref[j]
        o_ref[...] = acc

    return pl.pallas_call(
        body, name="sum_slots", grid=(rows // tr,),
        in_specs=[pl.BlockSpec((n_slot, tr, LANE), lambda i: (0, i, 0))],
        out_specs=pl.BlockSpec((tr, LANE), lambda i: (i, 0)),
        out_shape=jax.ShapeDtypeStruct((rows, LANE), F32),
        compiler_params=_params(1),
    )(parts)


def adamw(parts, w, m, v, rows_blk, name):
    n_slot, n_l, n_a, n_b = parts.shape

    def body(p_ref, w_ref, m_ref, v_ref, g_ref, d_ref, m2_ref, v2_ref):
        g = p_ref[0].astype(F32)
        for j in range(1, n_slot):
            g = g + p_ref[j].astype(F32)
        m2 = ADAM_B1 * m_ref[...] + (1.0 - ADAM_B1) * g
        v2 = ADAM_B2 * v_ref[...] + (1.0 - ADAM_B2) * (g * g)
        m_hat = m2 / (1.0 - ADAM_B1 ** ADAM_STEP)
        v_hat = v2 / (1.0 - ADAM_B2 ** ADAM_STEP)
        g_ref[...] = g
        m2_ref[...] = m2
        v2_ref[...] = v2
        d_ref[...] = -ADAM_LR * (m_hat / (jnp.sqrt(v_hat) + ADAM_EPS) + ADAM_WD * w_ref[...])

    blk = pl.BlockSpec((1, rows_blk, n_b), lambda l, i: (l, i, 0))
    return pl.pallas_call(
        body, name=name, grid=(n_l, n_a // rows_blk),
        in_specs=[pl.BlockSpec((n_slot, 1, rows_blk, n_b), lambda l, i: (0, l, i, 0)), blk, blk, blk],
        out_specs=[blk] * 4,
        out_shape=[jax.ShapeDtypeStruct((n_l, n_a, n_b), F32)] * 4,
        compiler_params=_params(2),
    )(parts, w, m, v)


MESH = pl.DeviceIdType.MESH


def _my_index():
    return 4 * lax.axis_index("x") + 2 * lax.axis_index("y") + lax.axis_index("c")


def all_gather(shards, name):
    n = len(shards)

    def body(*refs):
        x_refs, out_refs = refs[:n], refs[n:2 * n]
        send_sems, recv_sems, local_sems = refs[2 * n:]
        x, y, c = lax.axis_index("x"), lax.axis_index("y"), lax.axis_index("c")
        me, sibling = (x, y, c), (x, y, 1 - c)
        chips = [(1 - x, y), (x, 1 - y), (1 - x, 1 - y)]

        def slot(a, px, py, pc):
            return out_refs[a].at[4 * px + 2 * py + pc]

        def copy(a, k, block, to, src=None):
            return pltpu.make_async_remote_copy(
                src_ref=slot(a, *block) if src is None else src, dst_ref=slot(a, *block),
                send_sem=send_sems.at[7 * a + k], recv_sem=recv_sems.at[7 * a + k], device_id=to, device_id_type=MESH)

        mine = [pltpu.make_async_copy(x_refs[a], slot(a, *me), local_sems.at[a]) for a in range(n)]
        for cp in mine:
            cp.start()
        first = []
        for a in range(n):
            first.append(copy(a, 0, me, sibling, src=x_refs[a]))
            first += [copy(a, 1 + j, me, (*chip, c), src=x_refs[a]) for j, chip in enumerate(chips)]
        for cp in first:
            cp.start()
        passed = []
        for j, chip in enumerate(chips):
            for a in range(n):
                copy(a, 1 + j, (*chip, c), me).wait_recv()
                fwd = copy(a, 4 + j, (*chip, c), sibling)
                fwd.start()
                passed.append(fwd)
        for a in range(n):
            copy(a, 0, sibling, me).wait_recv()
            for j, chip in enumerate(chips):
                copy(a, 4 + j, (*chip, 1 - c), me).wait_recv()
        for cp in first + passed:
            cp.wait_send()
        for cp in mine:
            cp.wait()

    return pl.pallas_call(
        body, name=name,
        in_specs=[pl.BlockSpec(memory_space=pl.ANY)] * n,
        out_specs=[pl.BlockSpec(memory_space=pl.ANY)] * n,
        out_shape=[jax.ShapeDtypeStruct((N_DEV,) + s.shape, s.dtype) for s in shards],
        scratch_shapes=[pltpu.SemaphoreType.DMA((7 * n,)), pltpu.SemaphoreType.DMA((7 * n,)),
                        pltpu.SemaphoreType.DMA((n,))],
    )(*shards)


def all_to_all(parts, name):
    n = len(parts)

    def body(*refs):
        p_refs, q_refs = refs[:n], refs[n:2 * n]
        send_sems, recv_sems, local_sems = refs[2 * n:]
        x, y, c = lax.axis_index("x"), lax.axis_index("y"), lax.axis_index("c")
        me = 4 * x + 2 * y + c
        mine = [pltpu.make_async_copy(p_refs[a].at[me], q_refs[a].at[me], local_sems.at[a]) for a in range(n)]
        for cp in mine:
            cp.start()
        copies = []
        for r in range(1, N_DEV):
            px, py, pc = x ^ (r >> 2), y ^ ((r >> 1) & 1), c ^ (r & 1)
            peer = 4 * px + 2 * py + pc
            for a in range(n):
                k = 7 * a + r - 1
                cp = pltpu.make_async_remote_copy(
                    src_ref=p_refs[a].at[peer], dst_ref=q_refs[a].at[me], send_sem=send_sems.at[k],
                    recv_sem=recv_sems.at[k], device_id=(px, py, pc), device_id_type=MESH)
                cp.start()
                copies.append((cp, a, peer, k))
        for cp, a, peer, k in copies:
            pltpu.make_async_remote_copy(
                src_ref=p_refs[a].at[peer], dst_ref=q_refs[a].at[peer], send_sem=send_sems.at[k],
                recv_sem=recv_sems.at[k], device_id=(x, y, c), device_id_type=MESH).wait_recv()
        for cp, _, _, _ in copies:
            cp.wait_send()
        for cp in mine:
            cp.wait()

    return pl.pallas_call(
        body, name=name,
        in_specs=[pl.BlockSpec(memory_space=pl.ANY)] * n,
        out_specs=[pl.BlockSpec(memory_space=pl.ANY)] * n,
        out_shape=[jax.ShapeDtypeStruct(p.shape, p.dtype) for p in parts],
        scratch_shapes=[pltpu.SemaphoreType.DMA((7 * n,)), pltpu.SemaphoreType.DMA((7 * n,)),
                        pltpu.SemaphoreType.DMA((n,))],
    )(*parts)


WEIGHT_ORDER = ("c_ctx", "ada_w", "ada_b", "norm_w", "ffn1_wgu", "ffn1_wd", "w_in", "dn_conv_w", "dn_a_log",
                "dn_dt_bias", "dn_norm_w", "w_ret_out", "w_dn_out", "w_o", "ffn2_wgu", "ffn2_wd", "final_norm_w")
BIG = (("ffn1_wgu", "col"), ("ffn1_wd", "row"), ("w_in", "col"), ("w_ret_out", "row"), ("w_dn_out", "row"),
       ("w_o", "row"), ("ffn2_wgu", "col"), ("ffn2_wd", "row"))
LOCAL = ("ada_w", "c_ctx", "ada_b", "norm_w", "dn_conv_w", "dn_a_log", "dn_dt_bias", "dn_norm_w", "final_norm_w")


def _pack(arrs, row_mult, lead=None):
    if lead is None:
        flat = jnp.concatenate([a.reshape(-1) for a in arrs])
        n = flat.shape[0]
    else:
        flat = jnp.concatenate([a.reshape(lead, -1) for a in arrs], axis=1)
        n = flat.shape[1]
    unit = LANE * row_mult
    total = -(-n // unit) * unit
    if lead is None:
        return jnp.pad(flat, (0, total - n)).reshape(total // LANE, LANE)
    return jnp.pad(flat, ((0, 0), (0, total - n))).reshape(lead, total // LANE, LANE)


def _unpack(packed, shapes, lead=None):
    flat = packed.reshape(-1) if lead is None else packed.reshape(lead, -1)
    out, off = [], 0
    for shp in shapes:
        n = math.prod(shp)
        if lead is None:
            out.append(flat[off:off + n].reshape(shp))
        else:
            out.append(flat[:, off:off + n].reshape((lead,) + tuple(shp)))
        off += n
    return out


def _join_shards(g, kind):
    lead = tuple(range(1, g.ndim - 2))
    a, b = g.shape[-2:]
    if kind == "col":
        return g.transpose(*lead, g.ndim - 2, 0, g.ndim - 1).reshape(g.shape[1:-2] + (a, N_DEV * b))
    return g.transpose(*lead, 0, g.ndim - 2, g.ndim - 1).reshape(g.shape[1:-2] + (N_DEV * a, b))


def _split_shards(full, kind):
    a, b = full.shape
    if kind == "col":
        return full.reshape(a, N_DEV, b // N_DEV).transpose(1, 0, 2)
    return full.reshape(N_DEV, a // N_DEV, b)


def _pad_w_in(w):
    return jnp.concatenate([w[..., :6144], w[..., 6160:8208], w[..., 6144:6160],
                            jnp.zeros(w.shape[:-1] + (PW - 8208,), w.dtype)], axis=-1)


def _unpad_w_in(g):
    return jnp.concatenate([g[..., :6144], g[..., 8192:8208], g[..., 6144:8192]], axis=-1)


def kernel(x, c, ctx, c_ctx, ada_w, ada_b, norm_w, ffn1_wgu, ffn1_wd, w_in, dn_conv_w, dn_a_log, dn_dt_bias, dn_norm_w, w_ret_out, w_dn_out, w_o, ffn2_wgu, ffn2_wd, final_norm_w, loss_target, m_c_ctx, m_ada_w, m_ada_b, m_norm_w, m_ffn1_wgu, m_ffn1_wd, m_w_in, m_dn_conv_w, m_dn_a_log, m_dn_dt_bias, m_dn_norm_w, m_w_ret_out, m_w_dn_out, m_w_o, m_ffn2_wgu, m_ffn2_wd, m_final_norm_w, v_c_ctx, v_ada_w, v_ada_b, v_norm_w, v_ffn1_wgu, v_ffn1_wd, v_w_in, v_dn_conv_w, v_dn_a_log, v_dn_dt_bias, v_dn_norm_w, v_w_ret_out, v_w_dn_out, v_w_o, v_ffn2_wgu, v_ffn2_wd, v_final_norm_w):
    w = dict(c_ctx=c_ctx, ada_w=ada_w, ada_b=ada_b, norm_w=norm_w, ffn1_wgu=ffn1_wgu, ffn1_wd=ffn1_wd, w_in=w_in, dn_conv_w=dn_conv_w, dn_a_log=dn_a_log, dn_dt_bias=dn_dt_bias, dn_norm_w=dn_norm_w, w_ret_out=w_ret_out, w_dn_out=w_dn_out, w_o=w_o, ffn2_wgu=ffn2_wgu, ffn2_wd=ffn2_wd, final_norm_w=final_norm_w)
    m = dict(c_ctx=m_c_ctx, ada_w=m_ada_w, ada_b=m_ada_b, norm_w=m_norm_w, ffn1_wgu=m_ffn1_wgu, ffn1_wd=m_ffn1_wd, w_in=m_w_in, dn_conv_w=m_dn_conv_w, dn_a_log=m_dn_a_log, dn_dt_bias=m_dn_dt_bias, dn_norm_w=m_dn_norm_w, w_ret_out=m_w_ret_out, w_dn_out=m_w_dn_out, w_o=m_w_o, ffn2_wgu=m_ffn2_wgu, ffn2_wd=m_ffn2_wd, final_norm_w=m_final_norm_w)
    v = dict(c_ctx=v_c_ctx, ada_w=v_ada_w, ada_b=v_ada_b, norm_w=v_norm_w, ffn1_wgu=v_ffn1_wgu, ffn1_wd=v_ffn1_wd, w_in=v_w_in, dn_conv_w=v_dn_conv_w, dn_a_log=v_dn_a_log, dn_dt_bias=v_dn_dt_bias, dn_norm_w=v_dn_norm_w, w_ret_out=v_w_ret_out, w_dn_out=v_w_dn_out, w_o=v_w_o, ffn2_wgu=v_ffn2_wgu, ffn2_wd=v_ffn2_wd, final_norm_w=v_final_norm_w)
    me = _my_index()
    big_names = [n for n, _ in BIG]

    g_big = all_gather([w[n].astype(BF16) for n in big_names], "ag_weights")
    full = {n: _join_shards(s, kind) for (n, kind), s in zip(BIG, g_big)}
    small_shapes = [norm_w.shape, dn_conv_w.shape, c.shape]
    g_small = all_gather([_pack([norm_w, dn_conv_w, c], 8)], "ag_small")[0]
    norm_s, conv_s, c_all = _unpack(g_small, small_shapes, lead=N_DEV)
    norm_full, conv_full = _join_shards(norm_s, "col"), _join_shards(conv_s, "col")

    cc = jnp.concatenate([c_all.reshape(N_DEV, D), c_ctx[None], jnp.zeros((ADA_ROWS - N_DEV - 1, D), F32)])
    ada_b_cols = lax.dynamic_slice_in_dim(ada_b, me * ADA_COLS, ADA_COLS, axis=1)[:, None, :]
    mods_part = ada_fwd(cc, ada_w, ada_b_cols)
    g_mods = all_gather([mods_part], "ag_mods")[0]
    mods_all = g_mods.transpose(1, 2, 0, 3).reshape(DEPTH, ADA_ROWS, N_MOD * D)
    mx = lax.dynamic_index_in_dim(mods_all, me, axis=1, keepdims=False).reshape(DEPTH, N_MOD, D)
    mc = mods_all[:, N_DEV].reshape(DEPTH, N_MOD, D)

    gparams = jnp.pad(jnp.stack([-jnp.exp(dn_a_log).reshape(DEPTH, 8), dn_dt_bias.reshape(DEPTH, 8)], axis=1),
                      ((0, 0), (0, 6), (8, LANE - 16)))
    stacked = dict(
        wgu1=full["ffn1_wgu"], wd1=full["ffn1_wd"], w_in=_pad_w_in(full["w_in"]), w_r=full["w_ret_out"],
        w_d=full["w_dn_out"], w_o=full["w_o"], wgu2=full["ffn2_wgu"], wd2=full["ffn2_wd"], norm_w=norm_full,
        conv_w8=jnp.pad(conv_full, ((0, 0), (0, 8 - CONV_K), (0, 0))), gparams=gparams,
        dn_norm_t=jnp.tile(dn_norm_w, (1, NH)), mx=mx, mc=mc)
    lws = [{k: a[l] for k, a in stacked.items()} for l in range(DEPTH)]
    loss_l, gx, gws, smalls, d_fnw = local_step(x[0], ctx[0], loss_target[0], final_norm_w, lws)

    sm_list = [jnp.stack([s[k] for s in smalls]) for k in ("dmx", "dmc", "norm_w", "conv_w", "a_log", "dt_bias",
                                                            "dn_norm_w")] + [d_fnw, loss_l.reshape(1)]
    sm_shapes = [a.shape for a in sm_list]
    g_sm = all_gather([_pack(sm_list, 8)], "ag_small_grads")[0]
    (dmx_sum, dmc_sum, g_norm, g_conv, g_alog, g_dtb, g_dnw, g_fnw, loss) = _unpack(sum_slots(g_sm), sm_shapes)
    dmx_all = _unpack(g_sm, sm_shapes[:1], lead=N_DEV)[0].reshape(N_DEV, DEPTH, N_MOD * D)
    dmc_sum = dmc_sum.reshape(DEPTH, N_MOD * D)
    dmx_sum = dmx_sum.reshape(DEPTH, N_MOD * D)
    dmod = jnp.concatenate([
        lax.dynamic_slice_in_dim(dmx_all, me * ADA_COLS, ADA_COLS, axis=2).transpose(1, 0, 2),
        lax.dynamic_slice_in_dim(dmc_sum, me * ADA_COLS, ADA_COLS, axis=1)[:, None, :],
        jnp.zeros((DEPTH, ADA_ROWS - N_DEV - 1, ADA_COLS), F32)], axis=1)
    g_ada_w, d_sil = ada_bwd(cc, dmod, ada_w)
    g_cc = all_gather([d_sil[:, N_DEV].sum(axis=0).reshape(8, LANE)], "ag_c_ctx")[0]
    grads = dict(
        ada_w=g_ada_w, c_ctx=c_ctx_grad(g_cc, c_ctx.reshape(8, LANE)).reshape(D), ada_b=dmx_sum + dmc_sum,
        norm_w=lax.dynamic_slice_in_dim(g_norm, me * (D // N_DEV), D // N_DEV, axis=2),
        dn_conv_w=lax.dynamic_slice_in_dim(g_conv, me * (2 * D // N_DEV), 2 * D // N_DEV, axis=2),
        dn_a_log=g_alog.reshape(dn_a_log.shape), dn_dt_bias=g_dtb.reshape(dn_dt_bias.shape), dn_norm_w=g_dnw,
        final_norm_w=g_fnw)

    def layer_full(gw):
        return dict(ffn1_wgu=jnp.concatenate([gw["wg1"], gw["wu1"]], axis=-1), ffn1_wd=gw["wd1"],
                    w_in=_unpad_w_in(gw["w_in"]), w_ret_out=gw["w_r"], w_dn_out=gw["w_d"], w_o=gw["w_o"],
                    ffn2_wgu=jnp.concatenate([gw["wg2"], gw["wu2"]], axis=-1), ffn2_wd=gw["wd2"])

    full_g = [layer_full(gw) for gw in gws]
    parts = [jnp.stack([_split_shards(fg[n], kind) for fg in full_g], axis=1).astype(BF16) for n, kind in BIG]
    recv = all_to_all(parts, "a2a_grads")
    res = {}
    for n, r in zip(big_names, recv):
        rows_blk = 256 if w[n].shape[1] % 256 == 0 else w[n].shape[1]
        res[n] = adamw(r, w[n], m[n], v[n], rows_blk, "adamw_" + n)
    res["ada_w"] = adamw(g_ada_w[None], ada_w, m["ada_w"], v["ada_w"], 256, "adamw_ada_w")
    small_names = [n for n in LOCAL if n != "ada_w"]
    small_pack = lambda d: _pack([d[n] for n in small_names], 8)[None]
    res_small = adamw(small_pack(grads)[None], small_pack(w), small_pack(m), small_pack(v),
                      small_pack(w).shape[1], "adamw_small")
    unpacked = [_unpack(r, [w[n].shape for n in small_names]) for r in res_small]
    for i, n in enumerate(small_names):
        res[n] = tuple(u[i] for u in unpacked)
    outs = [[res[n][k] for n in WEIGHT_ORDER] for k in range(4)]
    return (loss.reshape(()), gx[None], *outs[0], *outs[1], *outs[2], *outs[3])
```

```python
import functools
import math

import jax
import jax.numpy as jnp
from jax import lax
from jax.experimental import pallas as pl
from jax.experimental.pallas import tpu as pltpu

F32 = jnp.float32
BF16 = jnp.bfloat16

D = 1024
NH = 4
DK = 128
DV = 256
RET_C = 128
DN_C = 64
FFN_H = 2816
FFN_HB = 1408
N_MOD = 9
DEPTH = 4
N_DEV = 8
EPS = 1e-6
CONV_K = 5
HALO = 8

PW = 8320
C_RQK, C_RV, C_RG, C_DQK, C_DVV, C_DZ, C_GA, C_GB = 0, 1, 2, 3, 4, 5, 6, 7
C_DBA = 64
PROJ_TN = 1664

LANE = 128
VMEM_LIMIT = 56 * 1024 * 1024

ADAM_LR, ADAM_B1, ADAM_B2, ADAM_EPS, ADAM_WD, ADAM_STEP = 0.001, 0.9, 0.999, 1e-08, 0.01, 10

NN = ((1,), (0,))
NT = ((1,), (1,))
TN = ((0,), (0,))
HI = lax.Precision.HIGHEST


def _params(n_grid):
    return pltpu.CompilerParams(dimension_semantics=("arbitrary",) * n_grid, vmem_limit_bytes=VMEM_LIMIT)


def _dot(a, b, dims):
    return lax.dot_general(a.astype(BF16), b.astype(BF16), (dims, ((), ())), preferred_element_type=F32)


def _dot_hi(a, b, dims=NN):
    return lax.dot_general(a, b, (dims, ((), ())), precision=HI, preferred_element_type=F32)


@jax.custom_vjp
def mm_nn(a, b):
    return _dot(a, b, NN)


mm_nn.defvjp(lambda a, b: (_dot(a, b, NN), (a, b)),
             lambda r, g: (_dot(g, r[1], NT), _dot(r[0], g, TN)))


@jax.custom_vjp
def mm_nt(a, b):
    return _dot(a, b, NT)


mm_nt.defvjp(lambda a, b: (_dot(a, b, NT), (a, b)),
             lambda r, g: (_dot(g, r[1], NN), _dot(g, r[0], TN)))


@jax.custom_vjp
def mm_tn(a, b):
    return _dot(a, b, TN)


mm_tn.defvjp(lambda a, b: (_dot(a, b, TN), (a, b)),
             lambda r, g: (_dot(r[1], g, NT), _dot(r[0], g, NN)))


def _split_bf16(x, n):
    parts = []
    for _ in range(n):
        p = x.astype(BF16)
        parts.append(p)
        x = x - p.astype(F32)
    return parts


def _dot_f32(a, b, dims, exact=None):
    mm = lambda p, q: lax.dot_general(p, q, (dims, ((), ())), preferred_element_type=F32)
    if exact == "a":
        ab = a.astype(BF16)
        b1, b2, b3 = _split_bf16(b, 3)
        return mm(ab, b1) + (mm(ab, b2) + mm(ab, b3))
    if exact == "b":
        bb = b.astype(BF16)
        a1, a2, a3 = _split_bf16(a, 3)
        return mm(a1, bb) + (mm(a2, bb) + mm(a3, bb))
    a1, a2 = _split_bf16(a, 2)
    b1, b2 = _split_bf16(b, 2)
    return mm(a1, b1) + (mm(a1, b2) + mm(a2, b1))


@jax.custom_vjp
def _cum_rows(mask, x):
    return _dot_f32(mask, x, NN, exact="a")


_cum_rows.defvjp(lambda mask, x: (_dot_f32(mask, x, NN, exact="a"), mask),
                 lambda mask, g: (jnp.zeros_like(mask), _dot_f32(mask, g, TN, exact="a")))


@jax.custom_vjp
def _row_bcast(xb):
    return _dot_f32(jnp.full(xb.shape, 1.0 / LANE, F32), xb, NT, exact="a")


_row_bcast.defvjp(lambda xb: (_row_bcast(xb), None),
                  lambda _, g: (_dot_f32(g, jnp.full((g.shape[0], LANE), 1.0 / LANE, F32), TN, exact="b"),))


def _tri_inv_fwd(a):
    c = a.shape[0]
    eye = (lax.broadcasted_iota(jnp.int32, (c, c), 0) == lax.broadcasted_iota(jnp.int32, (c, c), 1)).astype(F32)
    pw = -a
    tinv = eye + pw
    m = 2
    while m < DN_C:
        pw = _dot_f32(pw, pw, NN)
        tinv = tinv + _dot_f32(tinv, pw, NN)
        m *= 2
    return tinv


@jax.custom_vjp
def _tri_inv(a):
    return _tri_inv_fwd(a)


def _tri_inv_bwd(tinv, g):
    return (-_dot_f32(_dot_f32(tinv, g, TN), tinv, NT),)


_tri_inv.defvjp(lambda a: (lambda t: (t, t))(_tri_inv_fwd(a)), _tri_inv_bwd)


@jax.custom_vjp
def _tri_inv_known(a, tinv):
    del a
    return tinv


_tri_inv_known.defvjp(lambda a, tinv: (tinv, tinv),
                      lambda tinv, g: (_tri_inv_bwd(tinv, g)[0], jnp.zeros_like(tinv)))


def _normmod(x, nw, shift, scale):
    r = lax.rsqrt(jnp.mean(x * x, axis=-1, keepdims=True) + EPS)
    return (x * r * nw) * (1.0 + scale) + shift


def _sigmoid(x):
    return 0.5 * jnp.tanh(0.5 * x) + 0.5


def _softplus(x):
    return jnp.maximum(x, 0.0) + jnp.log(1.0 + jnp.exp(-jnp.abs(x)))


def _row_block(t_len):
    return 512 if t_len % 512 == 0 else 256


def ffn_fwd(x, vec, wgu, wd):
    t_len = x.shape[0]
    tm = _row_block(t_len)

    def body(x_ref, vec_ref, wg_ref, wu_ref, wd_ref, x1_ref, h_ref, g_ref, u_ref, f_ref, hs, acc):
        j = pl.program_id(1)

        @pl.when(j == 0)
        def _():
            hb = _normmod(x_ref[...], vec_ref[0:1, :], vec_ref[1:2, :], vec_ref[2:3, :]).astype(BF16)
            hs[...] = hb
            h_ref[...] = hb
            acc[...] = jnp.zeros_like(acc)

        hb = hs[...]
        g = jnp.dot(hb, wg_ref[...], preferred_element_type=F32)
        u = jnp.dot(hb, wu_ref[...], preferred_element_type=F32)
        g_ref[...] = g.astype(BF16)
        u_ref[...] = u.astype(BF16)
        act = g * _sigmoid(g) * u
        acc[...] += jnp.dot(act.astype(BF16), wd_ref[...], preferred_element_type=F32)

        @pl.when(j == 1)
        def _():
            f = acc[...]
            f_ref[...] = f.astype(BF16)
            x1_ref[...] = x_ref[...] + 0.5 * vec_ref[3:4, :] * f

    return pl.pallas_call(
        body, name="ffn_fwd", grid=(t_len // tm, 2),
        in_specs=[
            pl.BlockSpec((tm, D), lambda i, j: (i, 0)),
            pl.BlockSpec((8, D), lambda i, j: (0, 0)),
            pl.BlockSpec((D, FFN_HB), lambda i, j: (0, j)),
            pl.BlockSpec((D, FFN_HB), lambda i, j: (0, j + 2)),
            pl.BlockSpec((FFN_HB, D), lambda i, j: (j, 0)),
        ],
        out_specs=[
            pl.BlockSpec((tm, D), lambda i, j: (i, 0)),
            pl.BlockSpec((tm, D), lambda i, j: (i, 0)),
            pl.BlockSpec((tm, FFN_HB), lambda i, j: (i, j)),
            pl.BlockSpec((tm, FFN_HB), lambda i, j: (i, j)),
            pl.BlockSpec((tm, D), lambda i, j: (i, 0)),
        ],
        out_shape=[
            jax.ShapeDtypeStruct((t_len, D), F32),
            jax.ShapeDtypeStruct((t_len, D), BF16),
            jax.ShapeDtypeStruct((t_len, FFN_H), BF16),
            jax.ShapeDtypeStruct((t_len, FFN_H), BF16),
            jax.ShapeDtypeStruct((t_len, D), BF16),
        ],
        scratch_shapes=[pltpu.VMEM((tm, D), BF16), pltpu.VMEM((tm, D), F32)],
        compiler_params=_params(2),
    )(x, vec, wgu, wgu, wd)


def ffn_bwd_act(dx1, vec, g, u, f, wd):
    t_len = dx1.shape[0]
    tm = _row_block(t_len)

    def body(dx_ref, vec_ref, g_ref, u_ref, f_ref, wd_ref, act_ref, dg_ref, du_ref, dy_ref, pg_ref, dys):
        i, j = pl.program_id(0), pl.program_id(1)

        @pl.when((i == 0) & (j == 0))
        def _():
            pg_ref[...] = jnp.zeros_like(pg_ref)

        @pl.when(j == 0)
        def _():
            dx = dx_ref[...]
            dyb = (0.5 * vec_ref[3:4, :] * dx).astype(BF16)
            dys[...] = dyb
            dy_ref[...] = dyb
            pg_ref[3:4, :] += jnp.sum(0.5 * dx * f_ref[...].astype(F32), axis=0, keepdims=True)

        dact = _dot(dys[...], wd_ref[...], NT)
        gg = g_ref[...].astype(F32)
        uu = u_ref[...].astype(F32)
        sg = _sigmoid(gg)
        sl = gg * sg
        act_ref[...] = (sl * uu).astype(BF16)
        du_ref[...] = (dact * sl).astype(BF16)
        dg_ref[...] = (dact * uu * (sg + sl * (1.0 - sg))).astype(BF16)

    return pl.pallas_call(
        body, name="ffn_bwd_act", grid=(t_len // tm, 2),
        in_specs=[
            pl.BlockSpec((tm, D), lambda i, j: (i, 0)),
            pl.BlockSpec((8, D), lambda i, j: (0, 0)),
            pl.BlockSpec((tm, FFN_HB), lambda i, j: (i, j)),
            pl.BlockSpec((tm, FFN_HB), lambda i, j: (i, j)),
            pl.BlockSpec((tm, D), lambda i, j: (i, 0)),
            pl.BlockSpec((FFN_HB, D), lambda i, j: (j, 0)),
        ],
        out_specs=[
            pl.BlockSpec((tm, FFN_HB), lambda i, j: (i, j)),
            pl.BlockSpec((tm, FFN_HB), lambda i, j: (i, j)),
            pl.BlockSpec((tm, FFN_HB), lambda i, j: (i, j)),
            pl.BlockSpec((tm, D), lambda i, j: (i, 0)),
            pl.BlockSpec((8, D), lambda i, j: (0, 0)),
        ],
        out_shape=[
            jax.ShapeDtypeStruct((t_len, FFN_H), BF16),
            jax.ShapeDtypeStruct((t_len, FFN_H), BF16),
            jax.ShapeDtypeStruct((t_len, FFN_H), BF16),
            jax.ShapeDtypeStruct((t_len, D), BF16),
            jax.ShapeDtypeStruct((8, D), F32),
        ],
        scratch_shapes=[pltpu.VMEM((tm, D), BF16)],
        compiler_params=_params(2),
    )(dx1, vec, g, u, f, wd)


def nt_norm_bwd(dys, w, col_offsets, tk, n_steps, x_in, dres, vec, name):
    t_len = x_in.shape[0]
    tm = _row_block(t_len)
    n_seg = len(dys)

    def body(*refs):
        dy_refs = refs[:n_seg]
        w_refs = refs[n_seg:2 * n_seg]
        x_ref, dres_ref, vec_ref, dx_ref, pg_ref, acc = refs[2 * n_seg:]
        i, j = pl.program_id(0), pl.program_id(1)

        @pl.when((i == 0) & (j == 0))
        def _():
            pg_ref[...] = jnp.zeros_like(pg_ref)

        @pl.when(j == 0)
        def _():
            acc[...] = jnp.zeros_like(acc)

        part = _dot(dy_refs[0][...], w_refs[0][...], NT)
        for s in range(1, n_seg):
            part += _dot(dy_refs[s][...], w_refs[s][...], NT)
        acc[...] += part

        @pl.when(j == n_steps - 1)
        def _():
            _, vjp = jax.vjp(_normmod, x_ref[...], vec_ref[0:1, :], vec_ref[1:2, :], vec_ref[2:3, :])
            dxn, dnw, dsh, dsc = vjp(acc[...])
            dx_ref[...] = dres_ref[...] + dxn
            pg_ref[0:1, :] += dnw
            pg_ref[1:2, :] += dsh
            pg_ref[2:3, :] += dsc

    in_specs = [pl.BlockSpec((tm, tk), lambda i, j: (i, j)) for _ in range(n_seg)]
    in_specs += [pl.BlockSpec((D, tk), functools.partial(lambda i, j, off: (0, off + j), off=off))
                 for off in col_offsets]
    in_specs += [pl.BlockSpec((tm, D), lambda i, j: (i, 0)), pl.BlockSpec((tm, D), lambda i, j: (i, 0)),
                 pl.BlockSpec((8, D), lambda i, j: (0, 0))]
    return pl.pallas_call(
        body, name=name, grid=(t_len // tm, n_steps),
        in_specs=in_specs,
        out_specs=[pl.BlockSpec((tm, D), lambda i, j: (i, 0)), pl.BlockSpec((8, D), lambda i, j: (0, 0))],
        out_shape=[jax.ShapeDtypeStruct((t_len, D), F32), jax.ShapeDtypeStruct((8, D), F32)],
        scratch_shapes=[pltpu.VMEM((tm, D), F32)],
        compiler_params=_params(2),
    )(*dys, *([w] * n_seg), x_in, dres, vec)


def tn_matmul(a_x, b_x, a_c, b_c, tn, name):
    t_len, k_dim = a_x.shape
    n_dim = b_x.shape[1]
    t_ctx = a_c.shape[0]
    tt = 512
    n_t = t_len // tt

    def body(ax_ref, bx_ref, ac_ref, bc_ref, o_ref):
        t = pl.program_id(1)

        @pl.when(t == 0)
        def _():
            o_ref[...] = _dot(ac_ref[...], bc_ref[...], TN)

        o_ref[...] += _dot(ax_ref[...], bx_ref[...], TN)

    return pl.pallas_call(
        body, name=name, grid=(n_dim // tn, n_t),
        in_specs=[
            pl.BlockSpec((tt, k_dim), lambda n, t: (t, 0)),
            pl.BlockSpec((tt, tn), lambda n, t: (t, n)),
            pl.BlockSpec((t_ctx, k_dim), lambda n, t: (0, 0)),
            pl.BlockSpec((t_ctx, tn), lambda n, t: (0, n)),
        ],
        out_specs=pl.BlockSpec((k_dim, tn), lambda n, t: (0, n)),
        out_shape=jax.ShapeDtypeStruct((k_dim, n_dim), F32),
        compiler_params=_params(2),
    )(a_x, b_x, a_c, b_c)


def proj_fwd(x, vec, w_in_p):
    t_len = x.shape[0]
    tm = _row_block(t_len)

    def body(x_ref, vec_ref, w_ref, p_ref, h_ref, hs):
        @pl.when(pl.program_id(1) == 0)
        def _():
            hb = _normmod(x_ref[...], vec_ref[0:1, :], vec_ref[1:2, :], vec_ref[2:3, :]).astype(BF16)
            hs[...] = hb
            h_ref[...] = hb

        p_ref[...] = jnp.dot(hs[...], w_ref[...], preferred_element_type=F32)

    return pl.pallas_call(
        body, name="proj_fwd", grid=(t_len // tm, PW // PROJ_TN),
        in_specs=[
            pl.BlockSpec((tm, D), lambda i, j: (i, 0)),
            pl.BlockSpec((8, D), lambda i, j: (0, 0)),
            pl.BlockSpec((D, PROJ_TN), lambda i, j: (0, j)),
        ],
        out_specs=[pl.BlockSpec((tm, PROJ_TN), lambda i, j: (i, j)), pl.BlockSpec((tm, D), lambda i, j: (i, 0))],
        out_shape=[jax.ShapeDtypeStruct((t_len, PW), F32), jax.ShapeDtypeStruct((t_len, D), BF16)],
        scratch_shapes=[pltpu.VMEM((tm, D), BF16)],
        compiler_params=_params(2),
    )(x, vec, w_in_p)


def _shift_rows(e, s):
    n = e.shape[0]
    return pltpu.roll(e, (-s) % n, 0)


def _swap_halves(t):
    return pltpu.roll(t, DK // 2, 1)


def _halo_specs(tm, t_len, width, col, lead=False):
    per = tm // HALO
    last = t_len // HALO - 1
    if lead:
        return (pl.BlockSpec((2, HALO, width), lambda i: (0, jnp.maximum(i * per - 1, 0), col)),
                pl.BlockSpec((2, HALO, width), lambda i: (0, jnp.minimum((i + 1) * per, last), col)))
    return (pl.BlockSpec((HALO, width), lambda i: (jnp.maximum(i * per - 1, 0), col)),
            pl.BlockSpec((HALO, width), lambda i: (jnp.minimum((i + 1) * per, last), col)))


def _gate_cols(ba, gp_ref):
    lane = lax.broadcasted_iota(jnp.int32, ba.shape, 1)
    z = ba + gp_ref[1:2, :]
    return jnp.where(lane < 8, _sigmoid(ba), gp_ref[0:1, :] * _softplus(z))


def _conv_chunk(win, w_ref, c0, width):
    y = w_ref[0:1, c0:c0 + width] * _shift_rows(win, -2)
    for j in range(1, CONV_K):
        y += w_ref[j:j + 1, c0:c0 + width] * _shift_rows(win, j - 2)
    return y


def feat_fwd(p, cos2, sin2, conv_w8, gparams):
    t_len = p.shape[0]
    tm = _row_block(t_len)
    n_t = t_len // tm
    cw = 512

    def body(rqk_ref, dqk_ref, dqk_p, dqk_n, dvv_ref, dvv_p, dvv_n, dba_ref, cos_ref, sin_ref, cw_ref, gp_ref,
             o_rqk, o_dqkv, o_gbc):
        i = pl.program_id(0)
        cos, sin = cos_ref[...], sin_ref[...]
        for s in range(2 * NH):
            t = rqk_ref[:, s * DK:(s + 1) * DK]
            if s >= NH:
                t = t * (DK ** -0.5)
            o_rqk[:, s * DK:(s + 1) * DK] = t * cos + _swap_halves(t) * sin
        o_gbc[...] = _gate_cols(dba_ref[...], gp_ref)
        first, last = i == 0, i == n_t - 1
        for ci in range(4):
            src, sp, sn = (dqk_ref, dqk_p, dqk_n) if ci < 2 else (dvv_ref, dvv_p, dvv_n)
            c0 = (ci % 2) * cw
            win = jnp.concatenate([
                jnp.where(first, 0.0, sp[:, c0:c0 + cw]), src[:, c0:c0 + cw],
                jnp.where(last, 0.0, sn[:, c0:c0 + cw])], axis=0)
            y = _conv_chunk(win, cw_ref, ci * cw, cw)[HALO:HALO + tm]
            sv = y * _sigmoid(y)
            if ci < 2:
                scale = DK ** -0.5 if ci == 0 else 1.0
                for hh in range(NH):
                    sh = sv[:, hh * DK:(hh + 1) * DK]
                    nrm = lax.rsqrt(jnp.sum(sh * sh, axis=-1, keepdims=True) + EPS)
                    o_dqkv[:, ci * cw + hh * DK:ci * cw + (hh + 1) * DK] = sh * nrm * scale
            else:
                o_dqkv[:, ci * cw:(ci + 1) * cw] = sv

    hq = _halo_specs(tm, t_len, D, C_DQK)
    hv = _halo_specs(tm, t_len, D, C_DVV)
    return pl.pallas_call(
        body, name="feat_fwd", grid=(n_t,),
        in_specs=[
            pl.BlockSpec((tm, D), lambda i: (i, C_RQK)),
            pl.BlockSpec((tm, D), lambda i: (i, C_DQK)), hq[0], hq[1],
            pl.BlockSpec((tm, D), lambda i: (i, C_DVV)), hv[0], hv[1],
            pl.BlockSpec((tm, LANE), lambda i: (i, C_DBA)),
            pl.BlockSpec((tm, LANE), lambda i: (i, 0)),
            pl.BlockSpec((tm, LANE), lambda i: (i, 0)),
            pl.BlockSpec((8, 2 * D), lambda i: (0, 0)),
            pl.BlockSpec((8, LANE), lambda i: (0, 0)),
        ],
        out_specs=[pl.BlockSpec((tm, D), lambda i: (i, 0)), pl.BlockSpec((tm, 2 * D), lambda i: (i, 0)),
                   pl.BlockSpec((tm, LANE), lambda i: (i, 0))],
        out_shape=[jax.ShapeDtypeStruct((t_len, D), F32), jax.ShapeDtypeStruct((t_len, 2 * D), F32),
                   jax.ShapeDtypeStruct((t_len, LANE), F32)],
        compiler_params=_params(1),
    )(p, p, p, p, p, p, p, p, cos2, sin2, conv_w8, gparams)


def feat_bwd(p, cos2, sin2, conv_w8, gparams, d_rqk, d_rv, d_dqk, d_dvv, d_gbc):
    t_len = p.shape[0]
    tm = 256
    n_t = t_len // tm
    cw = 512

    def body(dqk_ref, dqk_p, dqk_n, dvv_ref, dvv_p, dvv_n, dba_ref, cos_ref, sin_ref, cw_ref, gp_ref,
             g_rqk, g_rv, g_dqk, g_dqk_p, g_dqk_n, g_dvv, g_dvv_p, g_dvv_n, g_gbc,
             o_rqk, o_rv, o_dqk, o_dvv, o_dba, o_cw, o_gp):
        i = pl.program_id(0)
        first, last = i == 0, i == n_t - 1

        @pl.when(first)
        def _():
            o_cw[...] = jnp.zeros_like(o_cw)
            o_gp[...] = jnp.zeros_like(o_gp)

        cos, sin = cos_ref[...], sin_ref[...]
        for s in range(2 * NH):
            gsl = g_rqk[0, :, s * DK:(s + 1) * DK] + g_rqk[1, :, s * DK:(s + 1) * DK]
            dt = gsl * cos + _swap_halves(gsl * sin)
            if s >= NH:
                dt = dt * (DK ** -0.5)
            o_rqk[:, s * DK:(s + 1) * DK] = dt.astype(BF16)
        o_rv[...] = (g_rv[0] + g_rv[1]).astype(BF16)

        ba = dba_ref[...]
        gg = g_gbc[0] + g_gbc[1]
        lane = lax.broadcasted_iota(jnp.int32, ba.shape, 1)
        sb = _sigmoid(ba)
        z = ba + gp_ref[1:2, :]
        a_row = gp_ref[0:1, :]
        dz = gg * a_row * _sigmoid(z)
        o_dba[...] = jnp.where(lane < 8, gg * sb * (1.0 - sb), dz).astype(BF16)
        is_g = (lane >= 8) & (lane < 16)
        o_gp[0:1, :] += jnp.sum(jnp.where(is_g, gg * a_row * _softplus(z), 0.0), axis=0, keepdims=True)
        o_gp[1:2, :] += jnp.sum(jnp.where(is_g, dz, 0.0), axis=0, keepdims=True)

        for ci in range(4):
            src, sp, sn = (dqk_ref, dqk_p, dqk_n) if ci < 2 else (dvv_ref, dvv_p, dvv_n)
            c0 = (ci % 2) * cw
            gc0 = ci * cw
            win = jnp.concatenate([
                jnp.where(first, 0.0, sp[:, c0:c0 + cw]), src[:, c0:c0 + cw],
                jnp.where(last, 0.0, sn[:, c0:c0 + cw])], axis=0)
            gs, gsp, gsn = (g_dqk, g_dqk_p, g_dqk_n) if ci < 2 else (g_dvv, g_dvv_p, g_dvv_n)
            gext = jnp.concatenate([
                jnp.where(first, 0.0, gsp[0, :, c0:c0 + cw] + gsp[1, :, c0:c0 + cw]),
                gs[0, :, c0:c0 + cw] + gs[1, :, c0:c0 + cw],
                jnp.where(last, 0.0, gsn[0, :, c0:c0 + cw] + gsn[1, :, c0:c0 + cw])], axis=0)
            y = _conv_chunk(win, cw_ref, gc0, cw)
            sg = _sigmoid(y)
            sv = y * sg
            if ci < 2:
                scale = DK ** -0.5 if ci == 0 else 1.0
                parts = []
                for hh in range(NH):
                    sh = sv[:, hh * DK:(hh + 1) * DK]
                    gh = gext[:, hh * DK:(hh + 1) * DK]
                    nrm = lax.rsqrt(jnp.sum(sh * sh, axis=-1, keepdims=True) + EPS)
                    dot = jnp.sum(gh * sh, axis=-1, keepdims=True)
                    parts.append(scale * nrm * (gh - sh * (nrm * nrm) * dot))
                ds = jnp.concatenate(parts, axis=1)
            else:
                ds = gext
            dy = ds * (sg + sv * (1.0 - sg))
            dpe = cw_ref[0:1, gc0:gc0 + cw] * _shift_rows(dy, 2)
            for j in range(1, CONV_K):
                dpe += cw_ref[j:j + 1, gc0:gc0 + cw] * _shift_rows(dy, 2 - j)
            dst = o_dqk if ci < 2 else o_dvv
            dst[:, c0:c0 + cw] = dpe[HALO:HALO + tm].astype(BF16)
            dyc = dy[HALO:HALO + tm]
            for j in range(CONV_K):
                o_cw[j:j + 1, gc0:gc0 + cw] += jnp.sum(dyc * _shift_rows(win, j - 2)[HALO:HALO + tm], axis=0,
                                                      keepdims=True)

    hq = _halo_specs(tm, t_len, D, C_DQK)
    hv = _halo_specs(tm, t_len, D, C_DVV)
    hg = _halo_specs(tm, t_len, D, 0, lead=True)
    outs = pl.pallas_call(
        body, name="feat_bwd", grid=(n_t,),
        in_specs=[
            pl.BlockSpec((tm, D), lambda i: (i, C_DQK)), hq[0], hq[1],
            pl.BlockSpec((tm, D), lambda i: (i, C_DVV)), hv[0], hv[1],
            pl.BlockSpec((tm, LANE), lambda i: (i, C_DBA)),
            pl.BlockSpec((tm, LANE), lambda i: (i, 0)),
            pl.BlockSpec((tm, LANE), lambda i: (i, 0)),
            pl.BlockSpec((8, 2 * D), lambda i: (0, 0)),
            pl.BlockSpec((8, LANE), lambda i: (0, 0)),
            pl.BlockSpec((2, tm, D), lambda i: (0, i, 0)),
            pl.BlockSpec((2, tm, D), lambda i: (0, i, 0)),
            pl.BlockSpec((2, tm, D), lambda i: (0, i, 0)), hg[0], hg[1],
            pl.BlockSpec((2, tm, D), lambda i: (0, i, 0)), hg[0], hg[1],
            pl.BlockSpec((2, tm, LANE), lambda i: (0, i, 0)),
        ],
        out_specs=[pl.BlockSpec((tm, D), lambda i: (i, 0))] * 4
        + [pl.BlockSpec((tm, LANE), lambda i: (i, 0)),
           pl.BlockSpec((8, 2 * D), lambda i: (0, 0)),
           pl.BlockSpec((8, LANE), lambda i: (0, 0))],
        out_shape=[jax.ShapeDtypeStruct((t_len, D), BF16)] * 4
        + [jax.ShapeDtypeStruct((t_len, LANE), BF16),
           jax.ShapeDtypeStruct((8, 2 * D), F32), jax.ShapeDtypeStruct((8, LANE), F32)],
        compiler_params=_params(1),
    )(p, p, p, p, p, p, p, cos2, sin2, conv_w8, gparams, d_rqk, d_rv, d_dqk, d_dqk, d_dqk, d_dvv, d_dvv, d_dvv,
      d_gbc)
    return outs


def _ret_chunk(q, k, v, s, lg, rev):
    c = q.shape[0]
    ii = lax.broadcasted_iota(jnp.int32, (c, c), 0).astype(F32)
    jj = lax.broadcasted_iota(jnp.int32, (c, c), 1).astype(F32)
    diff = jnp.where(rev, jj - ii, ii - jj)
    mask = diff >= jnp.where(rev, 1.0, 0.0)
    dec =jnp.where(mask, jnp.exp(lg * jnp.where(mask, diff, 0.0)), 0.0)
    idx = lax.broadcasted_iota(jnp.int32, (c, 1), 0).astype(F32)
    idx = jnp.where(rev, c - 1.0 - idx, idx)
    kdec = k * jnp.exp(lg * (c - 1.0 - idx))
    qdec = q * jnp.exp(lg * (idx + 1.0))
    o = mm_nn(mm_nt(q, k) * dec, v) + mm_nn(qdec, s)
    s2 = s * math.exp(lg * c) + mm_tn(kdec, v)
    return o, s2


def _log_gamma(h):
    return math.log1p(-(2.0 ** (-5.0 - h)))


DN_W = 3 * NH * DK
HS = NH * DN_C


def _dn_intra4(q4, k4, v4, g4, beta4, rev, tinv_known=None):
    ii = lax.broadcasted_iota(jnp.int32, (HS, HS), 0)
    jj = lax.broadcasted_iota(jnp.int32, (HS, HS), 1)
    blk = (ii // DN_C) == (jj // DN_C)
    dd = jnp.where(rev, jj - ii, ii - jj)
    incl = blk & (dd >= 0)
    strict = blk & (dd > 0)
    gb = jnp.broadcast_to(g4, (HS, LANE))
    gcb = _cum_rows(incl.astype(F32), gb)
    gc = jnp.max(gcb, axis=1, keepdims=True)
    gcr = _row_bcast(gcb)
    at_end = blk & ((jj % DN_C) == jnp.where(rev, 0, DN_C - 1))
    glast = jnp.sum(jnp.where(at_end, gcr, 0.0), axis=1, keepdims=True)
    decay = jnp.where(incl, jnp.exp(jnp.where(incl, gc - gcr, 0.0)), 0.0)
    kb = k4 * beta4
    a = jnp.where(strict, mm_nt(kb, k4) * decay, 0.0)
    tinv = _tri_inv(a) if tinv_known is None else _tri_inv_known(a, tinv_known)
    u = mm_nn(tinv, v4 * beta4)
    w = mm_nn(tinv, kb * jnp.exp(gc))
    attn = jnp.where(incl, mm_nt(q4, k4) * decay, 0.0)
    return (u, w, q4 * jnp.exp(gc), k4 * jnp.exp(glast - gc), attn), tinv


def _dn_seq4(u, w, qg, kd, attn, glast, s):
    v_new = [uh - mm_nn(wh, sh) for uh, wh, sh in zip(u, w, s)]
    s2 = [sh * jnp.exp(gh) + mm_tn(kh, vh) for sh, gh, kh, vh in zip(s, glast, kd, v_new)]
    o = jnp.concatenate([mm_nn(qh, sh) for qh, sh in zip(qg, s)], axis=0) + mm_nn(attn, jnp.concatenate(v_new, axis=0))
    return o, s2


def _stack_heads(ref_rows, width, col0=0):
    return jnp.concatenate([ref_rows(slice(col0 + h * width, col0 + (h + 1) * width)) for h in range(NH)], axis=0)


def _head_rows(h):
    return slice(h * DN_C, (h + 1) * DN_C)


def _dn_gates4(gbv, d):
    g4 = jnp.concatenate([_pick_lane(gbv, 8 + 4 * d + h) for h in range(NH)], axis=0)
    b4 = jnp.concatenate([_pick_lane(gbv, 4 * d + h) for h in range(NH)], axis=0)
    return g4, b4


def dn_intra_fwd(dqkv, gbc):
    t_len = dqkv.shape[0]
    rb = 256
    n_g = rb // DN_C

    def body(qk_ref, v_ref, gb_ref, u_ref, wqk_ref, at_ref, ti_ref):
        d = pl.program_id(0)
        rev = d == 1
        for gi in range(n_g):
            rows = slice(gi * DN_C, (gi + 1) * DN_C)
            g4, b4 = _dn_gates4(gb_ref[rows, :], d)
            (u, w, qg, kd, attn), tinv = _dn_intra4(_stack_heads(lambda cs: qk_ref[rows, cs], DK),
                                                    _stack_heads(lambda cs: qk_ref[rows, cs], DK, NH * DK),
                                                    _stack_heads(lambda cs: v_ref[rows, cs], DV), g4, b4, rev)
            at_ref[0, gi] = attn.astype(BF16)
            ti_ref[0, gi] = tinv
            for h in range(NH):
                hr = _head_rows(h)
                u_ref[0, rows, h * DV:(h + 1) * DV] = u[hr]
                wqk_ref[0, rows, h * DK:(h + 1) * DK] = w[hr].astype(BF16)
                wqk_ref[0, rows, NH * DK + h * DK:NH * DK + (h + 1) * DK] = qg[hr].astype(BF16)
                wqk_ref[0, rows, 2 * NH * DK + h * DK:2 * NH * DK + (h + 1) * DK] = kd[hr].astype(BF16)

    return pl.pallas_call(
        body, name="dn_intra_fwd", grid=(2, t_len // rb),
        in_specs=[pl.BlockSpec((rb, D), lambda d, i: (i, 0)), pl.BlockSpec((rb, D), lambda d, i: (i, 1)),
                  pl.BlockSpec((rb, LANE), lambda d, i: (i, 0))],
        out_specs=[pl.BlockSpec((1, rb, D), lambda d, i: (d, i, 0)),
                   pl.BlockSpec((1, rb, DN_W), lambda d, i: (d, i, 0)),
                   pl.BlockSpec((1, n_g, HS, HS), lambda d, i: (d, i, 0, 0)),
                   pl.BlockSpec((1, n_g, HS, HS), lambda d, i: (d, i, 0, 0))],
        out_shape=[jax.ShapeDtypeStruct((2, t_len, D), F32), jax.ShapeDtypeStruct((2, t_len, DN_W), BF16),
                   jax.ShapeDtypeStruct((2, t_len // DN_C, HS, HS), BF16),
                   jax.ShapeDtypeStruct((2, t_len // DN_C, HS, HS), F32)],
        compiler_params=_params(2),
    )(dqkv, dqkv, gbc)


SEQ_G = 4
RET_G = 2


def _dn_seq_inputs(u_ref, wqk_ref, at_ref, gb_ref, d, gidx):
    rows = pl.ds(pl.multiple_of(gidx * DN_C, DN_C), DN_C)
    u = [u_ref[0, rows, h * DV:(h + 1) * DV] for h in range(NH)]
    w = [wqk_ref[0, rows, h * DK:(h + 1) * DK].astype(F32) for h in range(NH)]
    qg = [wqk_ref[0, rows, NH * DK + h * DK:NH * DK + (h + 1) * DK].astype(F32) for h in range(NH)]
    kd = [wqk_ref[0, rows, 2 * NH * DK + h * DK:2 * NH * DK + (h + 1) * DK].astype(F32) for h in range(NH)]
    gbv = gb_ref[rows, :]
    glast = [jnp.sum(_pick_lane(gbv, 8 + 4 * d + h), axis=0, keepdims=True) for h in range(NH)]
    return u, w, qg, kd, at_ref[0, gidx].astype(F32), glast


def dn_seq_fwd(u, wqk, attn, gbc, s0):
    t_len = u.shape[1]
    n_b = t_len // (DN_C * SEQ_G)
    rb = DN_C * SEQ_G

    def block_of(d, t):
        return jnp.where(d == 0, t, n_b - 1 - t)

    def body(u_ref, wqk_ref, at_ref, gb_ref, s0_ref, o_ref, ss_ref, sf_ref, st):
        d, t = pl.program_id(0), pl.program_id(1)

        @pl.when(t == 0)
        def _():
            st[...] = s0_ref[0]

        for gi in range(SEQ_G):
            gidx = jnp.where(d == 0, gi, SEQ_G - 1 - gi)
            rows = pl.ds(pl.multiple_of(gidx * DN_C, DN_C), DN_C)
            s_in = [st[h] for h in range(NH)]
            ss_ref[0, gidx] = st[...].astype(BF16)
            o, s2 = _dn_seq4(*_dn_seq_inputs(u_ref, wqk_ref, at_ref, gb_ref, d, gidx), s_in)
            for h in range(NH):
                o_ref[0, rows, h * DV:(h + 1) * DV] = o[_head_rows(h)]
                st[h] = s2[h]

        @pl.when(t == n_b - 1)
        def _():
            sf_ref[0] = st[...]

    return pl.pallas_call(
        body, name="dn_seq_fwd", grid=(2, n_b),
        in_specs=[pl.BlockSpec((1, rb, D), lambda d, t: (d, block_of(d, t), 0)),
                  pl.BlockSpec((1, rb, DN_W), lambda d, t: (d, block_of(d, t), 0)),
                  pl.BlockSpec((1, SEQ_G, HS, HS), lambda d, t: (d, block_of(d, t), 0, 0)),
                  pl.BlockSpec((rb, LANE), lambda d, t: (block_of(d, t), 0)),
                  _state_spec()],
        out_specs=[pl.BlockSpec((1, rb, D), lambda d, t: (d, block_of(d, t), 0)),
                   pl.BlockSpec((1, SEQ_G, NH, DK, DV), lambda d, t: (d, block_of(d, t), 0, 0, 0)),
                   _state_spec()],
        out_shape=[jax.ShapeDtypeStruct((2, t_len, D), F32),
                   jax.ShapeDtypeStruct((2, t_len // DN_C, NH, DK, DV), BF16),
                   jax.ShapeDtypeStruct((2, NH, DK, DV), F32)],
        scratch_shapes=[pltpu.VMEM((NH, DK, DV), F32)],
        compiler_params=_params(2),
    )(u, wqk, attn, gbc, s0)


def dn_seq_bwd(u, wqk, attn, gbc, ssave, do, ds_fin):
    t_len = u.shape[1]
    n_c = t_len // DN_C
    n_b = n_c // SEQ_G
    rb = DN_C * SEQ_G

    def block_of(d, t):
        return jnp.where(d == 0, n_b - 1 - t, t)

    def body(u_ref, wqk_ref, at_ref, gb_ref, ss_ref, do_ref, dsf_ref, du_ref, dwqk_ref, dat_ref, dgl_ref, ds0_ref, dst):
        d, t = pl.program_id(0), pl.program_id(1)

        @pl.when(t == 0)
        def _():
            dst[...] = dsf_ref[0]

        rows8 = lax.broadcasted_iota(jnp.int32, (8, LANE), 0)
        for gi in range(SEQ_G):
            gidx = jnp.where(d == 0, SEQ_G - 1 - gi, gi)
            rows = pl.ds(pl.multiple_of(gidx * DN_C, DN_C), DN_C)
            s_in = [ss_ref[0, gidx, h].astype(F32) for h in range(NH)]
            _, vjp = jax.vjp(_dn_seq4, *_dn_seq_inputs(u_ref, wqk_ref, at_ref, gb_ref, d, gidx), s_in)
            do4 = _stack_heads(lambda cs: do_ref[rows, cs], DV)
            du, dw, dqg, dkd, dat, dgl, ds = vjp((do4, [dst[h] for h in range(NH)]))
            dat_ref[0, gidx] = dat
            dgl_tile = jnp.zeros((8, LANE), F32)
            for h in range(NH):
                du_ref[0, rows, h * DV:(h + 1) * DV] = du[h]
                dwqk_ref[0, rows, h * DK:(h + 1) * DK] = dw[h]
                dwqk_ref[0, rows, NH * DK + h * DK:NH * DK + (h + 1) * DK] = dqg[h]
                dwqk_ref[0, rows, 2 * NH * DK + h * DK:2 * NH * DK + (h + 1) * DK] = dkd[h]
                dgl_tile = jnp.where(rows8 == h, dgl[h], dgl_tile)
                dst[h] = ds[h]
            dgl_ref[0, gidx] = dgl_tile

        @pl.when(t == n_b - 1)
        def _():
            ds0_ref[0] = dst[...]

    seq = lambda width: pl.BlockSpec((1, rb, width), lambda d, t: (d, block_of(d, t), 0))
    att = pl.BlockSpec((1, SEQ_G, HS, HS), lambda d, t: (d, block_of(d, t), 0, 0))
    return pl.pallas_call(
        body, name="dn_seq_bwd", grid=(2, n_b),
        in_specs=[seq(D), seq(DN_W), att, pl.BlockSpec((rb, LANE), lambda d, t: (block_of(d, t), 0)),
                  pl.BlockSpec((1, SEQ_G, NH, DK, DV), lambda d, t: (d, block_of(d, t), 0, 0, 0)),
                  pl.BlockSpec((rb, D), lambda d, t: (block_of(d, t), 0)), _state_spec()],
        out_specs=[seq(D), seq(DN_W), att, pl.BlockSpec((1, SEQ_G, 8, LANE), lambda d, t: (d, block_of(d, t), 0, 0)),
                   _state_spec()],
        out_shape=[jax.ShapeDtypeStruct((2, t_len, D), F32), jax.ShapeDtypeStruct((2, t_len, DN_W), F32),
                   jax.ShapeDtypeStruct((2, n_c, HS, HS), F32), jax.ShapeDtypeStruct((2, n_c, 8, LANE), F32),
                   jax.ShapeDtypeStruct((2, NH, DK, DV), F32)],
        scratch_shapes=[pltpu.VMEM((NH, DK, DV), F32)],
        compiler_params=_params(2),
    )(u, wqk, attn, gbc, ssave, do, ds_fin)


def dn_intra_bwd(dqkv, gbc, tinv, du, dwqk, dattn, dgl):
    t_len = dqkv.shape[0]
    rb = 128
    n_g = rb // DN_C

    def body(qk_ref, v_ref, gb_ref, ti_ref, du_ref, dwqk_ref, dat_ref, dgl_ref, dqk_ref, dv_ref, dgb_ref):
        d = pl.program_id(0)
        rev = d == 1
        for gi in range(n_g):
            rows = slice(gi * DN_C, (gi + 1) * DN_C)
            g4, b4 = _dn_gates4(gb_ref[rows, :], d)
            _, vjp, _ = jax.vjp(functools.partial(_dn_intra4, rev=rev, tinv_known=ti_ref[0, gi]),
                                _stack_heads(lambda cs: qk_ref[rows, cs], DK),
                                _stack_heads(lambda cs: qk_ref[rows, cs], DK, NH * DK),
                                _stack_heads(lambda cs: v_ref[rows, cs], DV), g4, b4, has_aux=True)
            dq, dk, dv, dg, db = vjp((_stack_heads(lambda cs: du_ref[0, rows, cs], DV),
                                      _stack_heads(lambda cs: dwqk_ref[0, rows, cs], DK),
                                      _stack_heads(lambda cs: dwqk_ref[0, rows, cs], DK, NH * DK),
                                      _stack_heads(lambda cs: dwqk_ref[0, rows, cs], DK, 2 * NH * DK),
                                      dat_ref[0, gi]))
            dgb = jnp.zeros((DN_C, LANE), F32)
            for h in range(NH):
                hr = _head_rows(h)
                dqk_ref[0, rows, h * DK:(h + 1) * DK] = dq[hr]
                dqk_ref[0, rows, NH * DK + h * DK:NH * DK + (h + 1) * DK] = dk[hr]
                dv_ref[0, rows, h * DV:(h + 1) * DV] = dv[hr]
                dgb = (dgb + _put_lane(dg[hr] + dgl_ref[0, gi, h:h + 1, 0:1], 8 + 4 * d + h, LANE)
                       + _put_lane(db[hr], 4 * d + h, LANE))
            dgb_ref[0, rows, :] = dgb

    seq = lambda width: pl.BlockSpec((1, rb, width), lambda d, i: (d, i, 0))
    return pl.pallas_call(
        body, name="dn_intra_bwd", grid=(2, t_len // rb),
        in_specs=[pl.BlockSpec((rb, D), lambda d, i: (i, 0)), pl.BlockSpec((rb, D), lambda d, i: (i, 1)),
                  pl.BlockSpec((rb, LANE), lambda d, i: (i, 0)),
                  pl.BlockSpec((1, n_g, HS, HS), lambda d, i: (d, i, 0, 0)), seq(D), seq(DN_W),
                  pl.BlockSpec((1, n_g, HS, HS), lambda d, i: (d, i, 0, 0)),
                  pl.BlockSpec((1, n_g, 8, LANE), lambda d, i: (d, i, 0, 0))],
        out_specs=[seq(D), seq(D), seq(LANE)],
        out_shape=[jax.ShapeDtypeStruct((2, t_len, D), F32), jax.ShapeDtypeStruct((2, t_len, D), F32),
                   jax.ShapeDtypeStruct((2, t_len, LANE), F32)],
        compiler_params=_params(2),
    )(dqkv, dqkv, gbc, tinv, du, dwqk, dattn, dgl)


def _pick_lane(x, lane_idx):
    lane = lax.broadcasted_iota(jnp.int32, x.shape, 1)
    return jnp.sum(jnp.where(lane == lane_idx, x, 0.0), axis=1, keepdims=True)


def _put_lane(col, lane_idx, width):
    lane = lax.broadcasted_iota(jnp.int32, (col.shape[0], width), 1)
    return jnp.where(lane == lane_idx, col, 0.0)


def _state_spec():
    return pl.BlockSpec((1, NH, DK, DV), lambda d, t: (d, 0, 0, 0))


def ret_fwd(rqk, p, s0):
    c = RET_C
    t_len = rqk.shape[0]
    n_c = t_len // c
    n_b = n_c // RET_G
    rb = c * RET_G

    def block_of(d, t):
        return jnp.where(d == 0, t, n_b - 1 - t)

    def body(qk_ref, v_ref, s0_ref, o_ref, ss_ref, sf_ref, st):
        d, t = pl.program_id(0), pl.program_id(1)
        rev = d == 1

        @pl.when(t == 0)
        def _():
            st[...] = s0_ref[0]

        for gi in range(RET_G):
            gidx = jnp.where(d == 0, gi, RET_G - 1 - gi)
            rows = pl.ds(pl.multiple_of(gidx * c, c), c)
            ss_ref[0, gidx] = st[...].astype(BF16)
            for h in range(NH):
                o, s2 = _ret_chunk(qk_ref[rows, h * DK:(h + 1) * DK],
                                   qk_ref[rows, NH * DK + h * DK:NH * DK + (h + 1) * DK],
                                   v_ref[rows, h * DV:(h + 1) * DV], st[h], _log_gamma(h), rev)
                o_ref[0, rows, h * DV:(h + 1) * DV] = o
                st[h] = s2

        @pl.when(t == n_b - 1)
        def _():
            sf_ref[0] = st[...]

    return pl.pallas_call(
        body, name="ret_fwd", grid=(2, n_b),
        in_specs=[pl.BlockSpec((rb, D), lambda d, t: (block_of(d, t), 0)),
                  pl.BlockSpec((rb, D), lambda d, t: (block_of(d, t), C_RV)),
                  _state_spec()],
        out_specs=[pl.BlockSpec((1, rb, D), lambda d, t: (d, block_of(d, t), 0)),
                   pl.BlockSpec((1, RET_G, NH, DK, DV), lambda d, t: (d, block_of(d, t), 0, 0, 0)),
                   _state_spec()],
        out_shape=[jax.ShapeDtypeStruct((2, t_len, D), F32),
                   jax.ShapeDtypeStruct((2, n_c, NH, DK, DV), BF16),
                   jax.ShapeDtypeStruct((2, NH, DK, DV), F32)],
        scratch_shapes=[pltpu.VMEM((NH, DK, DV), F32)],
        compiler_params=_params(2),
    )(rqk, p, s0)


def ret_bwd(rqk, p, ssave, do, ds_fin):
    c = RET_C
    t_len = rqk.shape[0]
    n_b = t_len // (c * RET_G)
    rb = c * RET_G

    def block_of(d, t):
        return jnp.where(d == 0, n_b - 1 - t, t)

    def body(qk_ref, v_ref, ss_ref, do_ref, dsf_ref, dqk_ref, dv_ref, ds0_ref, dst):
        d, t = pl.program_id(0), pl.program_id(1)
        rev = d == 1

        @pl.when(t == 0)
        def _():
            dst[...] = dsf_ref[0]

        for gi in range(RET_G):
            gidx = jnp.where(d == 0, RET_G - 1 - gi, gi)
            rows = pl.ds(pl.multiple_of(gidx * c, c), c)
            for h in range(NH):
                _, vjp = jax.vjp(functools.partial(_ret_chunk, lg=_log_gamma(h), rev=rev),
                                 qk_ref[rows, h * DK:(h + 1) * DK],
                                 qk_ref[rows, NH * DK + h * DK:NH * DK + (h + 1) * DK],
                                 v_ref[rows, h * DV:(h + 1) * DV], ss_ref[0, gidx, h].astype(F32))
                dq, dk, dv, ds = vjp((do_ref[rows, h * DV:(h + 1) * DV], dst[h]))
                dqk_ref[0, rows, h * DK:(h + 1) * DK] = dq
                dqk_ref[0, rows, NH * DK + h * DK:NH * DK + (h + 1) * DK] = dk
                dv_ref[0, rows, h * DV:(h + 1) * DV] = dv
                dst[h] = ds

        @pl.when(t == n_b - 1)
        def _():
            ds0_ref[0] = dst[...]

    seq_spec = pl.BlockSpec((1, rb, D), lambda d, t: (d, block_of(d, t), 0))
    return pl.pallas_call(
        body, name="ret_bwd", grid=(2, n_b),
        in_specs=[pl.BlockSpec((rb, D), lambda d, t: (block_of(d, t), 0)),
                  pl.BlockSpec((rb, D), lambda d, t: (block_of(d, t), C_RV)),
                  pl.BlockSpec((1, RET_G, NH, DK, DV), lambda d, t: (d, block_of(d, t), 0, 0, 0)),
                  pl.BlockSpec((rb, D), lambda d, t: (block_of(d, t), 0)),
                  _state_spec()],
        out_specs=[seq_spec, seq_spec, _state_spec()],
        out_shape=[jax.ShapeDtypeStruct((2, t_len, D), F32)] * 2 + [jax.ShapeDtypeStruct((2, NH, DK, DV), F32)],
        scratch_shapes=[pltpu.VMEM((NH, DK, DV), F32)],
        compiler_params=_params(2),
    )(rqk, p, ssave, do, ds_fin)


def _head_gate(o_ret, o_dn, rg, dz, nw):
    ret = o_ret * lax.rsqrt(jnp.mean(o_ret * o_ret, axis=-1, keepdims=True) + EPS) * (rg * _sigmoid(rg))
    dn = o_dn * lax.rsqrt(jnp.mean(o_dn * o_dn, axis=-1, keepdims=True) + EPS) * nw * (dz * _sigmoid(dz))
    return ret, dn


MIX_TM = 256


def _mix_specs():
    seq = lambda col: pl.BlockSpec((MIX_TM, D), functools.partial(lambda i, col: (i, col), col=col))
    pair = pl.BlockSpec((2, MIX_TM, D), lambda i: (0, i, 0))
    wfull = pl.BlockSpec((D, D), lambda i: (0, 0))
    vec = pl.BlockSpec((8, D), lambda i: (0, 0))
    return seq, pair, wfull, vec


def mixout_fwd(x1, vec, o_ret, o_dn, p, w_r, w_d, w_o):
    t_len = x1.shape[0]

    def body(x_ref, vec_ref, or_ref, od_ref, rg_ref, dz_ref, ga_ref, gb_ref, wr_ref, wd_ref, wo_ref,
             x2_ref, ret_ref, dn_ref, y_ref, yr_ref, yd_ref, z_ref):
        for h in range(NH):
            sl = slice(h * DV, (h + 1) * DV)
            ret, dn = _head_gate(or_ref[0, :, sl] + or_ref[1, :, sl], od_ref[0, :, sl] + od_ref[1, :, sl],
                                 rg_ref[:, sl], dz_ref[:, sl], vec_ref[1:2, sl])
            ret_ref[:, sl] = ret.astype(BF16)
            dn_ref[:, sl] = dn.astype(BF16)
        yr = jnp.dot(ret_ref[...], wr_ref[...], preferred_element_type=F32)
        yd = jnp.dot(dn_ref[...], wd_ref[...], preferred_element_type=F32)
        yr_ref[...] = yr.astype(BF16)
        yd_ref[...] = yd.astype(BF16)
        y = (_sigmoid(ga_ref[...]) * yr + _sigmoid(gb_ref[...]) * yd).astype(BF16)
        y_ref[...] = y
        z = jnp.dot(y, wo_ref[...], preferred_element_type=F32)
        z_ref[...] = z.astype(BF16)
        x2_ref[...] = x_ref[...] + vec_ref[0:1, :] * z

    seq, pair, wfull, vecs = _mix_specs()
    return pl.pallas_call(
        body, name="mixout_fwd", grid=(t_len // MIX_TM,),
        in_specs=[seq(0), vecs, pair, pair, seq(C_RG), seq(C_DZ), seq(C_GA), seq(C_GB), wfull, wfull, wfull],
        out_specs=[seq(0)] * 7,
        out_shape=[jax.ShapeDtypeStruct((t_len, D), F32)] + [jax.ShapeDtypeStruct((t_len, D), BF16)] * 6,
        compiler_params=_params(1),
    )(x1, vec, o_ret, o_dn, p, p, p, p, w_r, w_d, w_o)


def mixout_bwd(dx2, vec, o_ret, o_dn, p, yr, yd, z, w_r, w_d, w_o):
    t_len = dx2.shape[0]

    def body(dx_ref, vec_ref, or_ref, od_ref, rg_ref, dz_ref, ga_ref, gb_ref, yr_ref, yd_ref, z_ref,
             wr_ref, wd_ref, wo_ref,
             dor_ref, dod_ref, drg_ref, ddz_ref, dga_ref, dgb_ref, dyr_ref, dyd_ref, dzz_ref, pg_ref):
        @pl.when(pl.program_id(0) == 0)
        def _():
            pg_ref[...] = jnp.zeros_like(pg_ref)

        dx = dx_ref[...]
        pg_ref[0:1, :] += jnp.sum(dx * z_ref[...].astype(F32), axis=0, keepdims=True)
        dzz = (vec_ref[0:1, :] * dx).astype(BF16)
        dzz_ref[...] = dzz
        dy = _dot(dzz, wo_ref[...], NT)
        sa = _sigmoid(ga_ref[...])
        sb = _sigmoid(gb_ref[...])
        dyr = (dy * sa).astype(BF16)
        dyd = (dy * sb).astype(BF16)
        dyr_ref[...] = dyr
        dyd_ref[...] = dyd
        dga_ref[...] = (dy * yr_ref[...].astype(F32) * sa * (1.0 - sa)).astype(BF16)
        dgb_ref[...] = (dy * yd_ref[...].astype(F32) * sb * (1.0 - sb)).astype(BF16)
        dret = _dot(dyr, wr_ref[...], NT)
        ddn = _dot(dyd, wd_ref[...], NT)
        for h in range(NH):
            sl = slice(h * DV, (h + 1) * DV)
            _, vjp = jax.vjp(_head_gate, or_ref[0, :, sl] + or_ref[1, :, sl], od_ref[0, :, sl] + od_ref[1, :, sl],
                             rg_ref[:, sl], dz_ref[:, sl], vec_ref[1:2, sl])
            d_or, d_od, d_rg, d_dz, d_nw = vjp((dret[:, sl], ddn[:, sl]))
            dor_ref[:, sl] = d_or
            dod_ref[:, sl] = d_od
            drg_ref[:, sl] = d_rg.astype(BF16)
            ddz_ref[:, sl] = d_dz.astype(BF16)
            pg_ref[1:2, sl] += d_nw

    seq, pair, wfull, vecs = _mix_specs()
    return pl.pallas_call(
        body, name="mixout_bwd", grid=(t_len // MIX_TM,),
        in_specs=[seq(0), vecs, pair, pair, seq(C_RG), seq(C_DZ), seq(C_GA), seq(C_GB), seq(0), seq(0), seq(0),
                  wfull, wfull, wfull],
        out_specs=[seq(0)] * 9 + [vecs],
        out_shape=[jax.ShapeDtypeStruct((t_len, D), F32)] * 2 + [jax.ShapeDtypeStruct((t_len, D), BF16)] * 7
        + [jax.ShapeDtypeStruct((8, D), F32)],
        compiler_params=_params(1),
    )(dx2, vec, o_ret, o_dn, p, p, p, p, yr, yd, z, w_r, w_d, w_o)


def _final_loss(x, w, target):
    y = x * lax.rsqrt(jnp.mean(x * x, axis=-1, keepdims=True) + EPS) * w
    err = y - target
    return 0.5 * jnp.sum(jnp.mean(err * err, axis=-1, keepdims=True), axis=0, keepdims=True)


def final_fwd_bwd(x, vec, target):
    t_len = x.shape[0]
    tm = _row_block(t_len)

    def body(x_ref, vec_ref, t_ref, dx_ref, pg_ref, loss_ref):
        @pl.when(pl.program_id(0) == 0)
        def _():
            pg_ref[...] = jnp.zeros_like(pg_ref)
            loss_ref[...] = jnp.zeros_like(loss_ref)

        loss, vjp = jax.vjp(functools.partial(_final_loss, target=t_ref[...]), x_ref[...], vec_ref[0:1, :])
        dx, dw = vjp(jnp.ones((1, 1), F32))
        dx_ref[...] = dx
        pg_ref[0:1, :] += dw
        loss_ref[...] += jnp.broadcast_to(loss, loss_ref.shape)

    return pl.pallas_call(
        body, name="final_fwd_bwd", grid=(t_len // tm,),
        in_specs=[pl.BlockSpec((tm, D), lambda i: (i, 0)), pl.BlockSpec((8, D), lambda i: (0, 0)),
                  pl.BlockSpec((tm, D), lambda i: (i, 0))],
        out_specs=[pl.BlockSpec((tm, D), lambda i: (i, 0)), pl.BlockSpec((8, D), lambda i: (0, 0)),
                   pl.BlockSpec((8, LANE), lambda i: (0, 0))],
        out_shape=[jax.ShapeDtypeStruct((t_len, D), F32), jax.ShapeDtypeStruct((8, D), F32),
                   jax.ShapeDtypeStruct((8, LANE), F32)],
        compiler_params=_params(1),
    )(x, vec, target)


def _vec8(*rows):
    rows = list(rows) + [jnp.zeros((D,), F32)] * (8 - len(rows))
    return jnp.stack(rows)


def _layer_vecs(lw, m):
    nw = lw["norm_w"]
    return dict(ffn1=_vec8(nw[0], m[0], m[1], m[2]), proj=_vec8(nw[1], m[3], m[4]),
                mix=_vec8(m[5], lw["dn_norm_t"]), ffn2=_vec8(nw[2], m[6], m[7], m[8]))


def _rope_tables(t_len, grid_w=64, base=10000.0):
    n_freq = DK // 4
    inv = base ** (-jnp.arange(n_freq, dtype=F32) / n_freq)
    tok = jnp.arange(t_len)
    ang = jnp.concatenate([(tok // grid_w).astype(F32)[:, None] * inv, (tok % grid_w).astype(F32)[:, None] * inv],
                          axis=-1)
    cos, sin = jnp.cos(ang), jnp.sin(ang)
    return jnp.concatenate([cos, cos], axis=-1), jnp.concatenate([-sin, sin], axis=-1)


def _stream_fwd_a(x0, lw, vecs, rope):
    x1, h1, g1, u1, f1 = ffn_fwd(x0, vecs["ffn1"], lw["wgu1"], lw["wd1"])
    p, h2 = proj_fwd(x1, vecs["proj"], lw["w_in"])
    rqk, dqkv, gbc = feat_fwd(p, rope[0], rope[1], lw["conv_w8"], lw["gparams"])
    return dict(x0=x0, h1=h1, g1=g1, u1=u1, f1=f1, x1=x1, h2=h2, p=p, rqk=rqk, dqkv=dqkv, gbc=gbc)


def _stream_mix(sv, s0_ret, s0_dn):
    sv["o_ret"], sv["ss_ret"], sf_ret = ret_fwd(sv["rqk"], sv["p"], s0_ret)
    sv["dn_u"], sv["dn_wqk"], sv["dn_attn"], sv["dn_tinv"] = dn_intra_fwd(sv["dqkv"], sv["gbc"])
    sv["o_dn"], sv["ss_dn"], sf_dn = dn_seq_fwd(sv["dn_u"], sv["dn_wqk"], sv["dn_attn"], sv["gbc"], s0_dn)
    return sf_ret, sf_dn


def _stream_fwd_b(sv, lw, vecs):
    x2, ret, dn, y, yr, yd, z = mixout_fwd(sv["x1"], vecs["mix"], sv["o_ret"], sv["o_dn"], sv["p"],
                                           lw["w_r"], lw["w_d"], lw["w_o"])
    x3, h3, g3, u3, f3 = ffn_fwd(x2, vecs["ffn2"], lw["wgu2"], lw["wd2"])
    sv.update(ret=ret, dn=dn, y=y, yr=yr, yd=yd, z=z, x2=x2, h3=h3, g3=g3, u3=u3, f3=f3)
    return x3


def layer_fwd(xs, cs, lw, ropes):
    vx, vc = _layer_vecs(lw, lw["mx"]), _layer_vecs(lw, lw["mc"])
    sx = _stream_fwd_a(xs, lw, vx, ropes[0])
    sc = _stream_fwd_a(cs, lw, vc, ropes[1])
    zero = jnp.zeros((2, NH, DK, DV), F32)
    sf_ret, sf_dn = _stream_mix(sc, zero, zero)
    _stream_mix(sx, sf_ret, sf_dn)
    x3 = _stream_fwd_b(sx, lw, vx)
    c3 = _stream_fwd_b(sc, lw, vc)
    return x3, c3, (sx, sc)


def _stream_bwd_a(dx3, sv, lw, vecs):
    act3, dg3, du3, dy3, pg_a = ffn_bwd_act(dx3, vecs["ffn2"], sv["g3"], sv["u3"], sv["f3"], lw["wd2"])
    dx2, pg_b = nt_norm_bwd([dg3, du3], lw["wgu2"], [0, 2], FFN_HB, 2, sv["x2"], dx3, vecs["ffn2"], "ffn_bwd_in")
    (dor, dod, drg, ddz, dga, dgb, dyr, dyd, dzz, pg_m) = mixout_bwd(
        dx2, vecs["mix"], sv["o_ret"], sv["o_dn"], sv["p"], sv["yr"], sv["yd"], sv["z"], lw["w_r"], lw["w_d"], lw["w_o"])
    return dict(act3=act3, dg3=dg3, du3=du3, dy3=dy3, pg_a2=pg_a, pg_b2=pg_b, dx2=dx2, dor=dor, dod=dod,
                drg=drg, ddz=ddz, dga=dga, dgb=dgb, dyr=dyr, dyd=dyd, dzz=dzz, pg_m=pg_m)


def _stream_bwd_mix(bw, sv, dsf_ret, dsf_dn):
    bw["dqk_r"], bw["drv"], ds0_ret = ret_bwd(sv["rqk"], sv["p"], sv["ss_ret"], bw["dor"], dsf_ret)
    du, dwqk, dattn, dgl, ds0_dn = dn_seq_bwd(sv["dn_u"], sv["dn_wqk"], sv["dn_attn"], sv["gbc"], sv["ss_dn"],
                                              bw["dod"], dsf_dn)
    bw["dqk_d"], bw["dvv_d"], bw["dgbc"] = dn_intra_bwd(sv["dqkv"], sv["gbc"], sv["dn_tinv"], du, dwqk, dattn, dgl)
    return ds0_ret, ds0_dn


def _stream_bwd_b(bw, sv, lw, vecs, rope):
    dp_rqk, dp_rv, dp_dqk, dp_dvv, dp_dba, bw["dcw"], bw["dgp"] = feat_bwd(
        sv["p"], rope[0], rope[1], lw["conv_w8"], lw["gparams"], bw["dqk_r"], bw["drv"], bw["dqk_d"], bw["dvv_d"],
        bw["dgbc"])
    bw["dp"] = jnp.concatenate([dp_rqk, dp_rv, bw["drg"], dp_dqk, dp_dvv, bw["ddz"], bw["dga"], bw["dgb"], dp_dba],
                               axis=1)
    dx1, bw["pg_p"] = nt_norm_bwd([bw["dp"]], lw["w_in"], [0], PROJ_TN, PW // PROJ_TN, sv["x1"], bw["dx2"],
                                  vecs["proj"], "proj_bwd_in")
    bw["act1"], bw["dg1"], bw["du1"], bw["dy1"], bw["pg_a1"] = ffn_bwd_act(dx1, vecs["ffn1"], sv["g1"], sv["u1"],
                                                                          sv["f1"], lw["wd1"])
    dx0, bw["pg_b1"] = nt_norm_bwd([bw["dg1"], bw["du1"]], lw["wgu1"], [0, 2], FFN_HB, 2, sv["x0"], dx1,
                                   vecs["ffn1"], "ffn_bwd_in")
    return dx0


def _stream_pgrads(bw):
    a1, b1, pp, pm, a2, b2 = bw["pg_a1"], bw["pg_b1"], bw["pg_p"], bw["pg_m"], bw["pg_a2"], bw["pg_b2"]
    dmod = jnp.stack([b1[1], b1[2], a1[3], pp[1], pp[2], pm[0], b2[1], b2[2], a2[3]])
    dnorm = jnp.stack([b1[0], pp[0], b2[0]])
    ddnw = pm[1].reshape(NH, DV).sum(axis=0)
    return dmod, dnorm, ddnw, bw["dcw"][:CONV_K], bw["dgp"][0, 8:16], bw["dgp"][1, 8:16]


def layer_bwd(dx3, dc3, lw, saved, ropes):
    sx, sc = saved
    vx, vc = _layer_vecs(lw, lw["mx"]), _layer_vecs(lw, lw["mc"])
    bx = _stream_bwd_a(dx3, sx, lw, vx)
    bc = _stream_bwd_a(dc3, sc, lw, vc)
    zero = jnp.zeros((2, NH, DK, DV), F32)
    ds0_ret, ds0_dn = _stream_bwd_mix(bx, sx, zero, zero)
    _stream_bwd_mix(bc, sc, ds0_ret, ds0_dn)
    dx0 = _stream_bwd_b(bx, sx, lw, vx, ropes[0])
    dc0 = _stream_bwd_b(bc, sc, lw, vc, ropes[1])

    def wgrad(a, b, tn, name):
        ax = sx[a] if a in sx else bx[a]
        ac = sc[a] if a in sc else bc[a]
        return tn_matmul(ax, bx[b], ac, bc[b], tn, name)

    gw = dict(
        wg1=wgrad("h1", "dg1", FFN_HB, "dw_ffn_gu"), wu1=wgrad("h1", "du1", FFN_HB, "dw_ffn_gu"),
        wd1=wgrad("act1", "dy1", 512, "dw_ffn_d"),
        w_in=wgrad("h2", "dp", PROJ_TN, "dw_in"),
        w_r=wgrad("ret", "dyr", 512, "dw_sq"), w_d=wgrad("dn", "dyd", 512, "dw_sq"), w_o=wgrad("y", "dzz", 512, "dw_sq"),
        wg2=wgrad("h3", "dg3", FFN_HB, "dw_ffn_gu"), wu2=wgrad("h3", "du3", FFN_HB, "dw_ffn_gu"),
        wd2=wgrad("act3", "dy3", 512, "dw_ffn_d"),
    )
    px, pc = _stream_pgrads(bx), _stream_pgrads(bc)
    small = dict(dmx=px[0], dmc=pc[0], norm_w=px[1] + pc[1], dn_norm_w=px[2] + pc[2], conv_w=px[3] + pc[3],
                 a_log=px[4] + pc[4], dt_bias=px[5] + pc[5])
    return dx0, dc0, gw, small


def local_step(x, ctx, target, final_norm_w, lws):
    t_len, t_ctx = x.shape[0], ctx.shape[0]
    ropes = (_rope_tables(t_len), (jnp.ones((t_ctx, LANE), F32), jnp.zeros((t_ctx, LANE), F32)))
    xs, cs, saved = x, ctx, []
    for lw in lws:
        xs, cs, sv = layer_fwd(xs, cs, lw, ropes)
        saved.append(sv)
    dx, pg_f, loss = final_fwd_bwd(xs, _vec8(final_norm_w), target)
    dc = jnp.zeros_like(ctx)
    gws, smalls = [None] * len(lws), [None] * len(lws)
    for l in reversed(range(len(lws))):
        dx, dc, gws[l], smalls[l] = layer_bwd(dx, dc, lws[l], saved[l], ropes)
    return loss[0, 0], dx, gws, smalls, pg_f[0]


ADA_ROWS = 16
ADA_COLS = N_MOD * D // N_DEV


def ada_fwd(cc, ada_w, ada_b_cols):
    def body(cc_ref, w_ref, b_ref, o_ref):
        cv = cc_ref[...]
        o_ref[0] = _dot_hi(cv * _sigmoid(cv), w_ref[0]) + b_ref[0]

    return pl.pallas_call(
        body, name="ada_fwd", grid=(DEPTH,),
        in_specs=[pl.BlockSpec((ADA_ROWS, D), lambda l: (0, 0)), pl.BlockSpec((1, D, ADA_COLS), lambda l: (l, 0, 0)),
                  pl.BlockSpec((1, 1, ADA_COLS), lambda l: (l, 0, 0))],
        out_specs=pl.BlockSpec((1, ADA_ROWS, ADA_COLS), lambda l: (l, 0, 0)),
        out_shape=jax.ShapeDtypeStruct((DEPTH, ADA_ROWS, ADA_COLS), F32),
        compiler_params=_params(1),
    )(cc, ada_w, ada_b_cols)


def ada_bwd(cc, dmod, ada_w):
    def body(cc_ref, d_ref, w_ref, gw_ref, ds_ref):
        cv = cc_ref[...]
        gw_ref[0] = _dot_hi(cv * _sigmoid(cv), d_ref[0], TN)
        ds_ref[0] = _dot_hi(d_ref[0], w_ref[0], NT)

    return pl.pallas_call(
        body, name="ada_bwd", grid=(DEPTH,),
        in_specs=[pl.BlockSpec((ADA_ROWS, D), lambda l: (0, 0)),
                  pl.BlockSpec((1, ADA_ROWS, ADA_COLS), lambda l: (l, 0, 0)),
                  pl.BlockSpec((1, D, ADA_COLS), lambda l: (l, 0, 0))],
        out_specs=[pl.BlockSpec((1, D, ADA_COLS), lambda l: (l, 0, 0)), pl.BlockSpec((1, ADA_ROWS, D), lambda l: (l, 0, 0))],
        out_shape=[jax.ShapeDtypeStruct((DEPTH, D, ADA_COLS), F32), jax.ShapeDtypeStruct((DEPTH, ADA_ROWS, D), F32)],
        compiler_params=_params(1),
    )(cc, dmod, ada_w)


def c_ctx_grad(parts, c_ctx8):
    def body(p_ref, c_ref, o_ref):
        acc = p_ref[0]
        for j in range(1, N_DEV):
            acc = acc + p_ref[j]
        cv = c_ref[...]
        sg = _sigmoid(cv)
        o_ref[...] = acc * (sg + cv * sg * (1.0 - sg))

    return pl.pallas_call(body, name="c_ctx_grad", out_shape=jax.ShapeDtypeStruct((8, LANE), F32))(parts, c_ctx8)


def sum_slots(parts):
    n_slot, rows, _ = parts.shape
    tr = 8
    for cand in (1024, 512, 256, 128, 64, 32, 16, 8):
        if rows % cand == 0:
            tr = cand
            break

    def body(p_ref, o_ref):
        acc = p_ref[0]
        for j in range(1, n_slot):
            acc = acc + p_ref[j]
        o_ref[...] = acc

    return pl.pallas_call(
        body, name="sum_slots", grid=(rows // tr,),
        in_specs=[pl.BlockSpec((n_slot, tr, LANE), lambda i: (0, i, 0))],
        out_specs=pl.BlockSpec((tr, LANE), lambda i: (i, 0)),
        out_shape=jax.ShapeDtypeStruct((rows, LANE), F32),
        compiler_params=_params(1),
    )(parts)


def adamw(parts, w, m, v, rows_blk, name):
    n_slot, n_l, n_a, n_b = parts.shape

    def body(p_ref, w_ref, m_ref, v_ref, g_ref, d_ref, m2_ref, v2_ref):
        g = p_ref[0].astype(F32)
        for j in range(1, n_slot):
            g = g + p_ref[j].astype(F32)
        m2 = ADAM_B1 * m_ref[...] + (1.0 - ADAM_B1) * g
        v2 = ADAM_B2 * v_ref[...] + (1.0 - ADAM_B2) * (g * g)
        m_hat = m2 / (1.0 - ADAM_B1 ** ADAM_STEP)
        v_hat = v2 / (1.0 - ADAM_B2 ** ADAM_STEP)
        g_ref[...] = g
        m2_ref[...] = m2
        v2_ref[...] = v2
        d_ref[...] = -ADAM_LR * (m_hat / (jnp.sqrt(v_hat) + ADAM_EPS) + ADAM_WD * w_ref[...])

    blk = pl.BlockSpec((1, rows_blk, n_b), lambda l, i: (l, i, 0))
    return pl.pallas_call(
        body, name=name, grid=(n_l, n_a // rows_blk),
        in_specs=[pl.BlockSpec((n_slot, 1, rows_blk, n_b), lambda l, i: (0, l, i, 0)), blk, blk, blk],
        out_specs=[blk] * 4,
        out_shape=[jax.ShapeDtypeStruct((n_l, n_a, n_b), F32)] * 4,
        compiler_params=_params(2),
    )(parts, w, m, v)


MESH = pl.DeviceIdType.MESH


def _my_index():
    return 4 * lax.axis_index("x") + 2 * lax.axis_index("y") + lax.axis_index("c")


def all_gather(shards, name):
    n = len(shards)

    def body(*refs):
        x_refs, out_refs = refs[:n], refs[n:2 * n]
        send_sems, recv_sems, local_sems = refs[2 * n:]
        x, y, c = lax.axis_index("x"), lax.axis_index("y"), lax.axis_index("c")
        me, sibling = (x, y, c), (x, y, 1 - c)
        chips = [(1 - x, y), (x, 1 - y), (1 - x, 1 - y)]

        def slot(a, px, py, pc):
            return out_refs[a].at[4 * px + 2 * py + pc]

        def copy(a, k, block, to, src=None):
            return pltpu.make_async_remote_copy(
                src_ref=slot(a, *block) if src is None else src, dst_ref=slot(a, *block),
                send_sem=send_sems.at[7 * a + k], recv_sem=recv_sems.at[7 * a + k], device_id=to, device_id_type=MESH)

        mine = [pltpu.make_async_copy(x_refs[a], slot(a, *me), local_sems.at[a]) for a in range(n)]
        for cp in mine:
            cp.start()
        first = []
        for a in range(n):
            first.append(copy(a, 0, me, sibling, src=x_refs[a]))
            first += [copy(a, 1 + j, me, (*chip, c), src=x_refs[a]) for j, chip in enumerate(chips)]
        for cp in first:
            cp.start()
        passed = []
        for j, chip in enumerate(chips):
            for a in range(n):
                copy(a, 1 + j, (*chip, c), me).wait_recv()
                fwd = copy(a, 4 + j, (*chip, c), sibling)
                fwd.start()
                passed.append(fwd)
        for a in range(n):
            copy(a, 0, sibling, me).wait_recv()
            for j, chip in enumerate(chips):
                copy(a, 4 + j, (*chip, 1 - c), me).wait_recv()
        for cp in first + passed:
            cp.wait_send()
        for cp in mine:
            cp.wait()

    return pl.pallas_call(
        body, name=name,
        in_specs=[pl.BlockSpec(memory_space=pl.ANY)] * n,
        out_specs=[pl.BlockSpec(memory_space=pl.ANY)] * n,
        out_shape=[jax.ShapeDtypeStruct((N_DEV,) + s.shape, s.dtype) for s in shards],
        scratch_shapes=[pltpu.SemaphoreType.DMA((7 * n,)), pltpu.SemaphoreType.DMA((7 * n,)),
                        pltpu.SemaphoreType.DMA((n,))],
    )(*shards)


def all_to_all(parts, name):
    n = len(parts)

    def body(*refs):
        p_refs, q_refs = refs[:n], refs[n:2 * n]
        send_sems, recv_sems, local_sems = refs[2 * n:]
        x, y, c = lax.axis_index("x"), lax.axis_index("y"), lax.axis_index("c")
        me = 4 * x + 2 * y + c
        mine = [pltpu.make_async_copy(p_refs[a].at[me], q_refs[a].at[me], local_sems.at[a]) for a in range(n)]
        for cp in mine:
            cp.start()
        copies = []
        for r in range(1, N_DEV):
            px, py, pc = x ^ (r >> 2), y ^ ((r >> 1) & 1), c ^ (r & 1)
            peer = 4 * px + 2 * py + pc
            for a in range(n):
                k = 7 * a + r - 1
                cp = pltpu.make_async_remote_copy(
                    src_ref=p_refs[a].at[peer], dst_ref=q_refs[a].at[me], send_sem=send_sems.at[k],
                    recv_sem=recv_sems.at[k], device_id=(px, py, pc), device_id_type=MESH)
                cp.start()
                copies.append((cp, a, peer, k))
        for cp, a, peer, k in copies:
            pltpu.make_async_remote_copy(
                src_ref=p_refs[a].at[peer], dst_ref=q_refs[a].at[peer], send_sem=send_sems.at[k],
                recv_sem=recv_sems.at[k], device_id=(x, y, c), device_id_type=MESH).wait_recv()
        for cp, _, _, _ in copies:
            cp.wait_send()
        for cp in mine:
            cp.wait()

    return pl.pallas_call(
        body, name=name,
        in_specs=[pl.BlockSpec(memory_space=pl.ANY)] * n,
        out_specs=[pl.BlockSpec(memory_space=pl.ANY)] * n,
        out_shape=[jax.ShapeDtypeStruct(p.shape, p.dtype) for p in parts],
        scratch_shapes=[pltpu.SemaphoreType.DMA((7 * n,)), pltpu.SemaphoreType.DMA((7 * n,)),
                        pltpu.SemaphoreType.DMA((n,))],
    )(*parts)


WEIGHT_ORDER = ("c_ctx", "ada_w", "ada_b", "norm_w", "ffn1_wgu", "ffn1_wd", "w_in", "dn_conv_w", "dn_a_log",
                "dn_dt_bias", "dn_norm_w", "w_ret_out", "w_dn_out", "w_o", "ffn2_wgu", "ffn2_wd", "final_norm_w")
BIG = (("ffn1_wgu", "col"), ("ffn1_wd", "row"), ("w_in", "col"), ("w_ret_out", "row"), ("w_dn_out", "row"),
       ("w_o", "row"), ("ffn2_wgu", "col"), ("ffn2_wd", "row"))
LOCAL = ("ada_w", "c_ctx", "ada_b", "norm_w", "dn_conv_w", "dn_a_log", "dn_dt_bias", "dn_norm_w", "final_norm_w")


def _pack(arrs, row_mult, lead=None):
    if lead is None:
        flat = jnp.concatenate([a.reshape(-1) for a in arrs])
        n = flat.shape[0]
    else:
        flat = jnp.concatenate([a.reshape(lead, -1) for a in arrs], axis=1)
        n = flat.shape[1]
    unit = LANE * row_mult
    total = -(-n // unit) * unit
    if lead is None:
        return jnp.pad(flat, (0, total - n)).reshape(total // LANE, LANE)
    return jnp.pad(flat, ((0, 0), (0, total - n))).reshape(lead, total // LANE, LANE)


def _unpack(packed, shapes, lead=None):
    flat = packed.reshape(-1) if lead is None else packed.reshape(lead, -1)
    out, off = [], 0
    for shp in shapes:
        n = math.prod(shp)
        if lead is None:
            out.append(flat[off:off + n].reshape(shp))
        else:
            out.append(flat[:, off:off + n].reshape((lead,) + tuple(shp)))
        off += n
    return out


def _join_shards(g, kind):
    lead = tuple(range(1, g.ndim - 2))
    a, b = g.shape[-2:]
    if kind == "col":
        return g.transpose(*lead, g.ndim - 2, 0, g.ndim - 1).reshape(g.shape[1:-2] + (a, N_DEV * b))
    return g.transpose(*lead, 0, g.ndim - 2, g.ndim - 1).reshape(g.shape[1:-2] + (N_DEV * a, b))


def _split_shards(full, kind):
    a, b = full.shape
    if kind == "col":
        return full.reshape(a, N_DEV, b // N_DEV).transpose(1, 0, 2)
    return full.reshape(N_DEV, a // N_DEV, b)


def _pad_w_in(w):
    return jnp.concatenate([w[..., :6144], w[..., 6160:8208], w[..., 6144:6160],
                            jnp.zeros(w.shape[:-1] + (PW - 8208,), w.dtype)], axis=-1)


def _unpad_w_in(g):
    return jnp.concatenate([g[..., :6144], g[..., 8192:8208], g[..., 6144:8192]], axis=-1)


def kernel(x, c, ctx, c_ctx, ada_w, ada_b, norm_w, ffn1_wgu, ffn1_wd, w_in, dn_conv_w, dn_a_log, dn_dt_bias, dn_norm_w, w_ret_out, w_dn_out, w_o, ffn2_wgu, ffn2_wd, final_norm_w, loss_target, m_c_ctx, m_ada_w, m_ada_b, m_norm_w, m_ffn1_wgu, m_ffn1_wd, m_w_in, m_dn_conv_w, m_dn_a_log, m_dn_dt_bias, m_dn_norm_w, m_w_ret_out, m_w_dn_out, m_w_o, m_ffn2_wgu, m_ffn2_wd, m_final_norm_w, v_c_ctx, v_ada_w, v_ada_b, v_norm_w, v_ffn1_wgu, v_ffn1_wd, v_w_in, v_dn_conv_w, v_dn_a_log, v_dn_dt_bias, v_dn_norm_w, v_w_ret_out, v_w_dn_out, v_w_o, v_ffn2_wgu, v_ffn2_wd, v_final_norm_w):
    w = dict(c_ctx=c_ctx, ada_w=ada_w, ada_b=ada_b, norm_w=norm_w, ffn1_wgu=ffn1_wgu, ffn1_wd=ffn1_wd, w_in=w_in, dn_conv_w=dn_conv_w, dn_a_log=dn_a_log, dn_dt_bias=dn_dt_bias, dn_norm_w=dn_norm_w, w_ret_out=w_ret_out, w_dn_out=w_dn_out, w_o=w_o, ffn2_wgu=ffn2_wgu, ffn2_wd=ffn2_wd, final_norm_w=final_norm_w)
    m = dict(c_ctx=m_c_ctx, ada_w=m_ada_w, ada_b=m_ada_b, norm_w=m_norm_w, ffn1_wgu=m_ffn1_wgu, ffn1_wd=m_ffn1_wd, w_in=m_w_in, dn_conv_w=m_dn_conv_w, dn_a_log=m_dn_a_log, dn_dt_bias=m_dn_dt_bias, dn_norm_w=m_dn_norm_w, w_ret_out=m_w_ret_out, w_dn_out=m_w_dn_out, w_o=m_w_o, ffn2_wgu=m_ffn2_wgu, ffn2_wd=m_ffn2_wd, final_norm_w=m_final_norm_w)
    v = dict(c_ctx=v_c_ctx, ada_w=v_ada_w, ada_b=v_ada_b, norm_w=v_norm_w, ffn1_wgu=v_ffn1_wgu, ffn1_wd=v_ffn1_wd, w_in=v_w_in, dn_conv_w=v_dn_conv_w, dn_a_log=v_dn_a_log, dn_dt_bias=v_dn_dt_bias, dn_norm_w=v_dn_norm_w, w_ret_out=v_w_ret_out, w_dn_out=v_w_dn_out, w_o=v_w_o, ffn2_wgu=v_ffn2_wgu, ffn2_wd=v_ffn2_wd, final_norm_w=v_final_norm_w)
    me = _my_index()
    big_names = [n for n, _ in BIG]

    g_big = all_gather([w[n].astype(BF16) for n in big_names], "ag_weights")
    full = {n: _join_shards(s, kind) for (n, kind), s in zip(BIG, g_big)}
    small_shapes = [norm_w.shape, dn_conv_w.shape, c.shape]
    g_small = all_gather([_pack([norm_w, dn_conv_w, c], 8)], "ag_small")[0]
    norm_s, conv_s, c_all = _unpack(g_small, small_shapes, lead=N_DEV)
    norm_full, conv_full = _join_shards(norm_s, "col"), _join_shards(conv_s, "col")

    cc = jnp.concatenate([c_all.reshape(N_DEV, D), c_ctx[None], jnp.zeros((ADA_ROWS - N_DEV - 1, D), F32)])
    ada_b_cols = lax.dynamic_slice_in_dim(ada_b, me * ADA_COLS, ADA_COLS, axis=1)[:, None, :]
    mods_part = ada_fwd(cc, ada_w, ada_b_cols)
    g_mods = all_gather([mods_part], "ag_mods")[0]
    mods_all = g_mods.transpose(1, 2, 0, 3).reshape(DEPTH, ADA_ROWS, N_MOD * D)
    mx = lax.dynamic_index_in_dim(mods_all, me, axis=1, keepdims=False).reshape(DEPTH, N_MOD, D)
    mc = mods_all[:, N_DEV].reshape(DEPTH, N_MOD, D)

    gparams = jnp.pad(jnp.stack([-jnp.exp(dn_a_log).reshape(DEPTH, 8), dn_dt_bias.reshape(DEPTH, 8)], axis=1),
                      ((0, 0), (0, 6), (8, LANE - 16)))
    stacked = dict(
        wgu1=full["ffn1_wgu"], wd1=full["ffn1_wd"], w_in=_pad_w_in(full["w_in"]), w_r=full["w_ret_out"],
        w_d=full["w_dn_out"], w_o=full["w_o"], wgu2=full["ffn2_wgu"], wd2=full["ffn2_wd"], norm_w=norm_full,
        conv_w8=jnp.pad(conv_full, ((0, 0), (0, 8 - CONV_K), (0, 0))), gparams=gparams,
        dn_norm_t=jnp.tile(dn_norm_w, (1, NH)), mx=mx, mc=mc)
    lws = [{k: a[l] for k, a in stacked.items()} for l in range(DEPTH)]
    loss_l, gx, gws, smalls, d_fnw = local_step(x[0], ctx[0], loss_target[0], final_norm_w, lws)

    sm_list = [jnp.stack([s[k] for s in smalls]) for k in ("dmx", "dmc", "norm_w", "conv_w", "a_log", "dt_bias",
                                                            "dn_norm_w")] + [d_fnw, loss_l.reshape(1)]
    sm_shapes = [a.shape for a in sm_list]
    g_sm = all_gather([_pack(sm_list, 8)], "ag_small_grads")[0]
    (dmx_sum, dmc_sum, g_norm, g_conv, g_alog, g_dtb, g_dnw, g_fnw, loss) = _unpack(sum_slots(g_sm), sm_shapes)
    dmx_all = _unpack(g_sm, sm_shapes[:1], lead=N_DEV)[0].reshape(N_DEV, DEPTH, N_MOD * D)
    dmc_sum = dmc_sum.reshape(DEPTH, N_MOD * D)
    dmx_sum = dmx_sum.reshape(DEPTH, N_MOD * D)
    dmod = jnp.concatenate([
        lax.dynamic_slice_in_dim(dmx_all, me * ADA_COLS, ADA_COLS, axis=2).transpose(1, 0, 2),
        lax.dynamic_slice_in_dim(dmc_sum, me * ADA_COLS, ADA_COLS, axis=1)[:, None, :],
        jnp.zeros((DEPTH, ADA_ROWS - N_DEV - 1, ADA_COLS), F32)], axis=1)
    g_ada_w, d_sil = ada_bwd(cc, dmod, ada_w)
    g_cc = all_gather([d_sil[:, N_DEV].sum(axis=0).reshape(8, LANE)], "ag_c_ctx")[0]
    grads = dict(
        ada_w=g_ada_w, c_ctx=c_ctx_grad(g_cc, c_ctx.reshape(8, LANE)).reshape(D), ada_b=dmx_sum + dmc_sum,
        norm_w=lax.dynamic_slice_in_dim(g_norm, me * (D // N_DEV), D // N_DEV, axis=2),
        dn_conv_w=lax.dynamic_slice_in_dim(g_conv, me * (2 * D // N_DEV), 2 * D // N_DEV, axis=2),
        dn_a_log=g_alog.reshape(dn_a_log.shape), dn_dt_bias=g_dtb.reshape(dn_dt_bias.shape), dn_norm_w=g_dnw,
        final_norm_w=g_fnw)

    def layer_full(gw):
        return dict(ffn1_wgu=jnp.concatenate([gw["wg1"], gw["wu1"]], axis=-1), ffn1_wd=gw["wd1"],
                    w_in=_unpad_w_in(gw["w_in"]), w_ret_out=gw["w_r"], w_dn_out=gw["w_d"], w_o=gw["w_o"],
                    ffn2_wgu=jnp.concatenate([gw["wg2"], gw["wu2"]], axis=-1), ffn2_wd=gw["wd2"])

    full_g = [layer_full(gw) for gw in gws]
    parts = [jnp.stack([_split_shards(fg[n], kind) for fg in full_g], axis=1).astype(BF16) for n, kind in BIG]
    recv = all_to_all(parts, "a2a_grads")
    res = {}
    for n, r in zip(big_names, recv):
        rows_blk = 256 if w[n].shape[1] % 256 == 0 else w[n].shape[1]
        res[n] = adamw(r, w[n], m[n], v[n], rows_blk, "adamw_" + n)
    res["ada_w"] = adamw(g_ada_w[None], ada_w, m["ada_w"], v["ada_w"], 256, "adamw_ada_w")
    small_names = [n for n in LOCAL if n != "ada_w"]
    small_pack = lambda d: _pack([d[n] for n in small_names], 8)[None]
    res_small = adamw(small_pack(grads)[None], small_pack(w), small_pack(m), small_pack(v),
                      small_pack(w).shape[1], "adamw_small")
    unpacked = [_unpack(r, [w[n].shape for n in small_names]) for r in res_small]
    for i, n in enumerate(small_names):
        res[n] = tuple(u[i] for u in unpacked)
    outs = [[res[n][k] for n in WEIGHT_ORDER] for k in range(4)]
    return (loss.reshape(()), gx[None], *outs[0], *outs[1], *outs[2], *outs[3])
```

```python
import functools
import math

import jax
import jax.numpy as jnp
from jax import lax
from jax.experimental import pallas as pl
from jax.experimental.pallas import tpu as pltpu

F32 = jnp.float32
BF16 = jnp.bfloat16

D = 1024
NH = 4
DK = 128
DV = 256
RET_C = 128
DN_C = 64
FFN_H = 2816
FFN_HB = 1408
N_MOD = 9
DEPTH = 4
N_DEV = 8
EPS = 1e-6
CONV_K = 5
HALO = 8

PW = 8320
C_RQK, C_RV, C_RG, C_DQK, C_DVV, C_DZ, C_GA, C_GB = 0, 1, 2, 3, 4, 5, 6, 7
C_DBA = 64
PROJ_TN = 1664

LANE = 128
VMEM_LIMIT = 56 * 1024 * 1024

ADAM_LR, ADAM_B1, ADAM_B2, ADAM_EPS, ADAM_WD, ADAM_STEP = 0.001, 0.9, 0.999, 1e-08, 0.01, 10

NN = ((1,), (0,))
NT = ((1,), (1,))
TN = ((0,), (0,))
HI = lax.Precision.HIGHEST


def _params(n_grid):
    return pltpu.CompilerParams(dimension_semantics=("arbitrary",) * n_grid, vmem_limit_bytes=VMEM_LIMIT)


def _dot(a, b, dims):
    return lax.dot_general(a.astype(BF16), b.astype(BF16), (dims, ((), ())), preferred_element_type=F32)


def _dot_hi(a, b, dims=NN):
    return lax.dot_general(a, b, (dims, ((), ())), precision=HI, preferred_element_type=F32)


@jax.custom_vjp
def mm_nn(a, b):
    return _dot(a, b, NN)


mm_nn.defvjp(lambda a, b: (_dot(a, b, NN), (a, b)),
             lambda r, g: (_dot(g, r[1], NT), _dot(r[0], g, TN)))


@jax.custom_vjp
def mm_nt(a, b):
    return _dot(a, b, NT)


mm_nt.defvjp(lambda a, b: (_dot(a, b, NT), (a, b)),
             lambda r, g: (_dot(g, r[1], NN), _dot(g, r[0], TN)))


@jax.custom_vjp
def mm_tn(a, b):
    return _dot(a, b, TN)


mm_tn.defvjp(lambda a, b: (_dot(a, b, TN), (a, b)),
             lambda r, g: (_dot(r[1], g, NT), _dot(r[0], g, NN)))


def _split_bf16(x, n):
    parts = []
    for _ in range(n):
        p = x.astype(BF16)
        parts.append(p)
        x = x - p.astype(F32)
    return parts


def _dot_f32(a, b, dims, exact=None):
    mm = lambda p, q: lax.dot_general(p, q, (dims, ((), ())), preferred_element_type=F32)
    if exact == "a":
        ab = a.astype(BF16)
        b1, b2, b3 = _split_bf16(b, 3)
        return mm(ab, b1) + (mm(ab, b2) + mm(ab, b3))
    if exact == "b":
        bb = b.astype(BF16)
        a1, a2, a3 = _split_bf16(a, 3)
        return mm(a1, bb) + (mm(a2, bb) + mm(a3, bb))
    a1, a2 = _split_bf16(a, 2)
    b1, b2 = _split_bf16(b, 2)
    return mm(a1, b1) + (mm(a1, b2) + mm(a2, b1))


@jax.custom_vjp
def _cum_rows(mask, x):
    return _dot_f32(mask, x, NN, exact="a")


_cum_rows.defvjp(lambda mask, x: (_dot_f32(mask, x, NN, exact="a"), mask),
                 lambda mask, g: (jnp.zeros_like(mask), _dot_f32(mask, g, TN, exact="a")))


@jax.custom_vjp
def _row_bcast(xb):
    return _dot_f32(jnp.full(xb.shape, 1.0 / LANE, F32), xb, NT, exact="a")


_row_bcast.defvjp(lambda xb: (_row_bcast(xb), None),
                  lambda _, g: (_dot_f32(g, jnp.full((g.shape[0], LANE), 1.0 / LANE, F32), TN, exact="b"),))


def _tri_inv_fwd(a):
    c = a.shape[0]
    eye = (lax.broadcasted_iota(jnp.int32, (c, c), 0) == lax.broadcasted_iota(jnp.int32, (c, c), 1)).astype(F32)
    pw = -a
    tinv = eye + pw
    m = 2
    while m < DN_C:
        pw = _dot_f32(pw, pw, NN)
        tinv = tinv + _dot_f32(tinv, pw, NN)
        m *= 2
    return tinv


@jax.custom_vjp
def _tri_inv(a):
    return _tri_inv_fwd(a)


def _tri_inv_bwd(tinv, g):
    return (-_dot_f32(_dot_f32(tinv, g, TN), tinv, NT),)


_tri_inv.defvjp(lambda a: (lambda t: (t, t))(_tri_inv_fwd(a)), _tri_inv_bwd)


@jax.custom_vjp
def _tri_inv_known(a, tinv):
    del a
    return tinv


_tri_inv_known.defvjp(lambda a, tinv: (tinv, tinv),
                      lambda tinv, g: (_tri_inv_bwd(tinv, g)[0], jnp.zeros_like(tinv)))


def _normmod(x, nw, shift, scale):
    r = lax.rsqrt(jnp.mean(x * x, axis=-1, keepdims=True) + EPS)
    return (x * r * nw) * (1.0 + scale) + shift


def _sigmoid(x):
    return 0.5 * jnp.tanh(0.5 * x) + 0.5


def _softplus(x):
    return jnp.maximum(x, 0.0) + jnp.log(1.0 + jnp.exp(-jnp.abs(x)))


def _row_block(t_len):
    return 512 if t_len % 512 == 0 else 256


def ffn_fwd(x, vec, wgu, wd):
    t_len = x.shape[0]
    tm = _row_block(t_len)

    def body(x_ref, vec_ref, wg_ref, wu_ref, wd_ref, x1_ref, h_ref, a_ref, s_ref, m_ref, f_ref, hs, acc):
        j = pl.program_id(1)

        @pl.when(j == 0)
        def _():
            hb = _normmod(x_ref[...], vec_ref[0:1, :], vec_ref[1:2, :], vec_ref[2:3, :]).astype(BF16)
            hs[...] = hb
            h_ref[...] = hb
            acc[...] = jnp.zeros_like(acc)

        hb = hs[...]
        g = jnp.dot(hb, wg_ref[...], preferred_element_type=F32)
        u = jnp.dot(hb, wu_ref[...], preferred_element_type=F32)
        sg = _sigmoid(g)
        sl = g * sg
        act = (sl * u).astype(BF16)
        a_ref[...] = act
        s_ref[...] = sl.astype(BF16)
        m_ref[...] = (u * (sg + sl * (1.0 - sg))).astype(BF16)
        acc[...] += jnp.dot(act, wd_ref[...], preferred_element_type=F32)

        @pl.when(j == 1)
        def _():
            f = acc[...]
            f_ref[...] = f.astype(BF16)
            x1_ref[...] = x_ref[...] + 0.5 * vec_ref[3:4, :] * f

    return pl.pallas_call(
        body, name="ffn_fwd", grid=(t_len // tm, 2),
        in_specs=[
            pl.BlockSpec((tm, D), lambda i, j: (i, 0)),
            pl.BlockSpec((8, D), lambda i, j: (0, 0)),
            pl.BlockSpec((D, FFN_HB), lambda i, j: (0, j)),
            pl.BlockSpec((D, FFN_HB), lambda i, j: (0, j + 2)),
            pl.BlockSpec((FFN_HB, D), lambda i, j: (j, 0)),
        ],
        out_specs=[
            pl.BlockSpec((tm, D), lambda i, j: (i, 0)),
            pl.BlockSpec((tm, D), lambda i, j: (i, 0)),
            pl.BlockSpec((tm, FFN_HB), lambda i, j: (i, j)),
            pl.BlockSpec((tm, FFN_HB), lambda i, j: (i, j)),
            pl.BlockSpec((tm, FFN_HB), lambda i, j: (i, j)),
            pl.BlockSpec((tm, D), lambda i, j: (i, 0)),
        ],
        out_shape=[
            jax.ShapeDtypeStruct((t_len, D), F32),
            jax.ShapeDtypeStruct((t_len, D), BF16),
            jax.ShapeDtypeStruct((t_len, FFN_H), BF16),
            jax.ShapeDtypeStruct((t_len, FFN_H), BF16),
            jax.ShapeDtypeStruct((t_len, FFN_H), BF16),
            jax.ShapeDtypeStruct((t_len, D), BF16),
        ],
        scratch_shapes=[pltpu.VMEM((tm, D), BF16), pltpu.VMEM((tm, D), F32)],
        compiler_params=_params(2),
    )(x, vec, wgu, wgu, wd)


def ffn_bwd_act(dx1, vec, sl, dact_dg, f, wd):
    t_len = dx1.shape[0]
    tm = _row_block(t_len)

    def body(dx_ref, vec_ref, s_ref, m_ref, f_ref, wd_ref, dg_ref, du_ref, dy_ref, pg_ref, dys):
        i, j = pl.program_id(0), pl.program_id(1)

        @pl.when((i == 0) & (j == 0))
        def _():
            pg_ref[...] = jnp.zeros_like(pg_ref)

        @pl.when(j == 0)
        def _():
            dx = dx_ref[...]
            dyb = (0.5 * vec_ref[3:4, :] * dx).astype(BF16)
            dys[...] = dyb
            dy_ref[...] = dyb
            pg_ref[3:4, :] += jnp.sum(0.5 * dx * f_ref[...].astype(F32), axis=0, keepdims=True)

        dact = _dot(dys[...], wd_ref[...], NT)
        du_ref[...] = (dact * s_ref[...].astype(F32)).astype(BF16)
        dg_ref[...] = (dact * m_ref[...].astype(F32)).astype(BF16)

    return pl.pallas_call(
        body, name="ffn_bwd_act", grid=(t_len // tm, 2),
        in_specs=[
            pl.BlockSpec((tm, D), lambda i, j: (i, 0)),
            pl.BlockSpec((8, D), lambda i, j: (0, 0)),
            pl.BlockSpec((tm, FFN_HB), lambda i, j: (i, j)),
            pl.BlockSpec((tm, FFN_HB), lambda i, j: (i, j)),
            pl.BlockSpec((tm, D), lambda i, j: (i, 0)),
            pl.BlockSpec((FFN_HB, D), lambda i, j: (j, 0)),
        ],
        out_specs=[
            pl.BlockSpec((tm, FFN_HB), lambda i, j: (i, j)),
            pl.BlockSpec((tm, FFN_HB), lambda i, j: (i, j)),
            pl.BlockSpec((tm, D), lambda i, j: (i, 0)),
            pl.BlockSpec((8, D), lambda i, j: (0, 0)),
        ],
        out_shape=[
            jax.ShapeDtypeStruct((t_len, FFN_H), BF16),
            jax.ShapeDtypeStruct((t_len, FFN_H), BF16),
            jax.ShapeDtypeStruct((t_len, D), BF16),
            jax.ShapeDtypeStruct((8, D), F32),
        ],
        scratch_shapes=[pltpu.VMEM((tm, D), BF16)],
        compiler_params=_params(2),
    )(dx1, vec, sl, dact_dg, f, wd)


def nt_norm_bwd(dys, w, col_offsets, tk, n_steps, x_in, dres, vec, name):
    t_len = x_in.shape[0]
    tm = _row_block(t_len)
    n_seg = len(dys)

    def body(*refs):
        dy_refs = refs[:n_seg]
        w_refs = refs[n_seg:2 * n_seg]
        x_ref, dres_ref, vec_ref, dx_ref, pg_ref, acc = refs[2 * n_seg:]
        i, j = pl.program_id(0), pl.program_id(1)

        @pl.when((i == 0) & (j == 0))
        def _():
            pg_ref[...] = jnp.zeros_like(pg_ref)

        @pl.when(j == 0)
        def _():
            acc[...] = jnp.zeros_like(acc)

        part = _dot(dy_refs[0][...], w_refs[0][...], NT)
        for s in range(1, n_seg):
            part += _dot(dy_refs[s][...], w_refs[s][...], NT)
        acc[...] += part

        @pl.when(j == n_steps - 1)
        def _():
            _, vjp = jax.vjp(_normmod, x_ref[...], vec_ref[0:1, :], vec_ref[1:2, :], vec_ref[2:3, :])
            dxn, dnw, dsh, dsc = vjp(acc[...])
            dx_ref[...] = dres_ref[...] + dxn
            pg_ref[0:1, :] += dnw
            pg_ref[1:2, :] += dsh
            pg_ref[2:3, :] += dsc

    in_specs = [pl.BlockSpec((tm, tk), lambda i, j: (i, j)) for _ in range(n_seg)]
    in_specs += [pl.BlockSpec((D, tk), functools.partial(lambda i, j, off: (0, off + j), off=off))
                 for off in col_offsets]
    in_specs += [pl.BlockSpec((tm, D), lambda i, j: (i, 0)), pl.BlockSpec((tm, D), lambda i, j: (i, 0)),
                 pl.BlockSpec((8, D), lambda i, j: (0, 0))]
    return pl.pallas_call(
        body, name=name, grid=(t_len // tm, n_steps),
        in_specs=in_specs,
        out_specs=[pl.BlockSpec((tm, D), lambda i, j: (i, 0)), pl.BlockSpec((8, D), lambda i, j: (0, 0))],
        out_shape=[jax.ShapeDtypeStruct((t_len, D), F32), jax.ShapeDtypeStruct((8, D), F32)],
        scratch_shapes=[pltpu.VMEM((tm, D), F32)],
        compiler_params=_params(2),
    )(*dys, *([w] * n_seg), x_in, dres, vec)


def tn_matmul(a_x, b_x, a_c, b_c, tn, name):
    t_len, k_dim = a_x.shape
    n_dim = b_x.shape[1]
    t_ctx = a_c.shape[0]
    tt = 512
    n_t = t_len // tt

    def body(ax_ref, bx_ref, ac_ref, bc_ref, o_ref):
        t = pl.program_id(1)

        @pl.when(t == 0)
        def _():
            o_ref[...] = _dot(ac_ref[...], bc_ref[...], TN)

        o_ref[...] += _dot(ax_ref[...], bx_ref[...], TN)

    return pl.pallas_call(
        body, name=name, grid=(n_dim // tn, n_t),
        in_specs=[
            pl.BlockSpec((tt, k_dim), lambda n, t: (t, 0)),
            pl.BlockSpec((tt, tn), lambda n, t: (t, n)),
            pl.BlockSpec((t_ctx, k_dim), lambda n, t: (0, 0)),
            pl.BlockSpec((t_ctx, tn), lambda n, t: (0, n)),
        ],
        out_specs=pl.BlockSpec((k_dim, tn), lambda n, t: (0, n)),
        out_shape=jax.ShapeDtypeStruct((k_dim, n_dim), F32),
        compiler_params=_params(2),
    )(a_x, b_x, a_c, b_c)


N_SEG = 8


def proj_bwd_in(segs, dba, w, x_in, dres, vec):
    t_len = x_in.shape[0]
    tm = _row_block(t_len)

    def body(*refs):
        seg_refs = refs[:N_SEG]
        dba_ref, w_ref, wg_ref, x_ref, dres_ref, vec_ref, dx_ref, pg_ref, acc = refs[N_SEG:]
        i, j = pl.program_id(0), pl.program_id(1)

        @pl.when((i == 0) & (j == 0))
        def _():
            pg_ref[...] = jnp.zeros_like(pg_ref)

        @pl.when(j == 0)
        def _():
            acc[...] = _dot(dba_ref[...], wg_ref[...], NT)

        for s in range(N_SEG):
            @pl.when(j == s)
            def _():
                acc[...] += _dot(seg_refs[s][...], w_ref[...], NT)

        @pl.when(j == N_SEG - 1)
        def _():
            _, vjp = jax.vjp(_normmod, x_ref[...], vec_ref[0:1, :], vec_ref[1:2, :], vec_ref[2:3, :])
            dxn, dnw, dsh, dsc = vjp(acc[...])
            dx_ref[...] = dres_ref[...] + dxn
            pg_ref[0:1, :] += dnw
            pg_ref[1:2, :] += dsh
            pg_ref[2:3, :] += dsc

    row = lambda width: pl.BlockSpec((tm, width), lambda i, j: (i, 0))
    return pl.pallas_call(
        body, name="proj_bwd_in", grid=(t_len // tm, N_SEG),
        in_specs=[row(D)] * N_SEG + [row(LANE), pl.BlockSpec((D, D), lambda i, j: (0, j)),
                                     pl.BlockSpec((D, LANE), lambda i, j: (0, C_DBA)),
                                     row(D), row(D), pl.BlockSpec((8, D), lambda i, j: (0, 0))],
        out_specs=[row(D), pl.BlockSpec((8, D), lambda i, j: (0, 0))],
        out_shape=[jax.ShapeDtypeStruct((t_len, D), F32), jax.ShapeDtypeStruct((8, D), F32)],
        scratch_shapes=[pltpu.VMEM((tm, D), F32)],
        compiler_params=_params(2),
    )(*segs, dba, w, w, x_in, dres, vec)


def dw_in(a_x, segs_x, a_c, segs_c):
    t_len, k_dim = a_x.shape
    t_ctx = a_c.shape[0]
    tt = 512

    def body(*refs):
        ax_ref, ac_ref = refs[0], refs[1]
        bx_refs, bc_refs = refs[2:2 + N_SEG], refs[2 + N_SEG:2 + 2 * N_SEG]
        o_ref = refs[2 + 2 * N_SEG]
        n, t = pl.program_id(0), pl.program_id(1)
        for s in range(N_SEG):
            @pl.when(n == s)
            def _():
                @pl.when(t == 0)
                def _():
                    o_ref[...] = _dot(ac_ref[...], bc_refs[s][...], TN)

                o_ref[...] += _dot(ax_ref[...], bx_refs[s][...], TN)

    seg_x = [pl.BlockSpec((tt, D), functools.partial(lambda n, t, s: (jnp.where(n == s, t, 0), 0), s=s))
             for s in range(N_SEG)]
    seg_c = [pl.BlockSpec((t_ctx, D), lambda n, t: (0, 0)) for _ in range(N_SEG)]
    return pl.pallas_call(
        body, name="dw_in", grid=(N_SEG, t_len // tt),
        in_specs=[pl.BlockSpec((tt, k_dim), lambda n, t: (t, 0)), pl.BlockSpec((t_ctx, k_dim), lambda n, t: (0, 0))]
        + seg_x + seg_c,
        out_specs=pl.BlockSpec((k_dim, D), lambda n, t: (0, n)),
        out_shape=jax.ShapeDtypeStruct((k_dim, N_SEG * D), F32),
        compiler_params=_params(2),
    )(a_x, a_c, *segs_x, *segs_c)


def proj_fwd(x, vec, w_in_p):
    t_len = x.shape[0]
    tm = _row_block(t_len)

    def body(x_ref, vec_ref, w_ref, p_ref, h_ref, hs):
        @pl.when(pl.program_id(1) == 0)
        def _():
            hb = _normmod(x_ref[...], vec_ref[0:1, :], vec_ref[1:2, :], vec_ref[2:3, :]).astype(BF16)
            hs[...] = hb
            h_ref[...] = hb

        p_ref[...] = jnp.dot(hs[...], w_ref[...], preferred_element_type=F32)

    return pl.pallas_call(
        body, name="proj_fwd", grid=(t_len // tm, PW // PROJ_TN),
        in_specs=[
            pl.BlockSpec((tm, D), lambda i, j: (i, 0)),
            pl.BlockSpec((8, D), lambda i, j: (0, 0)),
            pl.BlockSpec((D, PROJ_TN), lambda i, j: (0, j)),
        ],
        out_specs=[pl.BlockSpec((tm, PROJ_TN), lambda i, j: (i, j)), pl.BlockSpec((tm, D), lambda i, j: (i, 0))],
        out_shape=[jax.ShapeDtypeStruct((t_len, PW), F32), jax.ShapeDtypeStruct((t_len, D), BF16)],
        scratch_shapes=[pltpu.VMEM((tm, D), BF16)],
        compiler_params=_params(2),
    )(x, vec, w_in_p)


def _shift_rows(e, s):
    n = e.shape[0]
    return pltpu.roll(e, (-s) % n, 0)


def _swap_halves(t):
    return pltpu.roll(t, DK // 2, 1)


def _halo_specs(tm, t_len, width, col, lead=False):
    per = tm // HALO
    last = t_len // HALO - 1
    if lead:
        return (pl.BlockSpec((2, HALO, width), lambda i: (0, jnp.maximum(i * per - 1, 0), col)),
                pl.BlockSpec((2, HALO, width), lambda i: (0, jnp.minimum((i + 1) * per, last), col)))
    return (pl.BlockSpec((HALO, width), lambda i: (jnp.maximum(i * per - 1, 0), col)),
            pl.BlockSpec((HALO, width), lambda i: (jnp.minimum((i + 1) * per, last), col)))


def _gate_cols(ba, gp_ref):
    lane = lax.broadcasted_iota(jnp.int32, ba.shape, 1)
    z = ba + gp_ref[1:2, :]
    return jnp.where(lane < 8, _sigmoid(ba), gp_ref[0:1, :] * _softplus(z))


def _conv_chunk(win, w_ref, c0, width):
    y = w_ref[0:1, c0:c0 + width] * _shift_rows(win, -2)
    for j in range(1, CONV_K):
        y += w_ref[j:j + 1, c0:c0 + width] * _shift_rows(win, j - 2)
    return y


def feat_fwd(p, cos2, sin2, conv_w8, gparams):
    t_len = p.shape[0]
    tm = _row_block(t_len)
    n_t = t_len // tm
    cw = 512

    def body(rqk_ref, dqk_ref, dqk_p, dqk_n, dvv_ref, dvv_p, dvv_n, dba_ref, cos_ref, sin_ref, cw_ref, gp_ref,
             o_rqk, o_dqkv, o_gbc):
        i = pl.program_id(0)
        cos, sin = cos_ref[...], sin_ref[...]
        for s in range(2 * NH):
            t = rqk_ref[:, s * DK:(s + 1) * DK]
            if s >= NH:
                t = t * (DK ** -0.5)
            o_rqk[:, s * DK:(s + 1) * DK] = t * cos + _swap_halves(t) * sin
        o_gbc[...] = _gate_cols(dba_ref[...], gp_ref)
        first, last = i == 0, i == n_t - 1
        for ci in range(4):
            src, sp, sn = (dqk_ref, dqk_p, dqk_n) if ci < 2 else (dvv_ref, dvv_p, dvv_n)
            c0 = (ci % 2) * cw
            win = jnp.concatenate([
                jnp.where(first, 0.0, sp[:, c0:c0 + cw]), src[:, c0:c0 + cw],
                jnp.where(last, 0.0, sn[:, c0:c0 + cw])], axis=0)
            y = _conv_chunk(win, cw_ref, ci * cw, cw)[HALO:HALO + tm]
            sv = y * _sigmoid(y)
            if ci < 2:
                scale = DK ** -0.5 if ci == 0 else 1.0
                for hh in range(NH):
                    sh = sv[:, hh * DK:(hh + 1) * DK]
                    nrm = lax.rsqrt(jnp.sum(sh * sh, axis=-1, keepdims=True) + EPS)
                    o_dqkv[:, ci * cw + hh * DK:ci * cw + (hh + 1) * DK] = sh * nrm * scale
            else:
                o_dqkv[:, ci * cw:(ci + 1) * cw] = sv

    hq = _halo_specs(tm, t_len, D, C_DQK)
    hv = _halo_specs(tm, t_len, D, C_DVV)
    return pl.pallas_call(
        body, name="feat_fwd", grid=(n_t,),
        in_specs=[
            pl.BlockSpec((tm, D), lambda i: (i, C_RQK)),
            pl.BlockSpec((tm, D), lambda i: (i, C_DQK)), hq[0], hq[1],
            pl.BlockSpec((tm, D), lambda i: (i, C_DVV)), hv[0], hv[1],
            pl.BlockSpec((tm, LANE), lambda i: (i, C_DBA)),
            pl.BlockSpec((tm, LANE), lambda i: (i, 0)),
            pl.BlockSpec((tm, LANE), lambda i: (i, 0)),
            pl.BlockSpec((8, 2 * D), lambda i: (0, 0)),
            pl.BlockSpec((8, LANE), lambda i: (0, 0)),
        ],
        out_specs=[pl.BlockSpec((tm, D), lambda i: (i, 0)), pl.BlockSpec((tm, 2 * D), lambda i: (i, 0)),
                   pl.BlockSpec((tm, LANE), lambda i: (i, 0))],
        out_shape=[jax.ShapeDtypeStruct((t_len, D), F32), jax.ShapeDtypeStruct((t_len, 2 * D), F32),
                   jax.ShapeDtypeStruct((t_len, LANE), F32)],
        compiler_params=_params(1),
    )(p, p, p, p, p, p, p, p, cos2, sin2, conv_w8, gparams)


def feat_bwd(p, cos2, sin2, conv_w8, gparams, d_rqk, d_rv, d_dqk, d_dvv, d_gbc):
    t_len = p.shape[0]
    tm = 256
    n_t = t_len // tm
    cw = 512

    def body(dqk_ref, dqk_p, dqk_n, dvv_ref, dvv_p, dvv_n, dba_ref, cos_ref, sin_ref, cw_ref, gp_ref,
             g_rqk, g_rv, g_dqk, g_dqk_p, g_dqk_n, g_dvv, g_dvv_p, g_dvv_n, g_gbc,
             o_rqk, o_rv, o_dqk, o_dvv, o_dba, o_cw, o_gp):
        i = pl.program_id(0)
        first, last = i == 0, i == n_t - 1

        @pl.when(first)
        def _():
            o_cw[...] = jnp.zeros_like(o_cw)
            o_gp[...] = jnp.zeros_like(o_gp)

        cos, sin = cos_ref[...], sin_ref[...]
        for s in range(2 * NH):
            gsl = g_rqk[0, :, s * DK:(s + 1) * DK] + g_rqk[1, :, s * DK:(s + 1) * DK]
            dt = gsl * cos + _swap_halves(gsl * sin)
            if s >= NH:
                dt = dt * (DK ** -0.5)
            o_rqk[:, s * DK:(s + 1) * DK] = dt.astype(BF16)
        o_rv[...] = (g_rv[0] + g_rv[1]).astype(BF16)

        ba = dba_ref[...]
        gg = g_gbc[0] + g_gbc[1]
        lane = lax.broadcasted_iota(jnp.int32, ba.shape, 1)
        sb = _sigmoid(ba)
        z = ba + gp_ref[1:2, :]
        a_row = gp_ref[0:1, :]
        dz = gg * a_row * _sigmoid(z)
        o_dba[...] = jnp.where(lane < 8, gg * sb * (1.0 - sb), dz).astype(BF16)
        is_g = (lane >= 8) & (lane < 16)
        o_gp[0:1, :] += jnp.sum(jnp.where(is_g, gg * a_row * _softplus(z), 0.0), axis=0, keepdims=True)
        o_gp[1:2, :] += jnp.sum(jnp.where(is_g, dz, 0.0), axis=0, keepdims=True)

        for ci in range(4):
            src, sp, sn = (dqk_ref, dqk_p, dqk_n) if ci < 2 else (dvv_ref, dvv_p, dvv_n)
            c0 = (ci % 2) * cw
            gc0 = ci * cw
            win = jnp.concatenate([
                jnp.where(first, 0.0, sp[:, c0:c0 + cw]), src[:, c0:c0 + cw],
                jnp.where(last, 0.0, sn[:, c0:c0 + cw])], axis=0)
            gs, gsp, gsn = (g_dqk, g_dqk_p, g_dqk_n) if ci < 2 else (g_dvv, g_dvv_p, g_dvv_n)
            gext = jnp.concatenate([
                jnp.where(first, 0.0, gsp[0, :, c0:c0 + cw] + gsp[1, :, c0:c0 + cw]),
                gs[0, :, c0:c0 + cw] + gs[1, :, c0:c0 + cw],
                jnp.where(last, 0.0, gsn[0, :, c0:c0 + cw] + gsn[1, :, c0:c0 + cw])], axis=0)
            y = _conv_chunk(win, cw_ref, gc0, cw)
            sg = _sigmoid(y)
            sv = y * sg
            if ci < 2:
                scale = DK ** -0.5 if ci == 0 else 1.0
                parts = []
                for hh in range(NH):
                    sh = sv[:, hh * DK:(hh + 1) * DK]
                    gh = gext[:, hh * DK:(hh + 1) * DK]
                    nrm = lax.rsqrt(jnp.sum(sh * sh, axis=-1, keepdims=True) + EPS)
                    dot = jnp.sum(gh * sh, axis=-1, keepdims=True)
                    parts.append(scale * nrm * (gh - sh * (nrm * nrm) * dot))
                ds = jnp.concatenate(parts, axis=1)
            else:
                ds = gext
            dy = ds * (sg + sv * (1.0 - sg))
            dpe = cw_ref[0:1, gc0:gc0 + cw] * _shift_rows(dy, 2)
            for j in range(1, CONV_K):
                dpe += cw_ref[j:j + 1, gc0:gc0 + cw] * _shift_rows(dy, 2 - j)
            dst = o_dqk if ci < 2 else o_dvv
            dst[:, c0:c0 + cw] = dpe[HALO:HALO + tm].astype(BF16)
            dyc = dy[HALO:HALO + tm]
            for j in range(CONV_K):
                o_cw[j:j + 1, gc0:gc0 + cw] += jnp.sum(dyc * _shift_rows(win, j - 2)[HALO:HALO + tm], axis=0,
                                                      keepdims=True)

    hq = _halo_specs(tm, t_len, D, C_DQK)
    hv = _halo_specs(tm, t_len, D, C_DVV)
    hg = _halo_specs(tm, t_len, D, 0, lead=True)
    outs = pl.pallas_call(
        body, name="feat_bwd", grid=(n_t,),
        in_specs=[
            pl.BlockSpec((tm, D), lambda i: (i, C_DQK)), hq[0], hq[1],
            pl.BlockSpec((tm, D), lambda i: (i, C_DVV)), hv[0], hv[1],
            pl.BlockSpec((tm, LANE), lambda i: (i, C_DBA)),
            pl.BlockSpec((tm, LANE), lambda i: (i, 0)),
            pl.BlockSpec((tm, LANE), lambda i: (i, 0)),
            pl.BlockSpec((8, 2 * D), lambda i: (0, 0)),
            pl.BlockSpec((8, LANE), lambda i: (0, 0)),
            pl.BlockSpec((2, tm, D), lambda i: (0, i, 0)),
            pl.BlockSpec((2, tm, D), lambda i: (0, i, 0)),
            pl.BlockSpec((2, tm, D), lambda i: (0, i, 0)), hg[0], hg[1],
            pl.BlockSpec((2, tm, D), lambda i: (0, i, 0)), hg[0], hg[1],
            pl.BlockSpec((2, tm, LANE), lambda i: (0, i, 0)),
        ],
        out_specs=[pl.BlockSpec((tm, D), lambda i: (i, 0))] * 4
        + [pl.BlockSpec((tm, LANE), lambda i: (i, 0)),
           pl.BlockSpec((8, 2 * D), lambda i: (0, 0)),
           pl.BlockSpec((8, LANE), lambda i: (0, 0))],
        out_shape=[jax.ShapeDtypeStruct((t_len, D), BF16)] * 4
        + [jax.ShapeDtypeStruct((t_len, LANE), BF16),
           jax.ShapeDtypeStruct((8, 2 * D), F32), jax.ShapeDtypeStruct((8, LANE), F32)],
        compiler_params=_params(1),
    )(p, p, p, p, p, p, p, cos2, sin2, conv_w8, gparams, d_rqk, d_rv, d_dqk, d_dqk, d_dqk, d_dvv, d_dvv, d_dvv,
      d_gbc)
    return outs


def _ret_chunk(q, k, v, s, lg, rev):
    c = q.shape[0]
    ii = lax.broadcasted_iota(jnp.int32, (c, c), 0).astype(F32)
    jj = lax.broadcasted_iota(jnp.int32, (c, c), 1).astype(F32)
    diff = jnp.where(rev, jj - ii, ii - jj)
    mask = diff >= jnp.where(rev, 1.0, 0.0)
    dec =jnp.where(mask, jnp.exp(lg * jnp.where(mask, diff, 0.0)), 0.0)
    idx = lax.broadcasted_iota(jnp.int32, (c, 1), 0).astype(F32)
    idx = jnp.where(rev, c - 1.0 - idx, idx)
    kdec = k * jnp.exp(lg * (c - 1.0 - idx))
    qdec = q * jnp.exp(lg * (idx + 1.0))
    o = mm_nn(mm_nt(q, k) * dec, v) + mm_nn(qdec, s)
    s2 = s * math.exp(lg * c) + mm_tn(kdec, v)
    return o, s2


def _log_gamma(h):
    return math.log1p(-(2.0 ** (-5.0 - h)))


DN_W = 3 * NH * DK
HS = NH * DN_C


def _dn_intra4(q4, k4, v4, g4, beta4, rev, tinv_known=None):
    ii = lax.broadcasted_iota(jnp.int32, (HS, HS), 0)
    jj = lax.broadcasted_iota(jnp.int32, (HS, HS), 1)
    blk = (ii // DN_C) == (jj // DN_C)
    dd = jnp.where(rev, jj - ii, ii - jj)
    incl = blk & (dd >= 0)
    strict = blk & (dd > 0)
    gb = jnp.broadcast_to(g4, (HS, LANE))
    gcb = _cum_rows(incl.astype(F32), gb)
    gc = jnp.max(gcb, axis=1, keepdims=True)
    gcr = _row_bcast(gcb)
    at_end = blk & ((jj % DN_C) == jnp.where(rev, 0, DN_C - 1))
    glast = jnp.sum(jnp.where(at_end, gcr, 0.0), axis=1, keepdims=True)
    decay = jnp.where(incl, jnp.exp(jnp.where(incl, gc - gcr, 0.0)), 0.0)
    kb = k4 * beta4
    a = jnp.where(strict, mm_nt(kb, k4) * decay, 0.0)
    tinv = _tri_inv(a) if tinv_known is None else _tri_inv_known(a, tinv_known)
    u = mm_nn(tinv, v4 * beta4)
    w = mm_nn(tinv, kb * jnp.exp(gc))
    attn = jnp.where(incl, mm_nt(q4, k4) * decay, 0.0)
    return (u, w, q4 * jnp.exp(gc), k4 * jnp.exp(glast - gc), attn), tinv


def _dn_seq4(u, w, qg, kd, attn, glast, s):
    v_new = [uh - mm_nn(wh, sh) for uh, wh, sh in zip(u, w, s)]
    s2 = [sh * jnp.exp(gh) + mm_tn(kh, vh) for sh, gh, kh, vh in zip(s, glast, kd, v_new)]
    o = jnp.concatenate([mm_nn(qh, sh) for qh, sh in zip(qg, s)], axis=0) + mm_nn(attn, jnp.concatenate(v_new, axis=0))
    return o, s2


def _stack_heads(ref_rows, width, col0=0):
    return jnp.concatenate([ref_rows(slice(col0 + h * width, col0 + (h + 1) * width)) for h in range(NH)], axis=0)


def _head_rows(h):
    return slice(h * DN_C, (h + 1) * DN_C)


def _dn_gates4(gbv, d):
    g4 = jnp.concatenate([_pick_lane(gbv, 8 + 4 * d + h) for h in range(NH)], axis=0)
    b4 = jnp.concatenate([_pick_lane(gbv, 4 * d + h) for h in range(NH)], axis=0)
    return g4, b4


def dn_intra_fwd(dqkv, gbc):
    t_len = dqkv.shape[0]
    rb = 256
    n_g = rb // DN_C

    def body(qk_ref, v_ref, gb_ref, u_ref, wqk_ref, at_ref, ti_ref):
        d = pl.program_id(0)
        rev = d == 1
        for gi in range(n_g):
            rows = slice(gi * DN_C, (gi + 1) * DN_C)
            g4, b4 = _dn_gates4(gb_ref[rows, :], d)
            (u, w, qg, kd, attn), tinv = _dn_intra4(_stack_heads(lambda cs: qk_ref[rows, cs], DK),
                                                    _stack_heads(lambda cs: qk_ref[rows, cs], DK, NH * DK),
                                                    _stack_heads(lambda cs: v_ref[rows, cs], DV), g4, b4, rev)
            at_ref[0, gi] = attn.astype(BF16)
            ti_ref[0, gi] = tinv
            for h in range(NH):
                hr = _head_rows(h)
                u_ref[0, rows, h * DV:(h + 1) * DV] = u[hr]
                wqk_ref[0, rows, h * DK:(h + 1) * DK] = w[hr].astype(BF16)
                wqk_ref[0, rows, NH * DK + h * DK:NH * DK + (h + 1) * DK] = qg[hr].astype(BF16)
                wqk_ref[0, rows, 2 * NH * DK + h * DK:2 * NH * DK + (h + 1) * DK] = kd[hr].astype(BF16)

    return pl.pallas_call(
        body, name="dn_intra_fwd", grid=(2, t_len // rb),
        in_specs=[pl.BlockSpec((rb, D), lambda d, i: (i, 0)), pl.BlockSpec((rb, D), lambda d, i: (i, 1)),
                  pl.BlockSpec((rb, LANE), lambda d, i: (i, 0))],
        out_specs=[pl.BlockSpec((1, rb, D), lambda d, i: (d, i, 0)),
                   pl.BlockSpec((1, rb, DN_W), lambda d, i: (d, i, 0)),
                   pl.BlockSpec((1, n_g, HS, HS), lambda d, i: (d, i, 0, 0)),
                   pl.BlockSpec((1, n_g, HS, HS), lambda d, i: (d, i, 0, 0))],
        out_shape=[jax.ShapeDtypeStruct((2, t_len, D), F32), jax.ShapeDtypeStruct((2, t_len, DN_W), BF16),
                   jax.ShapeDtypeStruct((2, t_len // DN_C, HS, HS), BF16),
                   jax.ShapeDtypeStruct((2, t_len // DN_C, HS, HS), F32)],
        compiler_params=_params(2),
    )(dqkv, dqkv, gbc)


SEQ_G = 4
RET_G = 2


def _dn_seq_inputs(u_ref, wqk_ref, at_ref, gb_ref, d, gidx):
    rows = pl.ds(pl.multiple_of(gidx * DN_C, DN_C), DN_C)
    u = [u_ref[0, rows, h * DV:(h + 1) * DV] for h in range(NH)]
    w = [wqk_ref[0, rows, h * DK:(h + 1) * DK].astype(F32) for h in range(NH)]
    qg = [wqk_ref[0, rows, NH * DK + h * DK:NH * DK + (h + 1) * DK].astype(F32) for h in range(NH)]
    kd = [wqk_ref[0, rows, 2 * NH * DK + h * DK:2 * NH * DK + (h + 1) * DK].astype(F32) for h in range(NH)]
    gbv = gb_ref[rows, :]
    glast = [jnp.sum(_pick_lane(gbv, 8 + 4 * d + h), axis=0, keepdims=True) for h in range(NH)]
    return u, w, qg, kd, at_ref[0, gidx].astype(F32), glast


def dn_seq_fwd(u, wqk, attn, gbc, s0):
    t_len = u.shape[1]
    n_b = t_len // (DN_C * SEQ_G)
    rb = DN_C * SEQ_G

    def block_of(d, t):
        return jnp.where(d == 0, t, n_b - 1 - t)

    def body(u_ref, wqk_ref, at_ref, gb_ref, s0_ref, o_ref, ss_ref, sf_ref, st):
        d, t = pl.program_id(0), pl.program_id(1)

        @pl.when(t == 0)
        def _():
            st[...] = s0_ref[0]

        for gi in range(SEQ_G):
            gidx = jnp.where(d == 0, gi, SEQ_G - 1 - gi)
            rows = pl.ds(pl.multiple_of(gidx * DN_C, DN_C), DN_C)
            s_in = [st[h] for h in range(NH)]
            ss_ref[0, gidx] = st[...].astype(BF16)
            o, s2 = _dn_seq4(*_dn_seq_inputs(u_ref, wqk_ref, at_ref, gb_ref, d, gidx), s_in)
            for h in range(NH):
                o_ref[0, rows, h * DV:(h + 1) * DV] = o[_head_rows(h)]
                st[h] = s2[h]

        @pl.when(t == n_b - 1)
        def _():
            sf_ref[0] = st[...]

    return pl.pallas_call(
        body, name="dn_seq_fwd", grid=(2, n_b),
        in_specs=[pl.BlockSpec((1, rb, D), lambda d, t: (d, block_of(d, t), 0)),
                  pl.BlockSpec((1, rb, DN_W), lambda d, t: (d, block_of(d, t), 0)),
                  pl.BlockSpec((1, SEQ_G, HS, HS), lambda d, t: (d, block_of(d, t), 0, 0)),
                  pl.BlockSpec((rb, LANE), lambda d, t: (block_of(d, t), 0)),
                  _state_spec()],
        out_specs=[pl.BlockSpec((1, rb, D), lambda d, t: (d, block_of(d, t), 0)),
                   pl.BlockSpec((1, SEQ_G, NH, DK, DV), lambda d, t: (d, block_of(d, t), 0, 0, 0)),
                   _state_spec()],
        out_shape=[jax.ShapeDtypeStruct((2, t_len, D), F32),
                   jax.ShapeDtypeStruct((2, t_len // DN_C, NH, DK, DV), BF16),
                   jax.ShapeDtypeStruct((2, NH, DK, DV), F32)],
        scratch_shapes=[pltpu.VMEM((NH, DK, DV), F32)],
        compiler_params=_params(2),
    )(u, wqk, attn, gbc, s0)


def dn_seq_bwd(u, wqk, attn, gbc, ssave, do, ds_fin):
    t_len = u.shape[1]
    n_c = t_len // DN_C
    n_b = n_c // SEQ_G
    rb = DN_C * SEQ_G

    def block_of(d, t):
        return jnp.where(d == 0, n_b - 1 - t, t)

    def body(u_ref, wqk_ref, at_ref, gb_ref, ss_ref, do_ref, dsf_ref, du_ref, dwqk_ref, dat_ref, dgl_ref, ds0_ref, dst):
        d, t = pl.program_id(0), pl.program_id(1)

        @pl.when(t == 0)
        def _():
            dst[...] = dsf_ref[0]

        rows8 = lax.broadcasted_iota(jnp.int32, (8, LANE), 0)
        for gi in range(SEQ_G):
            gidx = jnp.where(d == 0, SEQ_G - 1 - gi, gi)
            rows = pl.ds(pl.multiple_of(gidx * DN_C, DN_C), DN_C)
            s_in = [ss_ref[0, gidx, h].astype(F32) for h in range(NH)]
            _, vjp = jax.vjp(_dn_seq4, *_dn_seq_inputs(u_ref, wqk_ref, at_ref, gb_ref, d, gidx), s_in)
            do4 = _stack_heads(lambda cs: do_ref[rows, cs], DV)
            du, dw, dqg, dkd, dat, dgl, ds = vjp((do4, [dst[h] for h in range(NH)]))
            dat_ref[0, gidx] = dat
            dgl_tile = jnp.zeros((8, LANE), F32)
            for h in range(NH):
                du_ref[0, rows, h * DV:(h + 1) * DV] = du[h]
                dwqk_ref[0, rows, h * DK:(h + 1) * DK] = dw[h]
                dwqk_ref[0, rows, NH * DK + h * DK:NH * DK + (h + 1) * DK] = dqg[h]
                dwqk_ref[0, rows, 2 * NH * DK + h * DK:2 * NH * DK + (h + 1) * DK] = dkd[h]
                dgl_tile = jnp.where(rows8 == h, dgl[h], dgl_tile)
                dst[h] = ds[h]
            dgl_ref[0, gidx] = dgl_tile

        @pl.when(t == n_b - 1)
        def _():
            ds0_ref[0] = dst[...]

    seq = lambda width: pl.BlockSpec((1, rb, width), lambda d, t: (d, block_of(d, t), 0))
    att = pl.BlockSpec((1, SEQ_G, HS, HS), lambda d, t: (d, block_of(d, t), 0, 0))
    return pl.pallas_call(
        body, name="dn_seq_bwd", grid=(2, n_b),
        in_specs=[seq(D), seq(DN_W), att, pl.BlockSpec((rb, LANE), lambda d, t: (block_of(d, t), 0)),
                  pl.BlockSpec((1, SEQ_G, NH, DK, DV), lambda d, t: (d, block_of(d, t), 0, 0, 0)),
                  pl.BlockSpec((rb, D), lambda d, t: (block_of(d, t), 0)), _state_spec()],
        out_specs=[seq(D), seq(DN_W), att, pl.BlockSpec((1, SEQ_G, 8, LANE), lambda d, t: (d, block_of(d, t), 0, 0)),
                   _state_spec()],
        out_shape=[jax.ShapeDtypeStruct((2, t_len, D), F32), jax.ShapeDtypeStruct((2, t_len, DN_W), F32),
                   jax.ShapeDtypeStruct((2, n_c, HS, HS), F32), jax.ShapeDtypeStruct((2, n_c, 8, LANE), F32),
                   jax.ShapeDtypeStruct((2, NH, DK, DV), F32)],
        scratch_shapes=[pltpu.VMEM((NH, DK, DV), F32)],
        compiler_params=_params(2),
    )(u, wqk, attn, gbc, ssave, do, ds_fin)


def dn_intra_bwd(dqkv, gbc, tinv, du, dwqk, dattn, dgl):
    t_len = dqkv.shape[0]
    rb = 128
    n_g = rb // DN_C

    def body(qk_ref, v_ref, gb_ref, ti_ref, du_ref, dwqk_ref, dat_ref, dgl_ref, dqk_ref, dv_ref, dgb_ref):
        d = pl.program_id(0)
        rev = d == 1
        for gi in range(n_g):
            rows = slice(gi * DN_C, (gi + 1) * DN_C)
            g4, b4 = _dn_gates4(gb_ref[rows, :], d)
            _, vjp, _ = jax.vjp(functools.partial(_dn_intra4, rev=rev, tinv_known=ti_ref[0, gi]),
                                _stack_heads(lambda cs: qk_ref[rows, cs], DK),
                                _stack_heads(lambda cs: qk_ref[rows, cs], DK, NH * DK),
                                _stack_heads(lambda cs: v_ref[rows, cs], DV), g4, b4, has_aux=True)
            dq, dk, dv, dg, db = vjp((_stack_heads(lambda cs: du_ref[0, rows, cs], DV),
                                      _stack_heads(lambda cs: dwqk_ref[0, rows, cs], DK),
                                      _stack_heads(lambda cs: dwqk_ref[0, rows, cs], DK, NH * DK),
                                      _stack_heads(lambda cs: dwqk_ref[0, rows, cs], DK, 2 * NH * DK),
                                      dat_ref[0, gi]))
            dgb = jnp.zeros((DN_C, LANE), F32)
            for h in range(NH):
                hr = _head_rows(h)
                dqk_ref[0, rows, h * DK:(h + 1) * DK] = dq[hr]
                dqk_ref[0, rows, NH * DK + h * DK:NH * DK + (h + 1) * DK] = dk[hr]
                dv_ref[0, rows, h * DV:(h + 1) * DV] = dv[hr]
                dgb = (dgb + _put_lane(dg[hr] + dgl_ref[0, gi, h:h + 1, 0:1], 8 + 4 * d + h, LANE)
                       + _put_lane(db[hr], 4 * d + h, LANE))
            dgb_ref[0, rows, :] = dgb

    seq = lambda width: pl.BlockSpec((1, rb, width), lambda d, i: (d, i, 0))
    return pl.pallas_call(
        body, name="dn_intra_bwd", grid=(2, t_len // rb),
        in_specs=[pl.BlockSpec((rb, D), lambda d, i: (i, 0)), pl.BlockSpec((rb, D), lambda d, i: (i, 1)),
                  pl.BlockSpec((rb, LANE), lambda d, i: (i, 0)),
                  pl.BlockSpec((1, n_g, HS, HS), lambda d, i: (d, i, 0, 0)), seq(D), seq(DN_W),
                  pl.BlockSpec((1, n_g, HS, HS), lambda d, i: (d, i, 0, 0)),
                  pl.BlockSpec((1, n_g, 8, LANE), lambda d, i: (d, i, 0, 0))],
        out_specs=[seq(D), seq(D), seq(LANE)],
        out_shape=[jax.ShapeDtypeStruct((2, t_len, D), F32), jax.ShapeDtypeStruct((2, t_len, D), F32),
                   jax.ShapeDtypeStruct((2, t_len, LANE), F32)],
        compiler_params=_params(2),
    )(dqkv, dqkv, gbc, tinv, du, dwqk, dattn, dgl)


def _pick_lane(x, lane_idx):
    lane = lax.broadcasted_iota(jnp.int32, x.shape, 1)
    return jnp.sum(jnp.where(lane == lane_idx, x, 0.0), axis=1, keepdims=True)


def _put_lane(col, lane_idx, width):
    lane = lax.broadcasted_iota(jnp.int32, (col.shape[0], width), 1)
    return jnp.where(lane == lane_idx, col, 0.0)


def _state_spec():
    return pl.BlockSpec((1, NH, DK, DV), lambda d, t: (d, 0, 0, 0))


def ret_fwd(rqk, p, s0):
    c = RET_C
    t_len = rqk.shape[0]
    n_c = t_len // c
    n_b = n_c // RET_G
    rb = c * RET_G

    def block_of(d, t):
        return jnp.where(d == 0, t, n_b - 1 - t)

    def body(qk_ref, v_ref, s0_ref, o_ref, ss_ref, sf_ref, st):
        d, t = pl.program_id(0), pl.program_id(1)
        rev = d == 1

        @pl.when(t == 0)
        def _():
            st[...] = s0_ref[0]

        for gi in range(RET_G):
            gidx = jnp.where(d == 0, gi, RET_G - 1 - gi)
            rows = pl.ds(pl.multiple_of(gidx * c, c), c)
            ss_ref[0, gidx] = st[...].astype(BF16)
            for h in range(NH):
                o, s2 = _ret_chunk(qk_ref[rows, h * DK:(h + 1) * DK],
                                   qk_ref[rows, NH * DK + h * DK:NH * DK + (h + 1) * DK],
                                   v_ref[rows, h * DV:(h + 1) * DV], st[h], _log_gamma(h), rev)
                o_ref[0, rows, h * DV:(h + 1) * DV] = o
                st[h] = s2

        @pl.when(t == n_b - 1)
        def _():
            sf_ref[0] = st[...]

    return pl.pallas_call(
        body, name="ret_fwd", grid=(2, n_b),
        in_specs=[pl.BlockSpec((rb, D), lambda d, t: (block_of(d, t), 0)),
                  pl.BlockSpec((rb, D), lambda d, t: (block_of(d, t), C_RV)),
                  _state_spec()],
        out_specs=[pl.BlockSpec((1, rb, D), lambda d, t: (d, block_of(d, t), 0)),
                   pl.BlockSpec((1, RET_G, NH, DK, DV), lambda d, t: (d, block_of(d, t), 0, 0, 0)),
                   _state_spec()],
        out_shape=[jax.ShapeDtypeStruct((2, t_len, D), F32),
                   jax.ShapeDtypeStruct((2, n_c, NH, DK, DV), BF16),
                   jax.ShapeDtypeStruct((2, NH, DK, DV), F32)],
        scratch_shapes=[pltpu.VMEM((NH, DK, DV), F32)],
        compiler_params=_params(2),
    )(rqk, p, s0)


def ret_bwd(rqk, p, ssave, do, ds_fin):
    c = RET_C
    t_len = rqk.shape[0]
    n_b = t_len // (c * RET_G)
    rb = c * RET_G

    def block_of(d, t):
        return jnp.where(d == 0, n_b - 1 - t, t)

    def body(qk_ref, v_ref, ss_ref, do_ref, dsf_ref, dqk_ref, dv_ref, ds0_ref, dst):
        d, t = pl.program_id(0), pl.program_id(1)
        rev = d == 1

        @pl.when(t == 0)
        def _():
            dst[...] = dsf_ref[0]

        for gi in range(RET_G):
            gidx = jnp.where(d == 0, RET_G - 1 - gi, gi)
            rows = pl.ds(pl.multiple_of(gidx * c, c), c)
            for h in range(NH):
                _, vjp = jax.vjp(functools.partial(_ret_chunk, lg=_log_gamma(h), rev=rev),
                                 qk_ref[rows, h * DK:(h + 1) * DK],
                                 qk_ref[rows, NH * DK + h * DK:NH * DK + (h + 1) * DK],
                                 v_ref[rows, h * DV:(h + 1) * DV], ss_ref[0, gidx, h].astype(F32))
                dq, dk, dv, ds = vjp((do_ref[rows, h * DV:(h + 1) * DV], dst[h]))
                dqk_ref[0, rows, h * DK:(h + 1) * DK] = dq
                dqk_ref[0, rows, NH * DK + h * DK:NH * DK + (h + 1) * DK] = dk
                dv_ref[0, rows, h * DV:(h + 1) * DV] = dv
                dst[h] = ds

        @pl.when(t == n_b - 1)
        def _():
            ds0_ref[0] = dst[...]

    seq_spec = pl.BlockSpec((1, rb, D), lambda d, t: (d, block_of(d, t), 0))
    return pl.pallas_call(
        body, name="ret_bwd", grid=(2, n_b),
        in_specs=[pl.BlockSpec((rb, D), lambda d, t: (block_of(d, t), 0)),
                  pl.BlockSpec((rb, D), lambda d, t: (block_of(d, t), C_RV)),
                  pl.BlockSpec((1, RET_G, NH, DK, DV), lambda d, t: (d, block_of(d, t), 0, 0, 0)),
                  pl.BlockSpec((rb, D), lambda d, t: (block_of(d, t), 0)),
                  _state_spec()],
        out_specs=[seq_spec, seq_spec, _state_spec()],
        out_shape=[jax.ShapeDtypeStruct((2, t_len, D), F32)] * 2 + [jax.ShapeDtypeStruct((2, NH, DK, DV), F32)],
        scratch_shapes=[pltpu.VMEM((NH, DK, DV), F32)],
        compiler_params=_params(2),
    )(rqk, p, ssave, do, ds_fin)


def _head_gate(o_ret, o_dn, rg, dz, nw):
    ret = o_ret * lax.rsqrt(jnp.mean(o_ret * o_ret, axis=-1, keepdims=True) + EPS) * (rg * _sigmoid(rg))
    dn = o_dn * lax.rsqrt(jnp.mean(o_dn * o_dn, axis=-1, keepdims=True) + EPS) * nw * (dz * _sigmoid(dz))
    return ret, dn


MIX_TM = 256


def _mix_specs():
    seq = lambda col: pl.BlockSpec((MIX_TM, D), functools.partial(lambda i, col: (i, col), col=col))
    pair = pl.BlockSpec((2, MIX_TM, D), lambda i: (0, i, 0))
    wfull = pl.BlockSpec((D, D), lambda i: (0, 0))
    vec = pl.BlockSpec((8, D), lambda i: (0, 0))
    return seq, pair, wfull, vec


def mixout_fwd(x1, vec, o_ret, o_dn, p, w_r, w_d, w_o):
    t_len = x1.shape[0]

    def body(x_ref, vec_ref, or_ref, od_ref, rg_ref, dz_ref, ga_ref, gb_ref, wr_ref, wd_ref, wo_ref,
             x2_ref, ret_ref, dn_ref, y_ref, yr_ref, yd_ref, z_ref):
        for h in range(NH):
            sl = slice(h * DV, (h + 1) * DV)
            ret, dn = _head_gate(or_ref[0, :, sl] + or_ref[1, :, sl], od_ref[0, :, sl] + od_ref[1, :, sl],
                                 rg_ref[:, sl], dz_ref[:, sl], vec_ref[1:2, sl])
            ret_ref[:, sl] = ret.astype(BF16)
            dn_ref[:, sl] = dn.astype(BF16)
        yr = jnp.dot(ret_ref[...], wr_ref[...], preferred_element_type=F32)
        yd = jnp.dot(dn_ref[...], wd_ref[...], preferred_element_type=F32)
        yr_ref[...] = yr.astype(BF16)
        yd_ref[...] = yd.astype(BF16)
        y = (_sigmoid(ga_ref[...]) * yr + _sigmoid(gb_ref[...]) * yd).astype(BF16)
        y_ref[...] = y
        z = jnp.dot(y, wo_ref[...], preferred_element_type=F32)
        z_ref[...] = z.astype(BF16)
        x2_ref[...] = x_ref[...] + vec_ref[0:1, :] * z

    seq, pair, wfull, vecs = _mix_specs()
    return pl.pallas_call(
        body, name="mixout_fwd", grid=(t_len // MIX_TM,),
        in_specs=[seq(0), vecs, pair, pair, seq(C_RG), seq(C_DZ), seq(C_GA), seq(C_GB), wfull, wfull, wfull],
        out_specs=[seq(0)] * 7,
        out_shape=[jax.ShapeDtypeStruct((t_len, D), F32)] + [jax.ShapeDtypeStruct((t_len, D), BF16)] * 6,
        compiler_params=_params(1),
    )(x1, vec, o_ret, o_dn, p, p, p, p, w_r, w_d, w_o)


def mixout_bwd(dx2, vec, o_ret, o_dn, p, yr, yd, z, w_r, w_d, w_o):
    t_len = dx2.shape[0]

    def body(dx_ref, vec_ref, or_ref, od_ref, rg_ref, dz_ref, ga_ref, gb_ref, yr_ref, yd_ref, z_ref,
             wr_ref, wd_ref, wo_ref,
             dor_ref, dod_ref, drg_ref, ddz_ref, dga_ref, dgb_ref, dyr_ref, dyd_ref, dzz_ref, pg_ref):
        @pl.when(pl.program_id(0) == 0)
        def _():
            pg_ref[...] = jnp.zeros_like(pg_ref)

        dx = dx_ref[...]
        pg_ref[0:1, :] += jnp.sum(dx * z_ref[...].astype(F32), axis=0, keepdims=True)
        dzz = (vec_ref[0:1, :] * dx).astype(BF16)
        dzz_ref[...] = dzz
        dy = _dot(dzz, wo_ref[...], NT)
        sa = _sigmoid(ga_ref[...])
        sb = _sigmoid(gb_ref[...])
        dyr = (dy * sa).astype(BF16)
        dyd = (dy * sb).astype(BF16)
        dyr_ref[...] = dyr
        dyd_ref[...] = dyd
        dga_ref[...] = (dy * yr_ref[...].astype(F32) * sa * (1.0 - sa)).astype(BF16)
        dgb_ref[...] = (dy * yd_ref[...].astype(F32) * sb * (1.0 - sb)).astype(BF16)
        dret = _dot(dyr, wr_ref[...], NT)
        ddn = _dot(dyd, wd_ref[...], NT)
        for h in range(NH):
            sl = slice(h * DV, (h + 1) * DV)
            _, vjp = jax.vjp(_head_gate, or_ref[0, :, sl] + or_ref[1, :, sl], od_ref[0, :, sl] + od_ref[1, :, sl],
                             rg_ref[:, sl], dz_ref[:, sl], vec_ref[1:2, sl])
            d_or, d_od, d_rg, d_dz, d_nw = vjp((dret[:, sl], ddn[:, sl]))
            dor_ref[:, sl] = d_or
            dod_ref[:, sl] = d_od
            drg_ref[:, sl] = d_rg.astype(BF16)
            ddz_ref[:, sl] = d_dz.astype(BF16)
            pg_ref[1:2, sl] += d_nw

    seq, pair, wfull, vecs = _mix_specs()
    return pl.pallas_call(
        body, name="mixout_bwd", grid=(t_len // MIX_TM,),
        in_specs=[seq(0), vecs, pair, pair, seq(C_RG), seq(C_DZ), seq(C_GA), seq(C_GB), seq(0), seq(0), seq(0),
                  wfull, wfull, wfull],
        out_specs=[seq(0)] * 9 + [vecs],
        out_shape=[jax.ShapeDtypeStruct((t_len, D), F32)] * 2 + [jax.ShapeDtypeStruct((t_len, D), BF16)] * 7
        + [jax.ShapeDtypeStruct((8, D), F32)],
        compiler_params=_params(1),
    )(dx2, vec, o_ret, o_dn, p, p, p, p, yr, yd, z, w_r, w_d, w_o)


def _final_loss(x, w, target):
    y = x * lax.rsqrt(jnp.mean(x * x, axis=-1, keepdims=True) + EPS) * w
    err = y - target
    return 0.5 * jnp.sum(jnp.mean(err * err, axis=-1, keepdims=True), axis=0, keepdims=True)


def final_fwd_bwd(x, vec, target):
    t_len = x.shape[0]
    tm = _row_block(t_len)

    def body(x_ref, vec_ref, t_ref, dx_ref, pg_ref, loss_ref):
        @pl.when(pl.program_id(0) == 0)
        def _():
            pg_ref[...] = jnp.zeros_like(pg_ref)
            loss_ref[...] = jnp.zeros_like(loss_ref)

        loss, vjp = jax.vjp(functools.partial(_final_loss, target=t_ref[...]), x_ref[...], vec_ref[0:1, :])
        dx, dw = vjp(jnp.ones((1, 1), F32))
        dx_ref[...] = dx
        pg_ref[0:1, :] += dw
        loss_ref[...] += jnp.broadcast_to(loss, loss_ref.shape)

    return pl.pallas_call(
        body, name="final_fwd_bwd", grid=(t_len // tm,),
        in_specs=[pl.BlockSpec((tm, D), lambda i: (i, 0)), pl.BlockSpec((8, D), lambda i: (0, 0)),
                  pl.BlockSpec((tm, D), lambda i: (i, 0))],
        out_specs=[pl.BlockSpec((tm, D), lambda i: (i, 0)), pl.BlockSpec((8, D), lambda i: (0, 0)),
                   pl.BlockSpec((8, LANE), lambda i: (0, 0))],
        out_shape=[jax.ShapeDtypeStruct((t_len, D), F32), jax.ShapeDtypeStruct((8, D), F32),
                   jax.ShapeDtypeStruct((8, LANE), F32)],
        compiler_params=_params(1),
    )(x, vec, target)


def _vec8(*rows):
    rows = list(rows) + [jnp.zeros((D,), F32)] * (8 - len(rows))
    return jnp.stack(rows)


def _layer_vecs(lw, m):
    nw = lw["norm_w"]
    return dict(ffn1=_vec8(nw[0], m[0], m[1], m[2]), proj=_vec8(nw[1], m[3], m[4]),
                mix=_vec8(m[5], lw["dn_norm_t"]), ffn2=_vec8(nw[2], m[6], m[7], m[8]))


def _rope_tables(t_len, grid_w=64, base=10000.0):
    n_freq = DK // 4
    inv = base ** (-jnp.arange(n_freq, dtype=F32) / n_freq)
    tok = jnp.arange(t_len)
    ang = jnp.concatenate([(tok // grid_w).astype(F32)[:, None] * inv, (tok % grid_w).astype(F32)[:, None] * inv],
                          axis=-1)
    cos, sin = jnp.cos(ang), jnp.sin(ang)
    return jnp.concatenate([cos, cos], axis=-1), jnp.concatenate([-sin, sin], axis=-1)


def _stream_fwd_a(x0, lw, vecs, rope):
    x1, h1, act1, sl1, mg1, f1 = ffn_fwd(x0, vecs["ffn1"], lw["wgu1"], lw["wd1"])
    p, h2 = proj_fwd(x1, vecs["proj"], lw["w_in"])
    rqk, dqkv, gbc = feat_fwd(p, rope[0], rope[1], lw["conv_w8"], lw["gparams"])
    return dict(x0=x0, h1=h1, act1=act1, sl1=sl1, mg1=mg1, f1=f1, x1=x1, h2=h2, p=p, rqk=rqk, dqkv=dqkv, gbc=gbc)


def _stream_mix(sv, s0_ret, s0_dn):
    sv["o_ret"], sv["ss_ret"], sf_ret = ret_fwd(sv["rqk"], sv["p"], s0_ret)
    sv["dn_u"], sv["dn_wqk"], sv["dn_attn"], sv["dn_tinv"] = dn_intra_fwd(sv["dqkv"], sv["gbc"])
    sv["o_dn"], sv["ss_dn"], sf_dn = dn_seq_fwd(sv["dn_u"], sv["dn_wqk"], sv["dn_attn"], sv["gbc"], s0_dn)
    return sf_ret, sf_dn


def _stream_fwd_b(sv, lw, vecs):
    x2, ret, dn, y, yr, yd, z = mixout_fwd(sv["x1"], vecs["mix"], sv["o_ret"], sv["o_dn"], sv["p"],
                                           lw["w_r"], lw["w_d"], lw["w_o"])
    x3, h3, act3, sl3, mg3, f3 = ffn_fwd(x2, vecs["ffn2"], lw["wgu2"], lw["wd2"])
    sv.update(ret=ret, dn=dn, y=y, yr=yr, yd=yd, z=z, x2=x2, h3=h3, act3=act3, sl3=sl3, mg3=mg3, f3=f3)
    return x3


def layer_fwd(xs, cs, lw, ropes):
    vx, vc = _layer_vecs(lw, lw["mx"]), _layer_vecs(lw, lw["mc"])
    sx = _stream_fwd_a(xs, lw, vx, ropes[0])
    sc = _stream_fwd_a(cs, lw, vc, ropes[1])
    zero = jnp.zeros((2, NH, DK, DV), F32)
    sf_ret, sf_dn = _stream_mix(sc, zero, zero)
    _stream_mix(sx, sf_ret, sf_dn)
    x3 = _stream_fwd_b(sx, lw, vx)
    c3 = _stream_fwd_b(sc, lw, vc)
    return x3, c3, (sx, sc)


def _stream_bwd_a(dx3, sv, lw, vecs):
    dg3, du3, dy3, pg_a = ffn_bwd_act(dx3, vecs["ffn2"], sv["sl3"], sv["mg3"], sv["f3"], lw["wd2"])
    dx2, pg_b = nt_norm_bwd([dg3, du3], lw["wgu2"], [0, 2], FFN_HB, 2, sv["x2"], dx3, vecs["ffn2"], "ffn_bwd_in")
    (dor, dod, drg, ddz, dga, dgb, dyr, dyd, dzz, pg_m) = mixout_bwd(
        dx2, vecs["mix"], sv["o_ret"], sv["o_dn"], sv["p"], sv["yr"], sv["yd"], sv["z"], lw["w_r"], lw["w_d"], lw["w_o"])
    return dict(dg3=dg3, du3=du3, dy3=dy3, pg_a2=pg_a, pg_b2=pg_b, dx2=dx2, dor=dor, dod=dod,
                drg=drg, ddz=ddz, dga=dga, dgb=dgb, dyr=dyr, dyd=dyd, dzz=dzz, pg_m=pg_m)


def _stream_bwd_mix(bw, sv, dsf_ret, dsf_dn):
    bw["dqk_r"], bw["drv"], ds0_ret = ret_bwd(sv["rqk"], sv["p"], sv["ss_ret"], bw["dor"], dsf_ret)
    du, dwqk, dattn, dgl, ds0_dn = dn_seq_bwd(sv["dn_u"], sv["dn_wqk"], sv["dn_attn"], sv["gbc"], sv["ss_dn"],
                                              bw["dod"], dsf_dn)
    bw["dqk_d"], bw["dvv_d"], bw["dgbc"] = dn_intra_bwd(sv["dqkv"], sv["gbc"], sv["dn_tinv"], du, dwqk, dattn, dgl)
    return ds0_ret, ds0_dn


def _stream_bwd_b(bw, sv, lw, vecs, rope):
    dp_rqk, dp_rv, dp_dqk, dp_dvv, dp_dba, bw["dcw"], bw["dgp"] = feat_bwd(
        sv["p"], rope[0], rope[1], lw["conv_w8"], lw["gparams"], bw["dqk_r"], bw["drv"], bw["dqk_d"], bw["dvv_d"],
        bw["dgbc"])
    bw["dp_segs"] = [dp_rqk, dp_rv, bw["drg"], dp_dqk, dp_dvv, bw["ddz"], bw["dga"], bw["dgb"]]
    bw["dp_dba"] = dp_dba
    dx1, bw["pg_p"] = proj_bwd_in(bw["dp_segs"], dp_dba, lw["w_in"], sv["x1"], bw["dx2"], vecs["proj"])
    bw["dg1"], bw["du1"], bw["dy1"], bw["pg_a1"] = ffn_bwd_act(dx1, vecs["ffn1"], sv["sl1"], sv["mg1"], sv["f1"],
                                                              lw["wd1"])
    dx0, bw["pg_b1"] = nt_norm_bwd([bw["dg1"], bw["du1"]], lw["wgu1"], [0, 2], FFN_HB, 2, sv["x0"], dx1,
                                   vecs["ffn1"], "ffn_bwd_in")
    return dx0


def _stream_pgrads(bw):
    a1, b1, pp, pm, a2, b2 = bw["pg_a1"], bw["pg_b1"], bw["pg_p"], bw["pg_m"], bw["pg_a2"], bw["pg_b2"]
    dmod = jnp.stack([b1[1], b1[2], a1[3], pp[1], pp[2], pm[0], b2[1], b2[2], a2[3]])
    dnorm = jnp.stack([b1[0], pp[0], b2[0]])
    ddnw = pm[1].reshape(NH, DV).sum(axis=0)
    return dmod, dnorm, ddnw, bw["dcw"][:CONV_K], bw["dgp"][0, 8:16], bw["dgp"][1, 8:16]


def layer_bwd(dx3, dc3, lw, saved, ropes):
    sx, sc = saved
    vx, vc = _layer_vecs(lw, lw["mx"]), _layer_vecs(lw, lw["mc"])
    bx = _stream_bwd_a(dx3, sx, lw, vx)
    bc = _stream_bwd_a(dc3, sc, lw, vc)
    zero = jnp.zeros((2, NH, DK, DV), F32)
    ds0_ret, ds0_dn = _stream_bwd_mix(bx, sx, zero, zero)
    _stream_bwd_mix(bc, sc, ds0_ret, ds0_dn)
    dx0 = _stream_bwd_b(bx, sx, lw, vx, ropes[0])
    dc0 = _stream_bwd_b(bc, sc, lw, vc, ropes[1])

    def wgrad(a, b, tn, name):
        ax = sx[a] if a in sx else bx[a]
        ac = sc[a] if a in sc else bc[a]
        return tn_matmul(ax, bx[b], ac, bc[b], tn, name)

    gw = dict(
        wg1=wgrad("h1", "dg1", FFN_HB, "dw_ffn_gu"), wu1=wgrad("h1", "du1", FFN_HB, "dw_ffn_gu"),
        wd1=wgrad("act1", "dy1", 512, "dw_ffn_d"),
        w_in=jnp.concatenate([dw_in(sx["h2"], bx["dp_segs"], sc["h2"], bc["dp_segs"]),
                              wgrad("h2", "dp_dba", LANE, "dw_in_gate")], axis=-1),
        w_r=wgrad("ret", "dyr", 512, "dw_sq"), w_d=wgrad("dn", "dyd", 512, "dw_sq"), w_o=wgrad("y", "dzz", 512, "dw_sq"),
        wg2=wgrad("h3", "dg3", FFN_HB, "dw_ffn_gu"), wu2=wgrad("h3", "du3", FFN_HB, "dw_ffn_gu"),
        wd2=wgrad("act3", "dy3", 512, "dw_ffn_d"),
    )
    px, pc = _stream_pgrads(bx), _stream_pgrads(bc)
    small = dict(dmx=px[0], dmc=pc[0], norm_w=px[1] + pc[1], dn_norm_w=px[2] + pc[2], conv_w=px[3] + pc[3],
                 a_log=px[4] + pc[4], dt_bias=px[5] + pc[5])
    return dx0, dc0, gw, small


def local_step(x, ctx, target, final_norm_w, lws):
    t_len, t_ctx = x.shape[0], ctx.shape[0]
    ropes = (_rope_tables(t_len), (jnp.ones((t_ctx, LANE), F32), jnp.zeros((t_ctx, LANE), F32)))
    xs, cs, saved = x, ctx, []
    for lw in lws:
        xs, cs, sv = layer_fwd(xs, cs, lw, ropes)
        saved.append(sv)
    dx, pg_f, loss = final_fwd_bwd(xs, _vec8(final_norm_w), target)
    dc = jnp.zeros_like(ctx)
    gws, smalls = [None] * len(lws), [None] * len(lws)
    for l in reversed(range(len(lws))):
        dx, dc, gws[l], smalls[l] = layer_bwd(dx, dc, lws[l], saved[l], ropes)
    return loss[0, 0], dx, gws, smalls, pg_f[0]


ADA_ROWS = 16
ADA_COLS = N_MOD * D // N_DEV


def ada_fwd(cc, ada_w, ada_b_cols):
    def body(cc_ref, w_ref, b_ref, o_ref):
        cv = cc_ref[...]
        o_ref[0] = _dot_hi(cv * _sigmoid(cv), w_ref[0]) + b_ref[0]

    return pl.pallas_call(
        body, name="ada_fwd", grid=(DEPTH,),
        in_specs=[pl.BlockSpec((ADA_ROWS, D), lambda l: (0, 0)), pl.BlockSpec((1, D, ADA_COLS), lambda l: (l, 0, 0)),
                  pl.BlockSpec((1, 1, ADA_COLS), lambda l: (l, 0, 0))],
        out_specs=pl.BlockSpec((1, ADA_ROWS, ADA_COLS), lambda l: (l, 0, 0)),
        out_shape=jax.ShapeDtypeStruct((DEPTH, ADA_ROWS, ADA_COLS), F32),
        compiler_params=_params(1),
    )(cc, ada_w, ada_b_cols)


def ada_bwd(cc, dmod, ada_w):
    def body(cc_ref, d_ref, w_ref, gw_ref, ds_ref):
        cv = cc_ref[...]
        gw_ref[0] = _dot_hi(cv * _sigmoid(cv), d_ref[0], TN)
        ds_ref[0] = _dot_hi(d_ref[0], w_ref[0], NT)

    return pl.pallas_call(
        body, name="ada_bwd", grid=(DEPTH,),
        in_specs=[pl.BlockSpec((ADA_ROWS, D), lambda l: (0, 0)),
                  pl.BlockSpec((1, ADA_ROWS, ADA_COLS), lambda l: (l, 0, 0)),
                  pl.BlockSpec((1, D, ADA_COLS), lambda l: (l, 0, 0))],
        out_specs=[pl.BlockSpec((1, D, ADA_COLS), lambda l: (l, 0, 0)), pl.BlockSpec((1, ADA_ROWS, D), lambda l: (l, 0, 0))],
        out_shape=[jax.ShapeDtypeStruct((DEPTH, D, ADA_COLS), F32), jax.ShapeDtypeStruct((DEPTH, ADA_ROWS, D), F32)],
        compiler_params=_params(1),
    )(cc, dmod, ada_w)


def c_ctx_grad(parts, c_ctx8):
    def body(p_ref, c_ref, o_ref):
        acc = p_ref[0]
        for j in range(1, N_DEV):
            acc = acc + p_ref[j]
        cv = c_ref[...]
        sg = _sigmoid(cv)
        o_ref[...] = acc * (sg + cv * sg * (1.0 - sg))

    return pl.pallas_call(body, name="c_ctx_grad", out_shape=jax.ShapeDtypeStruct((8, LANE), F32))(parts, c_ctx8)


def sum_slots(parts):
    n_slot, rows, _ = parts.shape
    tr = 8
    for cand in (1024, 512, 256, 128, 64, 32, 16, 8):
        if rows % cand == 0:
            tr = cand
            break

    def body(p_ref, o_ref):
        acc = p_ref[0]
        for j in range(1, n_slot):
            acc = acc + p_ref[j]
        o_ref[...] = acc

    return pl.pallas_call(
        body, name="sum_slots", grid=(rows // tr,),
        in_specs=[pl.BlockSpec((n_slot, tr, LANE), lambda i: (0, i, 0))],
        out_specs=pl.BlockSpec((tr, LANE), lambda i: (i, 0)),
        out_shape=jax.ShapeDtypeStruct((rows, LANE), F32),
        compiler_params=_params(1),
    )(parts)


def adamw(parts, w, m, v, rows_blk, name):
    n_slot, n_l, n_a, n_b = parts.shape

    def body(p_ref, w_ref, m_ref, v_ref, g_ref, d_ref, m2_ref, v2_ref):
        g = p_ref[0].astype(F32)
        for j in range(1, n_slot):
            g = g + p_ref[j].astype(F32)
        m2 = ADAM_B1 * m_ref[...] + (1.0 - ADAM_B1) * g
        v2 = ADAM_B2 * v_ref[...] + (1.0 - ADAM_B2) * (g * g)
        m_hat = m2 / (1.0 - ADAM_B1 ** ADAM_STEP)
        v_hat = v2 / (1.0 - ADAM_B2 ** ADAM_STEP)
        g_ref[...] = g
        m2_ref[...] = m2
        v2_ref[...] = v2
        d_ref[...] = -ADAM_LR * (m_hat / (jnp.sqrt(v_hat) + ADAM_EPS) + ADAM_WD * w_ref[...])

    blk = pl.BlockSpec((1, rows_blk, n_b), lambda l, i: (l, i, 0))
    return pl.pallas_call(
        body, name=name, grid=(n_l, n_a // rows_blk),
        in_specs=[pl.BlockSpec((n_slot, 1, rows_blk, n_b), lambda l, i: (0, l, i, 0)), blk, blk, blk],
        out_specs=[blk] * 4,
        out_shape=[jax.ShapeDtypeStruct((n_l, n_a, n_b), F32)] * 4,
        compiler_params=_params(2),
    )(parts, w, m, v)


MESH = pl.DeviceIdType.MESH


def _my_index():
    return 4 * lax.axis_index("x") + 2 * lax.axis_index("y") + lax.axis_index("c")


def all_gather(shards, name):
    n = len(shards)

    def body(*refs):
        x_refs, out_refs = refs[:n], refs[n:2 * n]
        send_sems, recv_sems, local_sems = refs[2 * n:]
        x, y, c = lax.axis_index("x"), lax.axis_index("y"), lax.axis_index("c")
        me, sibling = (x, y, c), (x, y, 1 - c)
        chips = [(1 - x, y), (x, 1 - y), (1 - x, 1 - y)]

        def slot(a, px, py, pc):
            return out_refs[a].at[4 * px + 2 * py + pc]

        def copy(a, k, block, to, src=None):
            return pltpu.make_async_remote_copy(
                src_ref=slot(a, *block) if src is None else src, dst_ref=slot(a, *block),
                send_sem=send_sems.at[7 * a + k], recv_sem=recv_sems.at[7 * a + k], device_id=to, device_id_type=MESH)

        mine = [pltpu.make_async_copy(x_refs[a], slot(a, *me), local_sems.at[a]) for a in range(n)]
        for cp in mine:
            cp.start()
        first = []
        for a in range(n):
            first.append(copy(a, 0, me, sibling, src=x_refs[a]))
            first += [copy(a, 1 + j, me, (*chip, c), src=x_refs[a]) for j, chip in enumerate(chips)]
        for cp in first:
            cp.start()
        passed = []
        for j, chip in enumerate(chips):
            for a in range(n):
                copy(a, 1 + j, (*chip, c), me).wait_recv()
                fwd = copy(a, 4 + j, (*chip, c), sibling)
                fwd.start()
                passed.append(fwd)
        for a in range(n):
            copy(a, 0, sibling, me).wait_recv()
            for j, chip in enumerate(chips):
                copy(a, 4 + j, (*chip, 1 - c), me).wait_recv()
        for cp in first + passed:
            cp.wait_send()
        for cp in mine:
            cp.wait()

    return pl.pallas_call(
        body, name=name,
        in_specs=[pl.BlockSpec(memory_space=pl.ANY)] * n,
        out_specs=[pl.BlockSpec(memory_space=pl.ANY)] * n,
        out_shape=[jax.ShapeDtypeStruct((N_DEV,) + s.shape, s.dtype) for s in shards],
        scratch_shapes=[pltpu.SemaphoreType.DMA((7 * n,)), pltpu.SemaphoreType.DMA((7 * n,)),
                        pltpu.SemaphoreType.DMA((n,))],
    )(*shards)


def all_to_all(parts, name):
    n = len(parts)

    def body(*refs):
        p_refs, q_refs = refs[:n], refs[n:2 * n]
        send_sems, recv_sems, local_sems = refs[2 * n:]
        x, y, c = lax.axis_index("x"), lax.axis_index("y"), lax.axis_index("c")
        me = 4 * x + 2 * y + c
        mine = [pltpu.make_async_copy(p_refs[a].at[me], q_refs[a].at[me], local_sems.at[a]) for a in range(n)]
        for cp in mine:
            cp.start()
        copies = []
        for r in range(1, N_DEV):
            px, py, pc = x ^ (r >> 2), y ^ ((r >> 1) & 1), c ^ (r & 1)
            peer = 4 * px + 2 * py + pc
            for a in range(n):
                k = 7 * a + r - 1
                cp = pltpu.make_async_remote_copy(
                    src_ref=p_refs[a].at[peer], dst_ref=q_refs[a].at[me], send_sem=send_sems.at[k],
                    recv_sem=recv_sems.at[k], device_id=(px, py, pc), device_id_type=MESH)
                cp.start()
                copies.append((cp, a, peer, k))
        for cp, a, peer, k in copies:
            pltpu.make_async_remote_copy(
                src_ref=p_refs[a].at[peer], dst_ref=q_refs[a].at[peer], send_sem=send_sems.at[k],
                recv_sem=recv_sems.at[k], device_id=(x, y, c), device_id_type=MESH).wait_recv()
        for cp, _, _, _ in copies:
            cp.wait_send()
        for cp in mine:
            cp.wait()

    return pl.pallas_call(
        body, name=name,
        in_specs=[pl.BlockSpec(memory_space=pl.ANY)] * n,
        out_specs=[pl.BlockSpec(memory_space=pl.ANY)] * n,
        out_shape=[jax.ShapeDtypeStruct(p.shape, p.dtype) for p in parts],
        scratch_shapes=[pltpu.SemaphoreType.DMA((7 * n,)), pltpu.SemaphoreType.DMA((7 * n,)),
                        pltpu.SemaphoreType.DMA((n,))],
    )(*parts)


WEIGHT_ORDER = ("c_ctx", "ada_w", "ada_b", "norm_w", "ffn1_wgu", "ffn1_wd", "w_in", "dn_conv_w", "dn_a_log",
                "dn_dt_bias", "dn_norm_w", "w_ret_out", "w_dn_out", "w_o", "ffn2_wgu", "ffn2_wd", "final_norm_w")
BIG = (("ffn1_wgu", "col"), ("ffn1_wd", "row"), ("w_in", "col"), ("w_ret_out", "row"), ("w_dn_out", "row"),
       ("w_o", "row"), ("ffn2_wgu", "col"), ("ffn2_wd", "row"))
LOCAL = ("ada_w", "c_ctx", "ada_b", "norm_w", "dn_conv_w", "dn_a_log", "dn_dt_bias", "dn_norm_w", "final_norm_w")


def _pack(arrs, row_mult, lead=None):
    if lead is None:
        flat = jnp.concatenate([a.reshape(-1) for a in arrs])
        n = flat.shape[0]
    else:
        flat = jnp.concatenate([a.reshape(lead, -1) for a in arrs], axis=1)
        n = flat.shape[1]
    unit = LANE * row_mult
    total = -(-n // unit) * unit
    if lead is None:
        return jnp.pad(flat, (0, total - n)).reshape(total // LANE, LANE)
    return jnp.pad(flat, ((0, 0), (0, total - n))).reshape(lead, total // LANE, LANE)


def _unpack(packed, shapes, lead=None):
    flat = packed.reshape(-1) if lead is None else packed.reshape(lead, -1)
    out, off = [], 0
    for shp in shapes:
        n = math.prod(shp)
        if lead is None:
            out.append(flat[off:off + n].reshape(shp))
        else:
            out.append(flat[:, off:off + n].reshape((lead,) + tuple(shp)))
        off += n
    return out


def _join_shards(g, kind):
    lead = tuple(range(1, g.ndim - 2))
    a, b = g.shape[-2:]
    if kind == "col":
        return g.transpose(*lead, g.ndim - 2, 0, g.ndim - 1).reshape(g.shape[1:-2] + (a, N_DEV * b))
    return g.transpose(*lead, 0, g.ndim - 2, g.ndim - 1).reshape(g.shape[1:-2] + (N_DEV * a, b))


def _split_shards(full, kind):
    a, b = full.shape
    if kind == "col":
        return full.reshape(a, N_DEV, b // N_DEV).transpose(1, 0, 2)
    return full.reshape(N_DEV, a // N_DEV, b)


def _pad_w_in(w):
    return jnp.concatenate([w[..., :6144], w[..., 6160:8208], w[..., 6144:6160],
                            jnp.zeros(w.shape[:-1] + (PW - 8208,), w.dtype)], axis=-1)


def _unpad_w_in(g):
    return jnp.concatenate([g[..., :6144], g[..., 8192:8208], g[..., 6144:8192]], axis=-1)


def kernel(x, c, ctx, c_ctx, ada_w, ada_b, norm_w, ffn1_wgu, ffn1_wd, w_in, dn_conv_w, dn_a_log, dn_dt_bias, dn_norm_w, w_ret_out, w_dn_out, w_o, ffn2_wgu, ffn2_wd, final_norm_w, loss_target, m_c_ctx, m_ada_w, m_ada_b, m_norm_w, m_ffn1_wgu, m_ffn1_wd, m_w_in, m_dn_conv_w, m_dn_a_log, m_dn_dt_bias, m_dn_norm_w, m_w_ret_out, m_w_dn_out, m_w_o, m_ffn2_wgu, m_ffn2_wd, m_final_norm_w, v_c_ctx, v_ada_w, v_ada_b, v_norm_w, v_ffn1_wgu, v_ffn1_wd, v_w_in, v_dn_conv_w, v_dn_a_log, v_dn_dt_bias, v_dn_norm_w, v_w_ret_out, v_w_dn_out, v_w_o, v_ffn2_wgu, v_ffn2_wd, v_final_norm_w):
    w = dict(c_ctx=c_ctx, ada_w=ada_w, ada_b=ada_b, norm_w=norm_w, ffn1_wgu=ffn1_wgu, ffn1_wd=ffn1_wd, w_in=w_in, dn_conv_w=dn_conv_w, dn_a_log=dn_a_log, dn_dt_bias=dn_dt_bias, dn_norm_w=dn_norm_w, w_ret_out=w_ret_out, w_dn_out=w_dn_out, w_o=w_o, ffn2_wgu=ffn2_wgu, ffn2_wd=ffn2_wd, final_norm_w=final_norm_w)
    m = dict(c_ctx=m_c_ctx, ada_w=m_ada_w, ada_b=m_ada_b, norm_w=m_norm_w, ffn1_wgu=m_ffn1_wgu, ffn1_wd=m_ffn1_wd, w_in=m_w_in, dn_conv_w=m_dn_conv_w, dn_a_log=m_dn_a_log, dn_dt_bias=m_dn_dt_bias, dn_norm_w=m_dn_norm_w, w_ret_out=m_w_ret_out, w_dn_out=m_w_dn_out, w_o=m_w_o, ffn2_wgu=m_ffn2_wgu, ffn2_wd=m_ffn2_wd, final_norm_w=m_final_norm_w)
    v = dict(c_ctx=v_c_ctx, ada_w=v_ada_w, ada_b=v_ada_b, norm_w=v_norm_w, ffn1_wgu=v_ffn1_wgu, ffn1_wd=v_ffn1_wd, w_in=v_w_in, dn_conv_w=v_dn_conv_w, dn_a_log=v_dn_a_log, dn_dt_bias=v_dn_dt_bias, dn_norm_w=v_dn_norm_w, w_ret_out=v_w_ret_out, w_dn_out=v_w_dn_out, w_o=v_w_o, ffn2_wgu=v_ffn2_wgu, ffn2_wd=v_ffn2_wd, final_norm_w=v_final_norm_w)
    me = _my_index()
    big_names = [n for n, _ in BIG]

    g_big = all_gather([w[n].astype(BF16) for n in big_names], "ag_weights")
    full = {n: _join_shards(s, kind) for (n, kind), s in zip(BIG, g_big)}
    small_shapes = [norm_w.shape, dn_conv_w.shape, c.shape]
    g_small = all_gather([_pack([norm_w, dn_conv_w, c], 8)], "ag_small")[0]
    norm_s, conv_s, c_all = _unpack(g_small, small_shapes, lead=N_DEV)
    norm_full, conv_full = _join_shards(norm_s, "col"), _join_shards(conv_s, "col")

    cc = jnp.concatenate([c_all.reshape(N_DEV, D), c_ctx[None], jnp.zeros((ADA_ROWS - N_DEV - 1, D), F32)])
    ada_b_cols = lax.dynamic_slice_in_dim(ada_b, me * ADA_COLS, ADA_COLS, axis=1)[:, None, :]
    mods_part = ada_fwd(cc, ada_w, ada_b_cols)
    g_mods = all_gather([mods_part], "ag_mods")[0]
    mods_all = g_mods.transpose(1, 2, 0, 3).reshape(DEPTH, ADA_ROWS, N_MOD * D)
    mx = lax.dynamic_index_in_dim(mods_all, me, axis=1, keepdims=False).reshape(DEPTH, N_MOD, D)
    mc = mods_all[:, N_DEV].reshape(DEPTH, N_MOD, D)

    gparams = jnp.pad(jnp.stack([-jnp.exp(dn_a_log).reshape(DEPTH, 8), dn_dt_bias.reshape(DEPTH, 8)], axis=1),
                      ((0, 0), (0, 6), (8, LANE - 16)))
    stacked = dict(
        wgu1=full["ffn1_wgu"], wd1=full["ffn1_wd"], w_in=_pad_w_in(full["w_in"]), w_r=full["w_ret_out"],
        w_d=full["w_dn_out"], w_o=full["w_o"], wgu2=full["ffn2_wgu"], wd2=full["ffn2_wd"], norm_w=norm_full,
        conv_w8=jnp.pad(conv_full, ((0, 0), (0, 8 - CONV_K), (0, 0))), gparams=gparams,
        dn_norm_t=jnp.tile(dn_norm_w, (1, NH)), mx=mx, mc=mc)
    lws = [{k: a[l] for k, a in stacked.items()} for l in range(DEPTH)]
    loss_l, gx, gws, smalls, d_fnw = local_step(x[0], ctx[0], loss_target[0], final_norm_w, lws)

    sm_list = [jnp.stack([s[k] for s in smalls]) for k in ("dmx", "dmc", "norm_w", "conv_w", "a_log", "dt_bias",
                                                            "dn_norm_w")] + [d_fnw, loss_l.reshape(1)]
    sm_shapes = [a.shape for a in sm_list]
    g_sm = all_gather([_pack(sm_list, 8)], "ag_small_grads")[0]
    (dmx_sum, dmc_sum, g_norm, g_conv, g_alog, g_dtb, g_dnw, g_fnw, loss) = _unpack(sum_slots(g_sm), sm_shapes)
    dmx_all = _unpack(g_sm, sm_shapes[:1], lead=N_DEV)[0].reshape(N_DEV, DEPTH, N_MOD * D)
    dmc_sum = dmc_sum.reshape(DEPTH, N_MOD * D)
    dmx_sum = dmx_sum.reshape(DEPTH, N_MOD * D)
    dmod = jnp.concatenate([
        lax.dynamic_slice_in_dim(dmx_all, me * ADA_COLS, ADA_COLS, axis=2).transpose(1, 0, 2),
        lax.dynamic_slice_in_dim(dmc_sum, me * ADA_COLS, ADA_COLS, axis=1)[:, None, :],
        jnp.zeros((DEPTH, ADA_ROWS - N_DEV - 1, ADA_COLS), F32)], axis=1)
    g_ada_w, d_sil = ada_bwd(cc, dmod, ada_w)
    g_cc = all_gather([d_sil[:, N_DEV].sum(axis=0).reshape(8, LANE)], "ag_c_ctx")[0]
    grads = dict(
        ada_w=g_ada_w, c_ctx=c_ctx_grad(g_cc, c_ctx.reshape(8, LANE)).reshape(D), ada_b=dmx_sum + dmc_sum,
        norm_w=lax.dynamic_slice_in_dim(g_norm, me * (D // N_DEV), D // N_DEV, axis=2),
        dn_conv_w=lax.dynamic_slice_in_dim(g_conv, me * (2 * D // N_DEV), 2 * D // N_DEV, axis=2),
        dn_a_log=g_alog.reshape(dn_a_log.shape), dn_dt_bias=g_dtb.reshape(dn_dt_bias.shape), dn_norm_w=g_dnw,
        final_norm_w=g_fnw)

    def layer_full(gw):
        return dict(ffn1_wgu=jnp.concatenate([gw["wg1"], gw["wu1"]], axis=-1), ffn1_wd=gw["wd1"],
                    w_in=_unpad_w_in(gw["w_in"]), w_ret_out=gw["w_r"], w_dn_out=gw["w_d"], w_o=gw["w_o"],
                    ffn2_wgu=jnp.concatenate([gw["wg2"], gw["wu2"]], axis=-1), ffn2_wd=gw["wd2"])

    full_g = [layer_full(gw) for gw in gws]
    parts = [jnp.stack([_split_shards(fg[n], kind) for fg in full_g], axis=1).astype(BF16) for n, kind in BIG]
    recv = all_to_all(parts, "a2a_grads")
    res = {}
    for n, r in zip(big_names, recv):
        rows_blk = 256 if w[n].shape[1] % 256 == 0 else w[n].shape[1]
        res[n] = adamw(r, w[n], m[n], v[n], rows_blk, "adamw_" + n)
    res["ada_w"] = adamw(g_ada_w[None], ada_w, m["ada_w"], v["ada_w"], 256, "adamw_ada_w")
    small_names = [n for n in LOCAL if n != "ada_w"]
    small_pack = lambda d: _pack([d[n] for n in small_names], 8)[None]
    res_small = adamw(small_pack(grads)[None], small_pack(w), small_pack(m), small_pack(v),
                      small_pack(w).shape[1], "adamw_small")
    unpacked = [_unpack(r, [w[n].shape for n in small_names]) for r in res_small]
    for i, n in enumerate(small_names):
        res[n] = tuple(u[i] for u in unpacked)
    outs = [[res[n][k] for n in WEIGHT_ORDER] for k in range(4)]
    return (loss.reshape(()), gx[None], *outs[0], *outs[1], *outs[2], *outs[3])
```

```python
import functools
import math

import jax
import jax.numpy as jnp
from jax import lax
from jax.experimental import pallas as pl
from jax.experimental.pallas import tpu as pltpu

F32 = jnp.float32
BF16 = jnp.bfloat16

D = 1024
NH = 4
DK = 128
DV = 256
RET_C = 128
DN_C = 64
FFN_H = 2816
FFN_HB = 1408
N_MOD = 9
DEPTH = 4
N_DEV = 8
EPS = 1e-6
CONV_K = 5
HALO = 8

PW = 8320
C_RQK, C_RV, C_RG, C_DQK, C_DVV, C_DZ, C_GA, C_GB = 0, 1, 2, 3, 4, 5, 6, 7
C_DBA = 64
PROJ_TN = 1664

LANE = 128
VMEM_LIMIT = 56 * 1024 * 1024

ADAM_LR, ADAM_B1, ADAM_B2, ADAM_EPS, ADAM_WD, ADAM_STEP = 0.001, 0.9, 0.999, 1e-08, 0.01, 10

NN = ((1,), (0,))
NT = ((1,), (1,))
TN = ((0,), (0,))
HI = lax.Precision.HIGHEST


def _params(n_grid):
    return pltpu.CompilerParams(dimension_semantics=("arbitrary",) * n_grid, vmem_limit_bytes=VMEM_LIMIT)


def _dot(a, b, dims):
    return lax.dot_general(a.astype(BF16), b.astype(BF16), (dims, ((), ())), preferred_element_type=F32)


def _dot_hi(a, b, dims=NN):
    return lax.dot_general(a, b, (dims, ((), ())), precision=HI, preferred_element_type=F32)


@jax.custom_vjp
def mm_nn(a, b):
    return _dot(a, b, NN)


mm_nn.defvjp(lambda a, b: (_dot(a, b, NN), (a, b)),
             lambda r, g: (_dot(g, r[1], NT), _dot(r[0], g, TN)))


@jax.custom_vjp
def mm_nt(a, b):
    return _dot(a, b, NT)


mm_nt.defvjp(lambda a, b: (_dot(a, b, NT), (a, b)),
             lambda r, g: (_dot(g, r[1], NN), _dot(g, r[0], TN)))


@jax.custom_vjp
def mm_tn(a, b):
    return _dot(a, b, TN)


mm_tn.defvjp(lambda a, b: (_dot(a, b, TN), (a, b)),
             lambda r, g: (_dot(r[1], g, NT), _dot(r[0], g, NN)))


def _split_bf16(x, n):
    parts = []
    for _ in range(n):
        p = x.astype(BF16)
        parts.append(p)
        x = x - p.astype(F32)
    return parts


def _dot_f32(a, b, dims, exact=None):
    mm = lambda p, q: lax.dot_general(p, q, (dims, ((), ())), preferred_element_type=F32)
    if exact == "a":
        ab = a.astype(BF16)
        b1, b2, b3 = _split_bf16(b, 3)
        return mm(ab, b1) + (mm(ab, b2) + mm(ab, b3))
    if exact == "b":
        bb = b.astype(BF16)
        a1, a2, a3 = _split_bf16(a, 3)
        return mm(a1, bb) + (mm(a2, bb) + mm(a3, bb))
    a1, a2 = _split_bf16(a, 2)
    b1, b2 = _split_bf16(b, 2)
    return mm(a1, b1) + (mm(a1, b2) + mm(a2, b1))


@jax.custom_vjp
def _cum_rows(mask, x):
    return _dot_f32(mask, x, NN, exact="a")


_cum_rows.defvjp(lambda mask, x: (_dot_f32(mask, x, NN, exact="a"), mask),
                 lambda mask, g: (jnp.zeros_like(mask), _dot_f32(mask, g, TN, exact="a")))


@jax.custom_vjp
def _row_bcast(xb):
    return _dot_f32(jnp.full(xb.shape, 1.0 / LANE, F32), xb, NT, exact="a")


_row_bcast.defvjp(lambda xb: (_row_bcast(xb), None),
                  lambda _, g: (_dot_f32(g, jnp.full((g.shape[0], LANE), 1.0 / LANE, F32), TN, exact="b"),))


def _tri_inv_fwd(a):
    c = a.shape[0]
    eye = (lax.broadcasted_iota(jnp.int32, (c, c), 0) == lax.broadcasted_iota(jnp.int32, (c, c), 1)).astype(F32)
    pw = -a
    tinv = eye + pw
    m = 2
    while m < DN_C:
        pw = _dot_f32(pw, pw, NN)
        tinv = tinv + _dot_f32(tinv, pw, NN)
        m *= 2
    return tinv


@jax.custom_vjp
def _tri_inv(a):
    return _tri_inv_fwd(a)


def _tri_inv_bwd(tinv, g):
    return (-_dot_f32(_dot_f32(tinv, g, TN), tinv, NT),)


_tri_inv.defvjp(lambda a: (lambda t: (t, t))(_tri_inv_fwd(a)), _tri_inv_bwd)


@jax.custom_vjp
def _tri_inv_known(a, tinv):
    del a
    return tinv


_tri_inv_known.defvjp(lambda a, tinv: (tinv, tinv),
                      lambda tinv, g: (_tri_inv_bwd(tinv, g)[0], jnp.zeros_like(tinv)))


def _normmod(x, nw, shift, scale):
    r = lax.rsqrt(jnp.mean(x * x, axis=-1, keepdims=True) + EPS)
    return (x * r * nw) * (1.0 + scale) + shift


def _sigmoid(x):
    return 0.5 * jnp.tanh(0.5 * x) + 0.5


def _softplus(x):
    return jnp.maximum(x, 0.0) + jnp.log(1.0 + jnp.exp(-jnp.abs(x)))


def _row_block(t_len):
    return 512 if t_len % 512 == 0 else 256


def ffn_fwd(x, vec, wgu, wd):
    t_len = x.shape[0]
    tm = _row_block(t_len)

    def body(x_ref, vec_ref, wg_ref, wu_ref, wd_ref, x1_ref, h_ref, a_ref, s_ref, m_ref, f_ref, hs, acc):
        j = pl.program_id(1)

        @pl.when(j == 0)
        def _():
            hb = _normmod(x_ref[...], vec_ref[0:1, :], vec_ref[1:2, :], vec_ref[2:3, :]).astype(BF16)
            hs[...] = hb
            h_ref[...] = hb
            acc[...] = jnp.zeros_like(acc)

        hb = hs[...]
        g = jnp.dot(hb, wg_ref[...], preferred_element_type=F32)
        u = jnp.dot(hb, wu_ref[...], preferred_element_type=F32)
        sg = _sigmoid(g)
        sl = g * sg
        act = (sl * u).astype(BF16)
        a_ref[...] = act
        s_ref[...] = sl.astype(BF16)
        m_ref[...] = (u * (sg + sl * (1.0 - sg))).astype(BF16)
        acc[...] += jnp.dot(act, wd_ref[...], preferred_element_type=F32)

        @pl.when(j == 1)
        def _():
            f = acc[...]
            f_ref[...] = f.astype(BF16)
            x1_ref[...] = x_ref[...] + 0.5 * vec_ref[3:4, :] * f

    return pl.pallas_call(
        body, name="ffn_fwd", grid=(t_len // tm, 2),
        in_specs=[
            pl.BlockSpec((tm, D), lambda i, j: (i, 0)),
            pl.BlockSpec((8, D), lambda i, j: (0, 0)),
            pl.BlockSpec((D, FFN_HB), lambda i, j: (0, j)),
            pl.BlockSpec((D, FFN_HB), lambda i, j: (0, j + 2)),
            pl.BlockSpec((FFN_HB, D), lambda i, j: (j, 0)),
        ],
        out_specs=[
            pl.BlockSpec((tm, D), lambda i, j: (i, 0)),
            pl.BlockSpec((tm, D), lambda i, j: (i, 0)),
            pl.BlockSpec((tm, FFN_HB), lambda i, j: (i, j)),
            pl.BlockSpec((tm, FFN_HB), lambda i, j: (i, j)),
            pl.BlockSpec((tm, FFN_HB), lambda i, j: (i, j)),
            pl.BlockSpec((tm, D), lambda i, j: (i, 0)),
        ],
        out_shape=[
            jax.ShapeDtypeStruct((t_len, D), F32),
            jax.ShapeDtypeStruct((t_len, D), BF16),
            jax.ShapeDtypeStruct((t_len, FFN_H), BF16),
            jax.ShapeDtypeStruct((t_len, FFN_H), BF16),
            jax.ShapeDtypeStruct((t_len, FFN_H), BF16),
            jax.ShapeDtypeStruct((t_len, D), BF16),
        ],
        scratch_shapes=[pltpu.VMEM((tm, D), BF16), pltpu.VMEM((tm, D), F32)],
        compiler_params=_params(2),
    )(x, vec, wgu, wgu, wd)


def ffn_bwd_act(dx1, vec, sl, dact_dg, f, wd):
    t_len = dx1.shape[0]
    tm = _row_block(t_len)

    def body(dx_ref, vec_ref, s_ref, m_ref, f_ref, wd_ref, dg_ref, du_ref, dy_ref, pg_ref, dys):
        i, j = pl.program_id(0), pl.program_id(1)

        @pl.when((i == 0) & (j == 0))
        def _():
            pg_ref[...] = jnp.zeros_like(pg_ref)

        @pl.when(j == 0)
        def _():
            dx = dx_ref[...]
            dyb = (0.5 * vec_ref[3:4, :] * dx).astype(BF16)
            dys[...] = dyb
            dy_ref[...] = dyb
            pg_ref[3:4, :] += jnp.sum(0.5 * dx * f_ref[...].astype(F32), axis=0, keepdims=True)

        dact = _dot(dys[...], wd_ref[...], NT)
        du_ref[...] = (dact * s_ref[...].astype(F32)).astype(BF16)
        dg_ref[...] = (dact * m_ref[...].astype(F32)).astype(BF16)

    return pl.pallas_call(
        body, name="ffn_bwd_act", grid=(t_len // tm, 2),
        in_specs=[
            pl.BlockSpec((tm, D), lambda i, j: (i, 0)),
            pl.BlockSpec((8, D), lambda i, j: (0, 0)),
            pl.BlockSpec((tm, FFN_HB), lambda i, j: (i, j)),
            pl.BlockSpec((tm, FFN_HB), lambda i, j: (i, j)),
            pl.BlockSpec((tm, D), lambda i, j: (i, 0)),
            pl.BlockSpec((FFN_HB, D), lambda i, j: (j, 0)),
        ],
        out_specs=[
            pl.BlockSpec((tm, FFN_HB), lambda i, j: (i, j)),
            pl.BlockSpec((tm, FFN_HB), lambda i, j: (i, j)),
            pl.BlockSpec((tm, D), lambda i, j: (i, 0)),
            pl.BlockSpec((8, D), lambda i, j: (0, 0)),
        ],
        out_shape=[
            jax.ShapeDtypeStruct((t_len, FFN_H), BF16),
            jax.ShapeDtypeStruct((t_len, FFN_H), BF16),
            jax.ShapeDtypeStruct((t_len, D), BF16),
            jax.ShapeDtypeStruct((8, D), F32),
        ],
        scratch_shapes=[pltpu.VMEM((tm, D), BF16)],
        compiler_params=_params(2),
    )(dx1, vec, sl, dact_dg, f, wd)


def nt_norm_bwd(dys, w, col_offsets, tk, n_steps, x_in, dres, vec, name):
    t_len = x_in.shape[0]
    tm = _row_block(t_len)
    n_seg = len(dys)

    def body(*refs):
        dy_refs = refs[:n_seg]
        w_refs = refs[n_seg:2 * n_seg]
        x_ref, dres_ref, vec_ref, dx_ref, pg_ref, acc = refs[2 * n_seg:]
        i, j = pl.program_id(0), pl.program_id(1)

        @pl.when((i == 0) & (j == 0))
        def _():
            pg_ref[...] = jnp.zeros_like(pg_ref)

        @pl.when(j == 0)
        def _():
            acc[...] = jnp.zeros_like(acc)

        part = _dot(dy_refs[0][...], w_refs[0][...], NT)
        for s in range(1, n_seg):
            part += _dot(dy_refs[s][...], w_refs[s][...], NT)
        acc[...] += part

        @pl.when(j == n_steps - 1)
        def _():
            _, vjp = jax.vjp(_normmod, x_ref[...], vec_ref[0:1, :], vec_ref[1:2, :], vec_ref[2:3, :])
            dxn, dnw, dsh, dsc = vjp(acc[...])
            dx_ref[...] = dres_ref[...] + dxn
            pg_ref[0:1, :] += dnw
            pg_ref[1:2, :] += dsh
            pg_ref[2:3, :] += dsc

    in_specs = [pl.BlockSpec((tm, tk), lambda i, j: (i, j)) for _ in range(n_seg)]
    in_specs += [pl.BlockSpec((D, tk), functools.partial(lambda i, j, off: (0, off + j), off=off))
                 for off in col_offsets]
    in_specs += [pl.BlockSpec((tm, D), lambda i, j: (i, 0)), pl.BlockSpec((tm, D), lambda i, j: (i, 0)),
                 pl.BlockSpec((8, D), lambda i, j: (0, 0))]
    return pl.pallas_call(
        body, name=name, grid=(t_len // tm, n_steps),
        in_specs=in_specs,
        out_specs=[pl.BlockSpec((tm, D), lambda i, j: (i, 0)), pl.BlockSpec((8, D), lambda i, j: (0, 0))],
        out_shape=[jax.ShapeDtypeStruct((t_len, D), F32), jax.ShapeDtypeStruct((8, D), F32)],
        scratch_shapes=[pltpu.VMEM((tm, D), F32)],
        compiler_params=_params(2),
    )(*dys, *([w] * n_seg), x_in, dres, vec)


def tn_matmul(a_x, b_x, a_c, b_c, tn, name):
    t_len, k_dim = a_x.shape
    n_dim = b_x.shape[1]
    t_ctx = a_c.shape[0]
    tt = 512
    n_t = t_len // tt

    def body(ax_ref, bx_ref, ac_ref, bc_ref, o_ref):
        t = pl.program_id(1)

        @pl.when(t == 0)
        def _():
            o_ref[...] = _dot(ac_ref[...], bc_ref[...], TN)

        o_ref[...] += _dot(ax_ref[...], bx_ref[...], TN)

    return pl.pallas_call(
        body, name=name, grid=(n_dim // tn, n_t),
        in_specs=[
            pl.BlockSpec((tt, k_dim), lambda n, t: (t, 0)),
            pl.BlockSpec((tt, tn), lambda n, t: (t, n)),
            pl.BlockSpec((t_ctx, k_dim), lambda n, t: (0, 0)),
            pl.BlockSpec((t_ctx, tn), lambda n, t: (0, n)),
        ],
        out_specs=pl.BlockSpec((k_dim, tn), lambda n, t: (0, n)),
        out_shape=jax.ShapeDtypeStruct((k_dim, n_dim), F32),
        compiler_params=_params(2),
    )(a_x, b_x, a_c, b_c)


N_SEG = 8


def proj_bwd_in(segs, dba, w, x_in, dres, vec):
    t_len = x_in.shape[0]
    tm = _row_block(t_len)

    def body(*refs):
        seg_refs = refs[:N_SEG]
        dba_ref, w_ref, wg_ref, x_ref, dres_ref, vec_ref, dx_ref, pg_ref, acc = refs[N_SEG:]
        i, j = pl.program_id(0), pl.program_id(1)

        @pl.when((i == 0) & (j == 0))
        def _():
            pg_ref[...] = jnp.zeros_like(pg_ref)

        @pl.when(j == 0)
        def _():
            acc[...] = _dot(dba_ref[...], wg_ref[...], NT)

        for s in range(N_SEG):
            @pl.when(j == s)
            def _():
                acc[...] += _dot(seg_refs[s][...], w_ref[...], NT)

        @pl.when(j == N_SEG - 1)
        def _():
            _, vjp = jax.vjp(_normmod, x_ref[...], vec_ref[0:1, :], vec_ref[1:2, :], vec_ref[2:3, :])
            dxn, dnw, dsh, dsc = vjp(acc[...])
            dx_ref[...] = dres_ref[...] + dxn
            pg_ref[0:1, :] += dnw
            pg_ref[1:2, :] += dsh
            pg_ref[2:3, :] += dsc

    row = lambda width: pl.BlockSpec((tm, width), lambda i, j: (i, 0))
    return pl.pallas_call(
        body, name="proj_bwd_in", grid=(t_len // tm, N_SEG),
        in_specs=[row(D)] * N_SEG + [row(LANE), pl.BlockSpec((D, D), lambda i, j: (0, j)),
                                     pl.BlockSpec((D, LANE), lambda i, j: (0, C_DBA)),
                                     row(D), row(D), pl.BlockSpec((8, D), lambda i, j: (0, 0))],
        out_specs=[row(D), pl.BlockSpec((8, D), lambda i, j: (0, 0))],
        out_shape=[jax.ShapeDtypeStruct((t_len, D), F32), jax.ShapeDtypeStruct((8, D), F32)],
        scratch_shapes=[pltpu.VMEM((tm, D), F32)],
        compiler_params=_params(2),
    )(*segs, dba, w, w, x_in, dres, vec)


def dw_in(a_x, segs_x, a_c, segs_c):
    t_len, k_dim = a_x.shape
    t_ctx = a_c.shape[0]
    tt = 512

    def body(*refs):
        ax_ref, ac_ref = refs[0], refs[1]
        bx_refs, bc_refs = refs[2:2 + N_SEG], refs[2 + N_SEG:2 + 2 * N_SEG]
        o_ref = refs[2 + 2 * N_SEG]
        n, t = pl.program_id(0), pl.program_id(1)
        for s in range(N_SEG):
            @pl.when(n == s)
            def _():
                @pl.when(t == 0)
                def _():
                    o_ref[...] = _dot(ac_ref[...], bc_refs[s][...], TN)

                o_ref[...] += _dot(ax_ref[...], bx_refs[s][...], TN)

    seg_x = [pl.BlockSpec((tt, D), functools.partial(lambda n, t, s: (jnp.where(n == s, t, 0), 0), s=s))
             for s in range(N_SEG)]
    seg_c = [pl.BlockSpec((t_ctx, D), lambda n, t: (0, 0)) for _ in range(N_SEG)]
    return pl.pallas_call(
        body, name="dw_in", grid=(N_SEG, t_len // tt),
        in_specs=[pl.BlockSpec((tt, k_dim), lambda n, t: (t, 0)), pl.BlockSpec((t_ctx, k_dim), lambda n, t: (0, 0))]
        + seg_x + seg_c,
        out_specs=pl.BlockSpec((k_dim, D), lambda n, t: (0, n)),
        out_shape=jax.ShapeDtypeStruct((k_dim, N_SEG * D), F32),
        compiler_params=_params(2),
    )(a_x, a_c, *segs_x, *segs_c)


def proj_fwd(x, vec, w_in_p):
    t_len = x.shape[0]
    tm = _row_block(t_len)

    def body(x_ref, vec_ref, w_ref, p_ref, h_ref, hs):
        @pl.when(pl.program_id(1) == 0)
        def _():
            hb = _normmod(x_ref[...], vec_ref[0:1, :], vec_ref[1:2, :], vec_ref[2:3, :]).astype(BF16)
            hs[...] = hb
            h_ref[...] = hb

        p_ref[...] = jnp.dot(hs[...], w_ref[...], preferred_element_type=F32)

    return pl.pallas_call(
        body, name="proj_fwd", grid=(t_len // tm, PW // PROJ_TN),
        in_specs=[
            pl.BlockSpec((tm, D), lambda i, j: (i, 0)),
            pl.BlockSpec((8, D), lambda i, j: (0, 0)),
            pl.BlockSpec((D, PROJ_TN), lambda i, j: (0, j)),
        ],
        out_specs=[pl.BlockSpec((tm, PROJ_TN), lambda i, j: (i, j)), pl.BlockSpec((tm, D), lambda i, j: (i, 0))],
        out_shape=[jax.ShapeDtypeStruct((t_len, PW), F32), jax.ShapeDtypeStruct((t_len, D), BF16)],
        scratch_shapes=[pltpu.VMEM((tm, D), BF16)],
        compiler_params=_params(2),
    )(x, vec, w_in_p)


def _shift_rows(e, s):
    n = e.shape[0]
    return pltpu.roll(e, (-s) % n, 0)


def _swap_halves(t):
    return pltpu.roll(t, DK // 2, 1)


def _halo_specs(tm, t_len, width, col, lead=False):
    per = tm // HALO
    last = t_len // HALO - 1
    if lead:
        return (pl.BlockSpec((2, HALO, width), lambda i: (0, jnp.maximum(i * per - 1, 0), col)),
                pl.BlockSpec((2, HALO, width), lambda i: (0, jnp.minimum((i + 1) * per, last), col)))
    return (pl.BlockSpec((HALO, width), lambda i: (jnp.maximum(i * per - 1, 0), col)),
            pl.BlockSpec((HALO, width), lambda i: (jnp.minimum((i + 1) * per, last), col)))


def _gate_cols(ba, gp_ref):
    lane = lax.broadcasted_iota(jnp.int32, ba.shape, 1)
    z = ba + gp_ref[1:2, :]
    return jnp.where(lane < 8, _sigmoid(ba), gp_ref[0:1, :] * _softplus(z))


def _conv_chunk(win, w_ref, c0, width):
    y = w_ref[0:1, c0:c0 + width] * _shift_rows(win, -2)
    for j in range(1, CONV_K):
        y += w_ref[j:j + 1, c0:c0 + width] * _shift_rows(win, j - 2)
    return y


def feat_fwd(p, cos2, sin2, conv_w8, gparams):
    t_len = p.shape[0]
    tm = _row_block(t_len)
    n_t = t_len // tm
    cw = 512

    def body(rqk_ref, dqk_ref, dqk_p, dqk_n, dvv_ref, dvv_p, dvv_n, dba_ref, cos_ref, sin_ref, cw_ref, gp_ref,
             o_rqk, o_dqkv, o_gbc):
        i = pl.program_id(0)
        cos, sin = cos_ref[...], sin_ref[...]
        for s in range(2 * NH):
            t = rqk_ref[:, s * DK:(s + 1) * DK]
            if s >= NH:
                t = t * (DK ** -0.5)
            o_rqk[:, s * DK:(s + 1) * DK] = t * cos + _swap_halves(t) * sin
        o_gbc[...] = _gate_cols(dba_ref[...], gp_ref)
        first, last = i == 0, i == n_t - 1
        for ci in range(4):
            src, sp, sn = (dqk_ref, dqk_p, dqk_n) if ci < 2 else (dvv_ref, dvv_p, dvv_n)
            c0 = (ci % 2) * cw
            win = jnp.concatenate([
                jnp.where(first, 0.0, sp[:, c0:c0 + cw]), src[:, c0:c0 + cw],
                jnp.where(last, 0.0, sn[:, c0:c0 + cw])], axis=0)
            y = _conv_chunk(win, cw_ref, ci * cw, cw)[HALO:HALO + tm]
            sv = y * _sigmoid(y)
            if ci < 2:
                scale = DK ** -0.5 if ci == 0 else 1.0
                for hh in range(NH):
                    sh = sv[:, hh * DK:(hh + 1) * DK]
                    nrm = lax.rsqrt(jnp.sum(sh * sh, axis=-1, keepdims=True) + EPS)
                    o_dqkv[:, ci * cw + hh * DK:ci * cw + (hh + 1) * DK] = sh * nrm * scale
            else:
                o_dqkv[:, ci * cw:(ci + 1) * cw] = sv

    hq = _halo_specs(tm, t_len, D, C_DQK)
    hv = _halo_specs(tm, t_len, D, C_DVV)
    return pl.pallas_call(
        body, name="feat_fwd", grid=(n_t,),
        in_specs=[
            pl.BlockSpec((tm, D), lambda i: (i, C_RQK)),
            pl.BlockSpec((tm, D), lambda i: (i, C_DQK)), hq[0], hq[1],
            pl.BlockSpec((tm, D), lambda i: (i, C_DVV)), hv[0], hv[1],
            pl.BlockSpec((tm, LANE), lambda i: (i, C_DBA)),
            pl.BlockSpec((tm, LANE), lambda i: (i, 0)),
            pl.BlockSpec((tm, LANE), lambda i: (i, 0)),
            pl.BlockSpec((8, 2 * D), lambda i: (0, 0)),
            pl.BlockSpec((8, LANE), lambda i: (0, 0)),
        ],
        out_specs=[pl.BlockSpec((tm, D), lambda i: (i, 0)), pl.BlockSpec((tm, 2 * D), lambda i: (i, 0)),
                   pl.BlockSpec((tm, LANE), lambda i: (i, 0))],
        out_shape=[jax.ShapeDtypeStruct((t_len, D), F32), jax.ShapeDtypeStruct((t_len, 2 * D), F32),
                   jax.ShapeDtypeStruct((t_len, LANE), F32)],
        compiler_params=_params(1),
    )(p, p, p, p, p, p, p, p, cos2, sin2, conv_w8, gparams)


def feat_bwd(p, cos2, sin2, conv_w8, gparams, d_rqk, d_rv, d_dqk, d_dvv, d_gbc):
    t_len = p.shape[0]
    tm = 256
    n_t = t_len // tm
    cw = 512

    def body(dqk_ref, dqk_p, dqk_n, dvv_ref, dvv_p, dvv_n, dba_ref, cos_ref, sin_ref, cw_ref, gp_ref,
             g_rqk, g_rv, g_dqk, g_dqk_p, g_dqk_n, g_dvv, g_dvv_p, g_dvv_n, g_gbc,
             o_rqk, o_rv, o_dqk, o_dvv, o_dba, o_cw, o_gp):
        i = pl.program_id(0)
        first, last = i == 0, i == n_t - 1

        @pl.when(first)
        def _():
            o_cw[...] = jnp.zeros_like(o_cw)
            o_gp[...] = jnp.zeros_like(o_gp)

        cos, sin = cos_ref[...], sin_ref[...]
        for s in range(2 * NH):
            gsl = g_rqk[0, :, s * DK:(s + 1) * DK] + g_rqk[1, :, s * DK:(s + 1) * DK]
            dt = gsl * cos + _swap_halves(gsl * sin)
            if s >= NH:
                dt = dt * (DK ** -0.5)
            o_rqk[:, s * DK:(s + 1) * DK] = dt.astype(BF16)
        o_rv[...] = (g_rv[0] + g_rv[1]).astype(BF16)

        ba = dba_ref[...]
        gg = g_gbc[0] + g_gbc[1]
        lane = lax.broadcasted_iota(jnp.int32, ba.shape, 1)
        sb = _sigmoid(ba)
        z = ba + gp_ref[1:2, :]
        a_row = gp_ref[0:1, :]
        dz = gg * a_row * _sigmoid(z)
        o_dba[...] = jnp.where(lane < 8, gg * sb * (1.0 - sb), dz).astype(BF16)
        is_g = (lane >= 8) & (lane < 16)
        o_gp[0:1, :] += jnp.sum(jnp.where(is_g, gg * a_row * _softplus(z), 0.0), axis=0, keepdims=True)
        o_gp[1:2, :] += jnp.sum(jnp.where(is_g, dz, 0.0), axis=0, keepdims=True)

        for ci in range(4):
            src, sp, sn = (dqk_ref, dqk_p, dqk_n) if ci < 2 else (dvv_ref, dvv_p, dvv_n)
            c0 = (ci % 2) * cw
            gc0 = ci * cw
            win = jnp.concatenate([
                jnp.where(first, 0.0, sp[:, c0:c0 + cw]), src[:, c0:c0 + cw],
                jnp.where(last, 0.0, sn[:, c0:c0 + cw])], axis=0)
            gs, gsp, gsn = (g_dqk, g_dqk_p, g_dqk_n) if ci < 2 else (g_dvv, g_dvv_p, g_dvv_n)
            gext = jnp.concatenate([
                jnp.where(first, 0.0, gsp[0, :, c0:c0 + cw] + gsp[1, :, c0:c0 + cw]),
                gs[0, :, c0:c0 + cw] + gs[1, :, c0:c0 + cw],
                jnp.where(last, 0.0, gsn[0, :, c0:c0 + cw] + gsn[1, :, c0:c0 + cw])], axis=0)
            y = _conv_chunk(win, cw_ref, gc0, cw)
            sg = _sigmoid(y)
            sv = y * sg
            if ci < 2:
                scale = DK ** -0.5 if ci == 0 else 1.0
                parts = []
                for hh in range(NH):
                    sh = sv[:, hh * DK:(hh + 1) * DK]
                    gh = gext[:, hh * DK:(hh + 1) * DK]
                    nrm = lax.rsqrt(jnp.sum(sh * sh, axis=-1, keepdims=True) + EPS)
                    dot = jnp.sum(gh * sh, axis=-1, keepdims=True)
                    parts.append(scale * nrm * (gh - sh * (nrm * nrm) * dot))
                ds = jnp.concatenate(parts, axis=1)
            else:
                ds = gext
            dy = ds * (sg + sv * (1.0 - sg))
            dpe = cw_ref[0:1, gc0:gc0 + cw] * _shift_rows(dy, 2)
            for j in range(1, CONV_K):
                dpe += cw_ref[j:j + 1, gc0:gc0 + cw] * _shift_rows(dy, 2 - j)
            dst = o_dqk if ci < 2 else o_dvv
            dst[:, c0:c0 + cw] = dpe[HALO:HALO + tm].astype(BF16)
            dyc = dy[HALO:HALO + tm]
            for j in range(CONV_K):
                o_cw[j:j + 1, gc0:gc0 + cw] += jnp.sum(dyc * _shift_rows(win, j - 2)[HALO:HALO + tm], axis=0,
                                                      keepdims=True)

    hq = _halo_specs(tm, t_len, D, C_DQK)
    hv = _halo_specs(tm, t_len, D, C_DVV)
    hg = _halo_specs(tm, t_len, D, 0, lead=True)
    outs = pl.pallas_call(
        body, name="feat_bwd", grid=(n_t,),
        in_specs=[
            pl.BlockSpec((tm, D), lambda i: (i, C_DQK)), hq[0], hq[1],
            pl.BlockSpec((tm, D), lambda i: (i, C_DVV)), hv[0], hv[1],
            pl.BlockSpec((tm, LANE), lambda i: (i, C_DBA)),
            pl.BlockSpec((tm, LANE), lambda i: (i, 0)),
            pl.BlockSpec((tm, LANE), lambda i: (i, 0)),
            pl.BlockSpec((8, 2 * D), lambda i: (0, 0)),
            pl.BlockSpec((8, LANE), lambda i: (0, 0)),
            pl.BlockSpec((2, tm, D), lambda i: (0, i, 0)),
            pl.BlockSpec((2, tm, D), lambda i: (0, i, 0)),
            pl.BlockSpec((2, tm, D), lambda i: (0, i, 0)), hg[0], hg[1],
            pl.BlockSpec((2, tm, D), lambda i: (0, i, 0)), hg[0], hg[1],
            pl.BlockSpec((2, tm, LANE), lambda i: (0, i, 0)),
        ],
        out_specs=[pl.BlockSpec((tm, D), lambda i: (i, 0))] * 4
        + [pl.BlockSpec((tm, LANE), lambda i: (i, 0)),
           pl.BlockSpec((8, 2 * D), lambda i: (0, 0)),
           pl.BlockSpec((8, LANE), lambda i: (0, 0))],
        out_shape=[jax.ShapeDtypeStruct((t_len, D), BF16)] * 4
        + [jax.ShapeDtypeStruct((t_len, LANE), BF16),
           jax.ShapeDtypeStruct((8, 2 * D), F32), jax.ShapeDtypeStruct((8, LANE), F32)],
        compiler_params=_params(1),
    )(p, p, p, p, p, p, p, cos2, sin2, conv_w8, gparams, d_rqk, d_rv, d_dqk, d_dqk, d_dqk, d_dvv, d_dvv, d_dvv,
      d_gbc)
    return outs


def _ret_chunk(q, k, v, s, lg, rev):
    c = q.shape[0]
    ii = lax.broadcasted_iota(jnp.int32, (c, c), 0).astype(F32)
    jj = lax.broadcasted_iota(jnp.int32, (c, c), 1).astype(F32)
    diff = jnp.where(rev, jj - ii, ii - jj)
    mask = diff >= jnp.where(rev, 1.0, 0.0)
    dec =jnp.where(mask, jnp.exp(lg * jnp.where(mask, diff, 0.0)), 0.0)
    idx = lax.broadcasted_iota(jnp.int32, (c, 1), 0).astype(F32)
    idx = jnp.where(rev, c - 1.0 - idx, idx)
    kdec = k * jnp.exp(lg * (c - 1.0 - idx))
    qdec = q * jnp.exp(lg * (idx + 1.0))
    o = mm_nn(mm_nt(q, k) * dec, v) + mm_nn(qdec, s)
    s2 = s * math.exp(lg * c) + mm_tn(kdec, v)
    return o, s2


def _log_gamma(h):
    return math.log1p(-(2.0 ** (-5.0 - h)))


DN_W = 3 * NH * DK
HS = NH * DN_C


def _dn_intra4(q4, k4, v4, g4, beta4, rev, tinv_known=None):
    ii = lax.broadcasted_iota(jnp.int32, (HS, HS), 0)
    jj = lax.broadcasted_iota(jnp.int32, (HS, HS), 1)
    blk = (ii // DN_C) == (jj // DN_C)
    dd = jnp.where(rev, jj - ii, ii - jj)
    incl = blk & (dd >= 0)
    strict = blk & (dd > 0)
    gb = jnp.broadcast_to(g4, (HS, LANE))
    gcb = _cum_rows(incl.astype(F32), gb)
    gc = jnp.max(gcb, axis=1, keepdims=True)
    gcr = _row_bcast(gcb)
    at_end = blk & ((jj % DN_C) == jnp.where(rev, 0, DN_C - 1))
    glast = jnp.sum(jnp.where(at_end, gcr, 0.0), axis=1, keepdims=True)
    decay = jnp.where(incl, jnp.exp(jnp.where(incl, gc - gcr, 0.0)), 0.0)
    kb = k4 * beta4
    a = jnp.where(strict, mm_nt(kb, k4) * decay, 0.0)
    tinv = _tri_inv(a) if tinv_known is None else _tri_inv_known(a, tinv_known)
    u = mm_nn(tinv, v4 * beta4)
    w = mm_nn(tinv, kb * jnp.exp(gc))
    attn = jnp.where(incl, mm_nt(q4, k4) * decay, 0.0)
    return (u, w, q4 * jnp.exp(gc), k4 * jnp.exp(glast - gc), attn), tinv


def _dn_seq4(u, w, qg, kd, attn, glast, s):
    v_new = [uh - mm_nn(wh, sh) for uh, wh, sh in zip(u, w, s)]
    s2 = [sh * jnp.exp(gh) + mm_tn(kh, vh) for sh, gh, kh, vh in zip(s, glast, kd, v_new)]
    o = jnp.concatenate([mm_nn(qh, sh) for qh, sh in zip(qg, s)], axis=0) + mm_nn(attn, jnp.concatenate(v_new, axis=0))
    return o, s2


def _stack_heads(ref_rows, width, col0=0):
    return jnp.concatenate([ref_rows(slice(col0 + h * width, col0 + (h + 1) * width)) for h in range(NH)], axis=0)


def _head_rows(h):
    return slice(h * DN_C, (h + 1) * DN_C)


def _dn_gates4(gbv, d):
    g4 = jnp.concatenate([_pick_lane(gbv, 8 + 4 * d + h) for h in range(NH)], axis=0)
    b4 = jnp.concatenate([_pick_lane(gbv, 4 * d + h) for h in range(NH)], axis=0)
    return g4, b4


def dn_intra_fwd(dqkv, gbc):
    t_len = dqkv.shape[0]
    rb = 256
    n_g = rb // DN_C

    def body(qk_ref, v_ref, gb_ref, u_ref, wqk_ref, at_ref, ti_ref):
        d = pl.program_id(0)
        rev = d == 1
        for gi in range(n_g):
            rows = slice(gi * DN_C, (gi + 1) * DN_C)
            g4, b4 = _dn_gates4(gb_ref[rows, :], d)
            (u, w, qg, kd, attn), tinv = _dn_intra4(_stack_heads(lambda cs: qk_ref[rows, cs], DK),
                                                    _stack_heads(lambda cs: qk_ref[rows, cs], DK, NH * DK),
                                                    _stack_heads(lambda cs: v_ref[rows, cs], DV), g4, b4, rev)
            at_ref[0, gi] = attn.astype(BF16)
            ti_ref[0, gi] = tinv
            for h in range(NH):
                hr = _head_rows(h)
                u_ref[0, rows, h * DV:(h + 1) * DV] = u[hr]
                wqk_ref[0, rows, h * DK:(h + 1) * DK] = w[hr].astype(BF16)
                wqk_ref[0, rows, NH * DK + h * DK:NH * DK + (h + 1) * DK] = qg[hr].astype(BF16)
                wqk_ref[0, rows, 2 * NH * DK + h * DK:2 * NH * DK + (h + 1) * DK] = kd[hr].astype(BF16)

    return pl.pallas_call(
        body, name="dn_intra_fwd", grid=(2, t_len // rb),
        in_specs=[pl.BlockSpec((rb, D), lambda d, i: (i, 0)), pl.BlockSpec((rb, D), lambda d, i: (i, 1)),
                  pl.BlockSpec((rb, LANE), lambda d, i: (i, 0))],
        out_specs=[pl.BlockSpec((1, rb, D), lambda d, i: (d, i, 0)),
                   pl.BlockSpec((1, rb, DN_W), lambda d, i: (d, i, 0)),
                   pl.BlockSpec((1, n_g, HS, HS), lambda d, i: (d, i, 0, 0)),
                   pl.BlockSpec((1, n_g, HS, HS), lambda d, i: (d, i, 0, 0))],
        out_shape=[jax.ShapeDtypeStruct((2, t_len, D), F32), jax.ShapeDtypeStruct((2, t_len, DN_W), BF16),
                   jax.ShapeDtypeStruct((2, t_len // DN_C, HS, HS), BF16),
                   jax.ShapeDtypeStruct((2, t_len // DN_C, HS, HS), F32)],
        compiler_params=_params(2),
    )(dqkv, dqkv, gbc)


SEQ_G = 4
RET_G = 2


def _dn_seq_inputs(u_ref, wqk_ref, at_ref, gb_ref, d, gidx):
    rows = pl.ds(pl.multiple_of(gidx * DN_C, DN_C), DN_C)
    u = [u_ref[0, rows, h * DV:(h + 1) * DV] for h in range(NH)]
    w = [wqk_ref[0, rows, h * DK:(h + 1) * DK].astype(F32) for h in range(NH)]
    qg = [wqk_ref[0, rows, NH * DK + h * DK:NH * DK + (h + 1) * DK].astype(F32) for h in range(NH)]
    kd = [wqk_ref[0, rows, 2 * NH * DK + h * DK:2 * NH * DK + (h + 1) * DK].astype(F32) for h in range(NH)]
    gbv = gb_ref[rows, :]
    glast = [jnp.sum(_pick_lane(gbv, 8 + 4 * d + h), axis=0, keepdims=True) for h in range(NH)]
    return u, w, qg, kd, at_ref[0, gidx].astype(F32), glast


def dn_seq_fwd(u, wqk, attn, gbc, s0):
    t_len = u.shape[1]
    n_b = t_len // (DN_C * SEQ_G)
    rb = DN_C * SEQ_G

    def block_of(d, t):
        return jnp.where(d == 0, t, n_b - 1 - t)

    def body(u_ref, wqk_ref, at_ref, gb_ref, s0_ref, o_ref, ss_ref, sf_ref, st):
        d, t = pl.program_id(0), pl.program_id(1)

        @pl.when(t == 0)
        def _():
            st[...] = s0_ref[0]

        for gi in range(SEQ_G):
            gidx = jnp.where(d == 0, gi, SEQ_G - 1 - gi)
            rows = pl.ds(pl.multiple_of(gidx * DN_C, DN_C), DN_C)
            s_in = [st[h] for h in range(NH)]
            ss_ref[0, gidx] = st[...].astype(BF16)
            o, s2 = _dn_seq4(*_dn_seq_inputs(u_ref, wqk_ref, at_ref, gb_ref, d, gidx), s_in)
            for h in range(NH):
                o_ref[0, rows, h * DV:(h + 1) * DV] = o[_head_rows(h)]
                st[h] = s2[h]

        @pl.when(t == n_b - 1)
        def _():
            sf_ref[0] = st[...]

    return pl.pallas_call(
        body, name="dn_seq_fwd", grid=(2, n_b),
        in_specs=[pl.BlockSpec((1, rb, D), lambda d, t: (d, block_of(d, t), 0)),
                  pl.BlockSpec((1, rb, DN_W), lambda d, t: (d, block_of(d, t), 0)),
                  pl.BlockSpec((1, SEQ_G, HS, HS), lambda d, t: (d, block_of(d, t), 0, 0)),
                  pl.BlockSpec((rb, LANE), lambda d, t: (block_of(d, t), 0)),
                  _state_spec()],
        out_specs=[pl.BlockSpec((1, rb, D), lambda d, t: (d, block_of(d, t), 0)),
                   pl.BlockSpec((1, SEQ_G, NH, DK, DV), lambda d, t: (d, block_of(d, t), 0, 0, 0)),
                   _state_spec()],
        out_shape=[jax.ShapeDtypeStruct((2, t_len, D), F32),
                   jax.ShapeDtypeStruct((2, t_len // DN_C, NH, DK, DV), BF16),
                   jax.ShapeDtypeStruct((2, NH, DK, DV), F32)],
        scratch_shapes=[pltpu.VMEM((NH, DK, DV), F32)],
        compiler_params=_params(2),
    )(u, wqk, attn, gbc, s0)


def dn_seq_bwd(u, wqk, attn, gbc, ssave, do, ds_fin):
    t_len = u.shape[1]
    n_c = t_len // DN_C
    n_b = n_c // SEQ_G
    rb = DN_C * SEQ_G

    def block_of(d, t):
        return jnp.where(d == 0, n_b - 1 - t, t)

    def body(u_ref, wqk_ref, at_ref, gb_ref, ss_ref, do_ref, dsf_ref, du_ref, dwqk_ref, dat_ref, dgl_ref, ds0_ref, dst):
        d, t = pl.program_id(0), pl.program_id(1)

        @pl.when(t == 0)
        def _():
            dst[...] = dsf_ref[0]

        rows8 = lax.broadcasted_iota(jnp.int32, (8, LANE), 0)
        for gi in range(SEQ_G):
            gidx = jnp.where(d == 0, SEQ_G - 1 - gi, gi)
            rows = pl.ds(pl.multiple_of(gidx * DN_C, DN_C), DN_C)
            s_in = [ss_ref[0, gidx, h].astype(F32) for h in range(NH)]
            _, vjp = jax.vjp(_dn_seq4, *_dn_seq_inputs(u_ref, wqk_ref, at_ref, gb_ref, d, gidx), s_in)
            do4 = _stack_heads(lambda cs: do_ref[rows, cs], DV)
            du, dw, dqg, dkd, dat, dgl, ds = vjp((do4, [dst[h] for h in range(NH)]))
            dat_ref[0, gidx] = dat
            dgl_tile = jnp.zeros((8, LANE), F32)
            for h in range(NH):
                du_ref[0, rows, h * DV:(h + 1) * DV] = du[h]
                dwqk_ref[0, rows, h * DK:(h + 1) * DK] = dw[h]
                dwqk_ref[0, rows, NH * DK + h * DK:NH * DK + (h + 1) * DK] = dqg[h]
                dwqk_ref[0, rows, 2 * NH * DK + h * DK:2 * NH * DK + (h + 1) * DK] = dkd[h]
                dgl_tile = jnp.where(rows8 == h, dgl[h], dgl_tile)
                dst[h] = ds[h]
            dgl_ref[0, gidx] = dgl_tile

        @pl.when(t == n_b - 1)
        def _():
            ds0_ref[0] = dst[...]

    seq = lambda width: pl.BlockSpec((1, rb, width), lambda d, t: (d, block_of(d, t), 0))
    att = pl.BlockSpec((1, SEQ_G, HS, HS), lambda d, t: (d, block_of(d, t), 0, 0))
    return pl.pallas_call(
        body, name="dn_seq_bwd", grid=(2, n_b),
        in_specs=[seq(D), seq(DN_W), att, pl.BlockSpec((rb, LANE), lambda d, t: (block_of(d, t), 0)),
                  pl.BlockSpec((1, SEQ_G, NH, DK, DV), lambda d, t: (d, block_of(d, t), 0, 0, 0)),
                  pl.BlockSpec((rb, D), lambda d, t: (block_of(d, t), 0)), _state_spec()],
        out_specs=[seq(D), seq(DN_W), att, pl.BlockSpec((1, SEQ_G, 8, LANE), lambda d, t: (d, block_of(d, t), 0, 0)),
                   _state_spec()],
        out_shape=[jax.ShapeDtypeStruct((2, t_len, D), F32), jax.ShapeDtypeStruct((2, t_len, DN_W), F32),
                   jax.ShapeDtypeStruct((2, n_c, HS, HS), F32), jax.ShapeDtypeStruct((2, n_c, 8, LANE), F32),
                   jax.ShapeDtypeStruct((2, NH, DK, DV), F32)],
        scratch_shapes=[pltpu.VMEM((NH, DK, DV), F32)],
        compiler_params=_params(2),
    )(u, wqk, attn, gbc, ssave, do, ds_fin)


def dn_intra_bwd(dqkv, gbc, tinv, du, dwqk, dattn, dgl):
    t_len = dqkv.shape[0]
    rb = 128
    n_g = rb // DN_C

    def body(qk_ref, v_ref, gb_ref, ti_ref, du_ref, dwqk_ref, dat_ref, dgl_ref, dqk_ref, dv_ref, dgb_ref):
        d = pl.program_id(0)
        rev = d == 1
        for gi in range(n_g):
            rows = slice(gi * DN_C, (gi + 1) * DN_C)
            g4, b4 = _dn_gates4(gb_ref[rows, :], d)
            _, vjp, _ = jax.vjp(functools.partial(_dn_intra4, rev=rev, tinv_known=ti_ref[0, gi]),
                                _stack_heads(lambda cs: qk_ref[rows, cs], DK),
                                _stack_heads(lambda cs: qk_ref[rows, cs], DK, NH * DK),
                                _stack_heads(lambda cs: v_ref[rows, cs], DV), g4, b4, has_aux=True)
            dq, dk, dv, dg, db = vjp((_stack_heads(lambda cs: du_ref[0, rows, cs], DV),
                                      _stack_heads(lambda cs: dwqk_ref[0, rows, cs], DK),
                                      _stack_heads(lambda cs: dwqk_ref[0, rows, cs], DK, NH * DK),
                                      _stack_heads(lambda cs: dwqk_ref[0, rows, cs], DK, 2 * NH * DK),
                                      dat_ref[0, gi]))
            dgb = jnp.zeros((DN_C, LANE), F32)
            for h in range(NH):
                hr = _head_rows(h)
                dqk_ref[0, rows, h * DK:(h + 1) * DK] = dq[hr]
                dqk_ref[0, rows, NH * DK + h * DK:NH * DK + (h + 1) * DK] = dk[hr]
                dv_ref[0, rows, h * DV:(h + 1) * DV] = dv[hr]
                dgb = (dgb + _put_lane(dg[hr] + dgl_ref[0, gi, h:h + 1, 0:1], 8 + 4 * d + h, LANE)
                       + _put_lane(db[hr], 4 * d + h, LANE))
            dgb_ref[0, rows, :] = dgb

    seq = lambda width: pl.BlockSpec((1, rb, width), lambda d, i: (d, i, 0))
    return pl.pallas_call(
        body, name="dn_intra_bwd", grid=(2, t_len // rb),
        in_specs=[pl.BlockSpec((rb, D), lambda d, i: (i, 0)), pl.BlockSpec((rb, D), lambda d, i: (i, 1)),
                  pl.BlockSpec((rb, LANE), lambda d, i: (i, 0)),
                  pl.BlockSpec((1, n_g, HS, HS), lambda d, i: (d, i, 0, 0)), seq(D), seq(DN_W),
                  pl.BlockSpec((1, n_g, HS, HS), lambda d, i: (d, i, 0, 0)),
                  pl.BlockSpec((1, n_g, 8, LANE), lambda d, i: (d, i, 0, 0))],
        out_specs=[seq(D), seq(D), seq(LANE)],
        out_shape=[jax.ShapeDtypeStruct((2, t_len, D), F32), jax.ShapeDtypeStruct((2, t_len, D), F32),
                   jax.ShapeDtypeStruct((2, t_len, LANE), F32)],
        compiler_params=_params(2),
    )(dqkv, dqkv, gbc, tinv, du, dwqk, dattn, dgl)


def _pick_lane(x, lane_idx):
    lane = lax.broadcasted_iota(jnp.int32, x.shape, 1)
    return jnp.sum(jnp.where(lane == lane_idx, x, 0.0), axis=1, keepdims=True)


def _put_lane(col, lane_idx, width):
    lane = lax.broadcasted_iota(jnp.int32, (col.shape[0], width), 1)
    return jnp.where(lane == lane_idx, col, 0.0)


def _state_spec():
    return pl.BlockSpec((1, NH, DK, DV), lambda d, t: (d, 0, 0, 0))


def ret_fwd(rqk, p, s0):
    c = RET_C
    t_len = rqk.shape[0]
    n_c = t_len // c
    n_b = n_c // RET_G
    rb = c * RET_G

    def block_of(d, t):
        return jnp.where(d == 0, t, n_b - 1 - t)

    def body(qk_ref, v_ref, s0_ref, o_ref, ss_ref, sf_ref, st):
        d, t = pl.program_id(0), pl.program_id(1)
        rev = d == 1

        @pl.when(t == 0)
        def _():
            st[...] = s0_ref[0]

        for gi in range(RET_G):
            gidx = jnp.where(d == 0, gi, RET_G - 1 - gi)
            rows = pl.ds(pl.multiple_of(gidx * c, c), c)
            ss_ref[0, gidx] = st[...].astype(BF16)
            for h in range(NH):
                o, s2 = _ret_chunk(qk_ref[rows, h * DK:(h + 1) * DK],
                                   qk_ref[rows, NH * DK + h * DK:NH * DK + (h + 1) * DK],
                                   v_ref[rows, h * DV:(h + 1) * DV], st[h], _log_gamma(h), rev)
                o_ref[0, rows, h * DV:(h + 1) * DV] = o
                st[h] = s2

        @pl.when(t == n_b - 1)
        def _():
            sf_ref[0] = st[...]

    return pl.pallas_call(
        body, name="ret_fwd", grid=(2, n_b),
        in_specs=[pl.BlockSpec((rb, D), lambda d, t: (block_of(d, t), 0)),
                  pl.BlockSpec((rb, D), lambda d, t: (block_of(d, t), C_RV)),
                  _state_spec()],
        out_specs=[pl.BlockSpec((1, rb, D), lambda d, t: (d, block_of(d, t), 0)),
                   pl.BlockSpec((1, RET_G, NH, DK, DV), lambda d, t: (d, block_of(d, t), 0, 0, 0)),
                   _state_spec()],
        out_shape=[jax.ShapeDtypeStruct((2, t_len, D), F32),
                   jax.ShapeDtypeStruct((2, n_c, NH, DK, DV), BF16),
                   jax.ShapeDtypeStruct((2, NH, DK, DV), F32)],
        scratch_shapes=[pltpu.VMEM((NH, DK, DV), F32)],
        compiler_params=_params(2),
    )(rqk, p, s0)


def ret_bwd(rqk, p, ssave, do, ds_fin):
    c = RET_C
    t_len = rqk.shape[0]
    n_b = t_len // (c * RET_G)
    rb = c * RET_G

    def block_of(d, t):
        return jnp.where(d == 0, n_b - 1 - t, t)

    def body(qk_ref, v_ref, ss_ref, do_ref, dsf_ref, dqk_ref, dv_ref, ds0_ref, dst):
        d, t = pl.program_id(0), pl.program_id(1)
        rev = d == 1

        @pl.when(t == 0)
        def _():
            dst[...] = dsf_ref[0]

        for gi in range(RET_G):
            gidx = jnp.where(d == 0, RET_G - 1 - gi, gi)
            rows = pl.ds(pl.multiple_of(gidx * c, c), c)
            for h in range(NH):
                _, vjp = jax.vjp(functools.partial(_ret_chunk, lg=_log_gamma(h), rev=rev),
                                 qk_ref[rows, h * DK:(h + 1) * DK],
                                 qk_ref[rows, NH * DK + h * DK:NH * DK + (h + 1) * DK],
                                 v_ref[rows, h * DV:(h + 1) * DV], ss_ref[0, gidx, h].astype(F32))
                dq, dk, dv, ds = vjp((do_ref[rows, h * DV:(h + 1) * DV], dst[h]))
                dqk_ref[0, rows, h * DK:(h + 1) * DK] = dq
                dqk_ref[0, rows, NH * DK + h * DK:NH * DK + (h + 1) * DK] = dk
                dv_ref[0, rows, h * DV:(h + 1) * DV] = dv
                dst[h] = ds

        @pl.when(t == n_b - 1)
        def _():
            ds0_ref[0] = dst[...]

    seq_spec = pl.BlockSpec((1, rb, D), lambda d, t: (d, block_of(d, t), 0))
    return pl.pallas_call(
        body, name="ret_bwd", grid=(2, n_b),
        in_specs=[pl.BlockSpec((rb, D), lambda d, t: (block_of(d, t), 0)),
                  pl.BlockSpec((rb, D), lambda d, t: (block_of(d, t), C_RV)),
                  pl.BlockSpec((1, RET_G, NH, DK, DV), lambda d, t: (d, block_of(d, t), 0, 0, 0)),
                  pl.BlockSpec((rb, D), lambda d, t: (block_of(d, t), 0)),
                  _state_spec()],
        out_specs=[seq_spec, seq_spec, _state_spec()],
        out_shape=[jax.ShapeDtypeStruct((2, t_len, D), F32)] * 2 + [jax.ShapeDtypeStruct((2, NH, DK, DV), F32)],
        scratch_shapes=[pltpu.VMEM((NH, DK, DV), F32)],
        compiler_params=_params(2),
    )(rqk, p, ssave, do, ds_fin)


def _head_gate(o_ret, o_dn, rg, dz, nw):
    ret = o_ret * lax.rsqrt(jnp.mean(o_ret * o_ret, axis=-1, keepdims=True) + EPS) * (rg * _sigmoid(rg))
    dn = o_dn * lax.rsqrt(jnp.mean(o_dn * o_dn, axis=-1, keepdims=True) + EPS) * nw * (dz * _sigmoid(dz))
    return ret, dn


MIX_TM = 256


def _mix_specs():
    seq = lambda col: pl.BlockSpec((MIX_TM, D), functools.partial(lambda i, col: (i, col), col=col))
    pair = pl.BlockSpec((2, MIX_TM, D), lambda i: (0, i, 0))
    wfull = pl.BlockSpec((D, D), lambda i: (0, 0))
    vec = pl.BlockSpec((8, D), lambda i: (0, 0))
    return seq, pair, wfull, vec


def mixout_fwd(x1, vec, o_ret, o_dn, p, w_r, w_d, w_o):
    t_len = x1.shape[0]

    def body(x_ref, vec_ref, or_ref, od_ref, rg_ref, dz_ref, ga_ref, gb_ref, wr_ref, wd_ref, wo_ref,
             x2_ref, ret_ref, dn_ref, y_ref, yr_ref, yd_ref, z_ref):
        for h in range(NH):
            sl = slice(h * DV, (h + 1) * DV)
            ret, dn = _head_gate(or_ref[0, :, sl] + or_ref[1, :, sl], od_ref[0, :, sl] + od_ref[1, :, sl],
                                 rg_ref[:, sl], dz_ref[:, sl], vec_ref[1:2, sl])
            ret_ref[:, sl] = ret.astype(BF16)
            dn_ref[:, sl] = dn.astype(BF16)
        yr = jnp.dot(ret_ref[...], wr_ref[...], preferred_element_type=F32)
        yd = jnp.dot(dn_ref[...], wd_ref[...], preferred_element_type=F32)
        yr_ref[...] = yr.astype(BF16)
        yd_ref[...] = yd.astype(BF16)
        y = (_sigmoid(ga_ref[...]) * yr + _sigmoid(gb_ref[...]) * yd).astype(BF16)
        y_ref[...] = y
        z = jnp.dot(y, wo_ref[...], preferred_element_type=F32)
        z_ref[...] = z.astype(BF16)
        x2_ref[...] = x_ref[...] + vec_ref[0:1, :] * z

    seq, pair, wfull, vecs = _mix_specs()
    return pl.pallas_call(
        body, name="mixout_fwd", grid=(t_len // MIX_TM,),
        in_specs=[seq(0), vecs, pair, pair, seq(C_RG), seq(C_DZ), seq(C_GA), seq(C_GB), wfull, wfull, wfull],
        out_specs=[seq(0)] * 7,
        out_shape=[jax.ShapeDtypeStruct((t_len, D), F32)] + [jax.ShapeDtypeStruct((t_len, D), BF16)] * 6,
        compiler_params=_params(1),
    )(x1, vec, o_ret, o_dn, p, p, p, p, w_r, w_d, w_o)


def mixout_bwd(dx2, vec, o_ret, o_dn, p, yr, yd, z, w_r, w_d, w_o):
    t_len = dx2.shape[0]

    def body(dx_ref, vec_ref, or_ref, od_ref, rg_ref, dz_ref, ga_ref, gb_ref, yr_ref, yd_ref, z_ref,
             wr_ref, wd_ref, wo_ref,
             dor_ref, dod_ref, drg_ref, ddz_ref, dga_ref, dgb_ref, dyr_ref, dyd_ref, dzz_ref, pg_ref):
        @pl.when(pl.program_id(0) == 0)
        def _():
            pg_ref[...] = jnp.zeros_like(pg_ref)

        dx = dx_ref[...]
        pg_ref[0:1, :] += jnp.sum(dx * z_ref[...].astype(F32), axis=0, keepdims=True)
        dzz = (vec_ref[0:1, :] * dx).astype(BF16)
        dzz_ref[...] = dzz
        dy = _dot(dzz, wo_ref[...], NT)
        sa = _sigmoid(ga_ref[...])
        sb = _sigmoid(gb_ref[...])
        dyr = (dy * sa).astype(BF16)
        dyd = (dy * sb).astype(BF16)
        dyr_ref[...] = dyr
        dyd_ref[...] = dyd
        dga_ref[...] = (dy * yr_ref[...].astype(F32) * sa * (1.0 - sa)).astype(BF16)
        dgb_ref[...] = (dy * yd_ref[...].astype(F32) * sb * (1.0 - sb)).astype(BF16)
        dret = _dot(dyr, wr_ref[...], NT)
        ddn = _dot(dyd, wd_ref[...], NT)
        for h in range(NH):
            sl = slice(h * DV, (h + 1) * DV)
            _, vjp = jax.vjp(_head_gate, or_ref[0, :, sl] + or_ref[1, :, sl], od_ref[0, :, sl] + od_ref[1, :, sl],
                             rg_ref[:, sl], dz_ref[:, sl], vec_ref[1:2, sl])
            d_or, d_od, d_rg, d_dz, d_nw = vjp((dret[:, sl], ddn[:, sl]))
            dor_ref[:, sl] = d_or
            dod_ref[:, sl] = d_od
            drg_ref[:, sl] = d_rg.astype(BF16)
            ddz_ref[:, sl] = d_dz.astype(BF16)
            pg_ref[1:2, sl] += d_nw

    seq, pair, wfull, vecs = _mix_specs()
    return pl.pallas_call(
        body, name="mixout_bwd", grid=(t_len // MIX_TM,),
        in_specs=[seq(0), vecs, pair, pair, seq(C_RG), seq(C_DZ), seq(C_GA), seq(C_GB), seq(0), seq(0), seq(0),
                  wfull, wfull, wfull],
        out_specs=[seq(0)] * 9 + [vecs],
        out_shape=[jax.ShapeDtypeStruct((t_len, D), F32)] * 2 + [jax.ShapeDtypeStruct((t_len, D), BF16)] * 7
        + [jax.ShapeDtypeStruct((8, D), F32)],
        compiler_params=_params(1),
    )(dx2, vec, o_ret, o_dn, p, p, p, p, yr, yd, z, w_r, w_d, w_o)


def _final_loss(x, w, target):
    y = x * lax.rsqrt(jnp.mean(x * x, axis=-1, keepdims=True) + EPS) * w
    err = y - target
    return 0.5 * jnp.sum(jnp.mean(err * err, axis=-1, keepdims=True), axis=0, keepdims=True)


def final_fwd_bwd(x, vec, target):
    t_len = x.shape[0]
    tm = _row_block(t_len)

    def body(x_ref, vec_ref, t_ref, dx_ref, pg_ref, loss_ref):
        @pl.when(pl.program_id(0) == 0)
        def _():
            pg_ref[...] = jnp.zeros_like(pg_ref)
            loss_ref[...] = jnp.zeros_like(loss_ref)

        loss, vjp = jax.vjp(functools.partial(_final_loss, target=t_ref[...]), x_ref[...], vec_ref[0:1, :])
        dx, dw = vjp(jnp.ones((1, 1), F32))
        dx_ref[...] = dx
        pg_ref[0:1, :] += dw
        loss_ref[...] += jnp.broadcast_to(loss, loss_ref.shape)

    return pl.pallas_call(
        body, name="final_fwd_bwd", grid=(t_len // tm,),
        in_specs=[pl.BlockSpec((tm, D), lambda i: (i, 0)), pl.BlockSpec((8, D), lambda i: (0, 0)),
                  pl.BlockSpec((tm, D), lambda i: (i, 0))],
        out_specs=[pl.BlockSpec((tm, D), lambda i: (i, 0)), pl.BlockSpec((8, D), lambda i: (0, 0)),
                   pl.BlockSpec((8, LANE), lambda i: (0, 0))],
        out_shape=[jax.ShapeDtypeStruct((t_len, D), F32), jax.ShapeDtypeStruct((8, D), F32),
                   jax.ShapeDtypeStruct((8, LANE), F32)],
        compiler_params=_params(1),
    )(x, vec, target)


def _vec8(*rows):
    rows = list(rows) + [jnp.zeros((D,), F32)] * (8 - len(rows))
    return jnp.stack(rows)


def _layer_vecs(lw, m):
    nw = lw["norm_w"]
    return dict(ffn1=_vec8(nw[0], m[0], m[1], m[2]), proj=_vec8(nw[1], m[3], m[4]),
                mix=_vec8(m[5], lw["dn_norm_t"]), ffn2=_vec8(nw[2], m[6], m[7], m[8]))


def _rope_tables(t_len, grid_w=64, base=10000.0):
    n_freq = DK // 4
    inv = base ** (-jnp.arange(n_freq, dtype=F32) / n_freq)
    tok = jnp.arange(t_len)
    ang = jnp.concatenate([(tok // grid_w).astype(F32)[:, None] * inv, (tok % grid_w).astype(F32)[:, None] * inv],
                          axis=-1)
    cos, sin = jnp.cos(ang), jnp.sin(ang)
    return jnp.concatenate([cos, cos], axis=-1), jnp.concatenate([-sin, sin], axis=-1)


def _stream_fwd_a(x0, lw, vecs, rope):
    x1, h1, act1, sl1, mg1, f1 = ffn_fwd(x0, vecs["ffn1"], lw["wgu1"], lw["wd1"])
    p, h2 = proj_fwd(x1, vecs["proj"], lw["w_in"])
    rqk, dqkv, gbc = feat_fwd(p, rope[0], rope[1], lw["conv_w8"], lw["gparams"])
    return dict(x0=x0, h1=h1, act1=act1, sl1=sl1, mg1=mg1, f1=f1, x1=x1, h2=h2, p=p, rqk=rqk, dqkv=dqkv, gbc=gbc)


def _stream_mix(sv, s0_ret, s0_dn):
    sv["o_ret"], sv["ss_ret"], sf_ret = ret_fwd(sv["rqk"], sv["p"], s0_ret)
    sv["dn_u"], sv["dn_wqk"], sv["dn_attn"], sv["dn_tinv"] = dn_intra_fwd(sv["dqkv"], sv["gbc"])
    sv["o_dn"], sv["ss_dn"], sf_dn = dn_seq_fwd(sv["dn_u"], sv["dn_wqk"], sv["dn_attn"], sv["gbc"], s0_dn)
    return sf_ret, sf_dn


def _stream_fwd_b(sv, lw, vecs):
    x2, ret, dn, y, yr, yd, z = mixout_fwd(sv["x1"], vecs["mix"], sv["o_ret"], sv["o_dn"], sv["p"],
                                           lw["w_r"], lw["w_d"], lw["w_o"])
    x3, h3, act3, sl3, mg3, f3 = ffn_fwd(x2, vecs["ffn2"], lw["wgu2"], lw["wd2"])
    sv.update(ret=ret, dn=dn, y=y, yr=yr, yd=yd, z=z, x2=x2, h3=h3, act3=act3, sl3=sl3, mg3=mg3, f3=f3)
    return x3


def layer_fwd(xs, cs, lw, ropes):
    vx, vc = _layer_vecs(lw, lw["mx"]), _layer_vecs(lw, lw["mc"])
    sx = _stream_fwd_a(xs, lw, vx, ropes[0])
    sc = _stream_fwd_a(cs, lw, vc, ropes[1])
    zero = jnp.zeros((2, NH, DK, DV), F32)
    sf_ret, sf_dn = _stream_mix(sc, zero, zero)
    _stream_mix(sx, sf_ret, sf_dn)
    x3 = _stream_fwd_b(sx, lw, vx)
    c3 = _stream_fwd_b(sc, lw, vc)
    return x3, c3, (sx, sc)


def _stream_bwd_a(dx3, sv, lw, vecs):
    dg3, du3, dy3, pg_a = ffn_bwd_act(dx3, vecs["ffn2"], sv["sl3"], sv["mg3"], sv["f3"], lw["wd2"])
    dx2, pg_b = nt_norm_bwd([dg3, du3], lw["wgu2"], [0, 2], FFN_HB, 2, sv["x2"], dx3, vecs["ffn2"], "ffn_bwd_in")
    (dor, dod, drg, ddz, dga, dgb, dyr, dyd, dzz, pg_m) = mixout_bwd(
        dx2, vecs["mix"], sv["o_ret"], sv["o_dn"], sv["p"], sv["yr"], sv["yd"], sv["z"], lw["w_r"], lw["w_d"], lw["w_o"])
    return dict(dg3=dg3, du3=du3, dy3=dy3, pg_a2=pg_a, pg_b2=pg_b, dx2=dx2, dor=dor, dod=dod,
                drg=drg, ddz=ddz, dga=dga, dgb=dgb, dyr=dyr, dyd=dyd, dzz=dzz, pg_m=pg_m)


def _stream_bwd_mix(bw, sv, dsf_ret, dsf_dn):
    bw["dqk_r"], bw["drv"], ds0_ret = ret_bwd(sv["rqk"], sv["p"], sv["ss_ret"], bw["dor"], dsf_ret)
    du, dwqk, dattn, dgl, ds0_dn = dn_seq_bwd(sv["dn_u"], sv["dn_wqk"], sv["dn_attn"], sv["gbc"], sv["ss_dn"],
                                              bw["dod"], dsf_dn)
    bw["dqk_d"], bw["dvv_d"], bw["dgbc"] = dn_intra_bwd(sv["dqkv"], sv["gbc"], sv["dn_tinv"], du, dwqk, dattn, dgl)
    return ds0_ret, ds0_dn


def _stream_bwd_b(bw, sv, lw, vecs, rope):
    dp_rqk, dp_rv, dp_dqk, dp_dvv, dp_dba, bw["dcw"], bw["dgp"] = feat_bwd(
        sv["p"], rope[0], rope[1], lw["conv_w8"], lw["gparams"], bw["dqk_r"], bw["drv"], bw["dqk_d"], bw["dvv_d"],
        bw["dgbc"])
    bw["dp_segs"] = [dp_rqk, dp_rv, bw["drg"], dp_dqk, dp_dvv, bw["ddz"], bw["dga"], bw["dgb"]]
    bw["dp_dba"] = dp_dba
    dx1, bw["pg_p"] = proj_bwd_in(bw["dp_segs"], dp_dba, lw["w_in"], sv["x1"], bw["dx2"], vecs["proj"])
    bw["dg1"], bw["du1"], bw["dy1"], bw["pg_a1"] = ffn_bwd_act(dx1, vecs["ffn1"], sv["sl1"], sv["mg1"], sv["f1"],
                                                              lw["wd1"])
    dx0, bw["pg_b1"] = nt_norm_bwd([bw["dg1"], bw["du1"]], lw["wgu1"], [0, 2], FFN_HB, 2, sv["x0"], dx1,
                                   vecs["ffn1"], "ffn_bwd_in")
    return dx0


def _stream_pgrads(bw):
    a1, b1, pp, pm, a2, b2 = bw["pg_a1"], bw["pg_b1"], bw["pg_p"], bw["pg_m"], bw["pg_a2"], bw["pg_b2"]
    dmod = jnp.stack([b1[1], b1[2], a1[3], pp[1], pp[2], pm[0], b2[1], b2[2], a2[3]])
    dnorm = jnp.stack([b1[0], pp[0], b2[0]])
    ddnw = pm[1].reshape(NH, DV).sum(axis=0)
    return dmod, dnorm, ddnw, bw["dcw"][:CONV_K], bw["dgp"][0, 8:16], bw["dgp"][1, 8:16]


def layer_bwd(dx3, dc3, lw, saved, ropes):
    sx, sc = saved
    vx, vc = _layer_vecs(lw, lw["mx"]), _layer_vecs(lw, lw["mc"])
    bx = _stream_bwd_a(dx3, sx, lw, vx)
    bc = _stream_bwd_a(dc3, sc, lw, vc)
    zero = jnp.zeros((2, NH, DK, DV), F32)
    ds0_ret, ds0_dn = _stream_bwd_mix(bx, sx, zero, zero)
    _stream_bwd_mix(bc, sc, ds0_ret, ds0_dn)
    dx0 = _stream_bwd_b(bx, sx, lw, vx, ropes[0])
    dc0 = _stream_bwd_b(bc, sc, lw, vc, ropes[1])

    def wgrad(a, b, tn, name):
        ax = sx[a] if a in sx else bx[a]
        ac = sc[a] if a in sc else bc[a]
        return tn_matmul(ax, bx[b], ac, bc[b], tn, name)

    gw = dict(
        wg1=wgrad("h1", "dg1", FFN_HB, "dw_ffn_gu"), wu1=wgrad("h1", "du1", FFN_HB, "dw_ffn_gu"),
        wd1=wgrad("act1", "dy1", 512, "dw_ffn_d"),
        w_in=jnp.concatenate([dw_in(sx["h2"], bx["dp_segs"], sc["h2"], bc["dp_segs"]),
                              wgrad("h2", "dp_dba", LANE, "dw_in_gate")], axis=-1),
        w_r=wgrad("ret", "dyr", 512, "dw_sq"), w_d=wgrad("dn", "dyd", 512, "dw_sq"), w_o=wgrad("y", "dzz", 512, "dw_sq"),
        wg2=wgrad("h3", "dg3", FFN_HB, "dw_ffn_gu"), wu2=wgrad("h3", "du3", FFN_HB, "dw_ffn_gu"),
        wd2=wgrad("act3", "dy3", 512, "dw_ffn_d"),
    )
    px, pc = _stream_pgrads(bx), _stream_pgrads(bc)
    small = dict(dmx=px[0], dmc=pc[0], norm_w=px[1] + pc[1], dn_norm_w=px[2] + pc[2], conv_w=px[3] + pc[3],
                 a_log=px[4] + pc[4], dt_bias=px[5] + pc[5])
    return dx0, dc0, gw, small


def local_step(x, ctx, target, final_norm_w, lws):
    t_len, t_ctx = x.shape[0], ctx.shape[0]
    ropes = (_rope_tables(t_len), (jnp.ones((t_ctx, LANE), F32), jnp.zeros((t_ctx, LANE), F32)))
    xs, cs, saved = x, ctx, []
    for lw in lws:
        xs, cs, sv = layer_fwd(xs, cs, lw, ropes)
        saved.append(sv)
    dx, pg_f, loss = final_fwd_bwd(xs, _vec8(final_norm_w), target)
    dc = jnp.zeros_like(ctx)
    gws, smalls = [None] * len(lws), [None] * len(lws)
    for l in reversed(range(len(lws))):
        dx, dc, gws[l], smalls[l] = layer_bwd(dx, dc, lws[l], saved[l], ropes)
    return loss[0, 0], dx, gws, smalls, pg_f[0]


ADA_ROWS = 16
ADA_COLS = N_MOD * D // N_DEV


def ada_fwd(cc, ada_w, ada_b_cols):
    def body(cc_ref, w_ref, b_ref, o_ref):
        cv = cc_ref[...]
        o_ref[0] = _dot_hi(cv * _sigmoid(cv), w_ref[0]) + b_ref[0]

    return pl.pallas_call(
        body, name="ada_fwd", grid=(DEPTH,),
        in_specs=[pl.BlockSpec((ADA_ROWS, D), lambda l: (0, 0)), pl.BlockSpec((1, D, ADA_COLS), lambda l: (l, 0, 0)),
                  pl.BlockSpec((1, 1, ADA_COLS), lambda l: (l, 0, 0))],
        out_specs=pl.BlockSpec((1, ADA_ROWS, ADA_COLS), lambda l: (l, 0, 0)),
        out_shape=jax.ShapeDtypeStruct((DEPTH, ADA_ROWS, ADA_COLS), F32),
        compiler_params=_params(1),
    )(cc, ada_w, ada_b_cols)


def ada_bwd(cc, dmod, ada_w):
    def body(cc_ref, d_ref, w_ref, gw_ref, ds_ref):
        cv = cc_ref[...]
        gw_ref[0] = _dot_hi(cv * _sigmoid(cv), d_ref[0], TN)
        ds_ref[0] = _dot_hi(d_ref[0], w_ref[0], NT)

    return pl.pallas_call(
        body, name="ada_bwd", grid=(DEPTH,),
        in_specs=[pl.BlockSpec((ADA_ROWS, D), lambda l: (0, 0)),
                  pl.BlockSpec((1, ADA_ROWS, ADA_COLS), lambda l: (l, 0, 0)),
                  pl.BlockSpec((1, D, ADA_COLS), lambda l: (l, 0, 0))],
        out_specs=[pl.BlockSpec((1, D, ADA_COLS), lambda l: (l, 0, 0)), pl.BlockSpec((1, ADA_ROWS, D), lambda l: (l, 0, 0))],
        out_shape=[jax.ShapeDtypeStruct((DEPTH, D, ADA_COLS), F32), jax.ShapeDtypeStruct((DEPTH, ADA_ROWS, D), F32)],
        compiler_params=_params(1),
    )(cc, dmod, ada_w)


def c_ctx_grad(parts, c_ctx8):
    def body(p_ref, c_ref, o_ref):
        acc = p_ref[0]
        for j in range(1, N_DEV):
            acc = acc + p_ref[j]
        cv = c_ref[...]
        sg = _sigmoid(cv)
        o_ref[...] = acc * (sg + cv * sg * (1.0 - sg))

    return pl.pallas_call(body, name="c_ctx_grad", out_shape=jax.ShapeDtypeStruct((8, LANE), F32))(parts, c_ctx8)


def sum_slots(parts):
    n_slot, rows, _ = parts.shape
    tr = 8
    for cand in (1024, 512, 256, 128, 64, 32, 16, 8):
        if rows % cand == 0:
            tr = cand
            break

    def body(p_ref, o_ref):
        acc = p_ref[0]
        for j in range(1, n_slot):
            acc = acc + p_ref[j]
        o_ref[...] = acc

    return pl.pallas_call(
        body, name="sum_slots", grid=(rows // tr,),
        in_specs=[pl.BlockSpec((n_slot, tr, LANE), lambda i: (0, i, 0))],
        out_specs=pl.BlockSpec((tr, LANE), lambda i: (i, 0)),
        out_shape=jax.ShapeDtypeStruct((rows, LANE), F32),
        compiler_params=_params(1),
    )(parts)


def adamw(parts, w, m, v, rows_blk, name):
    n_slot, n_l, n_a, n_b = parts.shape

    def body(p_ref, w_ref, m_ref, v_ref, g_ref, d_ref, m2_ref, v2_ref):
        g = p_ref[0].astype(F32)
        for j in range(1, n_slot):
            g = g + p_ref[j].astype(F32)
        m2 = ADAM_B1 * m_ref[...] + (1.0 - ADAM_B1) * g
        v2 = ADAM_B2 * v_ref[...] + (1.0 - ADAM_B2) * (g * g)
        m_hat = m2 / (1.0 - ADAM_B1 ** ADAM_STEP)
        v_hat = v2 / (1.0 - ADAM_B2 ** ADAM_STEP)
        g_ref[...] = g
        m2_ref[...] = m2
        v2_ref[...] = v2
        d_ref[...] = -ADAM_LR * (m_hat / (jnp.sqrt(v_hat) + ADAM_EPS) + ADAM_WD * w_ref[...])

    blk = pl.BlockSpec((1, rows_blk, n_b), lambda l, i: (l, i, 0))
    return pl.pallas_call(
        body, name=name, grid=(n_l, n_a // rows_blk),
        in_specs=[pl.BlockSpec((n_slot, 1, rows_blk, n_b), lambda l, i: (0, l, i, 0)), blk, blk, blk],
        out_specs=[blk] * 4,
        out_shape=[jax.ShapeDtypeStruct((n_l, n_a, n_b), F32)] * 4,
        compiler_params=_params(2),
    )(parts, w, m, v)


MESH = pl.DeviceIdType.MESH


def _my_index():
    return 4 * lax.axis_index("x") + 2 * lax.axis_index("y") + lax.axis_index("c")


def all_gather(shards, name):
    n = len(shards)

    def body(*refs):
        x_refs, out_refs = refs[:n], refs[n:2 * n]
        send_sems, recv_sems, local_sems = refs[2 * n:]
        x, y, c = lax.axis_index("x"), lax.axis_index("y"), lax.axis_index("c")
        me, sibling = (x, y, c), (x, y, 1 - c)
        chips = [(1 - x, y), (x, 1 - y), (1 - x, 1 - y)]

        def slot(a, px, py, pc):
            return out_refs[a].at[4 * px + 2 * py + pc]

        def copy(a, k, block, to, src=None):
            return pltpu.make_async_remote_copy(
                src_ref=slot(a, *block) if src is None else src, dst_ref=slot(a, *block),
                send_sem=send_sems.at[7 * a + k], recv_sem=recv_sems.at[7 * a + k], device_id=to, device_id_type=MESH)

        mine = [pltpu.make_async_copy(x_refs[a], slot(a, *me), local_sems.at[a]) for a in range(n)]
        for cp in mine:
            cp.start()
        first = []
        for a in range(n):
            first.append(copy(a, 0, me, sibling, src=x_refs[a]))
            first += [copy(a, 1 + j, me, (*chip, c), src=x_refs[a]) for j, chip in enumerate(chips)]
        for cp in first:
            cp.start()
        passed = []
        for j, chip in enumerate(chips):
            for a in range(n):
                copy(a, 1 + j, (*chip, c), me).wait_recv()
                fwd = copy(a, 4 + j, (*chip, c), sibling)
                fwd.start()
                passed.append(fwd)
        for a in range(n):
            copy(a, 0, sibling, me).wait_recv()
            for j, chip in enumerate(chips):
                copy(a, 4 + j, (*chip, 1 - c), me).wait_recv()
        for cp in first + passed:
            cp.wait_send()
        for cp in mine:
            cp.wait()

    return pl.pallas_call(
        body, name=name,
        in_specs=[pl.BlockSpec(memory_space=pl.ANY)] * n,
        out_specs=[pl.BlockSpec(memory_space=pl.ANY)] * n,
        out_shape=[jax.ShapeDtypeStruct((N_DEV,) + s.shape, s.dtype) for s in shards],
        scratch_shapes=[pltpu.SemaphoreType.DMA((7 * n,)), pltpu.SemaphoreType.DMA((7 * n,)),
                        pltpu.SemaphoreType.DMA((n,))],
    )(*shards)


def pair_exchange(arrs, name):
    n = len(arrs)

    def body(*refs):
        x_refs, out_refs = refs[:n], refs[n:2 * n]
        send_sems, recv_sems = refs[2 * n:]
        sibling = (lax.axis_index("x"), lax.axis_index("y"), 1 - lax.axis_index("c"))
        copies = [pltpu.make_async_remote_copy(src_ref=x_refs[a], dst_ref=out_refs[a], send_sem=send_sems.at[a],
                                               recv_sem=recv_sems.at[a], device_id=sibling, device_id_type=MESH)
                  for a in range(n)]
        for cp in copies:
            cp.start()
        for cp in copies:
            cp.wait()

    return pl.pallas_call(
        body, name=name,
        in_specs=[pl.BlockSpec(memory_space=pl.ANY)] * n,
        out_specs=[pl.BlockSpec(memory_space=pl.ANY)] * n,
        out_shape=[jax.ShapeDtypeStruct(s.shape, s.dtype) for s in arrs],
        scratch_shapes=[pltpu.SemaphoreType.DMA((n,)), pltpu.SemaphoreType.DMA((n,))],
    )(*arrs)


def chip_exchange(parts, name):
    n = len(parts)

    def body(*refs):
        p_refs, q_refs = refs[:n], refs[n:2 * n]
        send_sems, recv_sems, local_sems = refs[2 * n:]
        x, y, c = lax.axis_index("x"), lax.axis_index("y"), lax.axis_index("c")
        my_chip = 2 * x + y
        mine = [pltpu.make_async_copy(p_refs[a].at[my_chip], q_refs[a].at[my_chip], local_sems.at[a]) for a in range(n)]
        for cp in mine:
            cp.start()
        copies = []
        for r in range(1, 4):
            px, py = x ^ (r >> 1), y ^ (r & 1)
            peer = 2 * px + py
            for a in range(n):
                k = 3 * a + r - 1
                cp = pltpu.make_async_remote_copy(
                    src_ref=p_refs[a].at[peer], dst_ref=q_refs[a].at[my_chip], send_sem=send_sems.at[k],
                    recv_sem=recv_sems.at[k], device_id=(px, py, c), device_id_type=MESH)
                cp.start()
                copies.append((cp, a, peer, k))
        for cp, a, peer, k in copies:
            pltpu.make_async_remote_copy(
                src_ref=p_refs[a].at[peer], dst_ref=q_refs[a].at[peer], send_sem=send_sems.at[k],
                recv_sem=recv_sems.at[k], device_id=(x, y, c), device_id_type=MESH).wait_recv()
        for cp, _, _, _ in copies:
            cp.wait_send()
        for cp in mine:
            cp.wait()

    return pl.pallas_call(
        body, name=name,
        in_specs=[pl.BlockSpec(memory_space=pl.ANY)] * n,
        out_specs=[pl.BlockSpec(memory_space=pl.ANY)] * n,
        out_shape=[jax.ShapeDtypeStruct(p.shape, p.dtype) for p in parts],
        scratch_shapes=[pltpu.SemaphoreType.DMA((3 * n,)), pltpu.SemaphoreType.DMA((3 * n,)),
                        pltpu.SemaphoreType.DMA((n,))],
    )(*parts)


def add_bf16(a, b, rows_blk, name):
    n_l, n_a, n_b = a.shape

    def body(a_ref, b_ref, o_ref):
        o_ref[...] = (a_ref[...].astype(F32) + b_ref[...].astype(F32)).astype(BF16)

    blk = pl.BlockSpec((1, rows_blk, n_b), lambda l, i: (l, i, 0))
    return pl.pallas_call(
        body, name=name, grid=(n_l, n_a // rows_blk), in_specs=[blk, blk], out_specs=blk,
        out_shape=jax.ShapeDtypeStruct(a.shape, BF16), compiler_params=_params(2),
    )(a, b)


WEIGHT_ORDER = ("c_ctx", "ada_w", "ada_b", "norm_w", "ffn1_wgu", "ffn1_wd", "w_in", "dn_conv_w", "dn_a_log",
                "dn_dt_bias", "dn_norm_w", "w_ret_out", "w_dn_out", "w_o", "ffn2_wgu", "ffn2_wd", "final_norm_w")
BIG = (("ffn1_wgu", "col"), ("ffn1_wd", "row"), ("w_in", "col"), ("w_ret_out", "row"), ("w_dn_out", "row"),
       ("w_o", "row"), ("ffn2_wgu", "col"), ("ffn2_wd", "row"))
LOCAL = ("ada_w", "c_ctx", "ada_b", "norm_w", "dn_conv_w", "dn_a_log", "dn_dt_bias", "dn_norm_w", "final_norm_w")


def _pack(arrs, row_mult, lead=None):
    if lead is None:
        flat = jnp.concatenate([a.reshape(-1) for a in arrs])
        n = flat.shape[0]
    else:
        flat = jnp.concatenate([a.reshape(lead, -1) for a in arrs], axis=1)
        n = flat.shape[1]
    unit = LANE * row_mult
    total = -(-n // unit) * unit
    if lead is None:
        return jnp.pad(flat, (0, total - n)).reshape(total // LANE, LANE)
    return jnp.pad(flat, ((0, 0), (0, total - n))).reshape(lead, total // LANE, LANE)


def _unpack(packed, shapes, lead=None):
    flat = packed.reshape(-1) if lead is None else packed.reshape(lead, -1)
    out, off = [], 0
    for shp in shapes:
        n = math.prod(shp)
        if lead is None:
            out.append(flat[off:off + n].reshape(shp))
        else:
            out.append(flat[:, off:off + n].reshape((lead,) + tuple(shp)))
        off += n
    return out


def _join_shards(g, kind):
    lead = tuple(range(1, g.ndim - 2))
    a, b = g.shape[-2:]
    if kind == "col":
        return g.transpose(*lead, g.ndim - 2, 0, g.ndim - 1).reshape(g.shape[1:-2] + (a, N_DEV * b))
    return g.transpose(*lead, 0, g.ndim - 2, g.ndim - 1).reshape(g.shape[1:-2] + (N_DEV * a, b))


def _split_shards(full, kind):
    a, b = full.shape
    if kind == "col":
        return full.reshape(a, N_DEV, b // N_DEV).transpose(1, 0, 2)
    return full.reshape(N_DEV, a // N_DEV, b)


def _pad_w_in(w):
    return jnp.concatenate([w[..., :6144], w[..., 6160:8208], w[..., 6144:6160],
                            jnp.zeros(w.shape[:-1] + (PW - 8208,), w.dtype)], axis=-1)


def _unpad_w_in(g):
    return jnp.concatenate([g[..., :6144], g[..., 8192:8208], g[..., 6144:8192]], axis=-1)


def kernel(x, c, ctx, c_ctx, ada_w, ada_b, norm_w, ffn1_wgu, ffn1_wd, w_in, dn_conv_w, dn_a_log, dn_dt_bias, dn_norm_w, w_ret_out, w_dn_out, w_o, ffn2_wgu, ffn2_wd, final_norm_w, loss_target, m_c_ctx, m_ada_w, m_ada_b, m_norm_w, m_ffn1_wgu, m_ffn1_wd, m_w_in, m_dn_conv_w, m_dn_a_log, m_dn_dt_bias, m_dn_norm_w, m_w_ret_out, m_w_dn_out, m_w_o, m_ffn2_wgu, m_ffn2_wd, m_final_norm_w, v_c_ctx, v_ada_w, v_ada_b, v_norm_w, v_ffn1_wgu, v_ffn1_wd, v_w_in, v_dn_conv_w, v_dn_a_log, v_dn_dt_bias, v_dn_norm_w, v_w_ret_out, v_w_dn_out, v_w_o, v_ffn2_wgu, v_ffn2_wd, v_final_norm_w):
    w = dict(c_ctx=c_ctx, ada_w=ada_w, ada_b=ada_b, norm_w=norm_w, ffn1_wgu=ffn1_wgu, ffn1_wd=ffn1_wd, w_in=w_in, dn_conv_w=dn_conv_w, dn_a_log=dn_a_log, dn_dt_bias=dn_dt_bias, dn_norm_w=dn_norm_w, w_ret_out=w_ret_out, w_dn_out=w_dn_out, w_o=w_o, ffn2_wgu=ffn2_wgu, ffn2_wd=ffn2_wd, final_norm_w=final_norm_w)
    m = dict(c_ctx=m_c_ctx, ada_w=m_ada_w, ada_b=m_ada_b, norm_w=m_norm_w, ffn1_wgu=m_ffn1_wgu, ffn1_wd=m_ffn1_wd, w_in=m_w_in, dn_conv_w=m_dn_conv_w, dn_a_log=m_dn_a_log, dn_dt_bias=m_dn_dt_bias, dn_norm_w=m_dn_norm_w, w_ret_out=m_w_ret_out, w_dn_out=m_w_dn_out, w_o=m_w_o, ffn2_wgu=m_ffn2_wgu, ffn2_wd=m_ffn2_wd, final_norm_w=m_final_norm_w)
    v = dict(c_ctx=v_c_ctx, ada_w=v_ada_w, ada_b=v_ada_b, norm_w=v_norm_w, ffn1_wgu=v_ffn1_wgu, ffn1_wd=v_ffn1_wd, w_in=v_w_in, dn_conv_w=v_dn_conv_w, dn_a_log=v_dn_a_log, dn_dt_bias=v_dn_dt_bias, dn_norm_w=v_dn_norm_w, w_ret_out=v_w_ret_out, w_dn_out=v_w_dn_out, w_o=v_w_o, ffn2_wgu=v_ffn2_wgu, ffn2_wd=v_ffn2_wd, final_norm_w=v_final_norm_w)
    me = _my_index()
    big_names = [n for n, _ in BIG]

    g_big = all_gather([w[n].astype(BF16) for n in big_names], "ag_weights")
    full = {n: _join_shards(s, kind) for (n, kind), s in zip(BIG, g_big)}
    small_shapes = [norm_w.shape, dn_conv_w.shape, c.shape]
    g_small = all_gather([_pack([norm_w, dn_conv_w, c], 8)], "ag_small")[0]
    norm_s, conv_s, c_all = _unpack(g_small, small_shapes, lead=N_DEV)
    norm_full, conv_full = _join_shards(norm_s, "col"), _join_shards(conv_s, "col")

    cc = jnp.concatenate([c_all.reshape(N_DEV, D), c_ctx[None], jnp.zeros((ADA_ROWS - N_DEV - 1, D), F32)])
    ada_b_cols = lax.dynamic_slice_in_dim(ada_b, me * ADA_COLS, ADA_COLS, axis=1)[:, None, :]
    mods_part = ada_fwd(cc, ada_w, ada_b_cols)
    g_mods = all_gather([mods_part], "ag_mods")[0]
    mods_all = g_mods.transpose(1, 2, 0, 3).reshape(DEPTH, ADA_ROWS, N_MOD * D)
    mx = lax.dynamic_index_in_dim(mods_all, me, axis=1, keepdims=False).reshape(DEPTH, N_MOD, D)
    mc = mods_all[:, N_DEV].reshape(DEPTH, N_MOD, D)

    gparams = jnp.pad(jnp.stack([-jnp.exp(dn_a_log).reshape(DEPTH, 8), dn_dt_bias.reshape(DEPTH, 8)], axis=1),
                      ((0, 0), (0, 6), (8, LANE - 16)))
    stacked = dict(
        wgu1=full["ffn1_wgu"], wd1=full["ffn1_wd"], w_in=_pad_w_in(full["w_in"]), w_r=full["w_ret_out"],
        w_d=full["w_dn_out"], w_o=full["w_o"], wgu2=full["ffn2_wgu"], wd2=full["ffn2_wd"], norm_w=norm_full,
        conv_w8=jnp.pad(conv_full, ((0, 0), (0, 8 - CONV_K), (0, 0))), gparams=gparams,
        dn_norm_t=jnp.tile(dn_norm_w, (1, NH)), mx=mx, mc=mc)
    lws = [{k: a[l] for k, a in stacked.items()} for l in range(DEPTH)]
    loss_l, gx, gws, smalls, d_fnw = local_step(x[0], ctx[0], loss_target[0], final_norm_w, lws)

    sm_list = [jnp.stack([s[k] for s in smalls]) for k in ("dmx", "dmc", "norm_w", "conv_w", "a_log", "dt_bias",
                                                            "dn_norm_w")] + [d_fnw, loss_l.reshape(1)]
    sm_shapes = [a.shape for a in sm_list]
    g_sm = all_gather([_pack(sm_list, 8)], "ag_small_grads")[0]
    (dmx_sum, dmc_sum, g_norm, g_conv, g_alog, g_dtb, g_dnw, g_fnw, loss) = _unpack(sum_slots(g_sm), sm_shapes)
    dmx_all = _unpack(g_sm, sm_shapes[:1], lead=N_DEV)[0].reshape(N_DEV, DEPTH, N_MOD * D)
    dmc_sum = dmc_sum.reshape(DEPTH, N_MOD * D)
    dmx_sum = dmx_sum.reshape(DEPTH, N_MOD * D)
    dmod = jnp.concatenate([
        lax.dynamic_slice_in_dim(dmx_all, me * ADA_COLS, ADA_COLS, axis=2).transpose(1, 0, 2),
        lax.dynamic_slice_in_dim(dmc_sum, me * ADA_COLS, ADA_COLS, axis=1)[:, None, :],
        jnp.zeros((DEPTH, ADA_ROWS - N_DEV - 1, ADA_COLS), F32)], axis=1)
    g_ada_w, d_sil = ada_bwd(cc, dmod, ada_w)
    g_cc = all_gather([d_sil[:, N_DEV].sum(axis=0).reshape(8, LANE)], "ag_c_ctx")[0]
    grads = dict(
        ada_w=g_ada_w, c_ctx=c_ctx_grad(g_cc, c_ctx.reshape(8, LANE)).reshape(D), ada_b=dmx_sum + dmc_sum,
        norm_w=lax.dynamic_slice_in_dim(g_norm, me * (D // N_DEV), D // N_DEV, axis=2),
        dn_conv_w=lax.dynamic_slice_in_dim(g_conv, me * (2 * D // N_DEV), 2 * D // N_DEV, axis=2),
        dn_a_log=g_alog.reshape(dn_a_log.shape), dn_dt_bias=g_dtb.reshape(dn_dt_bias.shape), dn_norm_w=g_dnw,
        final_norm_w=g_fnw)

    def layer_full(gw):
        return dict(ffn1_wgu=jnp.concatenate([gw["wg1"], gw["wu1"]], axis=-1), ffn1_wd=gw["wd1"],
                    w_in=_unpad_w_in(gw["w_in"]), w_ret_out=gw["w_r"], w_dn_out=gw["w_d"], w_o=gw["w_o"],
                    ffn2_wgu=jnp.concatenate([gw["wg2"], gw["wu2"]], axis=-1), ffn2_wd=gw["wd2"])

    full_g = [layer_full(gw) for gw in gws]
    parts = [jnp.stack([_split_shards(fg[n], kind) for fg in full_g], axis=1).astype(BF16) for n, kind in BIG]
    core = lax.axis_index("c")
    rows_of = {n: (256 if w[n].shape[1] % 256 == 0 else w[n].shape[1]) for n in big_names}

    def for_core(p, which):
        return lax.dynamic_index_in_dim(p.reshape((4, 2) + p.shape[1:]), which, axis=1, keepdims=False)

    from_sibling = pair_exchange([for_core(p, 1 - core) for p in parts], "a2a_pair")
    summed = [add_bf16(for_core(p, core).reshape((-1,) + p.shape[2:]), f.reshape((-1,) + p.shape[2:]),
                       rows_of[n], "a2a_add").reshape((4,) + p.shape[1:])
              for n, p, f in zip(big_names, parts, from_sibling)]
    recv = chip_exchange(summed, "a2a_chips")
    res = {}
    for n, r in zip(big_names, recv):
        res[n] = adamw(r, w[n], m[n], v[n], rows_of[n], "adamw_" + n)
    res["ada_w"] = adamw(g_ada_w[None], ada_w, m["ada_w"], v["ada_w"], 256, "adamw_ada_w")
    small_names = [n for n in LOCAL if n != "ada_w"]
    small_pack = lambda d: _pack([d[n] for n in small_names], 8)[None]
    res_small = adamw(small_pack(grads)[None], small_pack(w), small_pack(m), small_pack(v),
                      small_pack(w).shape[1], "adamw_small")
    unpacked = [_unpack(r, [w[n].shape for n in small_names]) for r in res_small]
    for i, n in enumerate(small_names):
        res[n] = tuple(u[i] for u in unpacked)
    outs = [[res[n][k] for n in WEIGHT_ORDER] for k in range(4)]
    return (loss.reshape(()), gx[None], *outs[0], *outs[1], *outs[2], *outs[3])
```

```python
import functools
import math

import jax
import jax.numpy as jnp
from jax import lax
from jax.experimental import pallas as pl
from jax.experimental.pallas import tpu as pltpu

F32 = jnp.float32
BF16 = jnp.bfloat16

D = 1024
NH = 4
DK = 128
DV = 256
RET_C = 128
DN_C = 64
FFN_H = 2816
FFN_HB = 1408
N_MOD = 9
DEPTH = 4
N_DEV = 8
EPS = 1e-6
CONV_K = 5
HALO = 8

PW = 8320
C_RQK, C_RV, C_RG, C_DQK, C_DVV, C_DZ, C_GA, C_GB = 0, 1, 2, 3, 4, 5, 6, 7
C_DBA = 64
PROJ_TN = 1664

LANE = 128
VMEM_LIMIT = 56 * 1024 * 1024

ADAM_LR, ADAM_B1, ADAM_B2, ADAM_EPS, ADAM_WD, ADAM_STEP = 0.001, 0.9, 0.999, 1e-08, 0.01, 10

NN = ((1,), (0,))
NT = ((1,), (1,))
TN = ((0,), (0,))
HI = lax.Precision.HIGHEST


def _params(n_grid):
    return pltpu.CompilerParams(dimension_semantics=("arbitrary",) * n_grid, vmem_limit_bytes=VMEM_LIMIT)


def _dot(a, b, dims):
    return lax.dot_general(a.astype(BF16), b.astype(BF16), (dims, ((), ())), preferred_element_type=F32)


def _dot_hi(a, b, dims=NN):
    return lax.dot_general(a, b, (dims, ((), ())), precision=HI, preferred_element_type=F32)


@jax.custom_vjp
def mm_nn(a, b):
    return _dot(a, b, NN)


mm_nn.defvjp(lambda a, b: (_dot(a, b, NN), (a, b)),
             lambda r, g: (_dot(g, r[1], NT), _dot(r[0], g, TN)))


@jax.custom_vjp
def mm_nt(a, b):
    return _dot(a, b, NT)


mm_nt.defvjp(lambda a, b: (_dot(a, b, NT), (a, b)),
             lambda r, g: (_dot(g, r[1], NN), _dot(g, r[0], TN)))


@jax.custom_vjp
def mm_tn(a, b):
    return _dot(a, b, TN)


mm_tn.defvjp(lambda a, b: (_dot(a, b, TN), (a, b)),
             lambda r, g: (_dot(r[1], g, NT), _dot(r[0], g, NN)))


def _split_bf16(x, n):
    parts = []
    for _ in range(n):
        p = x.astype(BF16)
        parts.append(p)
        x = x - p.astype(F32)
    return parts


def _dot_f32(a, b, dims, exact=None):
    mm = lambda p, q: lax.dot_general(p, q, (dims, ((), ())), preferred_element_type=F32)
    if exact == "a":
        ab = a.astype(BF16)
        b1, b2, b3 = _split_bf16(b, 3)
        return mm(ab, b1) + (mm(ab, b2) + mm(ab, b3))
    if exact == "b":
        bb = b.astype(BF16)
        a1, a2, a3 = _split_bf16(a, 3)
        return mm(a1, bb) + (mm(a2, bb) + mm(a3, bb))
    a1, a2 = _split_bf16(a, 2)
    b1, b2 = _split_bf16(b, 2)
    return mm(a1, b1) + (mm(a1, b2) + mm(a2, b1))


@jax.custom_vjp
def _cum_rows(mask, x):
    return _dot_f32(mask, x, NN, exact="a")


_cum_rows.defvjp(lambda mask, x: (_dot_f32(mask, x, NN, exact="a"), mask),
                 lambda mask, g: (jnp.zeros_like(mask), _dot_f32(mask, g, TN, exact="a")))


@jax.custom_vjp
def _row_bcast(xb):
    return _dot_f32(jnp.full(xb.shape, 1.0 / LANE, F32), xb, NT, exact="a")


_row_bcast.defvjp(lambda xb: (_row_bcast(xb), None),
                  lambda _, g: (_dot_f32(g, jnp.full((g.shape[0], LANE), 1.0 / LANE, F32), TN, exact="b"),))


def _tri_inv_fwd(a):
    c = a.shape[0]
    eye = (lax.broadcasted_iota(jnp.int32, (c, c), 0) == lax.broadcasted_iota(jnp.int32, (c, c), 1)).astype(F32)
    pw = -a
    tinv = eye + pw
    m = 2
    while m < DN_C:
        pw = _dot_f32(pw, pw, NN)
        tinv = tinv + _dot_f32(tinv, pw, NN)
        m *= 2
    return tinv


@jax.custom_vjp
def _tri_inv(a):
    return _tri_inv_fwd(a)


def _tri_inv_bwd(tinv, g):
    return (-_dot_f32(_dot_f32(tinv, g, TN), tinv, NT),)


_tri_inv.defvjp(lambda a: (lambda t: (t, t))(_tri_inv_fwd(a)), _tri_inv_bwd)


@jax.custom_vjp
def _tri_inv_known(a, tinv):
    del a
    return tinv


_tri_inv_known.defvjp(lambda a, tinv: (tinv, tinv),
                      lambda tinv, g: (_tri_inv_bwd(tinv, g)[0], jnp.zeros_like(tinv)))


def _normmod(x, nw, shift, scale):
    r = lax.rsqrt(jnp.mean(x * x, axis=-1, keepdims=True) + EPS)
    return (x * r * nw) * (1.0 + scale) + shift


def _sigmoid(x):
    return 0.5 * jnp.tanh(0.5 * x) + 0.5


def _softplus(x):
    return jnp.maximum(x, 0.0) + jnp.log(1.0 + jnp.exp(-jnp.abs(x)))


def _row_block(t_len):
    return 512 if t_len % 512 == 0 else 256


def ffn_fwd(x, vec, wgu, wd):
    t_len = x.shape[0]
    tm = _row_block(t_len)

    def body(x_ref, vec_ref, wg_ref, wu_ref, wd_ref, x1_ref, h_ref, a_ref, s_ref, m_ref, f_ref, hs, acc):
        j = pl.program_id(1)

        @pl.when(j == 0)
        def _():
            hb = _normmod(x_ref[...], vec_ref[0:1, :], vec_ref[1:2, :], vec_ref[2:3, :]).astype(BF16)
            hs[...] = hb
            h_ref[...] = hb
            acc[...] = jnp.zeros_like(acc)

        hb = hs[...]
        g = jnp.dot(hb, wg_ref[...], preferred_element_type=F32)
        u = jnp.dot(hb, wu_ref[...], preferred_element_type=F32)
        sg = _sigmoid(g)
        sl = g * sg
        act = (sl * u).astype(BF16)
        a_ref[...] = act
        s_ref[...] = sl.astype(BF16)
        m_ref[...] = (u * (sg + sl * (1.0 - sg))).astype(BF16)
        acc[...] += jnp.dot(act, wd_ref[...], preferred_element_type=F32)

        @pl.when(j == 1)
        def _():
            f = acc[...]
            f_ref[...] = f.astype(BF16)
            x1_ref[...] = x_ref[...] + 0.5 * vec_ref[3:4, :] * f

    return pl.pallas_call(
        body, name="ffn_fwd", grid=(t_len // tm, 2),
        in_specs=[
            pl.BlockSpec((tm, D), lambda i, j: (i, 0)),
            pl.BlockSpec((8, D), lambda i, j: (0, 0)),
            pl.BlockSpec((D, FFN_HB), lambda i, j: (0, j)),
            pl.BlockSpec((D, FFN_HB), lambda i, j: (0, j + 2)),
            pl.BlockSpec((FFN_HB, D), lambda i, j: (j, 0)),
        ],
        out_specs=[
            pl.BlockSpec((tm, D), lambda i, j: (i, 0)),
            pl.BlockSpec((tm, D), lambda i, j: (i, 0)),
            pl.BlockSpec((tm, FFN_HB), lambda i, j: (i, j)),
            pl.BlockSpec((tm, FFN_HB), lambda i, j: (i, j)),
            pl.BlockSpec((tm, FFN_HB), lambda i, j: (i, j)),
            pl.BlockSpec((tm, D), lambda i, j: (i, 0)),
        ],
        out_shape=[
            jax.ShapeDtypeStruct((t_len, D), F32),
            jax.ShapeDtypeStruct((t_len, D), BF16),
            jax.ShapeDtypeStruct((t_len, FFN_H), BF16),
            jax.ShapeDtypeStruct((t_len, FFN_H), BF16),
            jax.ShapeDtypeStruct((t_len, FFN_H), BF16),
            jax.ShapeDtypeStruct((t_len, D), BF16),
        ],
        scratch_shapes=[pltpu.VMEM((tm, D), BF16), pltpu.VMEM((tm, D), F32)],
        compiler_params=_params(2),
    )(x, vec, wgu, wgu, wd)


def ffn_bwd_act(dx1, vec, sl, dact_dg, f, wd):
    t_len = dx1.shape[0]
    tm = _row_block(t_len)

    def body(dx_ref, vec_ref, s_ref, m_ref, f_ref, wd_ref, dg_ref, du_ref, dy_ref, pg_ref, dys):
        i, j = pl.program_id(0), pl.program_id(1)

        @pl.when((i == 0) & (j == 0))
        def _():
            pg_ref[...] = jnp.zeros_like(pg_ref)

        @pl.when(j == 0)
        def _():
            dx = dx_ref[...]
            dyb = (0.5 * vec_ref[3:4, :] * dx).astype(BF16)
            dys[...] = dyb
            dy_ref[...] = dyb
            pg_ref[3:4, :] += jnp.sum(0.5 * dx * f_ref[...].astype(F32), axis=0, keepdims=True)

        dact = _dot(dys[...], wd_ref[...], NT)
        du_ref[...] = (dact * s_ref[...].astype(F32)).astype(BF16)
        dg_ref[...] = (dact * m_ref[...].astype(F32)).astype(BF16)

    return pl.pallas_call(
        body, name="ffn_bwd_act", grid=(t_len // tm, 2),
        in_specs=[
            pl.BlockSpec((tm, D), lambda i, j: (i, 0)),
            pl.BlockSpec((8, D), lambda i, j: (0, 0)),
            pl.BlockSpec((tm, FFN_HB), lambda i, j: (i, j)),
            pl.BlockSpec((tm, FFN_HB), lambda i, j: (i, j)),
            pl.BlockSpec((tm, D), lambda i, j: (i, 0)),
            pl.BlockSpec((FFN_HB, D), lambda i, j: (j, 0)),
        ],
        out_specs=[
            pl.BlockSpec((tm, FFN_HB), lambda i, j: (i, j)),
            pl.BlockSpec((tm, FFN_HB), lambda i, j: (i, j)),
            pl.BlockSpec((tm, D), lambda i, j: (i, 0)),
            pl.BlockSpec((8, D), lambda i, j: (0, 0)),
        ],
        out_shape=[
            jax.ShapeDtypeStruct((t_len, FFN_H), BF16),
            jax.ShapeDtypeStruct((t_len, FFN_H), BF16),
            jax.ShapeDtypeStruct((t_len, D), BF16),
            jax.ShapeDtypeStruct((8, D), F32),
        ],
        scratch_shapes=[pltpu.VMEM((tm, D), BF16)],
        compiler_params=_params(2),
    )(dx1, vec, sl, dact_dg, f, wd)


def nt_norm_bwd(dys, w, col_offsets, tk, n_steps, x_in, dres, vec, name):
    t_len = x_in.shape[0]
    tm = _row_block(t_len)
    n_seg = len(dys)

    def body(*refs):
        dy_refs = refs[:n_seg]
        w_refs = refs[n_seg:2 * n_seg]
        x_ref, dres_ref, vec_ref, dx_ref, pg_ref, acc = refs[2 * n_seg:]
        i, j = pl.program_id(0), pl.program_id(1)

        @pl.when((i == 0) & (j == 0))
        def _():
            pg_ref[...] = jnp.zeros_like(pg_ref)

        @pl.when(j == 0)
        def _():
            acc[...] = jnp.zeros_like(acc)

        part = _dot(dy_refs[0][...], w_refs[0][...], NT)
        for s in range(1, n_seg):
            part += _dot(dy_refs[s][...], w_refs[s][...], NT)
        acc[...] += part

        @pl.when(j == n_steps - 1)
        def _():
            _, vjp = jax.vjp(_normmod, x_ref[...], vec_ref[0:1, :], vec_ref[1:2, :], vec_ref[2:3, :])
            dxn, dnw, dsh, dsc = vjp(acc[...])
            dx_ref[...] = dres_ref[...] + dxn
            pg_ref[0:1, :] += dnw
            pg_ref[1:2, :] += dsh
            pg_ref[2:3, :] += dsc

    in_specs = [pl.BlockSpec((tm, tk), lambda i, j: (i, j)) for _ in range(n_seg)]
    in_specs += [pl.BlockSpec((D, tk), functools.partial(lambda i, j, off: (0, off + j), off=off))
                 for off in col_offsets]
    in_specs += [pl.BlockSpec((tm, D), lambda i, j: (i, 0)), pl.BlockSpec((tm, D), lambda i, j: (i, 0)),
                 pl.BlockSpec((8, D), lambda i, j: (0, 0))]
    return pl.pallas_call(
        body, name=name, grid=(t_len // tm, n_steps),
        in_specs=in_specs,
        out_specs=[pl.BlockSpec((tm, D), lambda i, j: (i, 0)), pl.BlockSpec((8, D), lambda i, j: (0, 0))],
        out_shape=[jax.ShapeDtypeStruct((t_len, D), F32), jax.ShapeDtypeStruct((8, D), F32)],
        scratch_shapes=[pltpu.VMEM((tm, D), F32)],
        compiler_params=_params(2),
    )(*dys, *([w] * n_seg), x_in, dres, vec)


def tn_matmul(a_x, b_x, a_c, b_c, tn, name):
    t_len, k_dim = a_x.shape
    n_dim = b_x.shape[1]
    t_ctx = a_c.shape[0]
    tt = 512
    n_t = t_len // tt

    def body(ax_ref, bx_ref, ac_ref, bc_ref, o_ref):
        t = pl.program_id(1)

        @pl.when(t == 0)
        def _():
            o_ref[...] = _dot(ac_ref[...], bc_ref[...], TN)

        o_ref[...] += _dot(ax_ref[...], bx_ref[...], TN)

    return pl.pallas_call(
        body, name=name, grid=(n_dim // tn, n_t),
        in_specs=[
            pl.BlockSpec((tt, k_dim), lambda n, t: (t, 0)),
            pl.BlockSpec((tt, tn), lambda n, t: (t, n)),
            pl.BlockSpec((t_ctx, k_dim), lambda n, t: (0, 0)),
            pl.BlockSpec((t_ctx, tn), lambda n, t: (0, n)),
        ],
        out_specs=pl.BlockSpec((k_dim, tn), lambda n, t: (0, n)),
        out_shape=jax.ShapeDtypeStruct((k_dim, n_dim), F32),
        compiler_params=_params(2),
    )(a_x, b_x, a_c, b_c)


N_SEG = 8


def proj_bwd_in(segs, dba, w, x_in, dres, vec):
    t_len = x_in.shape[0]
    tm = _row_block(t_len)

    def body(*refs):
        seg_refs = refs[:N_SEG]
        dba_ref, w_ref, wg_ref, x_ref, dres_ref, vec_ref, dx_ref, pg_ref, acc = refs[N_SEG:]
        i, j = pl.program_id(0), pl.program_id(1)

        @pl.when((i == 0) & (j == 0))
        def _():
            pg_ref[...] = jnp.zeros_like(pg_ref)

        @pl.when(j == 0)
        def _():
            acc[...] = _dot(dba_ref[...], wg_ref[...], NT)

        for s in range(N_SEG // 2):
            @pl.when(j == s)
            def _():
                acc[...] += (_dot(seg_refs[2 * s][...], w_ref[:, 0:D], NT)
                             + _dot(seg_refs[2 * s + 1][...], w_ref[:, D:2 * D], NT))

        @pl.when(j == N_SEG // 2 - 1)
        def _():
            _, vjp = jax.vjp(_normmod, x_ref[...], vec_ref[0:1, :], vec_ref[1:2, :], vec_ref[2:3, :])
            dxn, dnw, dsh, dsc = vjp(acc[...])
            dx_ref[...] = dres_ref[...] + dxn
            pg_ref[0:1, :] += dnw
            pg_ref[1:2, :] += dsh
            pg_ref[2:3, :] += dsc

    row = lambda width: pl.BlockSpec((tm, width), lambda i, j: (i, 0))
    return pl.pallas_call(
        body, name="proj_bwd_in", grid=(t_len // tm, N_SEG // 2),
        in_specs=[row(D)] * N_SEG + [row(LANE), pl.BlockSpec((D, 2 * D), lambda i, j: (0, j)),
                                     pl.BlockSpec((D, LANE), lambda i, j: (0, C_DBA)),
                                     row(D), row(D), pl.BlockSpec((8, D), lambda i, j: (0, 0))],
        out_specs=[row(D), pl.BlockSpec((8, D), lambda i, j: (0, 0))],
        out_shape=[jax.ShapeDtypeStruct((t_len, D), F32), jax.ShapeDtypeStruct((8, D), F32)],
        scratch_shapes=[pltpu.VMEM((tm, D), F32)],
        compiler_params=_params(2),
    )(*segs, dba, w, w, x_in, dres, vec)


def dw_in(a_x, segs_x, a_c, segs_c):
    t_len, k_dim = a_x.shape
    t_ctx = a_c.shape[0]
    tt = 512

    def body(*refs):
        ax_ref, ac_ref = refs[0], refs[1]
        bx_refs, bc_refs = refs[2:2 + N_SEG], refs[2 + N_SEG:2 + 2 * N_SEG]
        o_ref = refs[2 + 2 * N_SEG]
        n, t = pl.program_id(0), pl.program_id(1)
        for s in range(N_SEG):
            @pl.when(n == s)
            def _():
                @pl.when(t == 0)
                def _():
                    o_ref[...] = _dot(ac_ref[...], bc_refs[s][...], TN)

                o_ref[...] += _dot(ax_ref[...], bx_refs[s][...], TN)

    seg_x = [pl.BlockSpec((tt, D), functools.partial(lambda n, t, s: (jnp.where(n == s, t, 0), 0), s=s))
             for s in range(N_SEG)]
    seg_c = [pl.BlockSpec((t_ctx, D), lambda n, t: (0, 0)) for _ in range(N_SEG)]
    return pl.pallas_call(
        body, name="dw_in", grid=(N_SEG, t_len // tt),
        in_specs=[pl.BlockSpec((tt, k_dim), lambda n, t: (t, 0)), pl.BlockSpec((t_ctx, k_dim), lambda n, t: (0, 0))]
        + seg_x + seg_c,
        out_specs=pl.BlockSpec((k_dim, D), lambda n, t: (0, n)),
        out_shape=jax.ShapeDtypeStruct((k_dim, N_SEG * D), F32),
        compiler_params=_params(2),
    )(a_x, a_c, *segs_x, *segs_c)


def proj_fwd(x, vec, w_in_p):
    t_len = x.shape[0]
    tm = _row_block(t_len)

    def body(x_ref, vec_ref, w_ref, p_ref, h_ref, hs):
        @pl.when(pl.program_id(1) == 0)
        def _():
            hb = _normmod(x_ref[...], vec_ref[0:1, :], vec_ref[1:2, :], vec_ref[2:3, :]).astype(BF16)
            hs[...] = hb
            h_ref[...] = hb

        p_ref[...] = jnp.dot(hs[...], w_ref[...], preferred_element_type=F32)

    return pl.pallas_call(
        body, name="proj_fwd", grid=(t_len // tm, PW // PROJ_TN),
        in_specs=[
            pl.BlockSpec((tm, D), lambda i, j: (i, 0)),
            pl.BlockSpec((8, D), lambda i, j: (0, 0)),
            pl.BlockSpec((D, PROJ_TN), lambda i, j: (0, j)),
        ],
        out_specs=[pl.BlockSpec((tm, PROJ_TN), lambda i, j: (i, j)), pl.BlockSpec((tm, D), lambda i, j: (i, 0))],
        out_shape=[jax.ShapeDtypeStruct((t_len, PW), F32), jax.ShapeDtypeStruct((t_len, D), BF16)],
        scratch_shapes=[pltpu.VMEM((tm, D), BF16)],
        compiler_params=_params(2),
    )(x, vec, w_in_p)


def _shift_rows(e, s):
    n = e.shape[0]
    return pltpu.roll(e, (-s) % n, 0)


def _swap_halves(t):
    return pltpu.roll(t, DK // 2, 1)


def _halo_specs(tm, t_len, width, col, lead=False):
    per = tm // HALO
    last = t_len // HALO - 1
    if lead:
        return (pl.BlockSpec((2, HALO, width), lambda i: (0, jnp.maximum(i * per - 1, 0), col)),
                pl.BlockSpec((2, HALO, width), lambda i: (0, jnp.minimum((i + 1) * per, last), col)))
    return (pl.BlockSpec((HALO, width), lambda i: (jnp.maximum(i * per - 1, 0), col)),
            pl.BlockSpec((HALO, width), lambda i: (jnp.minimum((i + 1) * per, last), col)))


def _gate_cols(ba, gp_ref):
    lane = lax.broadcasted_iota(jnp.int32, ba.shape, 1)
    z = ba + gp_ref[1:2, :]
    return jnp.where(lane < 8, _sigmoid(ba), gp_ref[0:1, :] * _softplus(z))


def _conv_chunk(win, w_ref, c0, width):
    y = w_ref[0:1, c0:c0 + width] * _shift_rows(win, -2)
    for j in range(1, CONV_K):
        y += w_ref[j:j + 1, c0:c0 + width] * _shift_rows(win, j - 2)
    return y


def feat_fwd(p, cos2, sin2, conv_w8, gparams):
    t_len = p.shape[0]
    tm = _row_block(t_len)
    n_t = t_len // tm
    cw = 512

    def body(rqk_ref, dqk_ref, dqk_p, dqk_n, dvv_ref, dvv_p, dvv_n, dba_ref, cos_ref, sin_ref, cw_ref, gp_ref,
             o_rqk, o_dqkv, o_gbc):
        i = pl.program_id(0)
        cos, sin = cos_ref[...], sin_ref[...]
        for s in range(2 * NH):
            t = rqk_ref[:, s * DK:(s + 1) * DK]
            if s >= NH:
                t = t * (DK ** -0.5)
            o_rqk[:, s * DK:(s + 1) * DK] = t * cos + _swap_halves(t) * sin
        o_gbc[...] = _gate_cols(dba_ref[...], gp_ref)
        first, last = i == 0, i == n_t - 1
        for ci in range(4):
            src, sp, sn = (dqk_ref, dqk_p, dqk_n) if ci < 2 else (dvv_ref, dvv_p, dvv_n)
            c0 = (ci % 2) * cw
            win = jnp.concatenate([
                jnp.where(first, 0.0, sp[:, c0:c0 + cw]), src[:, c0:c0 + cw],
                jnp.where(last, 0.0, sn[:, c0:c0 + cw])], axis=0)
            y = _conv_chunk(win, cw_ref, ci * cw, cw)[HALO:HALO + tm]
            sv = y * _sigmoid(y)
            if ci < 2:
                scale = DK ** -0.5 if ci == 0 else 1.0
                for hh in range(NH):
                    sh = sv[:, hh * DK:(hh + 1) * DK]
                    nrm = lax.rsqrt(jnp.sum(sh * sh, axis=-1, keepdims=True) + EPS)
                    o_dqkv[:, ci * cw + hh * DK:ci * cw + (hh + 1) * DK] = sh * nrm * scale
            else:
                o_dqkv[:, ci * cw:(ci + 1) * cw] = sv

    hq = _halo_specs(tm, t_len, D, C_DQK)
    hv = _halo_specs(tm, t_len, D, C_DVV)
    return pl.pallas_call(
        body, name="feat_fwd", grid=(n_t,),
        in_specs=[
            pl.BlockSpec((tm, D), lambda i: (i, C_RQK)),
            pl.BlockSpec((tm, D), lambda i: (i, C_DQK)), hq[0], hq[1],
            pl.BlockSpec((tm, D), lambda i: (i, C_DVV)), hv[0], hv[1],
            pl.BlockSpec((tm, LANE), lambda i: (i, C_DBA)),
            pl.BlockSpec((tm, LANE), lambda i: (i, 0)),
            pl.BlockSpec((tm, LANE), lambda i: (i, 0)),
            pl.BlockSpec((8, 2 * D), lambda i: (0, 0)),
            pl.BlockSpec((8, LANE), lambda i: (0, 0)),
        ],
        out_specs=[pl.BlockSpec((tm, D), lambda i: (i, 0)), pl.BlockSpec((tm, 2 * D), lambda i: (i, 0)),
                   pl.BlockSpec((tm, LANE), lambda i: (i, 0))],
        out_shape=[jax.ShapeDtypeStruct((t_len, D), F32), jax.ShapeDtypeStruct((t_len, 2 * D), F32),
                   jax.ShapeDtypeStruct((t_len, LANE), F32)],
        compiler_params=_params(1),
    )(p, p, p, p, p, p, p, p, cos2, sin2, conv_w8, gparams)


def feat_bwd(p, cos2, sin2, conv_w8, gparams, d_rqk, d_rv, d_dqk, d_dvv, d_gbc):
    t_len = p.shape[0]
    tm = 256
    n_t = t_len // tm
    cw = 512

    def body(dqk_ref, dqk_p, dqk_n, dvv_ref, dvv_p, dvv_n, dba_ref, cos_ref, sin_ref, cw_ref, gp_ref,
             g_rqk, g_rv, g_dqk, g_dqk_p, g_dqk_n, g_dvv, g_dvv_p, g_dvv_n, g_gbc,
             o_rqk, o_rv, o_dqk, o_dvv, o_dba, o_cw, o_gp):
        i = pl.program_id(0)
        first, last = i == 0, i == n_t - 1

        @pl.when(first)
        def _():
            o_cw[...] = jnp.zeros_like(o_cw)
            o_gp[...] = jnp.zeros_like(o_gp)

        cos, sin = cos_ref[...], sin_ref[...]
        for s in range(2 * NH):
            gsl = g_rqk[0, :, s * DK:(s + 1) * DK] + g_rqk[1, :, s * DK:(s + 1) * DK]
            dt = gsl * cos + _swap_halves(gsl * sin)
            if s >= NH:
                dt = dt * (DK ** -0.5)
            o_rqk[:, s * DK:(s + 1) * DK] = dt.astype(BF16)
        o_rv[...] = (g_rv[0] + g_rv[1]).astype(BF16)

        ba = dba_ref[...]
        gg = g_gbc[0] + g_gbc[1]
        lane = lax.broadcasted_iota(jnp.int32, ba.shape, 1)
        sb = _sigmoid(ba)
        z = ba + gp_ref[1:2, :]
        a_row = gp_ref[0:1, :]
        dz = gg * a_row * _sigmoid(z)
        o_dba[...] = jnp.where(lane < 8, gg * sb * (1.0 - sb), dz).astype(BF16)
        is_g = (lane >= 8) & (lane < 16)
        o_gp[0:1, :] += jnp.sum(jnp.where(is_g, gg * a_row * _softplus(z), 0.0), axis=0, keepdims=True)
        o_gp[1:2, :] += jnp.sum(jnp.where(is_g, dz, 0.0), axis=0, keepdims=True)

        for ci in range(4):
            src, sp, sn = (dqk_ref, dqk_p, dqk_n) if ci < 2 else (dvv_ref, dvv_p, dvv_n)
            c0 = (ci % 2) * cw
            gc0 = ci * cw
            win = jnp.concatenate([
                jnp.where(first, 0.0, sp[:, c0:c0 + cw]), src[:, c0:c0 + cw],
                jnp.where(last, 0.0, sn[:, c0:c0 + cw])], axis=0)
            gs, gsp, gsn = (g_dqk, g_dqk_p, g_dqk_n) if ci < 2 else (g_dvv, g_dvv_p, g_dvv_n)
            gext = jnp.concatenate([
                jnp.where(first, 0.0, gsp[0, :, c0:c0 + cw] + gsp[1, :, c0:c0 + cw]),
                gs[0, :, c0:c0 + cw] + gs[1, :, c0:c0 + cw],
                jnp.where(last, 0.0, gsn[0, :, c0:c0 + cw] + gsn[1, :, c0:c0 + cw])], axis=0)
            y = _conv_chunk(win, cw_ref, gc0, cw)
            sg = _sigmoid(y)
            sv = y * sg
            if ci < 2:
                scale = DK ** -0.5 if ci == 0 else 1.0
                parts = []
                for hh in range(NH):
                    sh = sv[:, hh * DK:(hh + 1) * DK]
                    gh = gext[:, hh * DK:(hh + 1) * DK]
                    nrm = lax.rsqrt(jnp.sum(sh * sh, axis=-1, keepdims=True) + EPS)
                    dot = jnp.sum(gh * sh, axis=-1, keepdims=True)
                    parts.append(scale * nrm * (gh - sh * (nrm * nrm) * dot))
                ds = jnp.concatenate(parts, axis=1)
            else:
                ds = gext
            dy = ds * (sg + sv * (1.0 - sg))
            dpe = cw_ref[0:1, gc0:gc0 + cw] * _shift_rows(dy, 2)
            for j in range(1, CONV_K):
                dpe += cw_ref[j:j + 1, gc0:gc0 + cw] * _shift_rows(dy, 2 - j)
            dst = o_dqk if ci < 2 else o_dvv
            dst[:, c0:c0 + cw] = dpe[HALO:HALO + tm].astype(BF16)
            dyc = dy[HALO:HALO + tm]
            for j in range(CONV_K):
                o_cw[j:j + 1, gc0:gc0 + cw] += jnp.sum(dyc * _shift_rows(win, j - 2)[HALO:HALO + tm], axis=0,
                                                      keepdims=True)

    hq = _halo_specs(tm, t_len, D, C_DQK)
    hv = _halo_specs(tm, t_len, D, C_DVV)
    hg = _halo_specs(tm, t_len, D, 0, lead=True)
    outs = pl.pallas_call(
        body, name="feat_bwd", grid=(n_t,),
        in_specs=[
            pl.BlockSpec((tm, D), lambda i: (i, C_DQK)), hq[0], hq[1],
            pl.BlockSpec((tm, D), lambda i: (i, C_DVV)), hv[0], hv[1],
            pl.BlockSpec((tm, LANE), lambda i: (i, C_DBA)),
            pl.BlockSpec((tm, LANE), lambda i: (i, 0)),
            pl.BlockSpec((tm, LANE), lambda i: (i, 0)),
            pl.BlockSpec((8, 2 * D), lambda i: (0, 0)),
            pl.BlockSpec((8, LANE), lambda i: (0, 0)),
            pl.BlockSpec((2, tm, D), lambda i: (0, i, 0)),
            pl.BlockSpec((2, tm, D), lambda i: (0, i, 0)),
            pl.BlockSpec((2, tm, D), lambda i: (0, i, 0)), hg[0], hg[1],
            pl.BlockSpec((2, tm, D), lambda i: (0, i, 0)), hg[0], hg[1],
            pl.BlockSpec((2, tm, LANE), lambda i: (0, i, 0)),
        ],
        out_specs=[pl.BlockSpec((tm, D), lambda i: (i, 0))] * 4
        + [pl.BlockSpec((tm, LANE), lambda i: (i, 0)),
           pl.BlockSpec((8, 2 * D), lambda i: (0, 0)),
           pl.BlockSpec((8, LANE), lambda i: (0, 0))],
        out_shape=[jax.ShapeDtypeStruct((t_len, D), BF16)] * 4
        + [jax.ShapeDtypeStruct((t_len, LANE), BF16),
           jax.ShapeDtypeStruct((8, 2 * D), F32), jax.ShapeDtypeStruct((8, LANE), F32)],
        compiler_params=_params(1),
    )(p, p, p, p, p, p, p, cos2, sin2, conv_w8, gparams, d_rqk, d_rv, d_dqk, d_dqk, d_dqk, d_dvv, d_dvv, d_dvv,
      d_gbc)
    return outs


def _ret_chunk(q, k, v, s, lg, rev):
    c = q.shape[0]
    ii = lax.broadcasted_iota(jnp.int32, (c, c), 0).astype(F32)
    jj = lax.broadcasted_iota(jnp.int32, (c, c), 1).astype(F32)
    diff = jnp.where(rev, jj - ii, ii - jj)
    mask = diff >= jnp.where(rev, 1.0, 0.0)
    dec =jnp.where(mask, jnp.exp(lg * jnp.where(mask, diff, 0.0)), 0.0)
    idx = lax.broadcasted_iota(jnp.int32, (c, 1), 0).astype(F32)
    idx = jnp.where(rev, c - 1.0 - idx, idx)
    kdec = k * jnp.exp(lg * (c - 1.0 - idx))
    qdec = q * jnp.exp(lg * (idx + 1.0))
    o = mm_nn(mm_nt(q, k) * dec, v) + mm_nn(qdec, s)
    s2 = s * math.exp(lg * c) + mm_tn(kdec, v)
    return o, s2


def _log_gamma(h):
    return math.log1p(-(2.0 ** (-5.0 - h)))


DN_W = 3 * NH * DK
HS = NH * DN_C


def _dn_intra4(q4, k4, v4, g4, beta4, rev, tinv_known=None):
    ii = lax.broadcasted_iota(jnp.int32, (HS, HS), 0)
    jj = lax.broadcasted_iota(jnp.int32, (HS, HS), 1)
    blk = (ii // DN_C) == (jj // DN_C)
    dd = jnp.where(rev, jj - ii, ii - jj)
    incl = blk & (dd >= 0)
    strict = blk & (dd > 0)
    gb = jnp.broadcast_to(g4, (HS, LANE))
    gcb = _cum_rows(incl.astype(F32), gb)
    gc = jnp.max(gcb, axis=1, keepdims=True)
    gcr = _row_bcast(gcb)
    at_end = blk & ((jj % DN_C) == jnp.where(rev, 0, DN_C - 1))
    glast = jnp.sum(jnp.where(at_end, gcr, 0.0), axis=1, keepdims=True)
    decay = jnp.where(incl, jnp.exp(jnp.where(incl, gc - gcr, 0.0)), 0.0)
    kb = k4 * beta4
    a = jnp.where(strict, mm_nt(kb, k4) * decay, 0.0)
    tinv = _tri_inv(a) if tinv_known is None else _tri_inv_known(a, tinv_known)
    u = mm_nn(tinv, v4 * beta4)
    w = mm_nn(tinv, kb * jnp.exp(gc))
    attn = jnp.where(incl, mm_nt(q4, k4) * decay, 0.0)
    return (u, w, q4 * jnp.exp(gc), k4 * jnp.exp(glast - gc), attn), tinv


def _dn_seq4(u, w, qg, kd, attn, glast, s):
    v_new = [uh - mm_nn(wh, sh) for uh, wh, sh in zip(u, w, s)]
    s2 = [sh * jnp.exp(gh) + mm_tn(kh, vh) for sh, gh, kh, vh in zip(s, glast, kd, v_new)]
    o = jnp.concatenate([mm_nn(qh, sh) for qh, sh in zip(qg, s)], axis=0) + mm_nn(attn, jnp.concatenate(v_new, axis=0))
    return o, s2


def _stack_heads(ref_rows, width, col0=0):
    return jnp.concatenate([ref_rows(slice(col0 + h * width, col0 + (h + 1) * width)) for h in range(NH)], axis=0)


def _head_rows(h):
    return slice(h * DN_C, (h + 1) * DN_C)


def _dn_gates4(gbv, d):
    g4 = jnp.concatenate([_pick_lane(gbv, 8 + 4 * d + h) for h in range(NH)], axis=0)
    b4 = jnp.concatenate([_pick_lane(gbv, 4 * d + h) for h in range(NH)], axis=0)
    return g4, b4


def dn_intra_fwd(dqkv, gbc):
    t_len = dqkv.shape[0]
    rb = 256
    n_g = rb // DN_C

    def body(qk_ref, v_ref, gb_ref, u_ref, wqk_ref, at_ref, ti_ref):
        d = pl.program_id(0)
        rev = d == 1
        for gi in range(n_g):
            rows = slice(gi * DN_C, (gi + 1) * DN_C)
            g4, b4 = _dn_gates4(gb_ref[rows, :], d)
            (u, w, qg, kd, attn), tinv = _dn_intra4(_stack_heads(lambda cs: qk_ref[rows, cs], DK),
                                                    _stack_heads(lambda cs: qk_ref[rows, cs], DK, NH * DK),
                                                    _stack_heads(lambda cs: v_ref[rows, cs], DV), g4, b4, rev)
            at_ref[0, gi] = attn.astype(BF16)
            ti_ref[0, gi] = tinv
            for h in range(NH):
                hr = _head_rows(h)
                u_ref[0, rows, h * DV:(h + 1) * DV] = u[hr]
                wqk_ref[0, rows, h * DK:(h + 1) * DK] = w[hr].astype(BF16)
                wqk_ref[0, rows, NH * DK + h * DK:NH * DK + (h + 1) * DK] = qg[hr].astype(BF16)
                wqk_ref[0, rows, 2 * NH * DK + h * DK:2 * NH * DK + (h + 1) * DK] = kd[hr].astype(BF16)

    return pl.pallas_call(
        body, name="dn_intra_fwd", grid=(2, t_len // rb),
        in_specs=[pl.BlockSpec((rb, D), lambda d, i: (i, 0)), pl.BlockSpec((rb, D), lambda d, i: (i, 1)),
                  pl.BlockSpec((rb, LANE), lambda d, i: (i, 0))],
        out_specs=[pl.BlockSpec((1, rb, D), lambda d, i: (d, i, 0)),
                   pl.BlockSpec((1, rb, DN_W), lambda d, i: (d, i, 0)),
                   pl.BlockSpec((1, n_g, HS, HS), lambda d, i: (d, i, 0, 0)),
                   pl.BlockSpec((1, n_g, HS, HS), lambda d, i: (d, i, 0, 0))],
        out_shape=[jax.ShapeDtypeStruct((2, t_len, D), F32), jax.ShapeDtypeStruct((2, t_len, DN_W), BF16),
                   jax.ShapeDtypeStruct((2, t_len // DN_C, HS, HS), BF16),
                   jax.ShapeDtypeStruct((2, t_len // DN_C, HS, HS), F32)],
        compiler_params=_params(2),
    )(dqkv, dqkv, gbc)


SEQ_G = 4
RET_G = 2


def _dn_seq_inputs(u_ref, wqk_ref, at_ref, gb_ref, d, gidx):
    rows = pl.ds(pl.multiple_of(gidx * DN_C, DN_C), DN_C)
    u = [u_ref[0, rows, h * DV:(h + 1) * DV] for h in range(NH)]
    w = [wqk_ref[0, rows, h * DK:(h + 1) * DK].astype(F32) for h in range(NH)]
    qg = [wqk_ref[0, rows, NH * DK + h * DK:NH * DK + (h + 1) * DK].astype(F32) for h in range(NH)]
    kd = [wqk_ref[0, rows, 2 * NH * DK + h * DK:2 * NH * DK + (h + 1) * DK].astype(F32) for h in range(NH)]
    gbv = gb_ref[rows, :]
    glast = [jnp.sum(_pick_lane(gbv, 8 + 4 * d + h), axis=0, keepdims=True) for h in range(NH)]
    return u, w, qg, kd, at_ref[0, gidx].astype(F32), glast


def dn_seq_fwd(u, wqk, attn, gbc, s0):
    t_len = u.shape[1]
    n_b = t_len // (DN_C * SEQ_G)
    rb = DN_C * SEQ_G

    def block_of(d, t):
        return jnp.where(d == 0, t, n_b - 1 - t)

    def body(u_ref, wqk_ref, at_ref, gb_ref, s0_ref, o_ref, ss_ref, sf_ref, st):
        d, t = pl.program_id(0), pl.program_id(1)

        @pl.when(t == 0)
        def _():
            st[...] = s0_ref[0]

        for gi in range(SEQ_G):
            gidx = jnp.where(d == 0, gi, SEQ_G - 1 - gi)
            rows = pl.ds(pl.multiple_of(gidx * DN_C, DN_C), DN_C)
            s_in = [st[h] for h in range(NH)]
            ss_ref[0, gidx] = st[...].astype(BF16)
            o, s2 = _dn_seq4(*_dn_seq_inputs(u_ref, wqk_ref, at_ref, gb_ref, d, gidx), s_in)
            for h in range(NH):
                o_ref[0, rows, h * DV:(h + 1) * DV] = o[_head_rows(h)]
                st[h] = s2[h]

        @pl.when(t == n_b - 1)
        def _():
            sf_ref[0] = st[...]

    return pl.pallas_call(
        body, name="dn_seq_fwd", grid=(2, n_b),
        in_specs=[pl.BlockSpec((1, rb, D), lambda d, t: (d, block_of(d, t), 0)),
                  pl.BlockSpec((1, rb, DN_W), lambda d, t: (d, block_of(d, t), 0)),
                  pl.BlockSpec((1, SEQ_G, HS, HS), lambda d, t: (d, block_of(d, t), 0, 0)),
                  pl.BlockSpec((rb, LANE), lambda d, t: (block_of(d, t), 0)),
                  _state_spec()],
        out_specs=[pl.BlockSpec((1, rb, D), lambda d, t: (d, block_of(d, t), 0)),
                   pl.BlockSpec((1, SEQ_G, NH, DK, DV), lambda d, t: (d, block_of(d, t), 0, 0, 0)),
                   _state_spec()],
        out_shape=[jax.ShapeDtypeStruct((2, t_len, D), F32),
                   jax.ShapeDtypeStruct((2, t_len // DN_C, NH, DK, DV), BF16),
                   jax.ShapeDtypeStruct((2, NH, DK, DV), F32)],
        scratch_shapes=[pltpu.VMEM((NH, DK, DV), F32)],
        compiler_params=_params(2),
    )(u, wqk, attn, gbc, s0)


def dn_seq_bwd(u, wqk, attn, gbc, ssave, do, ds_fin):
    t_len = u.shape[1]
    n_c = t_len // DN_C
    n_b = n_c // SEQ_G
    rb = DN_C * SEQ_G

    def block_of(d, t):
        return jnp.where(d == 0, n_b - 1 - t, t)

    def body(u_ref, wqk_ref, at_ref, gb_ref, ss_ref, do_ref, dsf_ref, du_ref, dwqk_ref, dat_ref, dgl_ref, ds0_ref, dst):
        d, t = pl.program_id(0), pl.program_id(1)

        @pl.when(t == 0)
        def _():
            dst[...] = dsf_ref[0]

        rows8 = lax.broadcasted_iota(jnp.int32, (8, LANE), 0)
        for gi in range(SEQ_G):
            gidx = jnp.where(d == 0, SEQ_G - 1 - gi, gi)
            rows = pl.ds(pl.multiple_of(gidx * DN_C, DN_C), DN_C)
            s_in = [ss_ref[0, gidx, h].astype(F32) for h in range(NH)]
            _, vjp = jax.vjp(_dn_seq4, *_dn_seq_inputs(u_ref, wqk_ref, at_ref, gb_ref, d, gidx), s_in)
            do4 = _stack_heads(lambda cs: do_ref[rows, cs], DV)
            du, dw, dqg, dkd, dat, dgl, ds = vjp((do4, [dst[h] for h in range(NH)]))
            dat_ref[0, gidx] = dat
            dgl_tile = jnp.zeros((8, LANE), F32)
            for h in range(NH):
                du_ref[0, rows, h * DV:(h + 1) * DV] = du[h]
                dwqk_ref[0, rows, h * DK:(h + 1) * DK] = dw[h]
                dwqk_ref[0, rows, NH * DK + h * DK:NH * DK + (h + 1) * DK] = dqg[h]
                dwqk_ref[0, rows, 2 * NH * DK + h * DK:2 * NH * DK + (h + 1) * DK] = dkd[h]
                dgl_tile = jnp.where(rows8 == h, dgl[h], dgl_tile)
                dst[h] = ds[h]
            dgl_ref[0, gidx] = dgl_tile

        @pl.when(t == n_b - 1)
        def _():
            ds0_ref[0] = dst[...]

    seq = lambda width: pl.BlockSpec((1, rb, width), lambda d, t: (d, block_of(d, t), 0))
    att = pl.BlockSpec((1, SEQ_G, HS, HS), lambda d, t: (d, block_of(d, t), 0, 0))
    return pl.pallas_call(
        body, name="dn_seq_bwd", grid=(2, n_b),
        in_specs=[seq(D), seq(DN_W), att, pl.BlockSpec((rb, LANE), lambda d, t: (block_of(d, t), 0)),
                  pl.BlockSpec((1, SEQ_G, NH, DK, DV), lambda d, t: (d, block_of(d, t), 0, 0, 0)),
                  pl.BlockSpec((rb, D), lambda d, t: (block_of(d, t), 0)), _state_spec()],
        out_specs=[seq(D), seq(DN_W), att, pl.BlockSpec((1, SEQ_G, 8, LANE), lambda d, t: (d, block_of(d, t), 0, 0)),
                   _state_spec()],
        out_shape=[jax.ShapeDtypeStruct((2, t_len, D), F32), jax.ShapeDtypeStruct((2, t_len, DN_W), F32),
                   jax.ShapeDtypeStruct((2, n_c, HS, HS), F32), jax.ShapeDtypeStruct((2, n_c, 8, LANE), F32),
                   jax.ShapeDtypeStruct((2, NH, DK, DV), F32)],
        scratch_shapes=[pltpu.VMEM((NH, DK, DV), F32)],
        compiler_params=_params(2),
    )(u, wqk, attn, gbc, ssave, do, ds_fin)


def dn_intra_bwd(dqkv, gbc, tinv, du, dwqk, dattn, dgl):
    t_len = dqkv.shape[0]
    rb = 128
    n_g = rb // DN_C

    def body(qk_ref, v_ref, gb_ref, ti_ref, du_ref, dwqk_ref, dat_ref, dgl_ref, dqk_ref, dv_ref, dgb_ref):
        d = pl.program_id(0)
        rev = d == 1
        for gi in range(n_g):
            rows = slice(gi * DN_C, (gi + 1) * DN_C)
            g4, b4 = _dn_gates4(gb_ref[rows, :], d)
            _, vjp, _ = jax.vjp(functools.partial(_dn_intra4, rev=rev, tinv_known=ti_ref[0, gi]),
                                _stack_heads(lambda cs: qk_ref[rows, cs], DK),
                                _stack_heads(lambda cs: qk_ref[rows, cs], DK, NH * DK),
                                _stack_heads(lambda cs: v_ref[rows, cs], DV), g4, b4, has_aux=True)
            dq, dk, dv, dg, db = vjp((_stack_heads(lambda cs: du_ref[0, rows, cs], DV),
                                      _stack_heads(lambda cs: dwqk_ref[0, rows, cs], DK),
                                      _stack_heads(lambda cs: dwqk_ref[0, rows, cs], DK, NH * DK),
                                      _stack_heads(lambda cs: dwqk_ref[0, rows, cs], DK, 2 * NH * DK),
                                      dat_ref[0, gi]))
            dgb = jnp.zeros((DN_C, LANE), F32)
            for h in range(NH):
                hr = _head_rows(h)
                dqk_ref[0, rows, h * DK:(h + 1) * DK] = dq[hr]
                dqk_ref[0, rows, NH * DK + h * DK:NH * DK + (h + 1) * DK] = dk[hr]
                dv_ref[0, rows, h * DV:(h + 1) * DV] = dv[hr]
                dgb = (dgb + _put_lane(dg[hr] + dgl_ref[0, gi, h:h + 1, 0:1], 8 + 4 * d + h, LANE)
                       + _put_lane(db[hr], 4 * d + h, LANE))
            dgb_ref[0, rows, :] = dgb

    seq = lambda width: pl.BlockSpec((1, rb, width), lambda d, i: (d, i, 0))
    return pl.pallas_call(
        body, name="dn_intra_bwd", grid=(2, t_len // rb),
        in_specs=[pl.BlockSpec((rb, D), lambda d, i: (i, 0)), pl.BlockSpec((rb, D), lambda d, i: (i, 1)),
                  pl.BlockSpec((rb, LANE), lambda d, i: (i, 0)),
                  pl.BlockSpec((1, n_g, HS, HS), lambda d, i: (d, i, 0, 0)), seq(D), seq(DN_W),
                  pl.BlockSpec((1, n_g, HS, HS), lambda d, i: (d, i, 0, 0)),
                  pl.BlockSpec((1, n_g, 8, LANE), lambda d, i: (d, i, 0, 0))],
        out_specs=[seq(D), seq(D), seq(LANE)],
        out_shape=[jax.ShapeDtypeStruct((2, t_len, D), F32), jax.ShapeDtypeStruct((2, t_len, D), F32),
                   jax.ShapeDtypeStruct((2, t_len, LANE), F32)],
        compiler_params=_params(2),
    )(dqkv, dqkv, gbc, tinv, du, dwqk, dattn, dgl)


def _pick_lane(x, lane_idx):
    lane = lax.broadcasted_iota(jnp.int32, x.shape, 1)
    return jnp.sum(jnp.where(lane == lane_idx, x, 0.0), axis=1, keepdims=True)


def _put_lane(col, lane_idx, width):
    lane = lax.broadcasted_iota(jnp.int32, (col.shape[0], width), 1)
    return jnp.where(lane == lane_idx, col, 0.0)


def _state_spec():
    return pl.BlockSpec((1, NH, DK, DV), lambda d, t: (d, 0, 0, 0))


def ret_fwd(rqk, p, s0):
    c = RET_C
    t_len = rqk.shape[0]
    n_c = t_len // c
    n_b = n_c // RET_G
    rb = c * RET_G

    def block_of(d, t):
        return jnp.where(d == 0, t, n_b - 1 - t)

    def body(qk_ref, v_ref, s0_ref, o_ref, ss_ref, sf_ref, st):
        d, t = pl.program_id(0), pl.program_id(1)
        rev = d == 1

        @pl.when(t == 0)
        def _():
            st[...] = s0_ref[0]

        for gi in range(RET_G):
            gidx = jnp.where(d == 0, gi, RET_G - 1 - gi)
            rows = pl.ds(pl.multiple_of(gidx * c, c), c)
            ss_ref[0, gidx] = st[...].astype(BF16)
            for h in range(NH):
                o, s2 = _ret_chunk(qk_ref[rows, h * DK:(h + 1) * DK],
                                   qk_ref[rows, NH * DK + h * DK:NH * DK + (h + 1) * DK],
                                   v_ref[rows, h * DV:(h + 1) * DV], st[h], _log_gamma(h), rev)
                o_ref[0, rows, h * DV:(h + 1) * DV] = o
                st[h] = s2

        @pl.when(t == n_b - 1)
        def _():
            sf_ref[0] = st[...]

    return pl.pallas_call(
        body, name="ret_fwd", grid=(2, n_b),
        in_specs=[pl.BlockSpec((rb, D), lambda d, t: (block_of(d, t), 0)),
                  pl.BlockSpec((rb, D), lambda d, t: (block_of(d, t), C_RV)),
                  _state_spec()],
        out_specs=[pl.BlockSpec((1, rb, D), lambda d, t: (d, block_of(d, t), 0)),
                   pl.BlockSpec((1, RET_G, NH, DK, DV), lambda d, t: (d, block_of(d, t), 0, 0, 0)),
                   _state_spec()],
        out_shape=[jax.ShapeDtypeStruct((2, t_len, D), F32),
                   jax.ShapeDtypeStruct((2, n_c, NH, DK, DV), BF16),
                   jax.ShapeDtypeStruct((2, NH, DK, DV), F32)],
        scratch_shapes=[pltpu.VMEM((NH, DK, DV), F32)],
        compiler_params=_params(2),
    )(rqk, p, s0)


def ret_bwd(rqk, p, ssave, do, ds_fin):
    c = RET_C
    t_len = rqk.shape[0]
    n_b = t_len // (c * RET_G)
    rb = c * RET_G

    def block_of(d, t):
        return jnp.where(d == 0, n_b - 1 - t, t)

    def body(qk_ref, v_ref, ss_ref, do_ref, dsf_ref, dqk_ref, dv_ref, ds0_ref, dst):
        d, t = pl.program_id(0), pl.program_id(1)
        rev = d == 1

        @pl.when(t == 0)
        def _():
            dst[...] = dsf_ref[0]

        for gi in range(RET_G):
            gidx = jnp.where(d == 0, RET_G - 1 - gi, gi)
            rows = pl.ds(pl.multiple_of(gidx * c, c), c)
            for h in range(NH):
                _, vjp = jax.vjp(functools.partial(_ret_chunk, lg=_log_gamma(h), rev=rev),
                                 qk_ref[rows, h * DK:(h + 1) * DK],
                                 qk_ref[rows, NH * DK + h * DK:NH * DK + (h + 1) * DK],
                                 v_ref[rows, h * DV:(h + 1) * DV], ss_ref[0, gidx, h].astype(F32))
                dq, dk, dv, ds = vjp((do_ref[rows, h * DV:(h + 1) * DV], dst[h]))
                dqk_ref[0, rows, h * DK:(h + 1) * DK] = dq
                dqk_ref[0, rows, NH * DK + h * DK:NH * DK + (h + 1) * DK] = dk
                dv_ref[0, rows, h * DV:(h + 1) * DV] = dv
                dst[h] = ds

        @pl.when(t == n_b - 1)
        def _():
            ds0_ref[0] = dst[...]

    seq_spec = pl.BlockSpec((1, rb, D), lambda d, t: (d, block_of(d, t), 0))
    return pl.pallas_call(
        body, name="ret_bwd", grid=(2, n_b),
        in_specs=[pl.BlockSpec((rb, D), lambda d, t: (block_of(d, t), 0)),
                  pl.BlockSpec((rb, D), lambda d, t: (block_of(d, t), C_RV)),
                  pl.BlockSpec((1, RET_G, NH, DK, DV), lambda d, t: (d, block_of(d, t), 0, 0, 0)),
                  pl.BlockSpec((rb, D), lambda d, t: (block_of(d, t), 0)),
                  _state_spec()],
        out_specs=[seq_spec, seq_spec, _state_spec()],
        out_shape=[jax.ShapeDtypeStruct((2, t_len, D), F32)] * 2 + [jax.ShapeDtypeStruct((2, NH, DK, DV), F32)],
        scratch_shapes=[pltpu.VMEM((NH, DK, DV), F32)],
        compiler_params=_params(2),
    )(rqk, p, ssave, do, ds_fin)


def _head_gate(o_ret, o_dn, rg, dz, nw):
    ret = o_ret * lax.rsqrt(jnp.mean(o_ret * o_ret, axis=-1, keepdims=True) + EPS) * (rg * _sigmoid(rg))
    dn = o_dn * lax.rsqrt(jnp.mean(o_dn * o_dn, axis=-1, keepdims=True) + EPS) * nw * (dz * _sigmoid(dz))
    return ret, dn


MIX_TM = 256


def _mix_specs():
    seq = lambda col: pl.BlockSpec((MIX_TM, D), functools.partial(lambda i, col: (i, col), col=col))
    pair = pl.BlockSpec((2, MIX_TM, D), lambda i: (0, i, 0))
    wfull = pl.BlockSpec((D, D), lambda i: (0, 0))
    vec = pl.BlockSpec((8, D), lambda i: (0, 0))
    return seq, pair, wfull, vec


def mixout_fwd(x1, vec, o_ret, o_dn, p, w_r, w_d, w_o):
    t_len = x1.shape[0]

    def body(x_ref, vec_ref, or_ref, od_ref, rg_ref, dz_ref, ga_ref, gb_ref, wr_ref, wd_ref, wo_ref,
             x2_ref, ret_ref, dn_ref, y_ref, yr_ref, yd_ref, z_ref):
        for h in range(NH):
            sl = slice(h * DV, (h + 1) * DV)
            ret, dn = _head_gate(or_ref[0, :, sl] + or_ref[1, :, sl], od_ref[0, :, sl] + od_ref[1, :, sl],
                                 rg_ref[:, sl], dz_ref[:, sl], vec_ref[1:2, sl])
            ret_ref[:, sl] = ret.astype(BF16)
            dn_ref[:, sl] = dn.astype(BF16)
        yr = jnp.dot(ret_ref[...], wr_ref[...], preferred_element_type=F32)
        yd = jnp.dot(dn_ref[...], wd_ref[...], preferred_element_type=F32)
        yr_ref[...] = yr.astype(BF16)
        yd_ref[...] = yd.astype(BF16)
        y = (_sigmoid(ga_ref[...]) * yr + _sigmoid(gb_ref[...]) * yd).astype(BF16)
        y_ref[...] = y
        z = jnp.dot(y, wo_ref[...], preferred_element_type=F32)
        z_ref[...] = z.astype(BF16)
        x2_ref[...] = x_ref[...] + vec_ref[0:1, :] * z

    seq, pair, wfull, vecs = _mix_specs()
    return pl.pallas_call(
        body, name="mixout_fwd", grid=(t_len // MIX_TM,),
        in_specs=[seq(0), vecs, pair, pair, seq(C_RG), seq(C_DZ), seq(C_GA), seq(C_GB), wfull, wfull, wfull],
        out_specs=[seq(0)] * 7,
        out_shape=[jax.ShapeDtypeStruct((t_len, D), F32)] + [jax.ShapeDtypeStruct((t_len, D), BF16)] * 6,
        compiler_params=_params(1),
    )(x1, vec, o_ret, o_dn, p, p, p, p, w_r, w_d, w_o)


def mixout_bwd(dx2, vec, o_ret, o_dn, p, yr, yd, z, w_r, w_d, w_o):
    t_len = dx2.shape[0]

    def body(dx_ref, vec_ref, or_ref, od_ref, rg_ref, dz_ref, ga_ref, gb_ref, yr_ref, yd_ref, z_ref,
             wr_ref, wd_ref, wo_ref,
             dor_ref, dod_ref, drg_ref, ddz_ref, dga_ref, dgb_ref, dyr_ref, dyd_ref, dzz_ref, pg_ref):
        @pl.when(pl.program_id(0) == 0)
        def _():
            pg_ref[...] = jnp.zeros_like(pg_ref)

        dx = dx_ref[...]
        pg_ref[0:1, :] += jnp.sum(dx * z_ref[...].astype(F32), axis=0, keepdims=True)
        dzz = (vec_ref[0:1, :] * dx).astype(BF16)
        dzz_ref[...] = dzz
        dy = _dot(dzz, wo_ref[...], NT)
        sa = _sigmoid(ga_ref[...])
        sb = _sigmoid(gb_ref[...])
        dyr = (dy * sa).astype(BF16)
        dyd = (dy * sb).astype(BF16)
        dyr_ref[...] = dyr
        dyd_ref[...] = dyd
        dga_ref[...] = (dy * yr_ref[...].astype(F32) * sa * (1.0 - sa)).astype(BF16)
        dgb_ref[...] = (dy * yd_ref[...].astype(F32) * sb * (1.0 - sb)).astype(BF16)
        dret = _dot(dyr, wr_ref[...], NT)
        ddn = _dot(dyd, wd_ref[...], NT)
        for h in range(NH):
            sl = slice(h * DV, (h + 1) * DV)
            _, vjp = jax.vjp(_head_gate, or_ref[0, :, sl] + or_ref[1, :, sl], od_ref[0, :, sl] + od_ref[1, :, sl],
                             rg_ref[:, sl], dz_ref[:, sl], vec_ref[1:2, sl])
            d_or, d_od, d_rg, d_dz, d_nw = vjp((dret[:, sl], ddn[:, sl]))
            dor_ref[:, sl] = d_or
            dod_ref[:, sl] = d_od
            drg_ref[:, sl] = d_rg.astype(BF16)
            ddz_ref[:, sl] = d_dz.astype(BF16)
            pg_ref[1:2, sl] += d_nw

    seq, pair, wfull, vecs = _mix_specs()
    return pl.pallas_call(
        body, name="mixout_bwd", grid=(t_len // MIX_TM,),
        in_specs=[seq(0), vecs, pair, pair, seq(C_RG), seq(C_DZ), seq(C_GA), seq(C_GB), seq(0), seq(0), seq(0),
                  wfull, wfull, wfull],
        out_specs=[seq(0)] * 9 + [vecs],
        out_shape=[jax.ShapeDtypeStruct((t_len, D), F32)] * 2 + [jax.ShapeDtypeStruct((t_len, D), BF16)] * 7
        + [jax.ShapeDtypeStruct((8, D), F32)],
        compiler_params=_params(1),
    )(dx2, vec, o_ret, o_dn, p, p, p, p, yr, yd, z, w_r, w_d, w_o)


def _final_loss(x, w, target):
    y = x * lax.rsqrt(jnp.mean(x * x, axis=-1, keepdims=True) + EPS) * w
    err = y - target
    return 0.5 * jnp.sum(jnp.mean(err * err, axis=-1, keepdims=True), axis=0, keepdims=True)


def final_fwd_bwd(x, vec, target):
    t_len = x.shape[0]
    tm = _row_block(t_len)

    def body(x_ref, vec_ref, t_ref, dx_ref, pg_ref, loss_ref):
        @pl.when(pl.program_id(0) == 0)
        def _():
            pg_ref[...] = jnp.zeros_like(pg_ref)
            loss_ref[...] = jnp.zeros_like(loss_ref)

        loss, vjp = jax.vjp(functools.partial(_final_loss, target=t_ref[...]), x_ref[...], vec_ref[0:1, :])
        dx, dw = vjp(jnp.ones((1, 1), F32))
        dx_ref[...] = dx
        pg_ref[0:1, :] += dw
        loss_ref[...] += jnp.broadcast_to(loss, loss_ref.shape)

    return pl.pallas_call(
        body, name="final_fwd_bwd", grid=(t_len // tm,),
        in_specs=[pl.BlockSpec((tm, D), lambda i: (i, 0)), pl.BlockSpec((8, D), lambda i: (0, 0)),
                  pl.BlockSpec((tm, D), lambda i: (i, 0))],
        out_specs=[pl.BlockSpec((tm, D), lambda i: (i, 0)), pl.BlockSpec((8, D), lambda i: (0, 0)),
                   pl.BlockSpec((8, LANE), lambda i: (0, 0))],
        out_shape=[jax.ShapeDtypeStruct((t_len, D), F32), jax.ShapeDtypeStruct((8, D), F32),
                   jax.ShapeDtypeStruct((8, LANE), F32)],
        compiler_params=_params(1),
    )(x, vec, target)


def _vec8(*rows):
    rows = list(rows) + [jnp.zeros((D,), F32)] * (8 - len(rows))
    return jnp.stack(rows)


def _layer_vecs(lw, m):
    nw = lw["norm_w"]
    return dict(ffn1=_vec8(nw[0], m[0], m[1], m[2]), proj=_vec8(nw[1], m[3], m[4]),
                mix=_vec8(m[5], lw["dn_norm_t"]), ffn2=_vec8(nw[2], m[6], m[7], m[8]))


def _rope_tables(t_len, grid_w=64, base=10000.0):
    n_freq = DK // 4
    inv = base ** (-jnp.arange(n_freq, dtype=F32) / n_freq)
    tok = jnp.arange(t_len)
    ang = jnp.concatenate([(tok // grid_w).astype(F32)[:, None] * inv, (tok % grid_w).astype(F32)[:, None] * inv],
                          axis=-1)
    cos, sin = jnp.cos(ang), jnp.sin(ang)
    return jnp.concatenate([cos, cos], axis=-1), jnp.concatenate([-sin, sin], axis=-1)


def _stream_fwd_a(x0, lw, vecs, rope):
    x1, h1, act1, sl1, mg1, f1 = ffn_fwd(x0, vecs["ffn1"], lw["wgu1"], lw["wd1"])
    p, h2 = proj_fwd(x1, vecs["proj"], lw["w_in"])
    rqk, dqkv, gbc = feat_fwd(p, rope[0], rope[1], lw["conv_w8"], lw["gparams"])
    return dict(x0=x0, h1=h1, act1=act1, sl1=sl1, mg1=mg1, f1=f1, x1=x1, h2=h2, p=p, rqk=rqk, dqkv=dqkv, gbc=gbc)


def _stream_mix(sv, s0_ret, s0_dn):
    sv["o_ret"], sv["ss_ret"], sf_ret = ret_fwd(sv["rqk"], sv["p"], s0_ret)
    sv["dn_u"], sv["dn_wqk"], sv["dn_attn"], sv["dn_tinv"] = dn_intra_fwd(sv["dqkv"], sv["gbc"])
    sv["o_dn"], sv["ss_dn"], sf_dn = dn_seq_fwd(sv["dn_u"], sv["dn_wqk"], sv["dn_attn"], sv["gbc"], s0_dn)
    return sf_ret, sf_dn


def _stream_fwd_b(sv, lw, vecs):
    x2, ret, dn, y, yr, yd, z = mixout_fwd(sv["x1"], vecs["mix"], sv["o_ret"], sv["o_dn"], sv["p"],
                                           lw["w_r"], lw["w_d"], lw["w_o"])
    x3, h3, act3, sl3, mg3, f3 = ffn_fwd(x2, vecs["ffn2"], lw["wgu2"], lw["wd2"])
    sv.update(ret=ret, dn=dn, y=y, yr=yr, yd=yd, z=z, x2=x2, h3=h3, act3=act3, sl3=sl3, mg3=mg3, f3=f3)
    return x3


def layer_fwd(xs, cs, lw, ropes):
    vx, vc = _layer_vecs(lw, lw["mx"]), _layer_vecs(lw, lw["mc"])
    sx = _stream_fwd_a(xs, lw, vx, ropes[0])
    sc = _stream_fwd_a(cs, lw, vc, ropes[1])
    zero = jnp.zeros((2, NH, DK, DV), F32)
    sf_ret, sf_dn = _stream_mix(sc, zero, zero)
    _stream_mix(sx, sf_ret, sf_dn)
    x3 = _stream_fwd_b(sx, lw, vx)
    c3 = _stream_fwd_b(sc, lw, vc)
    return x3, c3, (sx, sc)


def _stream_bwd_a(dx3, sv, lw, vecs):
    dg3, du3, dy3, pg_a = ffn_bwd_act(dx3, vecs["ffn2"], sv["sl3"], sv["mg3"], sv["f3"], lw["wd2"])
    dx2, pg_b = nt_norm_bwd([dg3, du3], lw["wgu2"], [0, 2], FFN_HB, 2, sv["x2"], dx3, vecs["ffn2"], "ffn_bwd_in")
    (dor, dod, drg, ddz, dga, dgb, dyr, dyd, dzz, pg_m) = mixout_bwd(
        dx2, vecs["mix"], sv["o_ret"], sv["o_dn"], sv["p"], sv["yr"], sv["yd"], sv["z"], lw["w_r"], lw["w_d"], lw["w_o"])
    return dict(dg3=dg3, du3=du3, dy3=dy3, pg_a2=pg_a, pg_b2=pg_b, dx2=dx2, dor=dor, dod=dod,
                drg=drg, ddz=ddz, dga=dga, dgb=dgb, dyr=dyr, dyd=dyd, dzz=dzz, pg_m=pg_m)


def _stream_bwd_mix(bw, sv, dsf_ret, dsf_dn):
    bw["dqk_r"], bw["drv"], ds0_ret = ret_bwd(sv["rqk"], sv["p"], sv["ss_ret"], bw["dor"], dsf_ret)
    du, dwqk, dattn, dgl, ds0_dn = dn_seq_bwd(sv["dn_u"], sv["dn_wqk"], sv["dn_attn"], sv["gbc"], sv["ss_dn"],
                                              bw["dod"], dsf_dn)
    bw["dqk_d"], bw["dvv_d"], bw["dgbc"] = dn_intra_bwd(sv["dqkv"], sv["gbc"], sv["dn_tinv"], du, dwqk, dattn, dgl)
    return ds0_ret, ds0_dn


def _stream_bwd_b(bw, sv, lw, vecs, rope):
    dp_rqk, dp_rv, dp_dqk, dp_dvv, dp_dba, bw["dcw"], bw["dgp"] = feat_bwd(
        sv["p"], rope[0], rope[1], lw["conv_w8"], lw["gparams"], bw["dqk_r"], bw["drv"], bw["dqk_d"], bw["dvv_d"],
        bw["dgbc"])
    bw["dp_segs"] = [dp_rqk, dp_rv, bw["drg"], dp_dqk, dp_dvv, bw["ddz"], bw["dga"], bw["dgb"]]
    bw["dp_dba"] = dp_dba
    dx1, bw["pg_p"] = proj_bwd_in(bw["dp_segs"], dp_dba, lw["w_in"], sv["x1"], bw["dx2"], vecs["proj"])
    bw["dg1"], bw["du1"], bw["dy1"], bw["pg_a1"] = ffn_bwd_act(dx1, vecs["ffn1"], sv["sl1"], sv["mg1"], sv["f1"],
                                                              lw["wd1"])
    dx0, bw["pg_b1"] = nt_norm_bwd([bw["dg1"], bw["du1"]], lw["wgu1"], [0, 2], FFN_HB, 2, sv["x0"], dx1,
                                   vecs["ffn1"], "ffn_bwd_in")
    return dx0


def _stream_pgrads(bw):
    a1, b1, pp, pm, a2, b2 = bw["pg_a1"], bw["pg_b1"], bw["pg_p"], bw["pg_m"], bw["pg_a2"], bw["pg_b2"]
    dmod = jnp.stack([b1[1], b1[2], a1[3], pp[1], pp[2], pm[0], b2[1], b2[2], a2[3]])
    dnorm = jnp.stack([b1[0], pp[0], b2[0]])
    ddnw = pm[1].reshape(NH, DV).sum(axis=0)
    return dmod, dnorm, ddnw, bw["dcw"][:CONV_K], bw["dgp"][0, 8:16], bw["dgp"][1, 8:16]


def layer_bwd(dx3, dc3, lw, saved, ropes):
    sx, sc = saved
    vx, vc = _layer_vecs(lw, lw["mx"]), _layer_vecs(lw, lw["mc"])
    bx = _stream_bwd_a(dx3, sx, lw, vx)
    bc = _stream_bwd_a(dc3, sc, lw, vc)
    zero = jnp.zeros((2, NH, DK, DV), F32)
    ds0_ret, ds0_dn = _stream_bwd_mix(bx, sx, zero, zero)
    _stream_bwd_mix(bc, sc, ds0_ret, ds0_dn)
    dx0 = _stream_bwd_b(bx, sx, lw, vx, ropes[0])
    dc0 = _stream_bwd_b(bc, sc, lw, vc, ropes[1])

    def wgrad(a, b, tn, name):
        ax = sx[a] if a in sx else bx[a]
        ac = sc[a] if a in sc else bc[a]
        return tn_matmul(ax, bx[b], ac, bc[b], tn, name)

    gw = dict(
        wg1=wgrad("h1", "dg1", FFN_HB, "dw_ffn_gu"), wu1=wgrad("h1", "du1", FFN_HB, "dw_ffn_gu"),
        wd1=wgrad("act1", "dy1", 512, "dw_ffn_d"),
        w_in=jnp.concatenate([dw_in(sx["h2"], bx["dp_segs"], sc["h2"], bc["dp_segs"]),
                              wgrad("h2", "dp_dba", LANE, "dw_in_gate")], axis=-1),
        w_r=wgrad("ret", "dyr", 512, "dw_sq"), w_d=wgrad("dn", "dyd", 512, "dw_sq"), w_o=wgrad("y", "dzz", 512, "dw_sq"),
        wg2=wgrad("h3", "dg3", FFN_HB, "dw_ffn_gu"), wu2=wgrad("h3", "du3", FFN_HB, "dw_ffn_gu"),
        wd2=wgrad("act3", "dy3", 512, "dw_ffn_d"),
    )
    px, pc = _stream_pgrads(bx), _stream_pgrads(bc)
    small = dict(dmx=px[0], dmc=pc[0], norm_w=px[1] + pc[1], dn_norm_w=px[2] + pc[2], conv_w=px[3] + pc[3],
                 a_log=px[4] + pc[4], dt_bias=px[5] + pc[5])
    return dx0, dc0, gw, small


def local_step(x, ctx, target, final_norm_w, lws):
    t_len, t_ctx = x.shape[0], ctx.shape[0]
    ropes = (_rope_tables(t_len), (jnp.ones((t_ctx, LANE), F32), jnp.zeros((t_ctx, LANE), F32)))
    xs, cs, saved = x, ctx, []
    for lw in lws:
        xs, cs, sv = layer_fwd(xs, cs, lw, ropes)
        saved.append(sv)
    dx, pg_f, loss = final_fwd_bwd(xs, _vec8(final_norm_w), target)
    dc = jnp.zeros_like(ctx)
    gws, smalls = [None] * len(lws), [None] * len(lws)
    for l in reversed(range(len(lws))):
        dx, dc, gws[l], smalls[l] = layer_bwd(dx, dc, lws[l], saved[l], ropes)
    return loss[0, 0], dx, gws, smalls, pg_f[0]


ADA_ROWS = 16
ADA_COLS = N_MOD * D // N_DEV


def ada_fwd(cc, ada_w, ada_b_cols):
    def body(cc_ref, w_ref, b_ref, o_ref):
        cv = cc_ref[...]
        o_ref[0] = _dot_hi(cv * _sigmoid(cv), w_ref[0]) + b_ref[0]

    return pl.pallas_call(
        body, name="ada_fwd", grid=(DEPTH,),
        in_specs=[pl.BlockSpec((ADA_ROWS, D), lambda l: (0, 0)), pl.BlockSpec((1, D, ADA_COLS), lambda l: (l, 0, 0)),
                  pl.BlockSpec((1, 1, ADA_COLS), lambda l: (l, 0, 0))],
        out_specs=pl.BlockSpec((1, ADA_ROWS, ADA_COLS), lambda l: (l, 0, 0)),
        out_shape=jax.ShapeDtypeStruct((DEPTH, ADA_ROWS, ADA_COLS), F32),
        compiler_params=_params(1),
    )(cc, ada_w, ada_b_cols)


def ada_bwd(cc, dmod, ada_w):
    def body(cc_ref, d_ref, w_ref, gw_ref, ds_ref):
        cv = cc_ref[...]
        gw_ref[0] = _dot_hi(cv * _sigmoid(cv), d_ref[0], TN)
        ds_ref[0] = _dot_hi(d_ref[0], w_ref[0], NT)

    return pl.pallas_call(
        body, name="ada_bwd", grid=(DEPTH,),
        in_specs=[pl.BlockSpec((ADA_ROWS, D), lambda l: (0, 0)),
                  pl.BlockSpec((1, ADA_ROWS, ADA_COLS), lambda l: (l, 0, 0)),
                  pl.BlockSpec((1, D, ADA_COLS), lambda l: (l, 0, 0))],
        out_specs=[pl.BlockSpec((1, D, ADA_COLS), lambda l: (l, 0, 0)), pl.BlockSpec((1, ADA_ROWS, D), lambda l: (l, 0, 0))],
        out_shape=[jax.ShapeDtypeStruct((DEPTH, D, ADA_COLS), F32), jax.ShapeDtypeStruct((DEPTH, ADA_ROWS, D), F32)],
        compiler_params=_params(1),
    )(cc, dmod, ada_w)


def c_ctx_grad(parts, c_ctx8):
    def body(p_ref, c_ref, o_ref):
        acc = p_ref[0]
        for j in range(1, N_DEV):
            acc = acc + p_ref[j]
        cv = c_ref[...]
        sg = _sigmoid(cv)
        o_ref[...] = acc * (sg + cv * sg * (1.0 - sg))

    return pl.pallas_call(body, name="c_ctx_grad", out_shape=jax.ShapeDtypeStruct((8, LANE), F32))(parts, c_ctx8)


def sum_slots(parts):
    n_slot, rows, _ = parts.shape
    tr = 8
    for cand in (1024, 512, 256, 128, 64, 32, 16, 8):
        if rows % cand == 0:
            tr = cand
            break

    def body(p_ref, o_ref):
        acc = p_ref[0]
        for j in range(1, n_slot):
            acc = acc + p_ref[j]
        o_ref[...] = acc

    return pl.pallas_call(
        body, name="sum_slots", grid=(rows // tr,),
        in_specs=[pl.BlockSpec((n_slot, tr, LANE), lambda i: (0, i, 0))],
        out_specs=pl.BlockSpec((tr, LANE), lambda i: (i, 0)),
        out_shape=jax.ShapeDtypeStruct((rows, LANE), F32),
        compiler_params=_params(1),
    )(parts)


def adamw(parts, w, m, v, rows_blk, name):
    n_slot, n_l, n_a, n_b = parts.shape

    def body(p_ref, w_ref, m_ref, v_ref, g_ref, d_ref, m2_ref, v2_ref):
        g = p_ref[0].astype(F32)
        for j in range(1, n_slot):
            g = g + p_ref[j].astype(F32)
        m2 = ADAM_B1 * m_ref[...] + (1.0 - ADAM_B1) * g
        v2 = ADAM_B2 * v_ref[...] + (1.0 - ADAM_B2) * (g * g)
        m_hat = m2 / (1.0 - ADAM_B1 ** ADAM_STEP)
        v_hat = v2 / (1.0 - ADAM_B2 ** ADAM_STEP)
        g_ref[...] = g
        m2_ref[...] = m2
        v2_ref[...] = v2
        d_ref[...] = -ADAM_LR * (m_hat / (jnp.sqrt(v_hat) + ADAM_EPS) + ADAM_WD * w_ref[...])

    blk = pl.BlockSpec((1, rows_blk, n_b), lambda l, i: (l, i, 0))
    return pl.pallas_call(
        body, name=name, grid=(n_l, n_a // rows_blk),
        in_specs=[pl.BlockSpec((n_slot, 1, rows_blk, n_b), lambda l, i: (0, l, i, 0)), blk, blk, blk],
        out_specs=[blk] * 4,
        out_shape=[jax.ShapeDtypeStruct((n_l, n_a, n_b), F32)] * 4,
        compiler_params=_params(2),
    )(parts, w, m, v)


MESH = pl.DeviceIdType.MESH


def _my_index():
    return 4 * lax.axis_index("x") + 2 * lax.axis_index("y") + lax.axis_index("c")


def all_gather(shards, name):
    n = len(shards)

    def body(*refs):
        x_refs, out_refs = refs[:n], refs[n:2 * n]
        send_sems, recv_sems, local_sems = refs[2 * n:]
        x, y, c = lax.axis_index("x"), lax.axis_index("y"), lax.axis_index("c")
        me, sibling = (x, y, c), (x, y, 1 - c)
        chips = [(1 - x, y), (x, 1 - y), (1 - x, 1 - y)]

        def slot(a, px, py, pc):
            return out_refs[a].at[4 * px + 2 * py + pc]

        def copy(a, k, block, to, src=None):
            return pltpu.make_async_remote_copy(
                src_ref=slot(a, *block) if src is None else src, dst_ref=slot(a, *block),
                send_sem=send_sems.at[7 * a + k], recv_sem=recv_sems.at[7 * a + k], device_id=to, device_id_type=MESH)

        mine = [pltpu.make_async_copy(x_refs[a], slot(a, *me), local_sems.at[a]) for a in range(n)]
        for cp in mine:
            cp.start()
        first = []
        for a in range(n):
            first.append(copy(a, 0, me, sibling, src=x_refs[a]))
            first += [copy(a, 1 + j, me, (*chip, c), src=x_refs[a]) for j, chip in enumerate(chips)]
        for cp in first:
            cp.start()
        passed = []
        for j, chip in enumerate(chips):
            for a in range(n):
                copy(a, 1 + j, (*chip, c), me).wait_recv()
                fwd = copy(a, 4 + j, (*chip, c), sibling)
                fwd.start()
                passed.append(fwd)
        for a in range(n):
            copy(a, 0, sibling, me).wait_recv()
            for j, chip in enumerate(chips):
                copy(a, 4 + j, (*chip, 1 - c), me).wait_recv()
        for cp in first + passed:
            cp.wait_send()
        for cp in mine:
            cp.wait()

    return pl.pallas_call(
        body, name=name,
        in_specs=[pl.BlockSpec(memory_space=pl.ANY)] * n,
        out_specs=[pl.BlockSpec(memory_space=pl.ANY)] * n,
        out_shape=[jax.ShapeDtypeStruct((N_DEV,) + s.shape, s.dtype) for s in shards],
        scratch_shapes=[pltpu.SemaphoreType.DMA((7 * n,)), pltpu.SemaphoreType.DMA((7 * n,)),
                        pltpu.SemaphoreType.DMA((n,))],
    )(*shards)


def pair_exchange(arrs, name):
    n = len(arrs)

    def body(*refs):
        x_refs, out_refs = refs[:n], refs[n:2 * n]
        send_sems, recv_sems = refs[2 * n:]
        sibling = (lax.axis_index("x"), lax.axis_index("y"), 1 - lax.axis_index("c"))
        copies = [pltpu.make_async_remote_copy(src_ref=x_refs[a], dst_ref=out_refs[a], send_sem=send_sems.at[a],
                                               recv_sem=recv_sems.at[a], device_id=sibling, device_id_type=MESH)
                  for a in range(n)]
        for cp in copies:
            cp.start()
        for cp in copies:
            cp.wait()

    return pl.pallas_call(
        body, name=name,
        in_specs=[pl.BlockSpec(memory_space=pl.ANY)] * n,
        out_specs=[pl.BlockSpec(memory_space=pl.ANY)] * n,
        out_shape=[jax.ShapeDtypeStruct(s.shape, s.dtype) for s in arrs],
        scratch_shapes=[pltpu.SemaphoreType.DMA((n,)), pltpu.SemaphoreType.DMA((n,))],
    )(*arrs)


def chip_exchange(parts, name):
    n = len(parts)

    def body(*refs):
        p_refs, q_refs = refs[:n], refs[n:2 * n]
        send_sems, recv_sems, local_sems = refs[2 * n:]
        x, y, c = lax.axis_index("x"), lax.axis_index("y"), lax.axis_index("c")
        my_chip = 2 * x + y
        mine = [pltpu.make_async_copy(p_refs[a].at[my_chip], q_refs[a].at[my_chip], local_sems.at[a]) for a in range(n)]
        for cp in mine:
            cp.start()
        copies = []
        for r in range(1, 4):
            px, py = x ^ (r >> 1), y ^ (r & 1)
            peer = 2 * px + py
            for a in range(n):
                k = 3 * a + r - 1
                cp = pltpu.make_async_remote_copy(
                    src_ref=p_refs[a].at[peer], dst_ref=q_refs[a].at[my_chip], send_sem=send_sems.at[k],
                    recv_sem=recv_sems.at[k], device_id=(px, py, c), device_id_type=MESH)
                cp.start()
                copies.append((cp, a, peer, k))
        for cp, a, peer, k in copies:
            pltpu.make_async_remote_copy(
                src_ref=p_refs[a].at[peer], dst_ref=q_refs[a].at[peer], send_sem=send_sems.at[k],
                recv_sem=recv_sems.at[k], device_id=(x, y, c), device_id_type=MESH).wait_recv()
        for cp, _, _, _ in copies:
            cp.wait_send()
        for cp in mine:
            cp.wait()

    return pl.pallas_call(
        body, name=name,
        in_specs=[pl.BlockSpec(memory_space=pl.ANY)] * n,
        out_specs=[pl.BlockSpec(memory_space=pl.ANY)] * n,
        out_shape=[jax.ShapeDtypeStruct(p.shape, p.dtype) for p in parts],
        scratch_shapes=[pltpu.SemaphoreType.DMA((3 * n,)), pltpu.SemaphoreType.DMA((3 * n,)),
                        pltpu.SemaphoreType.DMA((n,))],
    )(*parts)


def add_bf16(a, b, rows_blk, name):
    n_l, n_a, n_b = a.shape

    def body(a_ref, b_ref, o_ref):
        o_ref[...] = (a_ref[...].astype(F32) + b_ref[...].astype(F32)).astype(BF16)

    blk = pl.BlockSpec((1, rows_blk, n_b), lambda l, i: (l, i, 0))
    return pl.pallas_call(
        body, name=name, grid=(n_l, n_a // rows_blk), in_specs=[blk, blk], out_specs=blk,
        out_shape=jax.ShapeDtypeStruct(a.shape, BF16), compiler_params=_params(2),
    )(a, b)


WEIGHT_ORDER = ("c_ctx", "ada_w", "ada_b", "norm_w", "ffn1_wgu", "ffn1_wd", "w_in", "dn_conv_w", "dn_a_log",
                "dn_dt_bias", "dn_norm_w", "w_ret_out", "w_dn_out", "w_o", "ffn2_wgu", "ffn2_wd", "final_norm_w")
BIG = (("ffn1_wgu", "col"), ("ffn1_wd", "row"), ("w_in", "col"), ("w_ret_out", "row"), ("w_dn_out", "row"),
       ("w_o", "row"), ("ffn2_wgu", "col"), ("ffn2_wd", "row"))
LOCAL = ("ada_w", "c_ctx", "ada_b", "norm_w", "dn_conv_w", "dn_a_log", "dn_dt_bias", "dn_norm_w", "final_norm_w")


def _pack(arrs, row_mult, lead=None):
    if lead is None:
        flat = jnp.concatenate([a.reshape(-1) for a in arrs])
        n = flat.shape[0]
    else:
        flat = jnp.concatenate([a.reshape(lead, -1) for a in arrs], axis=1)
        n = flat.shape[1]
    unit = LANE * row_mult
    total = -(-n // unit) * unit
    if lead is None:
        return jnp.pad(flat, (0, total - n)).reshape(total // LANE, LANE)
    return jnp.pad(flat, ((0, 0), (0, total - n))).reshape(lead, total // LANE, LANE)


def _unpack(packed, shapes, lead=None):
    flat = packed.reshape(-1) if lead is None else packed.reshape(lead, -1)
    out, off = [], 0
    for shp in shapes:
        n = math.prod(shp)
        if lead is None:
            out.append(flat[off:off + n].reshape(shp))
        else:
            out.append(flat[:, off:off + n].reshape((lead,) + tuple(shp)))
        off += n
    return out


def _join_shards(g, kind):
    lead = tuple(range(1, g.ndim - 2))
    a, b = g.shape[-2:]
    if kind == "col":
        return g.transpose(*lead, g.ndim - 2, 0, g.ndim - 1).reshape(g.shape[1:-2] + (a, N_DEV * b))
    return g.transpose(*lead, 0, g.ndim - 2, g.ndim - 1).reshape(g.shape[1:-2] + (N_DEV * a, b))


def _split_shards(full, kind):
    a, b = full.shape
    if kind == "col":
        return full.reshape(a, N_DEV, b // N_DEV).transpose(1, 0, 2)
    return full.reshape(N_DEV, a // N_DEV, b)


def _pad_w_in(w):
    return jnp.concatenate([w[..., :6144], w[..., 6160:8208], w[..., 6144:6160],
                            jnp.zeros(w.shape[:-1] + (PW - 8208,), w.dtype)], axis=-1)


def _unpad_w_in(g):
    return jnp.concatenate([g[..., :6144], g[..., 8192:8208], g[..., 6144:8192]], axis=-1)


def kernel(x, c, ctx, c_ctx, ada_w, ada_b, norm_w, ffn1_wgu, ffn1_wd, w_in, dn_conv_w, dn_a_log, dn_dt_bias, dn_norm_w, w_ret_out, w_dn_out, w_o, ffn2_wgu, ffn2_wd, final_norm_w, loss_target, m_c_ctx, m_ada_w, m_ada_b, m_norm_w, m_ffn1_wgu, m_ffn1_wd, m_w_in, m_dn_conv_w, m_dn_a_log, m_dn_dt_bias, m_dn_norm_w, m_w_ret_out, m_w_dn_out, m_w_o, m_ffn2_wgu, m_ffn2_wd, m_final_norm_w, v_c_ctx, v_ada_w, v_ada_b, v_norm_w, v_ffn1_wgu, v_ffn1_wd, v_w_in, v_dn_conv_w, v_dn_a_log, v_dn_dt_bias, v_dn_norm_w, v_w_ret_out, v_w_dn_out, v_w_o, v_ffn2_wgu, v_ffn2_wd, v_final_norm_w):
    w = dict(c_ctx=c_ctx, ada_w=ada_w, ada_b=ada_b, norm_w=norm_w, ffn1_wgu=ffn1_wgu, ffn1_wd=ffn1_wd, w_in=w_in, dn_conv_w=dn_conv_w, dn_a_log=dn_a_log, dn_dt_bias=dn_dt_bias, dn_norm_w=dn_norm_w, w_ret_out=w_ret_out, w_dn_out=w_dn_out, w_o=w_o, ffn2_wgu=ffn2_wgu, ffn2_wd=ffn2_wd, final_norm_w=final_norm_w)
    m = dict(c_ctx=m_c_ctx, ada_w=m_ada_w, ada_b=m_ada_b, norm_w=m_norm_w, ffn1_wgu=m_ffn1_wgu, ffn1_wd=m_ffn1_wd, w_in=m_w_in, dn_conv_w=m_dn_conv_w, dn_a_log=m_dn_a_log, dn_dt_bias=m_dn_dt_bias, dn_norm_w=m_dn_norm_w, w_ret_out=m_w_ret_out, w_dn_out=m_w_dn_out, w_o=m_w_o, ffn2_wgu=m_ffn2_wgu, ffn2_wd=m_ffn2_wd, final_norm_w=m_final_norm_w)
    v = dict(c_ctx=v_c_ctx, ada_w=v_ada_w, ada_b=v_ada_b, norm_w=v_norm_w, ffn1_wgu=v_ffn1_wgu, ffn1_wd=v_ffn1_wd, w_in=v_w_in, dn_conv_w=v_dn_conv_w, dn_a_log=v_dn_a_log, dn_dt_bias=v_dn_dt_bias, dn_norm_w=v_dn_norm_w, w_ret_out=v_w_ret_out, w_dn_out=v_w_dn_out, w_o=v_w_o, ffn2_wgu=v_ffn2_wgu, ffn2_wd=v_ffn2_wd, final_norm_w=v_final_norm_w)
    me = _my_index()
    big_names = [n for n, _ in BIG]

    g_big = all_gather([w[n].astype(BF16) for n in big_names], "ag_weights")
    full = {n: _join_shards(s, kind) for (n, kind), s in zip(BIG, g_big)}
    small_shapes = [norm_w.shape, dn_conv_w.shape, c.shape]
    g_small = all_gather([_pack([norm_w, dn_conv_w, c], 8)], "ag_small")[0]
    norm_s, conv_s, c_all = _unpack(g_small, small_shapes, lead=N_DEV)
    norm_full, conv_full = _join_shards(norm_s, "col"), _join_shards(conv_s, "col")

    cc = jnp.concatenate([c_all.reshape(N_DEV, D), c_ctx[None], jnp.zeros((ADA_ROWS - N_DEV - 1, D), F32)])
    ada_b_cols = lax.dynamic_slice_in_dim(ada_b, me * ADA_COLS, ADA_COLS, axis=1)[:, None, :]
    mods_part = ada_fwd(cc, ada_w, ada_b_cols)
    g_mods = all_gather([mods_part], "ag_mods")[0]
    mods_all = g_mods.transpose(1, 2, 0, 3).reshape(DEPTH, ADA_ROWS, N_MOD * D)
    mx = lax.dynamic_index_in_dim(mods_all, me, axis=1, keepdims=False).reshape(DEPTH, N_MOD, D)
    mc = mods_all[:, N_DEV].reshape(DEPTH, N_MOD, D)

    gparams = jnp.pad(jnp.stack([-jnp.exp(dn_a_log).reshape(DEPTH, 8), dn_dt_bias.reshape(DEPTH, 8)], axis=1),
                      ((0, 0), (0, 6), (8, LANE - 16)))
    stacked = dict(
        wgu1=full["ffn1_wgu"], wd1=full["ffn1_wd"], w_in=_pad_w_in(full["w_in"]), w_r=full["w_ret_out"],
        w_d=full["w_dn_out"], w_o=full["w_o"], wgu2=full["ffn2_wgu"], wd2=full["ffn2_wd"], norm_w=norm_full,
        conv_w8=jnp.pad(conv_full, ((0, 0), (0, 8 - CONV_K), (0, 0))), gparams=gparams,
        dn_norm_t=jnp.tile(dn_norm_w, (1, NH)), mx=mx, mc=mc)
    lws = [{k: a[l] for k, a in stacked.items()} for l in range(DEPTH)]
    loss_l, gx, gws, smalls, d_fnw = local_step(x[0], ctx[0], loss_target[0], final_norm_w, lws)

    sm_list = [jnp.stack([s[k] for s in smalls]) for k in ("dmx", "dmc", "norm_w", "conv_w", "a_log", "dt_bias",
                                                            "dn_norm_w")] + [d_fnw, loss_l.reshape(1)]
    sm_shapes = [a.shape for a in sm_list]
    g_sm = all_gather([_pack(sm_list, 8)], "ag_small_grads")[0]
    (dmx_sum, dmc_sum, g_norm, g_conv, g_alog, g_dtb, g_dnw, g_fnw, loss) = _unpack(sum_slots(g_sm), sm_shapes)
    dmx_all = _unpack(g_sm, sm_shapes[:1], lead=N_DEV)[0].reshape(N_DEV, DEPTH, N_MOD * D)
    dmc_sum = dmc_sum.reshape(DEPTH, N_MOD * D)
    dmx_sum = dmx_sum.reshape(DEPTH, N_MOD * D)
    dmod = jnp.concatenate([
        lax.dynamic_slice_in_dim(dmx_all, me * ADA_COLS, ADA_COLS, axis=2).transpose(1, 0, 2),
        lax.dynamic_slice_in_dim(dmc_sum, me * ADA_COLS, ADA_COLS, axis=1)[:, None, :],
        jnp.zeros((DEPTH, ADA_ROWS - N_DEV - 1, ADA_COLS), F32)], axis=1)
    g_ada_w, d_sil = ada_bwd(cc, dmod, ada_w)
    g_cc = all_gather([d_sil[:, N_DEV].sum(axis=0).reshape(8, LANE)], "ag_c_ctx")[0]
    grads = dict(
        ada_w=g_ada_w, c_ctx=c_ctx_grad(g_cc, c_ctx.reshape(8, LANE)).reshape(D), ada_b=dmx_sum + dmc_sum,
        norm_w=lax.dynamic_slice_in_dim(g_norm, me * (D // N_DEV), D // N_DEV, axis=2),
        dn_conv_w=lax.dynamic_slice_in_dim(g_conv, me * (2 * D // N_DEV), 2 * D // N_DEV, axis=2),
        dn_a_log=g_alog.reshape(dn_a_log.shape), dn_dt_bias=g_dtb.reshape(dn_dt_bias.shape), dn_norm_w=g_dnw,
        final_norm_w=g_fnw)

    def layer_full(gw):
        return dict(ffn1_wgu=jnp.concatenate([gw["wg1"], gw["wu1"]], axis=-1), ffn1_wd=gw["wd1"],
                    w_in=_unpad_w_in(gw["w_in"]), w_ret_out=gw["w_r"], w_dn_out=gw["w_d"], w_o=gw["w_o"],
                    ffn2_wgu=jnp.concatenate([gw["wg2"], gw["wu2"]], axis=-1), ffn2_wd=gw["wd2"])

    full_g = [layer_full(gw) for gw in gws]
    parts = [jnp.stack([_split_shards(fg[n], kind) for fg in full_g], axis=1).astype(BF16) for n, kind in BIG]
    core = lax.axis_index("c")
    rows_of = {n: (256 if w[n].shape[1] % 256 == 0 else w[n].shape[1]) for n in big_names}

    def for_core(p, which):
        return lax.dynamic_index_in_dim(p.reshape((4, 2) + p.shape[1:]), which, axis=1, keepdims=False)

    from_sibling = pair_exchange([for_core(p, 1 - core) for p in parts], "a2a_pair")
    summed = [add_bf16(for_core(p, core).reshape((-1,) + p.shape[2:]), f.reshape((-1,) + p.shape[2:]),
                       rows_of[n], "a2a_add").reshape((4,) + p.shape[1:])
              for n, p, f in zip(big_names, parts, from_sibling)]
    recv = chip_exchange(summed, "a2a_chips")
    res = {}
    for n, r in zip(big_names, recv):
        res[n] = adamw(r, w[n], m[n], v[n], rows_of[n], "adamw_" + n)
    res["ada_w"] = adamw(g_ada_w[None], ada_w, m["ada_w"], v["ada_w"], 256, "adamw_ada_w")
    small_names = [n for n in LOCAL if n != "ada_w"]
    small_pack = lambda d: _pack([d[n] for n in small_names], 8)[None]
    res_small = adamw(small_pack(grads)[None], small_pack(w), small_pack(m), small_pack(v),
                      small_pack(w).shape[1], "adamw_small")
    unpacked = [_unpack(r, [w[n].shape for n in small_names]) for r in res_small]
    for i, n in enumerate(small_names):
        res[n] = tuple(u[i] for u in unpacked)
    outs = [[res[n][k] for n in WEIGHT_ORDER] for k in range(4)]
    return (loss.reshape(()), gx[None], *outs[0], *outs[1], *outs[2], *outs[3])
```

```python
import functools
import math

import jax
import jax.numpy as jnp
from jax import lax
from jax.experimental import pallas as pl
from jax.experimental.pallas import tpu as pltpu

F32 = jnp.float32
BF16 = jnp.bfloat16

D = 1024
NH = 4
DK = 128
DV = 256
RET_C = 128
DN_C = 64
FFN_H = 2816
FFN_HB = 1408
N_MOD = 9
DEPTH = 4
N_DEV = 8
EPS = 1e-6
CONV_K = 5
HALO = 8

PW = 8320
C_RQK, C_RV, C_RG, C_DQK, C_DVV, C_DZ, C_GA, C_GB = 0, 1, 2, 3, 4, 5, 6, 7
C_DBA = 64
PROJ_TN = 1664

LANE = 128
VMEM_LIMIT = 56 * 1024 * 1024

ADAM_LR, ADAM_B1, ADAM_B2, ADAM_EPS, ADAM_WD, ADAM_STEP = 0.001, 0.9, 0.999, 1e-08, 0.01, 10

NN = ((1,), (0,))
NT = ((1,), (1,))
TN = ((0,), (0,))
HI = lax.Precision.HIGHEST


def _params(n_grid):
    return pltpu.CompilerParams(dimension_semantics=("arbitrary",) * n_grid, vmem_limit_bytes=VMEM_LIMIT)


def _dot(a, b, dims):
    return lax.dot_general(a.astype(BF16), b.astype(BF16), (dims, ((), ())), preferred_element_type=F32)


def _dot_hi(a, b, dims=NN):
    return lax.dot_general(a, b, (dims, ((), ())), precision=HI, preferred_element_type=F32)


@jax.custom_vjp
def mm_nn(a, b):
    return _dot(a, b, NN)


mm_nn.defvjp(lambda a, b: (_dot(a, b, NN), (a, b)),
             lambda r, g: (_dot(g, r[1], NT), _dot(r[0], g, TN)))


@jax.custom_vjp
def mm_nt(a, b):
    return _dot(a, b, NT)


mm_nt.defvjp(lambda a, b: (_dot(a, b, NT), (a, b)),
             lambda r, g: (_dot(g, r[1], NN), _dot(g, r[0], TN)))


@jax.custom_vjp
def mm_tn(a, b):
    return _dot(a, b, TN)


mm_tn.defvjp(lambda a, b: (_dot(a, b, TN), (a, b)),
             lambda r, g: (_dot(r[1], g, NT), _dot(r[0], g, NN)))


def _split_bf16(x, n):
    parts = []
    for _ in range(n):
        p = x.astype(BF16)
        parts.append(p)
        x = x - p.astype(F32)
    return parts


def _dot_f32(a, b, dims, exact=None):
    mm = lambda p, q: lax.dot_general(p, q, (dims, ((), ())), preferred_element_type=F32)
    if exact == "a":
        ab = a.astype(BF16)
        b1, b2, b3 = _split_bf16(b, 3)
        return mm(ab, b1) + (mm(ab, b2) + mm(ab, b3))
    if exact == "b":
        bb = b.astype(BF16)
        a1, a2, a3 = _split_bf16(a, 3)
        return mm(a1, bb) + (mm(a2, bb) + mm(a3, bb))
    a1, a2 = _split_bf16(a, 2)
    b1, b2 = _split_bf16(b, 2)
    return mm(a1, b1) + (mm(a1, b2) + mm(a2, b1))


@jax.custom_vjp
def _cum_rows(mask, x):
    return _dot_f32(mask, x, NN, exact="a")


_cum_rows.defvjp(lambda mask, x: (_dot_f32(mask, x, NN, exact="a"), mask),
                 lambda mask, g: (jnp.zeros_like(mask), _dot_f32(mask, g, TN, exact="a")))


@jax.custom_vjp
def _row_bcast(xb):
    return _dot_f32(jnp.full(xb.shape, 1.0 / LANE, F32), xb, NT, exact="a")


_row_bcast.defvjp(lambda xb: (_row_bcast(xb), None),
                  lambda _, g: (_dot_f32(g, jnp.full((g.shape[0], LANE), 1.0 / LANE, F32), TN, exact="b"),))


def _tri_inv_fwd(a):
    c = a.shape[0]
    eye = (lax.broadcasted_iota(jnp.int32, (c, c), 0) == lax.broadcasted_iota(jnp.int32, (c, c), 1)).astype(F32)
    pw = -a
    tinv = eye + pw
    m = 2
    while m < DN_C:
        pw = _dot_f32(pw, pw, NN)
        tinv = tinv + _dot_f32(tinv, pw, NN)
        m *= 2
    return tinv


@jax.custom_vjp
def _tri_inv(a):
    return _tri_inv_fwd(a)


def _tri_inv_bwd(tinv, g):
    return (-_dot_f32(_dot_f32(tinv, g, TN), tinv, NT),)


_tri_inv.defvjp(lambda a: (lambda t: (t, t))(_tri_inv_fwd(a)), _tri_inv_bwd)


@jax.custom_vjp
def _tri_inv_known(a, tinv):
    del a
    return tinv


_tri_inv_known.defvjp(lambda a, tinv: (tinv, tinv),
                      lambda tinv, g: (_tri_inv_bwd(tinv, g)[0], jnp.zeros_like(tinv)))


def _normmod(x, nw, shift, scale):
    r = lax.rsqrt(jnp.mean(x * x, axis=-1, keepdims=True) + EPS)
    return (x * r * nw) * (1.0 + scale) + shift


def _sigmoid(x):
    return 0.5 * jnp.tanh(0.5 * x) + 0.5


def _softplus(x):
    return jnp.maximum(x, 0.0) + jnp.log(1.0 + jnp.exp(-jnp.abs(x)))


def _row_block(t_len):
    return 512 if t_len % 512 == 0 else 256


def ffn_fwd(x, vec, wgu, wd):
    t_len = x.shape[0]
    tm = _row_block(t_len)

    def body(x_ref, vec_ref, wg_ref, wu_ref, wd_ref, x1_ref, h_ref, a_ref, s_ref, m_ref, f_ref, hs, acc):
        j = pl.program_id(1)

        @pl.when(j == 0)
        def _():
            hb = _normmod(x_ref[...], vec_ref[0:1, :], vec_ref[1:2, :], vec_ref[2:3, :]).astype(BF16)
            hs[...] = hb
            h_ref[...] = hb
            acc[...] = jnp.zeros_like(acc)

        hb = hs[...]
        g = jnp.dot(hb, wg_ref[...], preferred_element_type=F32)
        u = jnp.dot(hb, wu_ref[...], preferred_element_type=F32)
        sg = _sigmoid(g)
        sl = g * sg
        act = (sl * u).astype(BF16)
        a_ref[...] = act
        s_ref[...] = sl.astype(BF16)
        m_ref[...] = (u * (sg + sl * (1.0 - sg))).astype(BF16)
        acc[...] += jnp.dot(act, wd_ref[...], preferred_element_type=F32)

        @pl.when(j == 1)
        def _():
            f = acc[...]
            f_ref[...] = f.astype(BF16)
            x1_ref[...] = x_ref[...] + 0.5 * vec_ref[3:4, :] * f

    return pl.pallas_call(
        body, name="ffn_fwd", grid=(t_len // tm, 2),
        in_specs=[
            pl.BlockSpec((tm, D), lambda i, j: (i, 0)),
            pl.BlockSpec((8, D), lambda i, j: (0, 0)),
            pl.BlockSpec((D, FFN_HB), lambda i, j: (0, j)),
            pl.BlockSpec((D, FFN_HB), lambda i, j: (0, j + 2)),
            pl.BlockSpec((FFN_HB, D), lambda i, j: (j, 0)),
        ],
        out_specs=[
            pl.BlockSpec((tm, D), lambda i, j: (i, 0)),
            pl.BlockSpec((tm, D), lambda i, j: (i, 0)),
            pl.BlockSpec((tm, FFN_HB), lambda i, j: (i, j)),
            pl.BlockSpec((tm, FFN_HB), lambda i, j: (i, j)),
            pl.BlockSpec((tm, FFN_HB), lambda i, j: (i, j)),
            pl.BlockSpec((tm, D), lambda i, j: (i, 0)),
        ],
        out_shape=[
            jax.ShapeDtypeStruct((t_len, D), F32),
            jax.ShapeDtypeStruct((t_len, D), BF16),
            jax.ShapeDtypeStruct((t_len, FFN_H), BF16),
            jax.ShapeDtypeStruct((t_len, FFN_H), BF16),
            jax.ShapeDtypeStruct((t_len, FFN_H), BF16),
            jax.ShapeDtypeStruct((t_len, D), BF16),
        ],
        scratch_shapes=[pltpu.VMEM((tm, D), BF16), pltpu.VMEM((tm, D), F32)],
        compiler_params=_params(2),
    )(x, vec, wgu, wgu, wd)


def ffn_bwd_act(dx1, vec, sl, dact_dg, f, wd):
    t_len = dx1.shape[0]
    tm = _row_block(t_len)

    def body(dx_ref, vec_ref, s_ref, m_ref, f_ref, wd_ref, dg_ref, du_ref, dy_ref, pg_ref, dys):
        i, j = pl.program_id(0), pl.program_id(1)

        @pl.when((i == 0) & (j == 0))
        def _():
            pg_ref[...] = jnp.zeros_like(pg_ref)

        @pl.when(j == 0)
        def _():
            dx = dx_ref[...]
            dyb = (0.5 * vec_ref[3:4, :] * dx).astype(BF16)
            dys[...] = dyb
            dy_ref[...] = dyb
            pg_ref[3:4, :] += jnp.sum(0.5 * dx * f_ref[...].astype(F32), axis=0, keepdims=True)

        dact = _dot(dys[...], wd_ref[...], NT)
        du_ref[...] = (dact * s_ref[...].astype(F32)).astype(BF16)
        dg_ref[...] = (dact * m_ref[...].astype(F32)).astype(BF16)

    return pl.pallas_call(
        body, name="ffn_bwd_act", grid=(t_len // tm, 2),
        in_specs=[
            pl.BlockSpec((tm, D), lambda i, j: (i, 0)),
            pl.BlockSpec((8, D), lambda i, j: (0, 0)),
            pl.BlockSpec((tm, FFN_HB), lambda i, j: (i, j)),
            pl.BlockSpec((tm, FFN_HB), lambda i, j: (i, j)),
            pl.BlockSpec((tm, D), lambda i, j: (i, 0)),
            pl.BlockSpec((FFN_HB, D), lambda i, j: (j, 0)),
        ],
        out_specs=[
            pl.BlockSpec((tm, FFN_HB), lambda i, j: (i, j)),
            pl.BlockSpec((tm, FFN_HB), lambda i, j: (i, j)),
            pl.BlockSpec((tm, D), lambda i, j: (i, 0)),
            pl.BlockSpec((8, D), lambda i, j: (0, 0)),
        ],
        out_shape=[
            jax.ShapeDtypeStruct((t_len, FFN_H), BF16),
            jax.ShapeDtypeStruct((t_len, FFN_H), BF16),
            jax.ShapeDtypeStruct((t_len, D), BF16),
            jax.ShapeDtypeStruct((8, D), F32),
        ],
        scratch_shapes=[pltpu.VMEM((tm, D), BF16)],
        compiler_params=_params(2),
    )(dx1, vec, sl, dact_dg, f, wd)


def nt_norm_bwd(dys, w, col_offsets, tk, n_steps, x_in, dres, vec, name):
    t_len = x_in.shape[0]
    tm = _row_block(t_len)
    n_seg = len(dys)

    def body(*refs):
        dy_refs = refs[:n_seg]
        w_refs = refs[n_seg:2 * n_seg]
        x_ref, dres_ref, vec_ref, dx_ref, pg_ref, acc = refs[2 * n_seg:]
        i, j = pl.program_id(0), pl.program_id(1)

        @pl.when((i == 0) & (j == 0))
        def _():
            pg_ref[...] = jnp.zeros_like(pg_ref)

        @pl.when(j == 0)
        def _():
            acc[...] = jnp.zeros_like(acc)

        part = _dot(dy_refs[0][...], w_refs[0][...], NT)
        for s in range(1, n_seg):
            part += _dot(dy_refs[s][...], w_refs[s][...], NT)
        acc[...] += part

        @pl.when(j == n_steps - 1)
        def _():
            _, vjp = jax.vjp(_normmod, x_ref[...], vec_ref[0:1, :], vec_ref[1:2, :], vec_ref[2:3, :])
            dxn, dnw, dsh, dsc = vjp(acc[...])
            dx_ref[...] = dres_ref[...] + dxn
            pg_ref[0:1, :] += dnw
            pg_ref[1:2, :] += dsh
            pg_ref[2:3, :] += dsc

    in_specs = [pl.BlockSpec((tm, tk), lambda i, j: (i, j)) for _ in range(n_seg)]
    in_specs += [pl.BlockSpec((D, tk), functools.partial(lambda i, j, off: (0, off + j), off=off))
                 for off in col_offsets]
    in_specs += [pl.BlockSpec((tm, D), lambda i, j: (i, 0)), pl.BlockSpec((tm, D), lambda i, j: (i, 0)),
                 pl.BlockSpec((8, D), lambda i, j: (0, 0))]
    return pl.pallas_call(
        body, name=name, grid=(t_len // tm, n_steps),
        in_specs=in_specs,
        out_specs=[pl.BlockSpec((tm, D), lambda i, j: (i, 0)), pl.BlockSpec((8, D), lambda i, j: (0, 0))],
        out_shape=[jax.ShapeDtypeStruct((t_len, D), F32), jax.ShapeDtypeStruct((8, D), F32)],
        scratch_shapes=[pltpu.VMEM((tm, D), F32)],
        compiler_params=_params(2),
    )(*dys, *([w] * n_seg), x_in, dres, vec)


def tn_matmul(a_x, b_x, a_c, b_c, tn, name):
    t_len, k_dim = a_x.shape
    n_dim = b_x.shape[1]
    t_ctx = a_c.shape[0]
    tt = 512
    n_t = t_len // tt

    def body(ax_ref, bx_ref, ac_ref, bc_ref, o_ref):
        t = pl.program_id(1)

        @pl.when(t == 0)
        def _():
            o_ref[...] = _dot(ac_ref[...], bc_ref[...], TN)

        o_ref[...] += _dot(ax_ref[...], bx_ref[...], TN)

    return pl.pallas_call(
        body, name=name, grid=(n_dim // tn, n_t),
        in_specs=[
            pl.BlockSpec((tt, k_dim), lambda n, t: (t, 0)),
            pl.BlockSpec((tt, tn), lambda n, t: (t, n)),
            pl.BlockSpec((t_ctx, k_dim), lambda n, t: (0, 0)),
            pl.BlockSpec((t_ctx, tn), lambda n, t: (0, n)),
        ],
        out_specs=pl.BlockSpec((k_dim, tn), lambda n, t: (0, n)),
        out_shape=jax.ShapeDtypeStruct((k_dim, n_dim), F32),
        compiler_params=_params(2),
    )(a_x, b_x, a_c, b_c)


N_SEG = 8


def proj_bwd_in(segs, dba, w, x_in, dres, vec):
    t_len = x_in.shape[0]
    tm = _row_block(t_len)

    def body(*refs):
        seg_refs = refs[:N_SEG]
        dba_ref, w_ref, wg_ref, x_ref, dres_ref, vec_ref, dx_ref, pg_ref, acc = refs[N_SEG:]
        i, j = pl.program_id(0), pl.program_id(1)

        @pl.when((i == 0) & (j == 0))
        def _():
            pg_ref[...] = jnp.zeros_like(pg_ref)

        @pl.when(j == 0)
        def _():
            acc[...] = _dot(dba_ref[...], wg_ref[...], NT)

        for s in range(N_SEG // 2):
            @pl.when(j == s)
            def _():
                acc[...] += (_dot(seg_refs[2 * s][...], w_ref[:, 0:D], NT)
                             + _dot(seg_refs[2 * s + 1][...], w_ref[:, D:2 * D], NT))

        @pl.when(j == N_SEG // 2 - 1)
        def _():
            _, vjp = jax.vjp(_normmod, x_ref[...], vec_ref[0:1, :], vec_ref[1:2, :], vec_ref[2:3, :])
            dxn, dnw, dsh, dsc = vjp(acc[...])
            dx_ref[...] = dres_ref[...] + dxn
            pg_ref[0:1, :] += dnw
            pg_ref[1:2, :] += dsh
            pg_ref[2:3, :] += dsc

    row = lambda width: pl.BlockSpec((tm, width), lambda i, j: (i, 0))
    return pl.pallas_call(
        body, name="proj_bwd_in", grid=(t_len // tm, N_SEG // 2),
        in_specs=[row(D)] * N_SEG + [row(LANE), pl.BlockSpec((D, 2 * D), lambda i, j: (0, j)),
                                     pl.BlockSpec((D, LANE), lambda i, j: (0, C_DBA)),
                                     row(D), row(D), pl.BlockSpec((8, D), lambda i, j: (0, 0))],
        out_specs=[row(D), pl.BlockSpec((8, D), lambda i, j: (0, 0))],
        out_shape=[jax.ShapeDtypeStruct((t_len, D), F32), jax.ShapeDtypeStruct((8, D), F32)],
        scratch_shapes=[pltpu.VMEM((tm, D), F32)],
        compiler_params=_params(2),
    )(*segs, dba, w, w, x_in, dres, vec)


def dw_in(a_x, segs_x, a_c, segs_c):
    t_len, k_dim = a_x.shape
    t_ctx = a_c.shape[0]
    tt = 512

    def body(*refs):
        ax_ref, ac_ref = refs[0], refs[1]
        bx_refs, bc_refs = refs[2:2 + N_SEG], refs[2 + N_SEG:2 + 2 * N_SEG]
        o_ref = refs[2 + 2 * N_SEG]
        n, t = pl.program_id(0), pl.program_id(1)
        for s in range(N_SEG):
            @pl.when(n == s)
            def _():
                @pl.when(t == 0)
                def _():
                    o_ref[...] = _dot(ac_ref[...], bc_refs[s][...], TN)

                o_ref[...] += _dot(ax_ref[...], bx_refs[s][...], TN)

    seg_x = [pl.BlockSpec((tt, D), functools.partial(lambda n, t, s: (jnp.where(n == s, t, 0), 0), s=s))
             for s in range(N_SEG)]
    seg_c = [pl.BlockSpec((t_ctx, D), lambda n, t: (0, 0)) for _ in range(N_SEG)]
    return pl.pallas_call(
        body, name="dw_in", grid=(N_SEG, t_len // tt),
        in_specs=[pl.BlockSpec((tt, k_dim), lambda n, t: (t, 0)), pl.BlockSpec((t_ctx, k_dim), lambda n, t: (0, 0))]
        + seg_x + seg_c,
        out_specs=pl.BlockSpec((k_dim, D), lambda n, t: (0, n)),
        out_shape=jax.ShapeDtypeStruct((k_dim, N_SEG * D), F32),
        compiler_params=_params(2),
    )(a_x, a_c, *segs_x, *segs_c)


def proj_fwd(x, vec, w_in_p):
    t_len = x.shape[0]
    tm = _row_block(t_len)

    def body(x_ref, vec_ref, w_ref, p_ref, h_ref, hs):
        @pl.when(pl.program_id(1) == 0)
        def _():
            hb = _normmod(x_ref[...], vec_ref[0:1, :], vec_ref[1:2, :], vec_ref[2:3, :]).astype(BF16)
            hs[...] = hb
            h_ref[...] = hb

        p_ref[...] = jnp.dot(hs[...], w_ref[...], preferred_element_type=F32)

    return pl.pallas_call(
        body, name="proj_fwd", grid=(t_len // tm, PW // PROJ_TN),
        in_specs=[
            pl.BlockSpec((tm, D), lambda i, j: (i, 0)),
            pl.BlockSpec((8, D), lambda i, j: (0, 0)),
            pl.BlockSpec((D, PROJ_TN), lambda i, j: (0, j)),
        ],
        out_specs=[pl.BlockSpec((tm, PROJ_TN), lambda i, j: (i, j)), pl.BlockSpec((tm, D), lambda i, j: (i, 0))],
        out_shape=[jax.ShapeDtypeStruct((t_len, PW), F32), jax.ShapeDtypeStruct((t_len, D), BF16)],
        scratch_shapes=[pltpu.VMEM((tm, D), BF16)],
        compiler_params=_params(2),
    )(x, vec, w_in_p)


def _shift_rows(e, s):
    n = e.shape[0]
    return pltpu.roll(e, (-s) % n, 0)


def _swap_halves(t):
    return pltpu.roll(t, DK // 2, 1)


def _halo_specs(tm, t_len, width, col, lead=False):
    per = tm // HALO
    last = t_len // HALO - 1
    if lead:
        return (pl.BlockSpec((2, HALO, width), lambda i: (0, jnp.maximum(i * per - 1, 0), col)),
                pl.BlockSpec((2, HALO, width), lambda i: (0, jnp.minimum((i + 1) * per, last), col)))
    return (pl.BlockSpec((HALO, width), lambda i: (jnp.maximum(i * per - 1, 0), col)),
            pl.BlockSpec((HALO, width), lambda i: (jnp.minimum((i + 1) * per, last), col)))


def _gate_cols(ba, gp_ref):
    lane = lax.broadcasted_iota(jnp.int32, ba.shape, 1)
    z = ba + gp_ref[1:2, :]
    return jnp.where(lane < 8, _sigmoid(ba), gp_ref[0:1, :] * _softplus(z))


def _conv_chunk(win, w_ref, c0, width):
    y = w_ref[0:1, c0:c0 + width] * _shift_rows(win, -2)
    for j in range(1, CONV_K):
        y += w_ref[j:j + 1, c0:c0 + width] * _shift_rows(win, j - 2)
    return y


def feat_fwd(p, cos2, sin2, conv_w8, gparams):
    t_len = p.shape[0]
    tm = _row_block(t_len)
    n_t = t_len // tm
    cw = 512

    def body(rqk_ref, dqk_ref, dqk_p, dqk_n, dvv_ref, dvv_p, dvv_n, dba_ref, cos_ref, sin_ref, cw_ref, gp_ref,
             o_rqk, o_dqkv, o_gbc):
        i = pl.program_id(0)
        cos, sin = cos_ref[...], sin_ref[...]
        for s in range(2 * NH):
            t = rqk_ref[:, s * DK:(s + 1) * DK]
            if s >= NH:
                t = t * (DK ** -0.5)
            o_rqk[:, s * DK:(s + 1) * DK] = t * cos + _swap_halves(t) * sin
        o_gbc[...] = _gate_cols(dba_ref[...], gp_ref)
        first, last = i == 0, i == n_t - 1
        for ci in range(4):
            src, sp, sn = (dqk_ref, dqk_p, dqk_n) if ci < 2 else (dvv_ref, dvv_p, dvv_n)
            c0 = (ci % 2) * cw
            win = jnp.concatenate([
                jnp.where(first, 0.0, sp[:, c0:c0 + cw]), src[:, c0:c0 + cw],
                jnp.where(last, 0.0, sn[:, c0:c0 + cw])], axis=0)
            y = _conv_chunk(win, cw_ref, ci * cw, cw)[HALO:HALO + tm]
            sv = y * _sigmoid(y)
            if ci < 2:
                scale = DK ** -0.5 if ci == 0 else 1.0
                for hh in range(NH):
                    sh = sv[:, hh * DK:(hh + 1) * DK]
                    nrm = lax.rsqrt(jnp.sum(sh * sh, axis=-1, keepdims=True) + EPS)
                    o_dqkv[:, ci * cw + hh * DK:ci * cw + (hh + 1) * DK] = sh * nrm * scale
            else:
                o_dqkv[:, ci * cw:(ci + 1) * cw] = sv

    hq = _halo_specs(tm, t_len, D, C_DQK)
    hv = _halo_specs(tm, t_len, D, C_DVV)
    return pl.pallas_call(
        body, name="feat_fwd", grid=(n_t,),
        in_specs=[
            pl.BlockSpec((tm, D), lambda i: (i, C_RQK)),
            pl.BlockSpec((tm, D), lambda i: (i, C_DQK)), hq[0], hq[1],
            pl.BlockSpec((tm, D), lambda i: (i, C_DVV)), hv[0], hv[1],
            pl.BlockSpec((tm, LANE), lambda i: (i, C_DBA)),
            pl.BlockSpec((tm, LANE), lambda i: (i, 0)),
            pl.BlockSpec((tm, LANE), lambda i: (i, 0)),
            pl.BlockSpec((8, 2 * D), lambda i: (0, 0)),
            pl.BlockSpec((8, LANE), lambda i: (0, 0)),
        ],
        out_specs=[pl.BlockSpec((tm, D), lambda i: (i, 0)), pl.BlockSpec((tm, 2 * D), lambda i: (i, 0)),
                   pl.BlockSpec((tm, LANE), lambda i: (i, 0))],
        out_shape=[jax.ShapeDtypeStruct((t_len, D), F32), jax.ShapeDtypeStruct((t_len, 2 * D), F32),
                   jax.ShapeDtypeStruct((t_len, LANE), F32)],
        compiler_params=_params(1),
    )(p, p, p, p, p, p, p, p, cos2, sin2, conv_w8, gparams)


def feat_bwd(p, cos2, sin2, conv_w8, gparams, d_rqk, d_rv, d_dqk, d_dvv, d_gbc):
    t_len = p.shape[0]
    tm = 256
    n_t = t_len // tm
    cw = 512

    def body(dqk_ref, dqk_p, dqk_n, dvv_ref, dvv_p, dvv_n, dba_ref, cos_ref, sin_ref, cw_ref, gp_ref,
             g_rqk, g_rv, g_dqk, g_dqk_p, g_dqk_n, g_dvv, g_dvv_p, g_dvv_n, g_gbc,
             o_rqk, o_rv, o_dqk, o_dvv, o_dba, o_cw, o_gp):
        i = pl.program_id(0)
        first, last = i == 0, i == n_t - 1

        @pl.when(first)
        def _():
            o_cw[...] = jnp.zeros_like(o_cw)
            o_gp[...] = jnp.zeros_like(o_gp)

        cos, sin = cos_ref[...], sin_ref[...]
        for s in range(2 * NH):
            gsl = g_rqk[0, :, s * DK:(s + 1) * DK] + g_rqk[1, :, s * DK:(s + 1) * DK]
            dt = gsl * cos + _swap_halves(gsl * sin)
            if s >= NH:
                dt = dt * (DK ** -0.5)
            o_rqk[:, s * DK:(s + 1) * DK] = dt.astype(BF16)
        o_rv[...] = (g_rv[0] + g_rv[1]).astype(BF16)

        ba = dba_ref[...]
        gg = g_gbc[0] + g_gbc[1]
        lane = lax.broadcasted_iota(jnp.int32, ba.shape, 1)
        sb = _sigmoid(ba)
        z = ba + gp_ref[1:2, :]
        a_row = gp_ref[0:1, :]
        dz = gg * a_row * _sigmoid(z)
        o_dba[...] = jnp.where(lane < 8, gg * sb * (1.0 - sb), dz).astype(BF16)
        is_g = (lane >= 8) & (lane < 16)
        o_gp[0:1, :] += jnp.sum(jnp.where(is_g, gg * a_row * _softplus(z), 0.0), axis=0, keepdims=True)
        o_gp[1:2, :] += jnp.sum(jnp.where(is_g, dz, 0.0), axis=0, keepdims=True)

        for ci in range(4):
            src, sp, sn = (dqk_ref, dqk_p, dqk_n) if ci < 2 else (dvv_ref, dvv_p, dvv_n)
            c0 = (ci % 2) * cw
            gc0 = ci * cw
            win = jnp.concatenate([
                jnp.where(first, 0.0, sp[:, c0:c0 + cw]), src[:, c0:c0 + cw],
                jnp.where(last, 0.0, sn[:, c0:c0 + cw])], axis=0)
            gs, gsp, gsn = (g_dqk, g_dqk_p, g_dqk_n) if ci < 2 else (g_dvv, g_dvv_p, g_dvv_n)
            gext = jnp.concatenate([
                jnp.where(first, 0.0, gsp[0, :, c0:c0 + cw] + gsp[1, :, c0:c0 + cw]),
                gs[0, :, c0:c0 + cw] + gs[1, :, c0:c0 + cw],
                jnp.where(last, 0.0, gsn[0, :, c0:c0 + cw] + gsn[1, :, c0:c0 + cw])], axis=0)
            y = _conv_chunk(win, cw_ref, gc0, cw)
            sg = _sigmoid(y)
            sv = y * sg
            if ci < 2:
                scale = DK ** -0.5 if ci == 0 else 1.0
                parts = []
                for hh in range(NH):
                    sh = sv[:, hh * DK:(hh + 1) * DK]
                    gh = gext[:, hh * DK:(hh + 1) * DK]
                    nrm = lax.rsqrt(jnp.sum(sh * sh, axis=-1, keepdims=True) + EPS)
                    dot = jnp.sum(gh * sh, axis=-1, keepdims=True)
                    parts.append(scale * nrm * (gh - sh * (nrm * nrm) * dot))
                ds = jnp.concatenate(parts, axis=1)
            else:
                ds = gext
            dy = ds * (sg + sv * (1.0 - sg))
            dpe = cw_ref[0:1, gc0:gc0 + cw] * _shift_rows(dy, 2)
            for j in range(1, CONV_K):
                dpe += cw_ref[j:j + 1, gc0:gc0 + cw] * _shift_rows(dy, 2 - j)
            dst = o_dqk if ci < 2 else o_dvv
            dst[:, c0:c0 + cw] = dpe[HALO:HALO + tm].astype(BF16)
            dyc = dy[HALO:HALO + tm]
            for j in range(CONV_K):
                o_cw[j:j + 1, gc0:gc0 + cw] += jnp.sum(dyc * _shift_rows(win, j - 2)[HALO:HALO + tm], axis=0,
                                                      keepdims=True)

    hq = _halo_specs(tm, t_len, D, C_DQK)
    hv = _halo_specs(tm, t_len, D, C_DVV)
    hg = _halo_specs(tm, t_len, D, 0, lead=True)
    outs = pl.pallas_call(
        body, name="feat_bwd", grid=(n_t,),
        in_specs=[
            pl.BlockSpec((tm, D), lambda i: (i, C_DQK)), hq[0], hq[1],
            pl.BlockSpec((tm, D), lambda i: (i, C_DVV)), hv[0], hv[1],
            pl.BlockSpec((tm, LANE), lambda i: (i, C_DBA)),
            pl.BlockSpec((tm, LANE), lambda i: (i, 0)),
            pl.BlockSpec((tm, LANE), lambda i: (i, 0)),
            pl.BlockSpec((8, 2 * D), lambda i: (0, 0)),
            pl.BlockSpec((8, LANE), lambda i: (0, 0)),
            pl.BlockSpec((2, tm, D), lambda i: (0, i, 0)),
            pl.BlockSpec((2, tm, D), lambda i: (0, i, 0)),
            pl.BlockSpec((2, tm, D), lambda i: (0, i, 0)), hg[0], hg[1],
            pl.BlockSpec((2, tm, D), lambda i: (0, i, 0)), hg[0], hg[1],
            pl.BlockSpec((2, tm, LANE), lambda i: (0, i, 0)),
        ],
        out_specs=[pl.BlockSpec((tm, D), lambda i: (i, 0))] * 4
        + [pl.BlockSpec((tm, LANE), lambda i: (i, 0)),
           pl.BlockSpec((8, 2 * D), lambda i: (0, 0)),
           pl.BlockSpec((8, LANE), lambda i: (0, 0))],
        out_shape=[jax.ShapeDtypeStruct((t_len, D), BF16)] * 4
        + [jax.ShapeDtypeStruct((t_len, LANE), BF16),
           jax.ShapeDtypeStruct((8, 2 * D), F32), jax.ShapeDtypeStruct((8, LANE), F32)],
        compiler_params=_params(1),
    )(p, p, p, p, p, p, p, cos2, sin2, conv_w8, gparams, d_rqk, d_rv, d_dqk, d_dqk, d_dqk, d_dvv, d_dvv, d_dvv,
      d_gbc)
    return outs


def _ret_chunk(q, k, v, s, lg, rev):
    c = q.shape[0]
    ii = lax.broadcasted_iota(jnp.int32, (c, c), 0).astype(F32)
    jj = lax.broadcasted_iota(jnp.int32, (c, c), 1).astype(F32)
    diff = jnp.where(rev, jj - ii, ii - jj)
    mask = diff >= jnp.where(rev, 1.0, 0.0)
    dec =jnp.where(mask, jnp.exp(lg * jnp.where(mask, diff, 0.0)), 0.0)
    idx = lax.broadcasted_iota(jnp.int32, (c, 1), 0).astype(F32)
    idx = jnp.where(rev, c - 1.0 - idx, idx)
    kdec = k * jnp.exp(lg * (c - 1.0 - idx))
    qdec = q * jnp.exp(lg * (idx + 1.0))
    o = mm_nn(mm_nt(q, k) * dec, v) + mm_nn(qdec, s)
    s2 = s * math.exp(lg * c) + mm_tn(kdec, v)
    return o, s2


def _log_gamma(h):
    return math.log1p(-(2.0 ** (-5.0 - h)))


DN_W = 3 * NH * DK
HS = NH * DN_C


def _dn_intra4(q4, k4, v4, g4, beta4, rev, tinv_known=None):
    ii = lax.broadcasted_iota(jnp.int32, (HS, HS), 0)
    jj = lax.broadcasted_iota(jnp.int32, (HS, HS), 1)
    blk = (ii // DN_C) == (jj // DN_C)
    dd = jnp.where(rev, jj - ii, ii - jj)
    incl = blk & (dd >= 0)
    strict = blk & (dd > 0)
    gb = jnp.broadcast_to(g4, (HS, LANE))
    gcb = _cum_rows(incl.astype(F32), gb)
    gc = jnp.max(gcb, axis=1, keepdims=True)
    gcr = _row_bcast(gcb)
    at_end = blk & ((jj % DN_C) == jnp.where(rev, 0, DN_C - 1))
    glast = jnp.sum(jnp.where(at_end, gcr, 0.0), axis=1, keepdims=True)
    decay = jnp.where(incl, jnp.exp(jnp.where(incl, gc - gcr, 0.0)), 0.0)
    kb = k4 * beta4
    a = jnp.where(strict, mm_nt(kb, k4) * decay, 0.0)
    tinv = _tri_inv(a) if tinv_known is None else _tri_inv_known(a, tinv_known)
    u = mm_nn(tinv, v4 * beta4)
    w = mm_nn(tinv, kb * jnp.exp(gc))
    attn = jnp.where(incl, mm_nt(q4, k4) * decay, 0.0)
    return (u, w, q4 * jnp.exp(gc), k4 * jnp.exp(glast - gc), attn), tinv


def _dn_seq4(u, w, qg, kd, attn, glast, s):
    v_new = [uh - mm_nn(wh, sh) for uh, wh, sh in zip(u, w, s)]
    s2 = [sh * jnp.exp(gh) + mm_tn(kh, vh) for sh, gh, kh, vh in zip(s, glast, kd, v_new)]
    o = jnp.concatenate([mm_nn(qh, sh) for qh, sh in zip(qg, s)], axis=0) + mm_nn(attn, jnp.concatenate(v_new, axis=0))
    return o, s2


def _stack_heads(ref_rows, width, col0=0):
    return jnp.concatenate([ref_rows(slice(col0 + h * width, col0 + (h + 1) * width)) for h in range(NH)], axis=0)


def _head_rows(h):
    return slice(h * DN_C, (h + 1) * DN_C)


def _dn_gates4(gbv, d):
    g4 = jnp.concatenate([_pick_lane(gbv, 8 + 4 * d + h) for h in range(NH)], axis=0)
    b4 = jnp.concatenate([_pick_lane(gbv, 4 * d + h) for h in range(NH)], axis=0)
    return g4, b4


def dn_intra_fwd(dqkv, gbc):
    t_len = dqkv.shape[0]
    rb = 256
    n_g = rb // DN_C

    def body(qk_ref, v_ref, gb_ref, u_ref, wqk_ref, at_ref, ti_ref):
        d = pl.program_id(0)
        rev = d == 1
        for gi in range(n_g):
            rows = slice(gi * DN_C, (gi + 1) * DN_C)
            g4, b4 = _dn_gates4(gb_ref[rows, :], d)
            (u, w, qg, kd, attn), tinv = _dn_intra4(_stack_heads(lambda cs: qk_ref[rows, cs], DK),
                                                    _stack_heads(lambda cs: qk_ref[rows, cs], DK, NH * DK),
                                                    _stack_heads(lambda cs: v_ref[rows, cs], DV), g4, b4, rev)
            at_ref[0, gi] = attn.astype(BF16)
            ti_ref[0, gi] = tinv
            for h in range(NH):
                hr = _head_rows(h)
                u_ref[0, rows, h * DV:(h + 1) * DV] = u[hr]
                wqk_ref[0, rows, h * DK:(h + 1) * DK] = w[hr].astype(BF16)
                wqk_ref[0, rows, NH * DK + h * DK:NH * DK + (h + 1) * DK] = qg[hr].astype(BF16)
                wqk_ref[0, rows, 2 * NH * DK + h * DK:2 * NH * DK + (h + 1) * DK] = kd[hr].astype(BF16)

    return pl.pallas_call(
        body, name="dn_intra_fwd", grid=(2, t_len // rb),
        in_specs=[pl.BlockSpec((rb, D), lambda d, i: (i, 0)), pl.BlockSpec((rb, D), lambda d, i: (i, 1)),
                  pl.BlockSpec((rb, LANE), lambda d, i: (i, 0))],
        out_specs=[pl.BlockSpec((1, rb, D), lambda d, i: (d, i, 0)),
                   pl.BlockSpec((1, rb, DN_W), lambda d, i: (d, i, 0)),
                   pl.BlockSpec((1, n_g, HS, HS), lambda d, i: (d, i, 0, 0)),
                   pl.BlockSpec((1, n_g, HS, HS), lambda d, i: (d, i, 0, 0))],
        out_shape=[jax.ShapeDtypeStruct((2, t_len, D), F32), jax.ShapeDtypeStruct((2, t_len, DN_W), BF16),
                   jax.ShapeDtypeStruct((2, t_len // DN_C, HS, HS), BF16),
                   jax.ShapeDtypeStruct((2, t_len // DN_C, HS, HS), F32)],
        compiler_params=_params(2),
    )(dqkv, dqkv, gbc)


SEQ_G = 4
RET_G = 2


def _dn_seq_inputs(u_ref, wqk_ref, at_ref, gb_ref, d, gidx):
    rows = pl.ds(pl.multiple_of(gidx * DN_C, DN_C), DN_C)
    u = [u_ref[0, rows, h * DV:(h + 1) * DV] for h in range(NH)]
    w = [wqk_ref[0, rows, h * DK:(h + 1) * DK].astype(F32) for h in range(NH)]
    qg = [wqk_ref[0, rows, NH * DK + h * DK:NH * DK + (h + 1) * DK].astype(F32) for h in range(NH)]
    kd = [wqk_ref[0, rows, 2 * NH * DK + h * DK:2 * NH * DK + (h + 1) * DK].astype(F32) for h in range(NH)]
    gbv = gb_ref[rows, :]
    glast = [jnp.sum(_pick_lane(gbv, 8 + 4 * d + h), axis=0, keepdims=True) for h in range(NH)]
    return u, w, qg, kd, at_ref[0, gidx].astype(F32), glast


def dn_seq_fwd(u, wqk, attn, gbc, s0):
    t_len = u.shape[1]
    n_b = t_len // (DN_C * SEQ_G)
    rb = DN_C * SEQ_G

    def block_of(d, t):
        return jnp.where(d == 0, t, n_b - 1 - t)

    def body(u_ref, wqk_ref, at_ref, gb_ref, s0_ref, o_ref, ss_ref, sf_ref, st):
        d, t = pl.program_id(0), pl.program_id(1)

        @pl.when(t == 0)
        def _():
            st[...] = s0_ref[0]

        for gi in range(SEQ_G):
            gidx = jnp.where(d == 0, gi, SEQ_G - 1 - gi)
            rows = pl.ds(pl.multiple_of(gidx * DN_C, DN_C), DN_C)
            s_in = [st[h] for h in range(NH)]
            ss_ref[0, gidx] = st[...].astype(BF16)
            o, s2 = _dn_seq4(*_dn_seq_inputs(u_ref, wqk_ref, at_ref, gb_ref, d, gidx), s_in)
            for h in range(NH):
                o_ref[0, rows, h * DV:(h + 1) * DV] = o[_head_rows(h)]
                st[h] = s2[h]

        @pl.when(t == n_b - 1)
        def _():
            sf_ref[0] = st[...]

    return pl.pallas_call(
        body, name="dn_seq_fwd", grid=(2, n_b),
        in_specs=[pl.BlockSpec((1, rb, D), lambda d, t: (d, block_of(d, t), 0)),
                  pl.BlockSpec((1, rb, DN_W), lambda d, t: (d, block_of(d, t), 0)),
                  pl.BlockSpec((1, SEQ_G, HS, HS), lambda d, t: (d, block_of(d, t), 0, 0)),
                  pl.BlockSpec((rb, LANE), lambda d, t: (block_of(d, t), 0)),
                  _state_spec()],
        out_specs=[pl.BlockSpec((1, rb, D), lambda d, t: (d, block_of(d, t), 0)),
                   pl.BlockSpec((1, SEQ_G, NH, DK, DV), lambda d, t: (d, block_of(d, t), 0, 0, 0)),
                   _state_spec()],
        out_shape=[jax.ShapeDtypeStruct((2, t_len, D), F32),
                   jax.ShapeDtypeStruct((2, t_len // DN_C, NH, DK, DV), BF16),
                   jax.ShapeDtypeStruct((2, NH, DK, DV), F32)],
        scratch_shapes=[pltpu.VMEM((NH, DK, DV), F32)],
        compiler_params=_params(2),
    )(u, wqk, attn, gbc, s0)


def dn_seq_bwd(u, wqk, attn, gbc, ssave, do, ds_fin):
    t_len = u.shape[1]
    n_c = t_len // DN_C
    n_b = n_c // SEQ_G
    rb = DN_C * SEQ_G

    def block_of(d, t):
        return jnp.where(d == 0, n_b - 1 - t, t)

    def body(u_ref, wqk_ref, at_ref, gb_ref, ss_ref, do_ref, dsf_ref, du_ref, dwqk_ref, dat_ref, dgl_ref, ds0_ref, dst):
        d, t = pl.program_id(0), pl.program_id(1)

        @pl.when(t == 0)
        def _():
            dst[...] = dsf_ref[0]

        rows8 = lax.broadcasted_iota(jnp.int32, (8, LANE), 0)
        for gi in range(SEQ_G):
            gidx = jnp.where(d == 0, SEQ_G - 1 - gi, gi)
            rows = pl.ds(pl.multiple_of(gidx * DN_C, DN_C), DN_C)
            s_in = [ss_ref[0, gidx, h].astype(F32) for h in range(NH)]
            _, vjp = jax.vjp(_dn_seq4, *_dn_seq_inputs(u_ref, wqk_ref, at_ref, gb_ref, d, gidx), s_in)
            do4 = _stack_heads(lambda cs: do_ref[rows, cs], DV)
            du, dw, dqg, dkd, dat, dgl, ds = vjp((do4, [dst[h] for h in range(NH)]))
            dat_ref[0, gidx] = dat
            dgl_tile = jnp.zeros((8, LANE), F32)
            for h in range(NH):
                du_ref[0, rows, h * DV:(h + 1) * DV] = du[h]
                dwqk_ref[0, rows, h * DK:(h + 1) * DK] = dw[h]
                dwqk_ref[0, rows, NH * DK + h * DK:NH * DK + (h + 1) * DK] = dqg[h]
                dwqk_ref[0, rows, 2 * NH * DK + h * DK:2 * NH * DK + (h + 1) * DK] = dkd[h]
                dgl_tile = jnp.where(rows8 == h, dgl[h], dgl_tile)
                dst[h] = ds[h]
            dgl_ref[0, gidx] = dgl_tile

        @pl.when(t == n_b - 1)
        def _():
            ds0_ref[0] = dst[...]

    seq = lambda width: pl.BlockSpec((1, rb, width), lambda d, t: (d, block_of(d, t), 0))
    att = pl.BlockSpec((1, SEQ_G, HS, HS), lambda d, t: (d, block_of(d, t), 0, 0))
    return pl.pallas_call(
        body, name="dn_seq_bwd", grid=(2, n_b),
        in_specs=[seq(D), seq(DN_W), att, pl.BlockSpec((rb, LANE), lambda d, t: (block_of(d, t), 0)),
                  pl.BlockSpec((1, SEQ_G, NH, DK, DV), lambda d, t: (d, block_of(d, t), 0, 0, 0)),
                  pl.BlockSpec((rb, D), lambda d, t: (block_of(d, t), 0)), _state_spec()],
        out_specs=[seq(D), seq(DN_W), att, pl.BlockSpec((1, SEQ_G, 8, LANE), lambda d, t: (d, block_of(d, t), 0, 0)),
                   _state_spec()],
        out_shape=[jax.ShapeDtypeStruct((2, t_len, D), F32), jax.ShapeDtypeStruct((2, t_len, DN_W), F32),
                   jax.ShapeDtypeStruct((2, n_c, HS, HS), F32), jax.ShapeDtypeStruct((2, n_c, 8, LANE), F32),
                   jax.ShapeDtypeStruct((2, NH, DK, DV), F32)],
        scratch_shapes=[pltpu.VMEM((NH, DK, DV), F32)],
        compiler_params=_params(2),
    )(u, wqk, attn, gbc, ssave, do, ds_fin)


def dn_intra_bwd(dqkv, gbc, tinv, du, dwqk, dattn, dgl):
    t_len = dqkv.shape[0]
    rb = 256
    n_g = rb // DN_C

    def body(qk_ref, v_ref, gb_ref, ti_ref, du_ref, dwqk_ref, dat_ref, dgl_ref, dqk_ref, dv_ref, dgb_ref):
        d = pl.program_id(0)
        rev = d == 1
        for gi in range(n_g):
            rows = slice(gi * DN_C, (gi + 1) * DN_C)
            g4, b4 = _dn_gates4(gb_ref[rows, :], d)
            _, vjp, _ = jax.vjp(functools.partial(_dn_intra4, rev=rev, tinv_known=ti_ref[0, gi]),
                                _stack_heads(lambda cs: qk_ref[rows, cs], DK),
                                _stack_heads(lambda cs: qk_ref[rows, cs], DK, NH * DK),
                                _stack_heads(lambda cs: v_ref[rows, cs], DV), g4, b4, has_aux=True)
            dq, dk, dv, dg, db = vjp((_stack_heads(lambda cs: du_ref[0, rows, cs], DV),
                                      _stack_heads(lambda cs: dwqk_ref[0, rows, cs], DK),
                                      _stack_heads(lambda cs: dwqk_ref[0, rows, cs], DK, NH * DK),
                                      _stack_heads(lambda cs: dwqk_ref[0, rows, cs], DK, 2 * NH * DK),
                                      dat_ref[0, gi]))
            dgb = jnp.zeros((DN_C, LANE), F32)
            for h in range(NH):
                hr = _head_rows(h)
                dqk_ref[0, rows, h * DK:(h + 1) * DK] = dq[hr]
                dqk_ref[0, rows, NH * DK + h * DK:NH * DK + (h + 1) * DK] = dk[hr]
                dv_ref[0, rows, h * DV:(h + 1) * DV] = dv[hr]
                dgb = (dgb + _put_lane(dg[hr] + dgl_ref[0, gi, h:h + 1, 0:1], 8 + 4 * d + h, LANE)
                       + _put_lane(db[hr], 4 * d + h, LANE))
            dgb_ref[0, rows, :] = dgb

    seq = lambda width: pl.BlockSpec((1, rb, width), lambda d, i: (d, i, 0))
    return pl.pallas_call(
        body, name="dn_intra_bwd", grid=(2, t_len // rb),
        in_specs=[pl.BlockSpec((rb, D), lambda d, i: (i, 0)), pl.BlockSpec((rb, D), lambda d, i: (i, 1)),
                  pl.BlockSpec((rb, LANE), lambda d, i: (i, 0)),
                  pl.BlockSpec((1, n_g, HS, HS), lambda d, i: (d, i, 0, 0)), seq(D), seq(DN_W),
                  pl.BlockSpec((1, n_g, HS, HS), lambda d, i: (d, i, 0, 0)),
                  pl.BlockSpec((1, n_g, 8, LANE), lambda d, i: (d, i, 0, 0))],
        out_specs=[seq(D), seq(D), seq(LANE)],
        out_shape=[jax.ShapeDtypeStruct((2, t_len, D), F32), jax.ShapeDtypeStruct((2, t_len, D), F32),
                   jax.ShapeDtypeStruct((2, t_len, LANE), F32)],
        compiler_params=_params(2),
    )(dqkv, dqkv, gbc, tinv, du, dwqk, dattn, dgl)


def _pick_lane(x, lane_idx):
    lane = lax.broadcasted_iota(jnp.int32, x.shape, 1)
    return jnp.sum(jnp.where(lane == lane_idx, x, 0.0), axis=1, keepdims=True)


def _put_lane(col, lane_idx, width):
    lane = lax.broadcasted_iota(jnp.int32, (col.shape[0], width), 1)
    return jnp.where(lane == lane_idx, col, 0.0)


def _state_spec():
    return pl.BlockSpec((1, NH, DK, DV), lambda d, t: (d, 0, 0, 0))


def ret_fwd(rqk, p, s0):
    c = RET_C
    t_len = rqk.shape[0]
    n_c = t_len // c
    n_b = n_c // RET_G
    rb = c * RET_G

    def block_of(d, t):
        return jnp.where(d == 0, t, n_b - 1 - t)

    def body(qk_ref, v_ref, s0_ref, o_ref, ss_ref, sf_ref, st):
        d, t = pl.program_id(0), pl.program_id(1)
        rev = d == 1

        @pl.when(t == 0)
        def _():
            st[...] = s0_ref[0]

        for gi in range(RET_G):
            gidx = jnp.where(d == 0, gi, RET_G - 1 - gi)
            rows = pl.ds(pl.multiple_of(gidx * c, c), c)
            ss_ref[0, gidx] = st[...].astype(BF16)
            for h in range(NH):
                o, s2 = _ret_chunk(qk_ref[rows, h * DK:(h + 1) * DK],
                                   qk_ref[rows, NH * DK + h * DK:NH * DK + (h + 1) * DK],
                                   v_ref[rows, h * DV:(h + 1) * DV], st[h], _log_gamma(h), rev)
                o_ref[0, rows, h * DV:(h + 1) * DV] = o
                st[h] = s2

        @pl.when(t == n_b - 1)
        def _():
            sf_ref[0] = st[...]

    return pl.pallas_call(
        body, name="ret_fwd", grid=(2, n_b),
        in_specs=[pl.BlockSpec((rb, D), lambda d, t: (block_of(d, t), 0)),
                  pl.BlockSpec((rb, D), lambda d, t: (block_of(d, t), C_RV)),
                  _state_spec()],
        out_specs=[pl.BlockSpec((1, rb, D), lambda d, t: (d, block_of(d, t), 0)),
                   pl.BlockSpec((1, RET_G, NH, DK, DV), lambda d, t: (d, block_of(d, t), 0, 0, 0)),
                   _state_spec()],
        out_shape=[jax.ShapeDtypeStruct((2, t_len, D), F32),
                   jax.ShapeDtypeStruct((2, n_c, NH, DK, DV), BF16),
                   jax.ShapeDtypeStruct((2, NH, DK, DV), F32)],
        scratch_shapes=[pltpu.VMEM((NH, DK, DV), F32)],
        compiler_params=_params(2),
    )(rqk, p, s0)


def ret_bwd(rqk, p, ssave, do, ds_fin):
    c = RET_C
    t_len = rqk.shape[0]
    n_b = t_len // (c * RET_G)
    rb = c * RET_G

    def block_of(d, t):
        return jnp.where(d == 0, n_b - 1 - t, t)

    def body(qk_ref, v_ref, ss_ref, do_ref, dsf_ref, dqk_ref, dv_ref, ds0_ref, dst):
        d, t = pl.program_id(0), pl.program_id(1)
        rev = d == 1

        @pl.when(t == 0)
        def _():
            dst[...] = dsf_ref[0]

        for gi in range(RET_G):
            gidx = jnp.where(d == 0, RET_G - 1 - gi, gi)
            rows = pl.ds(pl.multiple_of(gidx * c, c), c)
            for h in range(NH):
                _, vjp = jax.vjp(functools.partial(_ret_chunk, lg=_log_gamma(h), rev=rev),
                                 qk_ref[rows, h * DK:(h + 1) * DK],
                                 qk_ref[rows, NH * DK + h * DK:NH * DK + (h + 1) * DK],
                                 v_ref[rows, h * DV:(h + 1) * DV], ss_ref[0, gidx, h].astype(F32))
                dq, dk, dv, ds = vjp((do_ref[rows, h * DV:(h + 1) * DV], dst[h]))
                dqk_ref[0, rows, h * DK:(h + 1) * DK] = dq
                dqk_ref[0, rows, NH * DK + h * DK:NH * DK + (h + 1) * DK] = dk
                dv_ref[0, rows, h * DV:(h + 1) * DV] = dv
                dst[h] = ds

        @pl.when(t == n_b - 1)
        def _():
            ds0_ref[0] = dst[...]

    seq_spec = pl.BlockSpec((1, rb, D), lambda d, t: (d, block_of(d, t), 0))
    return pl.pallas_call(
        body, name="ret_bwd", grid=(2, n_b),
        in_specs=[pl.BlockSpec((rb, D), lambda d, t: (block_of(d, t), 0)),
                  pl.BlockSpec((rb, D), lambda d, t: (block_of(d, t), C_RV)),
                  pl.BlockSpec((1, RET_G, NH, DK, DV), lambda d, t: (d, block_of(d, t), 0, 0, 0)),
                  pl.BlockSpec((rb, D), lambda d, t: (block_of(d, t), 0)),
                  _state_spec()],
        out_specs=[seq_spec, seq_spec, _state_spec()],
        out_shape=[jax.ShapeDtypeStruct((2, t_len, D), F32)] * 2 + [jax.ShapeDtypeStruct((2, NH, DK, DV), F32)],
        scratch_shapes=[pltpu.VMEM((NH, DK, DV), F32)],
        compiler_params=_params(2),
    )(rqk, p, ssave, do, ds_fin)


def _head_gate(o_ret, o_dn, rg, dz, nw):
    ret = o_ret * lax.rsqrt(jnp.mean(o_ret * o_ret, axis=-1, keepdims=True) + EPS) * (rg * _sigmoid(rg))
    dn = o_dn * lax.rsqrt(jnp.mean(o_dn * o_dn, axis=-1, keepdims=True) + EPS) * nw * (dz * _sigmoid(dz))
    return ret, dn


MIX_TM = 256


def _mix_specs():
    seq = lambda col: pl.BlockSpec((MIX_TM, D), functools.partial(lambda i, col: (i, col), col=col))
    pair = pl.BlockSpec((2, MIX_TM, D), lambda i: (0, i, 0))
    wfull = pl.BlockSpec((D, D), lambda i: (0, 0))
    vec = pl.BlockSpec((8, D), lambda i: (0, 0))
    return seq, pair, wfull, vec


def mixout_fwd(x1, vec, o_ret, o_dn, p, w_r, w_d, w_o):
    t_len = x1.shape[0]

    def body(x_ref, vec_ref, or_ref, od_ref, rg_ref, dz_ref, ga_ref, gb_ref, wr_ref, wd_ref, wo_ref,
             x2_ref, ret_ref, dn_ref, y_ref, yr_ref, yd_ref, z_ref):
        for h in range(NH):
            sl = slice(h * DV, (h + 1) * DV)
            ret, dn = _head_gate(or_ref[0, :, sl] + or_ref[1, :, sl], od_ref[0, :, sl] + od_ref[1, :, sl],
                                 rg_ref[:, sl], dz_ref[:, sl], vec_ref[1:2, sl])
            ret_ref[:, sl] = ret.astype(BF16)
            dn_ref[:, sl] = dn.astype(BF16)
        yr = jnp.dot(ret_ref[...], wr_ref[...], preferred_element_type=F32)
        yd = jnp.dot(dn_ref[...], wd_ref[...], preferred_element_type=F32)
        yr_ref[...] = yr.astype(BF16)
        yd_ref[...] = yd.astype(BF16)
        y = (_sigmoid(ga_ref[...]) * yr + _sigmoid(gb_ref[...]) * yd).astype(BF16)
        y_ref[...] = y
        z = jnp.dot(y, wo_ref[...], preferred_element_type=F32)
        z_ref[...] = z.astype(BF16)
        x2_ref[...] = x_ref[...] + vec_ref[0:1, :] * z

    seq, pair, wfull, vecs = _mix_specs()
    return pl.pallas_call(
        body, name="mixout_fwd", grid=(t_len // MIX_TM,),
        in_specs=[seq(0), vecs, pair, pair, seq(C_RG), seq(C_DZ), seq(C_GA), seq(C_GB), wfull, wfull, wfull],
        out_specs=[seq(0)] * 7,
        out_shape=[jax.ShapeDtypeStruct((t_len, D), F32)] + [jax.ShapeDtypeStruct((t_len, D), BF16)] * 6,
        compiler_params=_params(1),
    )(x1, vec, o_ret, o_dn, p, p, p, p, w_r, w_d, w_o)


def mixout_bwd(dx2, vec, o_ret, o_dn, p, yr, yd, z, w_r, w_d, w_o):
    t_len = dx2.shape[0]

    def body(dx_ref, vec_ref, or_ref, od_ref, rg_ref, dz_ref, ga_ref, gb_ref, yr_ref, yd_ref, z_ref,
             wr_ref, wd_ref, wo_ref,
             dor_ref, dod_ref, drg_ref, ddz_ref, dga_ref, dgb_ref, dyr_ref, dyd_ref, dzz_ref, pg_ref):
        @pl.when(pl.program_id(0) == 0)
        def _():
            pg_ref[...] = jnp.zeros_like(pg_ref)

        dx = dx_ref[...]
        pg_ref[0:1, :] += jnp.sum(dx * z_ref[...].astype(F32), axis=0, keepdims=True)
        dzz = (vec_ref[0:1, :] * dx).astype(BF16)
        dzz_ref[...] = dzz
        dy = _dot(dzz, wo_ref[...], NT)
        sa = _sigmoid(ga_ref[...])
        sb = _sigmoid(gb_ref[...])
        dyr = (dy * sa).astype(BF16)
        dyd = (dy * sb).astype(BF16)
        dyr_ref[...] = dyr
        dyd_ref[...] = dyd
        dga_ref[...] = (dy * yr_ref[...].astype(F32) * sa * (1.0 - sa)).astype(BF16)
        dgb_ref[...] = (dy * yd_ref[...].astype(F32) * sb * (1.0 - sb)).astype(BF16)
        dret = _dot(dyr, wr_ref[...], NT)
        ddn = _dot(dyd, wd_ref[...], NT)
        for h in range(NH):
            sl = slice(h * DV, (h + 1) * DV)
            _, vjp = jax.vjp(_head_gate, or_ref[0, :, sl] + or_ref[1, :, sl], od_ref[0, :, sl] + od_ref[1, :, sl],
                             rg_ref[:, sl], dz_ref[:, sl], vec_ref[1:2, sl])
            d_or, d_od, d_rg, d_dz, d_nw = vjp((dret[:, sl], ddn[:, sl]))
            dor_ref[:, sl] = d_or
            dod_ref[:, sl] = d_od
            drg_ref[:, sl] = d_rg.astype(BF16)
            ddz_ref[:, sl] = d_dz.astype(BF16)
            pg_ref[1:2, sl] += d_nw

    seq, pair, wfull, vecs = _mix_specs()
    return pl.pallas_call(
        body, name="mixout_bwd", grid=(t_len // MIX_TM,),
        in_specs=[seq(0), vecs, pair, pair, seq(C_RG), seq(C_DZ), seq(C_GA), seq(C_GB), seq(0), seq(0), seq(0),
                  wfull, wfull, wfull],
        out_specs=[seq(0)] * 9 + [vecs],
        out_shape=[jax.ShapeDtypeStruct((t_len, D), F32)] * 2 + [jax.ShapeDtypeStruct((t_len, D), BF16)] * 7
        + [jax.ShapeDtypeStruct((8, D), F32)],
        compiler_params=_params(1),
    )(dx2, vec, o_ret, o_dn, p, p, p, p, yr, yd, z, w_r, w_d, w_o)


def _final_loss(x, w, target):
    y = x * lax.rsqrt(jnp.mean(x * x, axis=-1, keepdims=True) + EPS) * w
    err = y - target
    return 0.5 * jnp.sum(jnp.mean(err * err, axis=-1, keepdims=True), axis=0, keepdims=True)


def final_fwd_bwd(x, vec, target):
    t_len = x.shape[0]
    tm = _row_block(t_len)

    def body(x_ref, vec_ref, t_ref, dx_ref, pg_ref, loss_ref):
        @pl.when(pl.program_id(0) == 0)
        def _():
            pg_ref[...] = jnp.zeros_like(pg_ref)
            loss_ref[...] = jnp.zeros_like(loss_ref)

        loss, vjp = jax.vjp(functools.partial(_final_loss, target=t_ref[...]), x_ref[...], vec_ref[0:1, :])
        dx, dw = vjp(jnp.ones((1, 1), F32))
        dx_ref[...] = dx
        pg_ref[0:1, :] += dw
        loss_ref[...] += jnp.broadcast_to(loss, loss_ref.shape)

    return pl.pallas_call(
        body, name="final_fwd_bwd", grid=(t_len // tm,),
        in_specs=[pl.BlockSpec((tm, D), lambda i: (i, 0)), pl.BlockSpec((8, D), lambda i: (0, 0)),
                  pl.BlockSpec((tm, D), lambda i: (i, 0))],
        out_specs=[pl.BlockSpec((tm, D), lambda i: (i, 0)), pl.BlockSpec((8, D), lambda i: (0, 0)),
                   pl.BlockSpec((8, LANE), lambda i: (0, 0))],
        out_shape=[jax.ShapeDtypeStruct((t_len, D), F32), jax.ShapeDtypeStruct((8, D), F32),
                   jax.ShapeDtypeStruct((8, LANE), F32)],
        compiler_params=_params(1),
    )(x, vec, target)


def _vec8(*rows):
    rows = list(rows) + [jnp.zeros((D,), F32)] * (8 - len(rows))
    return jnp.stack(rows)


def _layer_vecs(lw, m):
    nw = lw["norm_w"]
    return dict(ffn1=_vec8(nw[0], m[0], m[1], m[2]), proj=_vec8(nw[1], m[3], m[4]),
                mix=_vec8(m[5], lw["dn_norm_t"]), ffn2=_vec8(nw[2], m[6], m[7], m[8]))


def _rope_tables(t_len, grid_w=64, base=10000.0):
    n_freq = DK // 4
    inv = base ** (-jnp.arange(n_freq, dtype=F32) / n_freq)
    tok = jnp.arange(t_len)
    ang = jnp.concatenate([(tok // grid_w).astype(F32)[:, None] * inv, (tok % grid_w).astype(F32)[:, None] * inv],
                          axis=-1)
    cos, sin = jnp.cos(ang), jnp.sin(ang)
    return jnp.concatenate([cos, cos], axis=-1), jnp.concatenate([-sin, sin], axis=-1)


def _stream_fwd_a(x0, lw, vecs, rope):
    x1, h1, act1, sl1, mg1, f1 = ffn_fwd(x0, vecs["ffn1"], lw["wgu1"], lw["wd1"])
    p, h2 = proj_fwd(x1, vecs["proj"], lw["w_in"])
    rqk, dqkv, gbc = feat_fwd(p, rope[0], rope[1], lw["conv_w8"], lw["gparams"])
    return dict(x0=x0, h1=h1, act1=act1, sl1=sl1, mg1=mg1, f1=f1, x1=x1, h2=h2, p=p, rqk=rqk, dqkv=dqkv, gbc=gbc)


def _stream_mix(sv, s0_ret, s0_dn):
    sv["o_ret"], sv["ss_ret"], sf_ret = ret_fwd(sv["rqk"], sv["p"], s0_ret)
    sv["dn_u"], sv["dn_wqk"], sv["dn_attn"], sv["dn_tinv"] = dn_intra_fwd(sv["dqkv"], sv["gbc"])
    sv["o_dn"], sv["ss_dn"], sf_dn = dn_seq_fwd(sv["dn_u"], sv["dn_wqk"], sv["dn_attn"], sv["gbc"], s0_dn)
    return sf_ret, sf_dn


def _stream_fwd_b(sv, lw, vecs):
    x2, ret, dn, y, yr, yd, z = mixout_fwd(sv["x1"], vecs["mix"], sv["o_ret"], sv["o_dn"], sv["p"],
                                           lw["w_r"], lw["w_d"], lw["w_o"])
    x3, h3, act3, sl3, mg3, f3 = ffn_fwd(x2, vecs["ffn2"], lw["wgu2"], lw["wd2"])
    sv.update(ret=ret, dn=dn, y=y, yr=yr, yd=yd, z=z, x2=x2, h3=h3, act3=act3, sl3=sl3, mg3=mg3, f3=f3)
    return x3


def layer_fwd(xs, cs, lw, ropes):
    vx, vc = _layer_vecs(lw, lw["mx"]), _layer_vecs(lw, lw["mc"])
    sx = _stream_fwd_a(xs, lw, vx, ropes[0])
    sc = _stream_fwd_a(cs, lw, vc, ropes[1])
    zero = jnp.zeros((2, NH, DK, DV), F32)
    sf_ret, sf_dn = _stream_mix(sc, zero, zero)
    _stream_mix(sx, sf_ret, sf_dn)
    x3 = _stream_fwd_b(sx, lw, vx)
    c3 = _stream_fwd_b(sc, lw, vc)
    return x3, c3, (sx, sc)


def _stream_bwd_a(dx3, sv, lw, vecs):
    dg3, du3, dy3, pg_a = ffn_bwd_act(dx3, vecs["ffn2"], sv["sl3"], sv["mg3"], sv["f3"], lw["wd2"])
    dx2, pg_b = nt_norm_bwd([dg3, du3], lw["wgu2"], [0, 2], FFN_HB, 2, sv["x2"], dx3, vecs["ffn2"], "ffn_bwd_in")
    (dor, dod, drg, ddz, dga, dgb, dyr, dyd, dzz, pg_m) = mixout_bwd(
        dx2, vecs["mix"], sv["o_ret"], sv["o_dn"], sv["p"], sv["yr"], sv["yd"], sv["z"], lw["w_r"], lw["w_d"], lw["w_o"])
    return dict(dg3=dg3, du3=du3, dy3=dy3, pg_a2=pg_a, pg_b2=pg_b, dx2=dx2, dor=dor, dod=dod,
                drg=drg, ddz=ddz, dga=dga, dgb=dgb, dyr=dyr, dyd=dyd, dzz=dzz, pg_m=pg_m)


def _stream_bwd_mix(bw, sv, dsf_ret, dsf_dn):
    bw["dqk_r"], bw["drv"], ds0_ret = ret_bwd(sv["rqk"], sv["p"], sv["ss_ret"], bw["dor"], dsf_ret)
    du, dwqk, dattn, dgl, ds0_dn = dn_seq_bwd(sv["dn_u"], sv["dn_wqk"], sv["dn_attn"], sv["gbc"], sv["ss_dn"],
                                              bw["dod"], dsf_dn)
    bw["dqk_d"], bw["dvv_d"], bw["dgbc"] = dn_intra_bwd(sv["dqkv"], sv["gbc"], sv["dn_tinv"], du, dwqk, dattn, dgl)
    return ds0_ret, ds0_dn


def _stream_bwd_b(bw, sv, lw, vecs, rope):
    dp_rqk, dp_rv, dp_dqk, dp_dvv, dp_dba, bw["dcw"], bw["dgp"] = feat_bwd(
        sv["p"], rope[0], rope[1], lw["conv_w8"], lw["gparams"], bw["dqk_r"], bw["drv"], bw["dqk_d"], bw["dvv_d"],
        bw["dgbc"])
    bw["dp_segs"] = [dp_rqk, dp_rv, bw["drg"], dp_dqk, dp_dvv, bw["ddz"], bw["dga"], bw["dgb"]]
    bw["dp_dba"] = dp_dba
    dx1, bw["pg_p"] = proj_bwd_in(bw["dp_segs"], dp_dba, lw["w_in"], sv["x1"], bw["dx2"], vecs["proj"])
    bw["dg1"], bw["du1"], bw["dy1"], bw["pg_a1"] = ffn_bwd_act(dx1, vecs["ffn1"], sv["sl1"], sv["mg1"], sv["f1"],
                                                              lw["wd1"])
    dx0, bw["pg_b1"] = nt_norm_bwd([bw["dg1"], bw["du1"]], lw["wgu1"], [0, 2], FFN_HB, 2, sv["x0"], dx1,
                                   vecs["ffn1"], "ffn_bwd_in")
    return dx0


def _stream_pgrads(bw):
    a1, b1, pp, pm, a2, b2 = bw["pg_a1"], bw["pg_b1"], bw["pg_p"], bw["pg_m"], bw["pg_a2"], bw["pg_b2"]
    dmod = jnp.stack([b1[1], b1[2], a1[3], pp[1], pp[2], pm[0], b2[1], b2[2], a2[3]])
    dnorm = jnp.stack([b1[0], pp[0], b2[0]])
    ddnw = pm[1].reshape(NH, DV).sum(axis=0)
    return dmod, dnorm, ddnw, bw["dcw"][:CONV_K], bw["dgp"][0, 8:16], bw["dgp"][1, 8:16]


def layer_bwd(dx3, dc3, lw, saved, ropes):
    sx, sc = saved
    vx, vc = _layer_vecs(lw, lw["mx"]), _layer_vecs(lw, lw["mc"])
    bx = _stream_bwd_a(dx3, sx, lw, vx)
    bc = _stream_bwd_a(dc3, sc, lw, vc)
    zero = jnp.zeros((2, NH, DK, DV), F32)
    ds0_ret, ds0_dn = _stream_bwd_mix(bx, sx, zero, zero)
    _stream_bwd_mix(bc, sc, ds0_ret, ds0_dn)
    dx0 = _stream_bwd_b(bx, sx, lw, vx, ropes[0])
    dc0 = _stream_bwd_b(bc, sc, lw, vc, ropes[1])

    def wgrad(a, b, tn, name):
        ax = sx[a] if a in sx else bx[a]
        ac = sc[a] if a in sc else bc[a]
        return tn_matmul(ax, bx[b], ac, bc[b], tn, name)

    gw = dict(
        wg1=wgrad("h1", "dg1", FFN_HB, "dw_ffn_gu"), wu1=wgrad("h1", "du1", FFN_HB, "dw_ffn_gu"),
        wd1=wgrad("act1", "dy1", 512, "dw_ffn_d"),
        w_in=jnp.concatenate([dw_in(sx["h2"], bx["dp_segs"], sc["h2"], bc["dp_segs"]),
                              wgrad("h2", "dp_dba", LANE, "dw_in_gate")], axis=-1),
        w_r=wgrad("ret", "dyr", 512, "dw_sq"), w_d=wgrad("dn", "dyd", 512, "dw_sq"), w_o=wgrad("y", "dzz", 512, "dw_sq"),
        wg2=wgrad("h3", "dg3", FFN_HB, "dw_ffn_gu"), wu2=wgrad("h3", "du3", FFN_HB, "dw_ffn_gu"),
        wd2=wgrad("act3", "dy3", 512, "dw_ffn_d"),
    )
    px, pc = _stream_pgrads(bx), _stream_pgrads(bc)
    small = dict(dmx=px[0], dmc=pc[0], norm_w=px[1] + pc[1], dn_norm_w=px[2] + pc[2], conv_w=px[3] + pc[3],
                 a_log=px[4] + pc[4], dt_bias=px[5] + pc[5])
    return dx0, dc0, gw, small


def local_step(x, ctx, target, final_norm_w, lws):
    t_len, t_ctx = x.shape[0], ctx.shape[0]
    ropes = (_rope_tables(t_len), (jnp.ones((t_ctx, LANE), F32), jnp.zeros((t_ctx, LANE), F32)))
    xs, cs, saved = x, ctx, []
    for lw in lws:
        xs, cs, sv = layer_fwd(xs, cs, lw, ropes)
        saved.append(sv)
    dx, pg_f, loss = final_fwd_bwd(xs, _vec8(final_norm_w), target)
    dc = jnp.zeros_like(ctx)
    gws, smalls = [None] * len(lws), [None] * len(lws)
    for l in reversed(range(len(lws))):
        dx, dc, gws[l], smalls[l] = layer_bwd(dx, dc, lws[l], saved[l], ropes)
    return loss[0, 0], dx, gws, smalls, pg_f[0]


ADA_ROWS = 16
ADA_COLS = N_MOD * D // N_DEV


def ada_fwd(cc, ada_w, ada_b_cols):
    def body(cc_ref, w_ref, b_ref, o_ref):
        cv = cc_ref[...]
        o_ref[0] = _dot_hi(cv * _sigmoid(cv), w_ref[0]) + b_ref[0]

    return pl.pallas_call(
        body, name="ada_fwd", grid=(DEPTH,),
        in_specs=[pl.BlockSpec((ADA_ROWS, D), lambda l: (0, 0)), pl.BlockSpec((1, D, ADA_COLS), lambda l: (l, 0, 0)),
                  pl.BlockSpec((1, 1, ADA_COLS), lambda l: (l, 0, 0))],
        out_specs=pl.BlockSpec((1, ADA_ROWS, ADA_COLS), lambda l: (l, 0, 0)),
        out_shape=jax.ShapeDtypeStruct((DEPTH, ADA_ROWS, ADA_COLS), F32),
        compiler_params=_params(1),
    )(cc, ada_w, ada_b_cols)


def ada_bwd(cc, dmod, ada_w):
    def body(cc_ref, d_ref, w_ref, gw_ref, ds_ref):
        cv = cc_ref[...]
        gw_ref[0] = _dot_hi(cv * _sigmoid(cv), d_ref[0], TN)
        ds_ref[0] = _dot_hi(d_ref[0], w_ref[0], NT)

    return pl.pallas_call(
        body, name="ada_bwd", grid=(DEPTH,),
        in_specs=[pl.BlockSpec((ADA_ROWS, D), lambda l: (0, 0)),
                  pl.BlockSpec((1, ADA_ROWS, ADA_COLS), lambda l: (l, 0, 0)),
                  pl.BlockSpec((1, D, ADA_COLS), lambda l: (l, 0, 0))],
        out_specs=[pl.BlockSpec((1, D, ADA_COLS), lambda l: (l, 0, 0)), pl.BlockSpec((1, ADA_ROWS, D), lambda l: (l, 0, 0))],
        out_shape=[jax.ShapeDtypeStruct((DEPTH, D, ADA_COLS), F32), jax.ShapeDtypeStruct((DEPTH, ADA_ROWS, D), F32)],
        compiler_params=_params(1),
    )(cc, dmod, ada_w)


def c_ctx_grad(parts, c_ctx8):
    def body(p_ref, c_ref, o_ref):
        acc = p_ref[0]
        for j in range(1, N_DEV):
            acc = acc + p_ref[j]
        cv = c_ref[...]
        sg = _sigmoid(cv)
        o_ref[...] = acc * (sg + cv * sg * (1.0 - sg))

    return pl.pallas_call(body, name="c_ctx_grad", out_shape=jax.ShapeDtypeStruct((8, LANE), F32))(parts, c_ctx8)


def sum_slots(parts):
    n_slot, rows, _ = parts.shape
    tr = 8
    for cand in (1024, 512, 256, 128, 64, 32, 16, 8):
        if rows % cand == 0:
            tr = cand
            break

    def body(p_ref, o_ref):
        acc = p_ref[0]
        for j in range(1, n_slot):
            acc = acc + p_ref[j]
        o_ref[...] = acc

    return pl.pallas_call(
        body, name="sum_slots", grid=(rows // tr,),
        in_specs=[pl.BlockSpec((n_slot, tr, LANE), lambda i: (0, i, 0))],
        out_specs=pl.BlockSpec((tr, LANE), lambda i: (i, 0)),
        out_shape=jax.ShapeDtypeStruct((rows, LANE), F32),
        compiler_params=_params(1),
    )(parts)


def adamw(parts, w, m, v, rows_blk, name):
    n_slot, n_l, n_a, n_b = parts.shape

    def body(p_ref, w_ref, m_ref, v_ref, g_ref, d_ref, m2_ref, v2_ref):
        g = p_ref[0].astype(F32)
        for j in range(1, n_slot):
            g = g + p_ref[j].astype(F32)
        m2 = ADAM_B1 * m_ref[...] + (1.0 - ADAM_B1) * g
        v2 = ADAM_B2 * v_ref[...] + (1.0 - ADAM_B2) * (g * g)
        m_hat = m2 / (1.0 - ADAM_B1 ** ADAM_STEP)
        v_hat = v2 / (1.0 - ADAM_B2 ** ADAM_STEP)
        g_ref[...] = g
        m2_ref[...] = m2
        v2_ref[...] = v2
        d_ref[...] = -ADAM_LR * (m_hat / (jnp.sqrt(v_hat) + ADAM_EPS) + ADAM_WD * w_ref[...])

    blk = pl.BlockSpec((1, rows_blk, n_b), lambda l, i: (l, i, 0))
    return pl.pallas_call(
        body, name=name, grid=(n_l, n_a // rows_blk),
        in_specs=[pl.BlockSpec((n_slot, 1, rows_blk, n_b), lambda l, i: (0, l, i, 0)), blk, blk, blk],
        out_specs=[blk] * 4,
        out_shape=[jax.ShapeDtypeStruct((n_l, n_a, n_b), F32)] * 4,
        compiler_params=_params(2),
    )(parts, w, m, v)


MESH = pl.DeviceIdType.MESH


def _my_index():
    return 4 * lax.axis_index("x") + 2 * lax.axis_index("y") + lax.axis_index("c")


def all_gather(shards, name):
    n = len(shards)

    def body(*refs):
        x_refs, out_refs = refs[:n], refs[n:2 * n]
        send_sems, recv_sems, local_sems = refs[2 * n:]
        x, y, c = lax.axis_index("x"), lax.axis_index("y"), lax.axis_index("c")
        me, sibling = (x, y, c), (x, y, 1 - c)
        chips = [(1 - x, y), (x, 1 - y), (1 - x, 1 - y)]

        def slot(a, px, py, pc):
            return out_refs[a].at[4 * px + 2 * py + pc]

        def copy(a, k, block, to, src=None):
            return pltpu.make_async_remote_copy(
                src_ref=slot(a, *block) if src is None else src, dst_ref=slot(a, *block),
                send_sem=send_sems.at[7 * a + k], recv_sem=recv_sems.at[7 * a + k], device_id=to, device_id_type=MESH)

        mine = [pltpu.make_async_copy(x_refs[a], slot(a, *me), local_sems.at[a]) for a in range(n)]
        for cp in mine:
            cp.start()
        first = []
        for a in range(n):
            first.append(copy(a, 0, me, sibling, src=x_refs[a]))
            first += [copy(a, 1 + j, me, (*chip, c), src=x_refs[a]) for j, chip in enumerate(chips)]
        for cp in first:
            cp.start()
        passed = []
        for j, chip in enumerate(chips):
            for a in range(n):
                copy(a, 1 + j, (*chip, c), me).wait_recv()
                fwd = copy(a, 4 + j, (*chip, c), sibling)
                fwd.start()
                passed.append(fwd)
        for a in range(n):
            copy(a, 0, sibling, me).wait_recv()
            for j, chip in enumerate(chips):
                copy(a, 4 + j, (*chip, 1 - c), me).wait_recv()
        for cp in first + passed:
            cp.wait_send()
        for cp in mine:
            cp.wait()

    return pl.pallas_call(
        body, name=name,
        in_specs=[pl.BlockSpec(memory_space=pl.ANY)] * n,
        out_specs=[pl.BlockSpec(memory_space=pl.ANY)] * n,
        out_shape=[jax.ShapeDtypeStruct((N_DEV,) + s.shape, s.dtype) for s in shards],
        scratch_shapes=[pltpu.SemaphoreType.DMA((7 * n,)), pltpu.SemaphoreType.DMA((7 * n,)),
                        pltpu.SemaphoreType.DMA((n,))],
    )(*shards)


def pair_exchange(arrs, name):
    n = len(arrs)

    def body(*refs):
        x_refs, out_refs = refs[:n], refs[n:2 * n]
        send_sems, recv_sems = refs[2 * n:]
        sibling = (lax.axis_index("x"), lax.axis_index("y"), 1 - lax.axis_index("c"))
        copies = [pltpu.make_async_remote_copy(src_ref=x_refs[a], dst_ref=out_refs[a], send_sem=send_sems.at[a],
                                               recv_sem=recv_sems.at[a], device_id=sibling, device_id_type=MESH)
                  for a in range(n)]
        for cp in copies:
            cp.start()
        for cp in copies:
            cp.wait()

    return pl.pallas_call(
        body, name=name,
        in_specs=[pl.BlockSpec(memory_space=pl.ANY)] * n,
        out_specs=[pl.BlockSpec(memory_space=pl.ANY)] * n,
        out_shape=[jax.ShapeDtypeStruct(s.shape, s.dtype) for s in arrs],
        scratch_shapes=[pltpu.SemaphoreType.DMA((n,)), pltpu.SemaphoreType.DMA((n,))],
    )(*arrs)


def chip_exchange(parts, name):
    n = len(parts)

    def body(*refs):
        p_refs, q_refs = refs[:n], refs[n:2 * n]
        send_sems, recv_sems, local_sems = refs[2 * n:]
        x, y, c = lax.axis_index("x"), lax.axis_index("y"), lax.axis_index("c")
        my_chip = 2 * x + y
        mine = [pltpu.make_async_copy(p_refs[a].at[my_chip], q_refs[a].at[my_chip], local_sems.at[a]) for a in range(n)]
        for cp in mine:
            cp.start()
        copies = []
        for r in range(1, 4):
            px, py = x ^ (r >> 1), y ^ (r & 1)
            peer = 2 * px + py
            for a in range(n):
                k = 3 * a + r - 1
                cp = pltpu.make_async_remote_copy(
                    src_ref=p_refs[a].at[peer], dst_ref=q_refs[a].at[my_chip], send_sem=send_sems.at[k],
                    recv_sem=recv_sems.at[k], device_id=(px, py, c), device_id_type=MESH)
                cp.start()
                copies.append((cp, a, peer, k))
        for cp, a, peer, k in copies:
            pltpu.make_async_remote_copy(
                src_ref=p_refs[a].at[peer], dst_ref=q_refs[a].at[peer], send_sem=send_sems.at[k],
                recv_sem=recv_sems.at[k], device_id=(x, y, c), device_id_type=MESH).wait_recv()
        for cp, _, _, _ in copies:
            cp.wait_send()
        for cp in mine:
            cp.wait()

    return pl.pallas_call(
        body, name=name,
        in_specs=[pl.BlockSpec(memory_space=pl.ANY)] * n,
        out_specs=[pl.BlockSpec(memory_space=pl.ANY)] * n,
        out_shape=[jax.ShapeDtypeStruct(p.shape, p.dtype) for p in parts],
        scratch_shapes=[pltpu.SemaphoreType.DMA((3 * n,)), pltpu.SemaphoreType.DMA((3 * n,)),
                        pltpu.SemaphoreType.DMA((n,))],
    )(*parts)


def add_bf16(a, b, rows_blk, name):
    n_l, n_a, n_b = a.shape

    def body(a_ref, b_ref, o_ref):
        o_ref[...] = (a_ref[...].astype(F32) + b_ref[...].astype(F32)).astype(BF16)

    blk = pl.BlockSpec((1, rows_blk, n_b), lambda l, i: (l, i, 0))
    return pl.pallas_call(
        body, name=name, grid=(n_l, n_a // rows_blk), in_specs=[blk, blk], out_specs=blk,
        out_shape=jax.ShapeDtypeStruct(a.shape, BF16), compiler_params=_params(2),
    )(a, b)


WEIGHT_ORDER = ("c_ctx", "ada_w", "ada_b", "norm_w", "ffn1_wgu", "ffn1_wd", "w_in", "dn_conv_w", "dn_a_log",
                "dn_dt_bias", "dn_norm_w", "w_ret_out", "w_dn_out", "w_o", "ffn2_wgu", "ffn2_wd", "final_norm_w")
BIG = (("ffn1_wgu", "col"), ("ffn1_wd", "row"), ("w_in", "col"), ("w_ret_out", "row"), ("w_dn_out", "row"),
       ("w_o", "row"), ("ffn2_wgu", "col"), ("ffn2_wd", "row"))
LOCAL = ("ada_w", "c_ctx", "ada_b", "norm_w", "dn_conv_w", "dn_a_log", "dn_dt_bias", "dn_norm_w", "final_norm_w")


def _pack(arrs, row_mult, lead=None):
    if lead is None:
        flat = jnp.concatenate([a.reshape(-1) for a in arrs])
        n = flat.shape[0]
    else:
        flat = jnp.concatenate([a.reshape(lead, -1) for a in arrs], axis=1)
        n = flat.shape[1]
    unit = LANE * row_mult
    total = -(-n // unit) * unit
    if lead is None:
        return jnp.pad(flat, (0, total - n)).reshape(total // LANE, LANE)
    return jnp.pad(flat, ((0, 0), (0, total - n))).reshape(lead, total // LANE, LANE)


def _unpack(packed, shapes, lead=None):
    flat = packed.reshape(-1) if lead is None else packed.reshape(lead, -1)
    out, off = [], 0
    for shp in shapes:
        n = math.prod(shp)
        if lead is None:
            out.append(flat[off:off + n].reshape(shp))
        else:
            out.append(flat[:, off:off + n].reshape((lead,) + tuple(shp)))
        off += n
    return out


def _join_shards(g, kind):
    lead = tuple(range(1, g.ndim - 2))
    a, b = g.shape[-2:]
    if kind == "col":
        return g.transpose(*lead, g.ndim - 2, 0, g.ndim - 1).reshape(g.shape[1:-2] + (a, N_DEV * b))
    return g.transpose(*lead, 0, g.ndim - 2, g.ndim - 1).reshape(g.shape[1:-2] + (N_DEV * a, b))


def _split_shards(full, kind):
    a, b = full.shape
    if kind == "col":
        return full.reshape(a, N_DEV, b // N_DEV).transpose(1, 0, 2)
    return full.reshape(N_DEV, a // N_DEV, b)


def _pad_w_in(w):
    return jnp.concatenate([w[..., :6144], w[..., 6160:8208], w[..., 6144:6160],
                            jnp.zeros(w.shape[:-1] + (PW - 8208,), w.dtype)], axis=-1)


def _unpad_w_in(g):
    return jnp.concatenate([g[..., :6144], g[..., 8192:8208], g[..., 6144:8192]], axis=-1)


def kernel(x, c, ctx, c_ctx, ada_w, ada_b, norm_w, ffn1_wgu, ffn1_wd, w_in, dn_conv_w, dn_a_log, dn_dt_bias, dn_norm_w, w_ret_out, w_dn_out, w_o, ffn2_wgu, ffn2_wd, final_norm_w, loss_target, m_c_ctx, m_ada_w, m_ada_b, m_norm_w, m_ffn1_wgu, m_ffn1_wd, m_w_in, m_dn_conv_w, m_dn_a_log, m_dn_dt_bias, m_dn_norm_w, m_w_ret_out, m_w_dn_out, m_w_o, m_ffn2_wgu, m_ffn2_wd, m_final_norm_w, v_c_ctx, v_ada_w, v_ada_b, v_norm_w, v_ffn1_wgu, v_ffn1_wd, v_w_in, v_dn_conv_w, v_dn_a_log, v_dn_dt_bias, v_dn_norm_w, v_w_ret_out, v_w_dn_out, v_w_o, v_ffn2_wgu, v_ffn2_wd, v_final_norm_w):
    w = dict(c_ctx=c_ctx, ada_w=ada_w, ada_b=ada_b, norm_w=norm_w, ffn1_wgu=ffn1_wgu, ffn1_wd=ffn1_wd, w_in=w_in, dn_conv_w=dn_conv_w, dn_a_log=dn_a_log, dn_dt_bias=dn_dt_bias, dn_norm_w=dn_norm_w, w_ret_out=w_ret_out, w_dn_out=w_dn_out, w_o=w_o, ffn2_wgu=ffn2_wgu, ffn2_wd=ffn2_wd, final_norm_w=final_norm_w)
    m = dict(c_ctx=m_c_ctx, ada_w=m_ada_w, ada_b=m_ada_b, norm_w=m_norm_w, ffn1_wgu=m_ffn1_wgu, ffn1_wd=m_ffn1_wd, w_in=m_w_in, dn_conv_w=m_dn_conv_w, dn_a_log=m_dn_a_log, dn_dt_bias=m_dn_dt_bias, dn_norm_w=m_dn_norm_w, w_ret_out=m_w_ret_out, w_dn_out=m_w_dn_out, w_o=m_w_o, ffn2_wgu=m_ffn2_wgu, ffn2_wd=m_ffn2_wd, final_norm_w=m_final_norm_w)
    v = dict(c_ctx=v_c_ctx, ada_w=v_ada_w, ada_b=v_ada_b, norm_w=v_norm_w, ffn1_wgu=v_ffn1_wgu, ffn1_wd=v_ffn1_wd, w_in=v_w_in, dn_conv_w=v_dn_conv_w, dn_a_log=v_dn_a_log, dn_dt_bias=v_dn_dt_bias, dn_norm_w=v_dn_norm_w, w_ret_out=v_w_ret_out, w_dn_out=v_w_dn_out, w_o=v_w_o, ffn2_wgu=v_ffn2_wgu, ffn2_wd=v_ffn2_wd, final_norm_w=v_final_norm_w)
    me = _my_index()
    big_names = [n for n, _ in BIG]

    g_big = all_gather([w[n].astype(BF16) for n in big_names], "ag_weights")
    full = {n: _join_shards(s, kind) for (n, kind), s in zip(BIG, g_big)}
    small_shapes = [norm_w.shape, dn_conv_w.shape, c.shape]
    g_small = all_gather([_pack([norm_w, dn_conv_w, c], 8)], "ag_small")[0]
    norm_s, conv_s, c_all = _unpack(g_small, small_shapes, lead=N_DEV)
    norm_full, conv_full = _join_shards(norm_s, "col"), _join_shards(conv_s, "col")

    cc = jnp.concatenate([c_all.reshape(N_DEV, D), c_ctx[None], jnp.zeros((ADA_ROWS - N_DEV - 1, D), F32)])
    ada_b_cols = lax.dynamic_slice_in_dim(ada_b, me * ADA_COLS, ADA_COLS, axis=1)[:, None, :]
    mods_part = ada_fwd(cc, ada_w, ada_b_cols)
    g_mods = all_gather([mods_part], "ag_mods")[0]
    mods_all = g_mods.transpose(1, 2, 0, 3).reshape(DEPTH, ADA_ROWS, N_MOD * D)
    mx = lax.dynamic_index_in_dim(mods_all, me, axis=1, keepdims=False).reshape(DEPTH, N_MOD, D)
    mc = mods_all[:, N_DEV].reshape(DEPTH, N_MOD, D)

    gparams = jnp.pad(jnp.stack([-jnp.exp(dn_a_log).reshape(DEPTH, 8), dn_dt_bias.reshape(DEPTH, 8)], axis=1),
                      ((0, 0), (0, 6), (8, LANE - 16)))
    stacked = dict(
        wgu1=full["ffn1_wgu"], wd1=full["ffn1_wd"], w_in=_pad_w_in(full["w_in"]), w_r=full["w_ret_out"],
        w_d=full["w_dn_out"], w_o=full["w_o"], wgu2=full["ffn2_wgu"], wd2=full["ffn2_wd"], norm_w=norm_full,
        conv_w8=jnp.pad(conv_full, ((0, 0), (0, 8 - CONV_K), (0, 0))), gparams=gparams,
        dn_norm_t=jnp.tile(dn_norm_w, (1, NH)), mx=mx, mc=mc)
    lws = [{k: a[l] for k, a in stacked.items()} for l in range(DEPTH)]
    loss_l, gx, gws, smalls, d_fnw = local_step(x[0], ctx[0], loss_target[0], final_norm_w, lws)

    sm_list = [jnp.stack([s[k] for s in smalls]) for k in ("dmx", "dmc", "norm_w", "conv_w", "a_log", "dt_bias",
                                                            "dn_norm_w")] + [d_fnw, loss_l.reshape(1)]
    sm_shapes = [a.shape for a in sm_list]
    g_sm = all_gather([_pack(sm_list, 8)], "ag_small_grads")[0]
    (dmx_sum, dmc_sum, g_norm, g_conv, g_alog, g_dtb, g_dnw, g_fnw, loss) = _unpack(sum_slots(g_sm), sm_shapes)
    dmx_all = _unpack(g_sm, sm_shapes[:1], lead=N_DEV)[0].reshape(N_DEV, DEPTH, N_MOD * D)
    dmc_sum = dmc_sum.reshape(DEPTH, N_MOD * D)
    dmx_sum = dmx_sum.reshape(DEPTH, N_MOD * D)
    dmod = jnp.concatenate([
        lax.dynamic_slice_in_dim(dmx_all, me * ADA_COLS, ADA_COLS, axis=2).transpose(1, 0, 2),
        lax.dynamic_slice_in_dim(dmc_sum, me * ADA_COLS, ADA_COLS, axis=1)[:, None, :],
        jnp.zeros((DEPTH, ADA_ROWS - N_DEV - 1, ADA_COLS), F32)], axis=1)
    g_ada_w, d_sil = ada_bwd(cc, dmod, ada_w)
    g_cc = all_gather([d_sil[:, N_DEV].sum(axis=0).reshape(8, LANE)], "ag_c_ctx")[0]
    grads = dict(
        ada_w=g_ada_w, c_ctx=c_ctx_grad(g_cc, c_ctx.reshape(8, LANE)).reshape(D), ada_b=dmx_sum + dmc_sum,
        norm_w=lax.dynamic_slice_in_dim(g_norm, me * (D // N_DEV), D // N_DEV, axis=2),
        dn_conv_w=lax.dynamic_slice_in_dim(g_conv, me * (2 * D // N_DEV), 2 * D // N_DEV, axis=2),
        dn_a_log=g_alog.reshape(dn_a_log.shape), dn_dt_bias=g_dtb.reshape(dn_dt_bias.shape), dn_norm_w=g_dnw,
        final_norm_w=g_fnw)

    def layer_full(gw):
        return dict(ffn1_wgu=jnp.concatenate([gw["wg1"], gw["wu1"]], axis=-1), ffn1_wd=gw["wd1"],
                    w_in=_unpad_w_in(gw["w_in"]), w_ret_out=gw["w_r"], w_dn_out=gw["w_d"], w_o=gw["w_o"],
                    ffn2_wgu=jnp.concatenate([gw["wg2"], gw["wu2"]], axis=-1), ffn2_wd=gw["wd2"])

    full_g = [layer_full(gw) for gw in gws]
    parts = [jnp.stack([_split_shards(fg[n], kind) for fg in full_g], axis=1).astype(BF16) for n, kind in BIG]
    core = lax.axis_index("c")
    rows_of = {n: (256 if w[n].shape[1] % 256 == 0 else w[n].shape[1]) for n in big_names}

    def for_core(p, which):
        return lax.dynamic_index_in_dim(p.reshape((4, 2) + p.shape[1:]), which, axis=1, keepdims=False)

    from_sibling = pair_exchange([for_core(p, 1 - core) for p in parts], "a2a_pair")
    summed = [add_bf16(for_core(p, core).reshape((-1,) + p.shape[2:]), f.reshape((-1,) + p.shape[2:]),
                       rows_of[n], "a2a_add").reshape((4,) + p.shape[1:])
              for n, p, f in zip(big_names, parts, from_sibling)]
    recv = chip_exchange(summed, "a2a_chips")
    res = {}
    for n, r in zip(big_names, recv):
        res[n] = adamw(r, w[n], m[n], v[n], rows_of[n], "adamw_" + n)
    res["ada_w"] = adamw(g_ada_w[None], ada_w, m["ada_w"], v["ada_w"], 256, "adamw_ada_w")
    small_names = [n for n in LOCAL if n != "ada_w"]
    small_pack = lambda d: _pack([d[n] for n in small_names], 8)[None]
    res_small = adamw(small_pack(grads)[None], small_pack(w), small_pack(m), small_pack(v),
                      small_pack(w).shape[1], "adamw_small")
    unpacked = [_unpack(r, [w[n].shape for n in small_names]) for r in res_small]
    for i, n in enumerate(small_names):
        res[n] = tuple(u[i] for u in unpacked)
    outs = [[res[n][k] for n in WEIGHT_ORDER] for k in range(4)]
    return (loss.reshape(()), gx[None], *outs[0], *outs[1], *outs[2], *outs[3])
```
